```python
import functools
import jax, jax.numpy as jnp
from jax import lax
import numpy as np

D_MODEL = 2048
BATCH = 4
SEQ = 4096
DEPTH = 1
DEC_BATCH = 16
DEC_SEQ = 32
PAST_LEN = 1024

CHUNK = 64
LEFT_CHUNKS = 8
WINDOW = LEFT_CHUNKS * CHUNK
BAND = (LEFT_CHUNKS + 1) * CHUNK
HEAD_DIM = 128
D_A = D_MODEL // 2
N_HEADS_A = D_A // HEAD_DIM
D_M = D_MODEL // 4
N_HEADS_M = D_M // HEAD_DIM
N_MEM = 256
D_RNN = D_MODEL // 4
N_RNN_BLOCKS = 8
RNN_BLOCK = D_RNN // N_RNN_BLOCKS
CONV_B = 4
LRU_C = 8.0
D_FF = 3 * D_MODEL
CONV_F = 3
REL_CLIP = 256
EPS = 1e-6
D_MIX = D_A + D_M + D_RNN
D_IN = 3 * D_A + D_M + 2 * D_RNN
SPLITS = (D_A, 2 * D_A, 3 * D_A, 3 * D_A + D_M, 3 * D_A + D_M + D_RNN)
NEG = -1e30

kernel_name = 'hybrid_stream_chunk_encoder'


def rmsnorm(x, g):
    xf = x.astype(jnp.float32)
    y = xf * lax.rsqrt(jnp.mean(xf * xf, axis=-1, keepdims=True) + EPS)
    return (y * g.astype(jnp.float32)).astype(x.dtype)


def causal_dwconv(x, past, w, b):
    width = w.shape[0]
    t = x.shape[1]
    xp = jnp.concatenate([past, x], axis=1)
    y = b
    for k in range(width):
        y = y + w[k] * xp[:, k:k + t]
    return y, xp[:, -(width - 1):]


def softmax_attend(q, k, v, bias):
    s = jnp.einsum('...qhd,...khd->...hqk', q, k).astype(jnp.float32) * (HEAD_DIM ** -0.5) + bias
    p = jax.nn.softmax(s, axis=-1)
    return jnp.einsum('...hqk,...khd->...qhd', p.astype(v.dtype), v)


def rel_bias(table, dist):
    idx = jnp.clip(dist, -REL_CLIP, REL_CLIP) + REL_CLIP
    return jnp.take(table, idx, axis=1).astype(jnp.float32)


def band_attention_prompt(q, k, v, rel_table):
    b, s, h, dh = q.shape
    nc = s // CHUNK
    pad = ((0, 0), (WINDOW, 0), (0, 0), (0, 0))
    kp = jnp.pad(k, pad)
    vp = jnp.pad(v, pad)
    idx = jnp.arange(nc)[:, None] * CHUNK + jnp.arange(BAND)[None, :]
    kb = kp[:, idx]
    vb = vp[:, idx]
    qb = q.reshape(b, nc, CHUNK, h, dh)
    dist = jnp.arange(CHUNK)[:, None] + WINDOW - jnp.arange(BAND)[None, :]
    valid = jnp.where(idx >= WINDOW, 0.0, NEG).astype(jnp.float32)
    bias = rel_bias(rel_table, dist)[None] + valid[:, None, None, :]
    out = softmax_attend(qb, kb, vb, bias[None])
    return out.reshape(b, s, h, dh)


def band_attention_sample(q, k, v, k_past, v_past, rel_table):
    p_len = k_past.shape[1]
    t = q.shape[1]
    kk = jnp.concatenate([k_past, k], axis=1)
    vv = jnp.concatenate([v_past, v], axis=1)
    dist = (p_len + jnp.arange(t))[:, None] - jnp.arange(p_len + t)[None, :]
    return softmax_attend(q, kk, vv, rel_bias(rel_table, dist)[None])


def memory_kv(mem, g_mem, w_mem_kv, g_k_m):
    b, n, _ = mem.shape
    k, v = jnp.split(rmsnorm(mem, g_mem) @ w_mem_kv, 2, axis=-1)
    k = rmsnorm(k.reshape(b, n, N_HEADS_M, HEAD_DIM), g_k_m)
    return k, v.reshape(b, n, N_HEADS_M, HEAD_DIM)


def _lin_combine(left, right):
    a1, b1 = left
    a2, b2 = right
    return a1 * a2, a2 * b1 + b2


def rg_lru(x, h0, w_a, b_a, w_i, b_i, lam):
    b, t, _ = x.shape
    xb = x.reshape(b, t, N_RNN_BLOCKS, RNN_BLOCK)

    def gate(w, bias):
        z = jnp.einsum('btni,nij->btnj', xb, w).reshape(b, t, D_RNN) + bias
        return jax.nn.sigmoid(z.astype(jnp.float32))

    r = gate(w_a, b_a)
    i = gate(w_i, b_i)
    log_a = -LRU_C * r * jax.nn.softplus(-lam.astype(jnp.float32))
    a = jnp.exp(log_a)
    u = jnp.sqrt(-jnp.expm1(2.0 * log_a)) * (i * x.astype(jnp.float32))
    a_cum, b_cum = lax.associative_scan(_lin_combine, (a, u), axis=1)
    h = a_cum * h0.astype(jnp.float32)[:, None] + b_cum
    return h.astype(x.dtype), h[:, -1].astype(x.dtype)


def trunk_layer(x, attn_a, mem_k, mem_v, b_conv_past, b_h0, f_conv_past,
                g_attn, w_in, g_q_a, g_k_a, g_q_m, conv_b_w, conv_b_b,
                w_rg_a, b_rg_a, w_rg_i, b_rg_i, lru_lambda, w_out,
                g_ffn, w_up, conv_f_w, conv_f_b, w_down):
    b, t, _ = x.shape
    z = rmsnorm(x, g_attn) @ w_in
    q_a, k_a, v_a, q_m, x_b, g_b = jnp.split(z, SPLITS, axis=-1)
    q_a = rmsnorm(q_a.reshape(b, t, N_HEADS_A, HEAD_DIM), g_q_a)
    k_a = rmsnorm(k_a.reshape(b, t, N_HEADS_A, HEAD_DIM), g_k_a)
    v_a = v_a.reshape(b, t, N_HEADS_A, HEAD_DIM)
    o_a = attn_a(q_a, k_a, v_a).reshape(b, t, D_A)
    q_m = rmsnorm(q_m.reshape(b, t, N_HEADS_M, HEAD_DIM), g_q_m)
    o_m = softmax_attend(q_m, mem_k, mem_v, 0.0).reshape(b, t, D_M)
    xc, b_conv_new = causal_dwconv(x_b, b_conv_past, conv_b_w, conv_b_b)
    h, b_h_new = rg_lru(xc, b_h0, w_rg_a, b_rg_a, w_rg_i, b_rg_i, lru_lambda)
    o_b = h * jax.nn.gelu(g_b)
    x = x + jnp.concatenate([o_a, o_m, o_b], axis=-1) @ w_out
    up = rmsnorm(x, g_ffn) @ w_up
    up, f_conv_new = causal_dwconv(up, f_conv_past, conv_f_w, conv_f_b)
    val, gt = jnp.split(up, 2, axis=-1)
    x = x + (jax.nn.gelu(gt) * val) @ w_down
    return x, k_a, v_a, b_conv_new, b_h_new, f_conv_new


def setup_inputs(seed: int = 0) -> dict:
    key = jax.random.key(seed)
    keys = jax.random.split(key, 40)
    counter = [0]
    f32 = jnp.float32

    def nk():
        k = keys[counter[0]]
        counter[0] += 1
        return k

    def nrm(shape, scale=1.0):
        return jax.random.normal(nk(), shape, f32) * scale

    def gain(shape):
        return 1.0 + 0.1 * nrm(shape)

    L = DEPTH
    a_rows = min(WINDOW, PAST_LEN)
    inputs = {
        'x_prompt': nrm((BATCH, SEQ, D_MODEL)),
        'x_sample': nrm((DEC_BATCH, DEC_SEQ, D_MODEL)),
        'cache_a_k': nrm((L, DEC_BATCH, a_rows, N_HEADS_A, HEAD_DIM)),
        'cache_a_v': nrm((L, DEC_BATCH, a_rows, N_HEADS_A, HEAD_DIM)),
        'cache_mem_k': nrm((L, DEC_BATCH, N_MEM, N_HEADS_M, HEAD_DIM)),
        'cache_mem_v': nrm((L, DEC_BATCH, N_MEM, N_HEADS_M, HEAD_DIM)),
        'cache_b_conv': nrm((L, DEC_BATCH, CONV_B - 1, D_RNN)),
        'state_b_h': nrm((L, DEC_BATCH, D_RNN), 0.5),
        'cache_f_conv': nrm((L, DEC_BATCH, CONV_F - 1, 2 * D_FF)),
        'mem_prompt': nrm((BATCH, N_MEM, D_MODEL)),
        'g_attn': gain((L, D_MODEL)),
        'w_in': nrm((L, D_MODEL, D_IN), D_MODEL ** -0.5),
        'g_q_a': gain((L, HEAD_DIM)),
        'g_k_a': gain((L, HEAD_DIM)),
        'rel_table': nrm((L, N_HEADS_A, 2 * REL_CLIP + 1), 0.3),
        'g_q_m': gain((L, HEAD_DIM)),
        'g_k_m': gain((L, HEAD_DIM)),
        'g_mem': gain((L, D_MODEL)),
        'w_mem_kv': nrm((L, D_MODEL, 2 * D_M), D_MODEL ** -0.5),
        'conv_b_w': nrm((L, CONV_B, D_RNN), CONV_B ** -0.5),
        'conv_b_b': nrm((L, D_RNN), 0.02),
        'w_rg_a': nrm((L, N_RNN_BLOCKS, RNN_BLOCK, RNN_BLOCK), RNN_BLOCK ** -0.5),
        'b_rg_a': nrm((L, D_RNN), 0.02),
        'w_rg_i': nrm((L, N_RNN_BLOCKS, RNN_BLOCK, RNN_BLOCK), RNN_BLOCK ** -0.5),
        'b_rg_i': nrm((L, D_RNN), 0.02),
    }
    a0 = jax.random.uniform(nk(), (L, D_RNN), f32, 0.9, 0.999)
    s = a0 ** (1.0 / LRU_C)
    inputs['lru_lambda'] = jnp.log(s) - jnp.log1p(-s)
    inputs['w_out'] = nrm((L, D_MIX, D_MODEL), D_MIX ** -0.5)
    inputs['g_ffn'] = gain((L, D_MODEL))
    inputs['w_up'] = nrm((L, D_MODEL, 2 * D_FF), D_MODEL ** -0.5)
    inputs['conv_f_w'] = nrm((L, CONV_F, 2 * D_FF), CONV_F ** -0.5)
    inputs['conv_f_b'] = nrm((L, 2 * D_FF), 0.02)
    inputs['w_down'] = nrm((L, D_FF, D_MODEL), D_FF ** -0.5)
    return inputs


def reference(x_prompt, x_sample, cache_a_k, cache_a_v, cache_mem_k, cache_mem_v,
              cache_b_conv, state_b_h, cache_f_conv, mem_prompt,
              g_attn, w_in, g_q_a, g_k_a, rel_table, g_q_m, g_k_m, g_mem, w_mem_kv,
              conv_b_w, conv_b_b, w_rg_a, b_rg_a, w_rg_i, b_rg_i, lru_lambda,
              w_out, g_ffn, w_up, conv_f_w, conv_f_b, w_down):
    xp, xs = x_prompt, x_sample
    bp, sp, _ = xp.shape
    keep = min(WINDOW, sp)
    ak_p, av_p, mk_p, mv_p, bc_p, bh_p, fc_p = [], [], [], [], [], [], []
    ak_s, av_s, bc_s, bh_s, fc_s = [], [], [], [], []
    for l in range(DEPTH):
        shared = (g_attn[l], w_in[l], g_q_a[l], g_k_a[l], g_q_m[l], conv_b_w[l], conv_b_b[l],
                  w_rg_a[l], b_rg_a[l], w_rg_i[l], b_rg_i[l], lru_lambda[l], w_out[l],
                  g_ffn[l], w_up[l], conv_f_w[l], conv_f_b[l], w_down[l])
        mem_k, mem_v = memory_kv(mem_prompt, g_mem[l], w_mem_kv[l], g_k_m[l])
        xp, k_a, v_a, bc, bh, fc = trunk_layer(
            xp, functools.partial(band_attention_prompt, rel_table=rel_table[l]),
            mem_k, mem_v,
            jnp.zeros((bp, CONV_B - 1, D_RNN), xp.dtype),
            jnp.zeros((bp, D_RNN), xp.dtype),
            jnp.zeros((bp, CONV_F - 1, 2 * D_FF), xp.dtype),
            *shared)
        ak_p.append(k_a[:, sp - keep:])
        av_p.append(v_a[:, sp - keep:])
        mk_p.append(mem_k)
        mv_p.append(mem_v)
        bc_p.append(bc)
        bh_p.append(bh)
        fc_p.append(fc)
        xs, k_a, v_a, bc, bh, fc = trunk_layer(
            xs, functools.partial(band_attention_sample, k_past=cache_a_k[l],
                                  v_past=cache_a_v[l], rel_table=rel_table[l]),
            cache_mem_k[l], cache_mem_v[l], cache_b_conv[l], state_b_h[l], cache_f_conv[l],
            *shared)
        ak_s.append(k_a)
        av_s.append(v_a)
        bc_s.append(bc)
        bh_s.append(bh)
        fc_s.append(fc)
    return (xp, xs,
            jnp.stack(ak_p), jnp.stack(av_p), jnp.stack(mk_p), jnp.stack(mv_p),
            jnp.stack(bc_p), jnp.stack(bh_p), jnp.stack(fc_p),
            jnp.stack(ak_s), jnp.stack(av_s), jnp.stack(bc_s), jnp.stack(bh_s), jnp.stack(fc_s))
```

```python
import functools

import jax
import jax.numpy as jnp
from jax import lax
from jax.experimental import pallas as pl
from jax.experimental.pallas import tpu as pltpu

EPS = 1e-6
HEAD_DIM = 128
CHUNK = 64
LEFT_CHUNKS = 8
WINDOW = LEFT_CHUNKS * CHUNK
REL_CLIP = 256
LRU_C = 8.0
NEG = -1e30
SCALE = HEAD_DIM ** -0.5

LANES = 128
SUBLANES = 8
VMEM_LIMIT = 56 * 2 ** 20

QSUB = 2 * CHUNK
KSPAN = WINDOW + QSUB

BF16 = jnp.bfloat16
F32 = jnp.float32


def _params(sem):
    return pltpu.CompilerParams(dimension_semantics=sem, vmem_limit_bytes=VMEM_LIMIT)


def _rms(x, g):
    ms = jnp.mean(x * x, axis=-1, keepdims=True)
    return x * lax.rsqrt(ms + EPS) * g


def _norm_matmul_kernel(x_ref, g_ref, w_ref, hg_ref, o_ref, xn_ref, *, norm_ranges, tn):
    j = pl.program_id(1)

    @pl.when(j == 0)
    def _():
        xn_ref[...] = _rms(x_ref[...], g_ref[...]).astype(BF16)

    z = jnp.dot(xn_ref[...], w_ref[...], preferred_element_type=F32)
    cond = None
    for lo, hi in norm_ranges:
        c = jnp.logical_and(j >= lo, j < hi)
        cond = c if cond is None else jnp.logical_or(cond, c)

    @pl.when(cond)
    def _():
        for h in range(tn // HEAD_DIM):
            cs = slice(h * HEAD_DIM, (h + 1) * HEAD_DIM)
            o_ref[:, cs] = _rms(z[:, cs], hg_ref[:, cs])

    @pl.when(jnp.logical_not(cond))
    def _():
        o_ref[...] = z


def _norm_matmul(x, g, w_bf16, head_gains, norm_ranges, tm, tn):
    m, k = x.shape
    n = w_bf16.shape[1]
    kern = functools.partial(_norm_matmul_kernel, norm_ranges=norm_ranges, tn=tn)
    return pl.pallas_call(
        kern,
        grid=(m // tm, n // tn),
        in_specs=[
            pl.BlockSpec((tm, k), lambda i, j: (i, 0)),
            pl.BlockSpec((1, k), lambda i, j: (0, 0)),
            pl.BlockSpec((k, tn), lambda i, j: (0, j)),
            pl.BlockSpec((1, tn), lambda i, j: (0, j)),
        ],
        out_specs=pl.BlockSpec((tm, tn), lambda i, j: (i, j)),
        out_shape=jax.ShapeDtypeStruct((m, n), F32),
        scratch_shapes=[pltpu.VMEM((tm, k), BF16)],
        compiler_params=_params(("parallel", "arbitrary")),
        name="norm_matmul",
    )(x, g.reshape(1, k), w_bf16, head_gains.reshape(1, n))


def _rel_bias_kernel(tab_ref, o_ref, *, n_heads, tpad):
    t = tab_ref[...]
    hi = t.astype(BF16)
    r1 = t - hi.astype(F32)
    mid = r1.astype(BF16)
    lo = (r1 - mid.astype(F32)).astype(BF16)
    tix = lax.broadcasted_iota(jnp.int32, (tpad, KSPAN), 0)
    j = lax.broadcasted_iota(jnp.int32, (1, KSPAN), 1)
    for i in range(QSUB):
        jj = j - CHUNK * (i // CHUNK)
        valid = jnp.logical_and(jj >= 0, jj < WINDOW + CHUNK)
        idx = jnp.clip((i % CHUNK) + WINDOW - jj, -REL_CLIP, REL_CLIP) + REL_CLIP
        onehot = jnp.where(tix == idx, 1.0, 0.0).astype(BF16)
        val = (jnp.dot(hi, onehot, preferred_element_type=F32)
               + jnp.dot(mid, onehot, preferred_element_type=F32)
               + jnp.dot(lo, onehot, preferred_element_type=F32))
        o_ref[i] = jnp.where(valid, val, NEG)


def _rel_bias(rel_table):
    n_heads, tlen = rel_table.shape
    tpad = -(-tlen // LANES) * LANES
    tab = jnp.pad(rel_table, ((0, 0), (0, tpad - tlen)))
    out = pl.pallas_call(
        functools.partial(_rel_bias_kernel, n_heads=n_heads, tpad=tpad),
        out_shape=jax.ShapeDtypeStruct((QSUB, n_heads, KSPAN), F32),
        compiler_params=_params(None),
        name="rel_bias",
    )(tab)
    return jnp.transpose(out, (1, 0, 2))


def _band_attn_kernel(q_ref, kp_ref, kc_ref, vp_ref, vc_ref, bias_ref, o_ref, kcat_ref, vcat_ref,
                      *, n_heads, tq):
    t = pl.program_id(1)
    kcat_ref[0:WINDOW, :] = kp_ref[0].astype(BF16)
    kcat_ref[WINDOW:WINDOW + tq, :] = kc_ref[0].astype(BF16)
    vcat_ref[0:WINDOW, :] = vp_ref[0].astype(BF16)
    vcat_ref[WINDOW:WINDOW + tq, :] = vc_ref[0].astype(BF16)
    kidx = lax.broadcasted_iota(jnp.int32, (QSUB, KSPAN), 1)
    first = t == 0

    for h in range(n_heads):
        cs = slice(h * HEAD_DIM, (h + 1) * HEAD_DIM)

        def body(s, carry, cs=cs, h=h):
            r0 = pl.multiple_of(s * QSUB, QSUB)
            q = q_ref[0, pl.ds(r0, QSUB), cs].astype(BF16)
            k = kcat_ref[pl.ds(r0, KSPAN), cs]
            v = vcat_ref[pl.ds(r0, KSPAN), cs]
            sc = lax.dot_general(q, k, (((1,), (1,)), ((), ())), preferred_element_type=F32)
            sc = sc * SCALE + bias_ref[h]
            sc = jnp.where(jnp.logical_and(first, kidx + r0 < WINDOW), NEG, sc)
            m = jnp.max(sc, axis=-1, keepdims=True)
            p = jnp.exp(sc - m)
            l = jnp.sum(p, axis=-1, keepdims=True)
            o = jnp.dot(p.astype(BF16), v, preferred_element_type=F32) / l
            o_ref[0, pl.ds(r0, QSUB), cs] = o.astype(o_ref.dtype)
            return carry

        lax.fori_loop(0, tq // QSUB, body, 0)


def _band_attn(z3, bias, n_heads, tq):
    b, s, _ = z3.shape
    d_a = n_heads * HEAD_DIM
    assert tq == WINDOW
    prev = lambda bi, t: (bi, jnp.maximum(t - 1, 0))
    return pl.pallas_call(
        functools.partial(_band_attn_kernel, n_heads=n_heads, tq=tq),
        grid=(b, s // tq),
        in_specs=[
            pl.BlockSpec((1, tq, d_a), lambda bi, t: (bi, t, 0)),
            pl.BlockSpec((1, WINDOW, d_a), lambda bi, t: prev(bi, t) + (1,)),
            pl.BlockSpec((1, tq, d_a), lambda bi, t: (bi, t, 1)),
            pl.BlockSpec((1, WINDOW, d_a), lambda bi, t: prev(bi, t) + (2,)),
            pl.BlockSpec((1, tq, d_a), lambda bi, t: (bi, t, 2)),
            pl.BlockSpec((n_heads, QSUB, KSPAN), lambda bi, t: (0, 0, 0)),
        ],
        out_specs=pl.BlockSpec((1, tq, d_a), lambda bi, t: (bi, t, 0)),
        out_shape=jax.ShapeDtypeStruct((b, s, d_a), BF16),
        scratch_shapes=[pltpu.VMEM((WINDOW + tq, d_a), BF16), pltpu.VMEM((WINDOW + tq, d_a), BF16)],
        compiler_params=_params(("parallel", "parallel")),
        name="band_attn",
    )(z3, z3, z3, z3, z3, bias)


def _band_attn_sample_kernel(q_ref, kn_ref, vn_ref, ck_ref, cv_ref, bias_ref, o_ref, *, n_heads, t, p_len):
    for h in range(n_heads):
        cs = slice(h * HEAD_DIM, (h + 1) * HEAD_DIM)
        q = q_ref[0, :, cs].astype(BF16)
        ck = ck_ref[0, :, cs].astype(BF16)
        cv = cv_ref[0, :, cs].astype(BF16)
        kn = kn_ref[0, :, cs].astype(BF16)
        vn = vn_ref[0, :, cs].astype(BF16)
        dn = (((1,), (1,)), ((), ()))
        s1 = lax.dot_general(q, ck, dn, preferred_element_type=F32) * SCALE + bias_ref[h, 0:t, 0:p_len]
        s2 = lax.dot_general(q, kn, dn, preferred_element_type=F32) * SCALE + bias_ref[h, 0:t, p_len:p_len + t]
        m = jnp.maximum(jnp.max(s1, axis=-1, keepdims=True), jnp.max(s2, axis=-1, keepdims=True))
        p1 = jnp.exp(s1 - m)
        p2 = jnp.exp(s2 - m)
        l = jnp.sum(p1, axis=-1, keepdims=True) + jnp.sum(p2, axis=-1, keepdims=True)
        o = (jnp.dot(p1.astype(BF16), cv, preferred_element_type=F32)
             + jnp.dot(p2.astype(BF16), vn, preferred_element_type=F32)) / l
        o_ref[0, :, cs] = o.astype(o_ref.dtype)


def _band_attn_sample(z3, cache_k, cache_v, bias, n_heads):
    b, t, _ = z3.shape
    d_a = n_heads * HEAD_DIM
    p_len = cache_k.shape[1]
    assert p_len == WINDOW and t <= CHUNK
    return pl.pallas_call(
        functools.partial(_band_attn_sample_kernel, n_heads=n_heads, t=t, p_len=p_len),
        grid=(b,),
        in_specs=[
            pl.BlockSpec((1, t, d_a), lambda bi: (bi, 0, 0)),
            pl.BlockSpec((1, t, d_a), lambda bi: (bi, 0, 1)),
            pl.BlockSpec((1, t, d_a), lambda bi: (bi, 0, 2)),
            pl.BlockSpec((1, p_len, d_a), lambda bi: (bi, 0, 0)),
            pl.BlockSpec((1, p_len, d_a), lambda bi: (bi, 0, 0)),
            pl.BlockSpec((n_heads, QSUB, KSPAN), lambda bi: (0, 0, 0)),
        ],
        out_specs=pl.BlockSpec((1, t, d_a), lambda bi: (bi, 0, 0)),
        out_shape=jax.ShapeDtypeStruct((b, t, d_a), BF16),
        compiler_params=_params(("parallel",)),
        name="band_attn_sample",
    )(z3, z3, z3, cache_k, cache_v, bias)


def _mem_attn_kernel(q_ref, mk_ref, mv_ref, o_ref, *, n_heads):
    for h in range(n_heads):
        cs = slice(h * HEAD_DIM, (h + 1) * HEAD_DIM)
        q = q_ref[0, :, cs].astype(BF16)
        k = mk_ref[0, :, cs].astype(BF16)
        v = mv_ref[0, :, cs].astype(BF16)
        sc = lax.dot_general(q, k, (((1,), (1,)), ((), ())), preferred_element_type=F32) * SCALE
        m = jnp.max(sc, axis=-1, keepdims=True)
        p = jnp.exp(sc - m)
        l = jnp.sum(p, axis=-1, keepdims=True)
        o = jnp.dot(p.astype(BF16), v, preferred_element_type=F32) / l
        o_ref[0, :, cs] = o.astype(o_ref.dtype)


def _mem_attn(z3, mem_k, mem_v, n_heads, q_col_block, tm):
    b, s, _ = z3.shape
    d_m = n_heads * HEAD_DIM
    n_mem = mem_k.shape[1]
    return pl.pallas_call(
        functools.partial(_mem_attn_kernel, n_heads=n_heads),
        grid=(b, s // tm),
        in_specs=[
            pl.BlockSpec((1, tm, d_m), lambda bi, t: (bi, t, q_col_block)),
            pl.BlockSpec((1, n_mem, d_m), lambda bi, t: (bi, 0, 0)),
            pl.BlockSpec((1, n_mem, d_m), lambda bi, t: (bi, 0, 0)),
        ],
        out_specs=pl.BlockSpec((1, tm, d_m), lambda bi, t: (bi, t, 0)),
        out_shape=jax.ShapeDtypeStruct((b, s, d_m), BF16),
        compiler_params=_params(("parallel", "parallel")),
        name="mem_attn",
    )(z3, mem_k, mem_v)


def _rglru_kernel(xb_ref, gb_ref, past_ref, h0_ref, cw_ref, cb_ref, wa_ref, ba_ref, wi_ref, bi_ref, lam_ref,
                  o_ref, hl_ref, buf_ref, a_ref, u_ref, hc_ref, *, tt, width):
    t = pl.program_id(1)
    pad = SUBLANES

    @pl.when(t == 0)
    def _():
        buf_ref[0:pad, :] = past_ref[0]
        hc_ref[...] = h0_ref[0]

    buf_ref[pad:pad + tt, :] = xb_ref[0]
    xc = cb_ref[...] + cw_ref[width - 1:width, :] * xb_ref[0]
    for k in range(1, width):
        xc = xc + cw_ref[width - 1 - k:width - k, :] * buf_ref[pad - k:pad - k + tt, :]
    buf_ref[0:pad, :] = buf_ref[tt:tt + pad, :]

    xcb = xc.astype(BF16)
    r = jax.nn.sigmoid(jnp.dot(xcb, wa_ref[...], preferred_element_type=F32) + ba_ref[...])
    i = jax.nn.sigmoid(jnp.dot(xcb, wi_ref[...], preferred_element_type=F32) + bi_ref[...])
    log_a = -LRU_C * r * jax.nn.softplus(-lam_ref[...])
    a = jnp.exp(log_a)
    th = jnp.tanh(log_a)
    u = jnp.sqrt(-2.0 * th / (1.0 - th)) * (i * xc)

    rowm = lax.broadcasted_iota(jnp.int32, (tt, 1), 0) % SUBLANES
    sh = 1
    while sh < SUBLANES:
        keep = rowm >= sh
        a_sh = pltpu.roll(a, sh, 0)
        u_sh = pltpu.roll(u, sh, 0)
        u = jnp.where(keep, a * u_sh + u, u)
        a = jnp.where(keep, a * a_sh, a)
        sh *= 2
    a_ref[...] = a
    u_ref[...] = u

    def body(g, h):
        r0 = pl.multiple_of(g * SUBLANES, SUBLANES)
        hr = a_ref[pl.ds(r0, SUBLANES), :] * h + u_ref[pl.ds(r0, SUBLANES), :]
        u_ref[pl.ds(r0, SUBLANES), :] = hr
        return hr[SUBLANES - 1:SUBLANES, :]

    h_last = lax.fori_loop(0, tt // SUBLANES, body, hc_ref[...])
    hc_ref[...] = h_last
    hl_ref[0] = h_last
    o_ref[0] = (u_ref[...] * jax.nn.gelu(gb_ref[0])).astype(o_ref.dtype)


def _rglru(z3, past8, h0, cw, cb, wa_bd, ba, wi_bd, bi, lam, xb_col_block, tt):
    b, s, _ = z3.shape
    d = cw.shape[1]
    width = cw.shape[0]
    vec = pl.BlockSpec((1, d), lambda bi_, t: (0, 0))
    mat = pl.BlockSpec((d, d), lambda bi_, t: (0, 0))
    return pl.pallas_call(
        functools.partial(_rglru_kernel, tt=tt, width=width),
        grid=(b, s // tt),
        in_specs=[
            pl.BlockSpec((1, tt, d), lambda bi_, t: (bi_, t, xb_col_block)),
            pl.BlockSpec((1, tt, d), lambda bi_, t: (bi_, t, xb_col_block + 1)),
            pl.BlockSpec((1, SUBLANES, d), lambda bi_, t: (bi_, 0, 0)),
            pl.BlockSpec((1, 1, d), lambda bi_, t: (bi_, 0, 0)),
            pl.BlockSpec((width, d), lambda bi_, t: (0, 0)),
            vec, mat, vec, mat, vec, vec,
        ],
        out_specs=[
            pl.BlockSpec((1, tt, d), lambda bi_, t: (bi_, t, 0)),
            pl.BlockSpec((1, 1, d), lambda bi_, t: (bi_, 0, 0)),
        ],
        out_shape=[jax.ShapeDtypeStruct((b, s, d), BF16), jax.ShapeDtypeStruct((b, 1, d), F32)],
        scratch_shapes=[
            pltpu.VMEM((tt + 2 * SUBLANES, d), F32),
            pltpu.VMEM((tt, d), F32),
            pltpu.VMEM((tt, d), F32),
            pltpu.VMEM((1, d), F32),
        ],
        compiler_params=_params(("parallel", "arbitrary")),
        name="rglru",
    )(z3, z3, past8, h0.reshape(b, 1, d), cw, cb.reshape(1, d), wa_bd, ba.reshape(1, d),
      wi_bd, bi.reshape(1, d), lam.reshape(1, d))


def _out_proj_kernel(x_ref, oa_ref, om_ref, ob_ref, wa_ref, wm_ref, wb_ref, o_ref):
    acc = jnp.dot(oa_ref[...], wa_ref[...], preferred_element_type=F32)
    acc = acc + jnp.dot(om_ref[...], wm_ref[...], preferred_element_type=F32)
    acc = acc + jnp.dot(ob_ref[...], wb_ref[...], preferred_element_type=F32)
    o_ref[...] = x_ref[...] + acc


def _out_proj(x, oa, om, ob, wa, wm, wb, tm):
    m, d = x.shape
    row = lambda c: pl.BlockSpec((tm, c), lambda i: (i, 0))
    full = lambda a: pl.BlockSpec(a.shape, lambda i: (0, 0))
    return pl.pallas_call(
        _out_proj_kernel,
        grid=(m // tm,),
        in_specs=[row(d), row(oa.shape[1]), row(om.shape[1]), row(ob.shape[1]), full(wa), full(wm), full(wb)],
        out_specs=row(d),
        out_shape=jax.ShapeDtypeStruct((m, d), F32),
        compiler_params=_params(("parallel",)),
        name="out_proj",
    )(x, oa, om, ob, wa, wm, wb)


def _conv_ffn_kernel(x_ref, g_ref, hv_ref, hg_ref, wv_ref, wg_ref, cwv_ref, cwg_ref, cbv_ref, cbg_ref, wd_ref,
                     o_ref, fv_ref, fg_ref, xn_ref, bv_ref, bg_ref, carv_ref, carg_ref,
                     *, nseq, seq_rows, tiles_per_seq, width):
    i = pl.program_id(0)
    c = pl.program_id(1)
    pad = SUBLANES
    tf = wv_ref.shape[1]

    @pl.when(c == 0)
    def _():
        x = x_ref[...]
        xn_ref[...] = _rms(x, g_ref[...]).astype(BF16)
        o_ref[...] = x

    xn = xn_ref[...]

    def conv(w_ref, hist_ref, car_ref, buf_ref, cw_ref, cb_ref, f_ref):
        up = jnp.dot(xn, w_ref[...], preferred_element_type=F32).reshape(nseq, seq_rows, tf)
        buf_ref[:, pad:pad + seq_rows, :] = up
        if tiles_per_seq == 1:
            buf_ref[:, 0:pad, :] = hist_ref[...]
        else:
            first = i % tiles_per_seq == 0

            @pl.when(first)
            def _():
                buf_ref[:, 0:pad, :] = hist_ref[...]

            @pl.when(jnp.logical_not(first))
            def _():
                buf_ref[:, 0:pad, :] = car_ref[c]

        y = cb_ref[...] + cw_ref[width - 1:width, :] * up
        for k in range(1, width):
            y = y + cw_ref[width - 1 - k:width - k, :] * buf_ref[:, pad - k:pad - k + seq_rows, :]
        last = buf_ref[:, seq_rows:seq_rows + pad, :]
        f_ref[...] = last
        if tiles_per_seq > 1:
            car_ref[c] = last
        return y

    val = conv(wv_ref, hv_ref, carv_ref, bv_ref, cwv_ref, cbv_ref, fv_ref)
    gate = conv(wg_ref, hg_ref, carg_ref, bg_ref, cwg_ref, cbg_ref, fg_ref)
    h = (jax.nn.gelu(gate) * val).astype(BF16).reshape(nseq * seq_rows, tf)
    o_ref[...] += jnp.dot(h, wd_ref[...], preferred_element_type=F32)


def _conv_ffn(x, g, hist8, w_up, cw, cb, w_down, nseq, seq_rows, tiles_per_seq, tf):
    m, d = x.shape
    d_ff = w_down.shape[0]
    width = cw.shape[0]
    tm = nseq * seq_rows
    nt = m // tm
    nc = d_ff // tf
    seq_blk = lambda i: i // tiles_per_seq
    kern = functools.partial(_conv_ffn_kernel, nseq=nseq, seq_rows=seq_rows,
                             tiles_per_seq=tiles_per_seq, width=width)
    car_shape = (nc, nseq, SUBLANES, tf) if tiles_per_seq > 1 else (1, 1, SUBLANES, LANES)
    return pl.pallas_call(
        kern,
        grid=(nt, nc),
        in_specs=[
            pl.BlockSpec((tm, d), lambda i, c: (i, 0)),
            pl.BlockSpec((1, d), lambda i, c: (0, 0)),
            pl.BlockSpec((nseq, SUBLANES, tf), lambda i, c: (seq_blk(i), 0, c)),
            pl.BlockSpec((nseq, SUBLANES, tf), lambda i, c: (seq_blk(i), 0, nc + c)),
            pl.BlockSpec((d, tf), lambda i, c: (0, c)),
            pl.BlockSpec((d, tf), lambda i, c: (0, nc + c)),
            pl.BlockSpec((width, tf), lambda i, c: (0, c)),
            pl.BlockSpec((width, tf), lambda i, c: (0, nc + c)),
            pl.BlockSpec((1, tf), lambda i, c: (0, c)),
            pl.BlockSpec((1, tf), lambda i, c: (0, nc + c)),
            pl.BlockSpec((tf, d), lambda i, c: (c, 0)),
        ],
        out_specs=[
            pl.BlockSpec((tm, d), lambda i, c: (i, 0)),
            pl.BlockSpec((nseq, SUBLANES, tf), lambda i, c: (i, 0, c)),
            pl.BlockSpec((nseq, SUBLANES, tf), lambda i, c: (i, 0, c)),
        ],
        out_shape=[
            jax.ShapeDtypeStruct((m, d), F32),
            jax.ShapeDtypeStruct((nt * nseq, SUBLANES, d_ff), F32),
            jax.ShapeDtypeStruct((nt * nseq, SUBLANES, d_ff), F32),
        ],
        scratch_shapes=[
            pltpu.VMEM((tm, d), BF16),
            pltpu.VMEM((nseq, seq_rows + 2 * SUBLANES, tf), F32),
            pltpu.VMEM((nseq, seq_rows + 2 * SUBLANES, tf), F32),
            pltpu.VMEM(car_shape, F32),
            pltpu.VMEM(car_shape, F32),
        ],
        compiler_params=_params(("arbitrary", "arbitrary")),
        name="conv_ffn",
    )(x, g.reshape(1, d), hist8, hist8, w_up, w_up, cw, cw, cb.reshape(1, 2 * d_ff), cb.reshape(1, 2 * d_ff), w_down)


def _pad_rows_front(a, rows):
    return jnp.pad(a, ((0, 0), (rows - a.shape[1], 0), (0, 0)))


def _block_diag(w):
    n, bi, bo = w.shape
    eye = jnp.eye(n, dtype=w.dtype)
    return (w[:, :, None, :] * eye[:, None, :, None]).reshape(n * bi, n * bo)


def _trunk(x3, band_fn, mem_k, mem_v, b_conv_past, b_h0, f_conv_past, bias, p, *, tm, tq_mem, tt, ffn_tiles):
    b, t, d = x3.shape
    n_heads_a, n_heads_m, d_rnn = p["n_heads_a"], p["n_heads_m"], p["d_rnn"]
    d_a, d_m = n_heads_a * HEAD_DIM, n_heads_m * HEAD_DIM
    x2 = x3.reshape(b * t, d)
    tn = d_m
    z = _norm_matmul(x2, p["g_attn"], p["w_in"], p["in_gains"], p["in_norm_ranges"], tm, tn)
    d_in = z.shape[1]
    z3 = z.reshape(b, t, d_in)

    o_a = band_fn(z3, bias)
    o_m = _mem_attn(z3, mem_k.reshape(b, -1, d_m), mem_v.reshape(b, -1, d_m), n_heads_m, 3 * d_a // d_m, tq_mem)
    o_b, h_last = _rglru(z3, _pad_rows_front(b_conv_past, SUBLANES), b_h0, p["conv_b_w"], p["conv_b_b"],
                         p["wa_bd"], p["b_rg_a"], p["wi_bd"], p["b_rg_i"], p["lru_lambda"],
                         (3 * d_a + d_m) // d_rnn, tt)
    x1 = _out_proj(x2, o_a.reshape(b * t, d_a), o_m.reshape(b * t, d_m), o_b.reshape(b * t, d_rnn),
                   p["wo_a"], p["wo_m"], p["wo_b"], min(tm, 512))

    nseq, seq_rows, tiles_per_seq = ffn_tiles
    y, fv, fg = _conv_ffn(x1, p["g_ffn"], _pad_rows_front(f_conv_past, SUBLANES), p["w_up"], p["conv_f_w"],
                          p["conv_f_b"], p["w_down"], nseq, seq_rows, tiles_per_seq, p["tf"])
    wf = p["conv_f_w"].shape[0]
    f_new = jnp.concatenate([fv, fg], axis=-1).reshape(b, -1, SUBLANES, fv.shape[-1] * 2)[:, -1, SUBLANES - (wf - 1):]

    k_a = z3[:, :, d_a:2 * d_a].reshape(b, t, n_heads_a, HEAD_DIM)
    v_a = z3[:, :, 2 * d_a:3 * d_a].reshape(b, t, n_heads_a, HEAD_DIM)
    wb = p["conv_b_w"].shape[0]
    xb = z3[:, :, 3 * d_a + d_m:3 * d_a + d_m + d_rnn]
    b_conv_new = jnp.concatenate([b_conv_past, xb], axis=1)[:, -(wb - 1):] if t < wb - 1 else xb[:, t - (wb - 1):]
    return y.reshape(b, t, d), k_a, v_a, b_conv_new, h_last.reshape(b, d_rnn), f_new


def kernel(x_prompt, x_sample, cache_a_k, cache_a_v, cache_mem_k, cache_mem_v, cache_b_conv, state_b_h, cache_f_conv, mem_prompt, g_attn, w_in, g_q_a, g_k_a, rel_table, g_q_m, g_k_m, g_mem, w_mem_kv, conv_b_w, conv_b_b, w_rg_a, b_rg_a, w_rg_i, b_rg_i, lru_lambda, w_out, g_ffn, w_up, conv_f_w, conv_f_b, w_down):
    depth = w_in.shape[0]
    bp, sp, d = x_prompt.shape
    bs, ts, _ = x_sample.shape
    n_heads_a = cache_a_k.shape[3]
    n_heads_m = cache_mem_k.shape[3]
    n_mem = mem_prompt.shape[1]
    d_a, d_m = n_heads_a * HEAD_DIM, n_heads_m * HEAD_DIM
    d_rnn = conv_b_w.shape[2]
    d_ff = w_down.shape[1]
    keep = min(WINDOW, sp)
    ones = lambda n: jnp.ones((n,), F32)

    xp, xs = x_prompt, x_sample
    outs = [[] for _ in range(12)]
    for l in range(depth):
        tn = d_m
        p = dict(
            n_heads_a=n_heads_a, n_heads_m=n_heads_m, d_rnn=d_rnn, tf=512,
            g_attn=g_attn[l], w_in=w_in[l].astype(BF16),
            in_gains=jnp.concatenate([jnp.tile(g_q_a[l], n_heads_a), jnp.tile(g_k_a[l], n_heads_a), ones(d_a),
                                      jnp.tile(g_q_m[l], n_heads_m), ones(2 * d_rnn)]),
            in_norm_ranges=((0, 2 * d_a // tn), (3 * d_a // tn, (3 * d_a + d_m) // tn)),
            conv_b_w=conv_b_w[l], conv_b_b=conv_b_b[l],
            wa_bd=_block_diag(w_rg_a[l]).astype(BF16), b_rg_a=b_rg_a[l],
            wi_bd=_block_diag(w_rg_i[l]).astype(BF16), b_rg_i=b_rg_i[l], lru_lambda=lru_lambda[l],
            wo_a=w_out[l, :d_a].astype(BF16), wo_m=w_out[l, d_a:d_a + d_m].astype(BF16),
            wo_b=w_out[l, d_a + d_m:].astype(BF16),
            g_ffn=g_ffn[l], w_up=w_up[l].astype(BF16), conv_f_w=conv_f_w[l], conv_f_b=conv_f_b[l],
            w_down=w_down[l].astype(BF16),
        )
        bias = _rel_bias(rel_table[l])

        mem_gains = jnp.concatenate([jnp.tile(g_k_m[l], n_heads_m), ones(d_m)])
        kv = _norm_matmul(mem_prompt.reshape(bp * n_mem, d), g_mem[l], w_mem_kv[l].astype(BF16), mem_gains,
                          ((0, 1),), 512, d_m)
        mem_k = kv[:, :d_m].reshape(bp, n_mem, n_heads_m, HEAD_DIM)
        mem_v = kv[:, d_m:].reshape(bp, n_mem, n_heads_m, HEAD_DIM)

        ffn_tm = 512
        xp, k_a, v_a, bc, bh, fc = _trunk(
            xp, functools.partial(_band_attn, n_heads=n_heads_a, tq=WINDOW), mem_k, mem_v,
            jnp.zeros((bp, conv_b_w.shape[1] - 1, d_rnn), F32), jnp.zeros((bp, d_rnn), F32),
            jnp.zeros((bp, conv_f_w.shape[1] - 1, 2 * d_ff), F32), bias, p,
            tm=1024, tq_mem=512, tt=512, ffn_tiles=(1, ffn_tm, sp // ffn_tm))
        for lst, v in zip(outs[:7], (k_a[:, sp - keep:], v_a[:, sp - keep:], mem_k, mem_v, bc, bh, fc)):
            lst.append(v)

        band_s = functools.partial(
            _band_attn_sample, cache_k=cache_a_k[l].reshape(bs, -1, d_a), cache_v=cache_a_v[l].reshape(bs, -1, d_a),
            n_heads=n_heads_a)
        xs, k_a, v_a, bc, bh, fc = _trunk(
            xs, lambda z3, bias_: band_s(z3, bias=bias_), cache_mem_k[l], cache_mem_v[l],
            cache_b_conv[l], state_b_h[l], cache_f_conv[l], bias, p,
            tm=bs * ts, tq_mem=ts, tt=ts, ffn_tiles=(bs, ts, 1))
        for lst, v in zip(outs[7:], (k_a, v_a, bc, bh, fc)):
            lst.append(v)

    stacked = [jnp.stack(o) for o in outs]
    return (xp, xs, *stacked)
```

```python
import functools

import jax
import jax.numpy as jnp
from jax import lax
from jax.experimental import pallas as pl
from jax.experimental.pallas import tpu as pltpu

EPS = 1e-6
HEAD_DIM = 128
CHUNK = 64
LEFT_CHUNKS = 8
WINDOW = LEFT_CHUNKS * CHUNK
REL_CLIP = 256
LRU_C = 8.0
NEG = -1e30
SCALE = HEAD_DIM ** -0.5

LANES = 128
SUBLANES = 8
MXU_COLS = 256
VMEM_LIMIT = 56 * 2 ** 20

QSUB = 2 * CHUNK
KSPAN = WINDOW + QSUB

BF16 = jnp.bfloat16
F32 = jnp.float32


def _params(sem):
    return pltpu.CompilerParams(dimension_semantics=sem, vmem_limit_bytes=VMEM_LIMIT)


def _rms(x, g):
    ms = jnp.mean(x * x, axis=-1, keepdims=True)
    return x * lax.rsqrt(ms + EPS) * g


def _norm_matmul_kernel(x_ref, g_ref, w_ref, hg_ref, o_ref, xn_ref, *, norm_ranges, tn):
    j = pl.program_id(1)

    @pl.when(j == 0)
    def _():
        xn_ref[...] = _rms(x_ref[...], g_ref[...]).astype(BF16)

    z = jnp.dot(xn_ref[...], w_ref[...], preferred_element_type=F32)
    cond = None
    for lo, hi in norm_ranges:
        c = jnp.logical_and(j >= lo, j < hi)
        cond = c if cond is None else jnp.logical_or(cond, c)

    @pl.when(cond)
    def _():
        for h in range(tn // HEAD_DIM):
            cs = slice(h * HEAD_DIM, (h + 1) * HEAD_DIM)
            o_ref[:, cs] = _rms(z[:, cs], hg_ref[:, cs])

    @pl.when(jnp.logical_not(cond))
    def _():
        o_ref[...] = z


def _norm_matmul(x, g, w_bf16, head_gains, norm_ranges, tm, tn):
    m, k = x.shape
    n = w_bf16.shape[1]
    kern = functools.partial(_norm_matmul_kernel, norm_ranges=norm_ranges, tn=tn)
    return pl.pallas_call(
        kern,
        grid=(m // tm, n // tn),
        in_specs=[
            pl.BlockSpec((tm, k), lambda i, j: (i, 0)),
            pl.BlockSpec((1, k), lambda i, j: (0, 0)),
            pl.BlockSpec((k, tn), lambda i, j: (0, j)),
            pl.BlockSpec((1, tn), lambda i, j: (0, j)),
        ],
        out_specs=pl.BlockSpec((tm, tn), lambda i, j: (i, j)),
        out_shape=jax.ShapeDtypeStruct((m, n), F32),
        scratch_shapes=[pltpu.VMEM((tm, k), BF16)],
        compiler_params=_params(("parallel", "arbitrary")),
        name="norm_matmul",
    )(x, g.reshape(1, k), w_bf16, head_gains.reshape(1, n))


def _rel_bias_kernel(tab_ref, o_ref, *, n_heads, tpad):
    t = tab_ref[...]
    hi = t.astype(BF16)
    r1 = t - hi.astype(F32)
    mid = r1.astype(BF16)
    lo = (r1 - mid.astype(F32)).astype(BF16)
    tix = lax.broadcasted_iota(jnp.int32, (tpad, KSPAN), 0)
    j = lax.broadcasted_iota(jnp.int32, (1, KSPAN), 1)
    for i in range(QSUB):
        jj = j - CHUNK * (i // CHUNK)
        valid = jnp.logical_and(jj >= 0, jj < WINDOW + CHUNK)
        idx = jnp.clip((i % CHUNK) + WINDOW - jj, -REL_CLIP, REL_CLIP) + REL_CLIP
        onehot = jnp.where(tix == idx, 1.0, 0.0).astype(BF16)
        val = (jnp.dot(hi, onehot, preferred_element_type=F32)
               + jnp.dot(mid, onehot, preferred_element_type=F32)
               + jnp.dot(lo, onehot, preferred_element_type=F32))
        o_ref[i] = jnp.where(valid, val, NEG)


def _rel_bias(rel_table):
    n_heads, tlen = rel_table.shape
    tpad = -(-tlen // LANES) * LANES
    tab = jnp.pad(rel_table, ((0, 0), (0, tpad - tlen)))
    out = pl.pallas_call(
        functools.partial(_rel_bias_kernel, n_heads=n_heads, tpad=tpad),
        out_shape=jax.ShapeDtypeStruct((QSUB, n_heads, KSPAN), F32),
        compiler_params=_params(None),
        name="rel_bias",
    )(tab)
    return jnp.transpose(out, (1, 0, 2))


def _band_attn_kernel(q_ref, kp_ref, kc_ref, vp_ref, vc_ref, bias_ref, o_ref, kcat_ref, vcat_ref,
                      *, n_heads, tq):
    t = pl.program_id(1)
    kcat_ref[0:WINDOW, :] = kp_ref[0].astype(BF16)
    kcat_ref[WINDOW:WINDOW + tq, :] = kc_ref[0].astype(BF16)
    vcat_ref[0:WINDOW, :] = vp_ref[0].astype(BF16)
    vcat_ref[WINDOW:WINDOW + tq, :] = vc_ref[0].astype(BF16)
    kidx = lax.broadcasted_iota(jnp.int32, (QSUB, KSPAN), 1)
    first = t == 0

    for h in range(n_heads):
        cs = slice(h * HEAD_DIM, (h + 1) * HEAD_DIM)

        def body(s, carry, cs=cs, h=h):
            r0 = pl.multiple_of(s * QSUB, QSUB)
            q = q_ref[0, pl.ds(r0, QSUB), cs].astype(BF16)
            k = kcat_ref[pl.ds(r0, KSPAN), cs]
            v = vcat_ref[pl.ds(r0, KSPAN), cs]
            sc = lax.dot_general(q, k, (((1,), (1,)), ((), ())), preferred_element_type=F32)
            sc = sc * SCALE + bias_ref[h]
            sc = jnp.where(jnp.logical_and(first, kidx + r0 < WINDOW), NEG, sc)
            m = jnp.max(sc, axis=-1, keepdims=True)
            p = jnp.exp(sc - m)
            l = jnp.sum(p, axis=-1, keepdims=True)
            o = jnp.dot(p.astype(BF16), v, preferred_element_type=F32) / l
            o_ref[0, pl.ds(r0, QSUB), cs] = o.astype(o_ref.dtype)
            return carry

        lax.fori_loop(0, tq // QSUB, body, 0)


def _band_attn(z3, bias, n_heads, tq):
    b, s, _ = z3.shape
    d_a = n_heads * HEAD_DIM
    assert tq == WINDOW
    prev = lambda bi, t: (bi, jnp.maximum(t - 1, 0))
    return pl.pallas_call(
        functools.partial(_band_attn_kernel, n_heads=n_heads, tq=tq),
        grid=(b, s // tq),
        in_specs=[
            pl.BlockSpec((1, tq, d_a), lambda bi, t: (bi, t, 0)),
            pl.BlockSpec((1, WINDOW, d_a), lambda bi, t: prev(bi, t) + (1,)),
            pl.BlockSpec((1, tq, d_a), lambda bi, t: (bi, t, 1)),
            pl.BlockSpec((1, WINDOW, d_a), lambda bi, t: prev(bi, t) + (2,)),
            pl.BlockSpec((1, tq, d_a), lambda bi, t: (bi, t, 2)),
            pl.BlockSpec((n_heads, QSUB, KSPAN), lambda bi, t: (0, 0, 0)),
        ],
        out_specs=pl.BlockSpec((1, tq, d_a), lambda bi, t: (bi, t, 0)),
        out_shape=jax.ShapeDtypeStruct((b, s, d_a), BF16),
        scratch_shapes=[pltpu.VMEM((WINDOW + tq, d_a), BF16), pltpu.VMEM((WINDOW + tq, d_a), BF16)],
        compiler_params=_params(("parallel", "parallel")),
        name="band_attn",
    )(z3, z3, z3, z3, z3, bias)


def _band_attn_sample_kernel(q_ref, kn_ref, vn_ref, ck_ref, cv_ref, bias_ref, o_ref, *, n_heads, t, p_len):
    for h in range(n_heads):
        cs = slice(h * HEAD_DIM, (h + 1) * HEAD_DIM)
        q = q_ref[0, :, cs].astype(BF16)
        ck = ck_ref[0, :, cs].astype(BF16)
        cv = cv_ref[0, :, cs].astype(BF16)
        kn = kn_ref[0, :, cs].astype(BF16)
        vn = vn_ref[0, :, cs].astype(BF16)
        dn = (((1,), (1,)), ((), ()))
        s1 = lax.dot_general(q, ck, dn, preferred_element_type=F32) * SCALE + bias_ref[h, 0:t, 0:p_len]
        s2 = lax.dot_general(q, kn, dn, preferred_element_type=F32) * SCALE + bias_ref[h, 0:t, p_len:p_len + t]
        m = jnp.maximum(jnp.max(s1, axis=-1, keepdims=True), jnp.max(s2, axis=-1, keepdims=True))
        p1 = jnp.exp(s1 - m)
        p2 = jnp.exp(s2 - m)
        l = jnp.sum(p1, axis=-1, keepdims=True) + jnp.sum(p2, axis=-1, keepdims=True)
        o = (jnp.dot(p1.astype(BF16), cv, preferred_element_type=F32)
             + jnp.dot(p2.astype(BF16), vn, preferred_element_type=F32)) / l
        o_ref[0, :, cs] = o.astype(o_ref.dtype)


def _band_attn_sample(z3, cache_k, cache_v, bias, n_heads):
    b, t, _ = z3.shape
    d_a = n_heads * HEAD_DIM
    p_len = cache_k.shape[1]
    assert p_len == WINDOW and t <= CHUNK
    return pl.pallas_call(
        functools.partial(_band_attn_sample_kernel, n_heads=n_heads, t=t, p_len=p_len),
        grid=(b,),
        in_specs=[
            pl.BlockSpec((1, t, d_a), lambda bi: (bi, 0, 0)),
            pl.BlockSpec((1, t, d_a), lambda bi: (bi, 0, 1)),
            pl.BlockSpec((1, t, d_a), lambda bi: (bi, 0, 2)),
            pl.BlockSpec((1, p_len, d_a), lambda bi: (bi, 0, 0)),
            pl.BlockSpec((1, p_len, d_a), lambda bi: (bi, 0, 0)),
            pl.BlockSpec((n_heads, QSUB, KSPAN), lambda bi: (0, 0, 0)),
        ],
        out_specs=pl.BlockSpec((1, t, d_a), lambda bi: (bi, 0, 0)),
        out_shape=jax.ShapeDtypeStruct((b, t, d_a), BF16),
        compiler_params=_params(("parallel",)),
        name="band_attn_sample",
    )(z3, z3, z3, cache_k, cache_v, bias)


def _mem_attn_kernel(q_ref, mk_ref, mv_ref, o_ref, *, n_heads):
    for h in range(n_heads):
        cs = slice(h * HEAD_DIM, (h + 1) * HEAD_DIM)
        q = q_ref[0, :, cs].astype(BF16)
        k = mk_ref[0, :, cs].astype(BF16)
        v = mv_ref[0, :, cs].astype(BF16)
        sc = lax.dot_general(q, k, (((1,), (1,)), ((), ())), preferred_element_type=F32) * SCALE
        m = jnp.max(sc, axis=-1, keepdims=True)
        p = jnp.exp(sc - m)
        l = jnp.sum(p, axis=-1, keepdims=True)
        o = jnp.dot(p.astype(BF16), v, preferred_element_type=F32) / l
        o_ref[0, :, cs] = o.astype(o_ref.dtype)


def _mem_attn(z3, mem_k, mem_v, n_heads, q_col_block, tm):
    b, s, _ = z3.shape
    d_m = n_heads * HEAD_DIM
    n_mem = mem_k.shape[1]
    return pl.pallas_call(
        functools.partial(_mem_attn_kernel, n_heads=n_heads),
        grid=(b, s // tm),
        in_specs=[
            pl.BlockSpec((1, tm, d_m), lambda bi, t: (bi, t, q_col_block)),
            pl.BlockSpec((1, n_mem, d_m), lambda bi, t: (bi, 0, 0)),
            pl.BlockSpec((1, n_mem, d_m), lambda bi, t: (bi, 0, 0)),
        ],
        out_specs=pl.BlockSpec((1, tm, d_m), lambda bi, t: (bi, t, 0)),
        out_shape=jax.ShapeDtypeStruct((b, s, d_m), BF16),
        compiler_params=_params(("parallel", "parallel")),
        name="mem_attn",
    )(z3, mem_k, mem_v)


def _rglru_kernel(xb_ref, gb_ref, past_ref, h0_ref, cw_ref, cb_ref, wa_ref, ba_ref, wi_ref, bi_ref, lam_ref,
                  o_ref, hl_ref, buf_ref, a_ref, u_ref, hc_ref, *, tt, width):
    t = pl.program_id(1)
    pad = SUBLANES

    @pl.when(t == 0)
    def _():
        buf_ref[0:pad, :] = past_ref[0]
        hc_ref[...] = h0_ref[0]

    buf_ref[pad:pad + tt, :] = xb_ref[0]
    xc = cb_ref[...] + cw_ref[width - 1:width, :] * xb_ref[0]
    for k in range(1, width):
        xc = xc + cw_ref[width - 1 - k:width - k, :] * buf_ref[pad - k:pad - k + tt, :]
    buf_ref[0:pad, :] = buf_ref[tt:tt + pad, :]

    xcb = xc.astype(BF16)
    r = jax.nn.sigmoid(jnp.dot(xcb, wa_ref[...], preferred_element_type=F32) + ba_ref[...])
    i = jax.nn.sigmoid(jnp.dot(xcb, wi_ref[...], preferred_element_type=F32) + bi_ref[...])
    log_a = -LRU_C * r * jax.nn.softplus(-lam_ref[...])
    a = jnp.exp(log_a)
    th = jnp.tanh(log_a)
    u = jnp.sqrt(-2.0 * th / (1.0 - th)) * (i * xc)

    rowm = lax.broadcasted_iota(jnp.int32, (tt, 1), 0) % SUBLANES
    sh = 1
    while sh < SUBLANES:
        keep = rowm >= sh
        a_sh = pltpu.roll(a, sh, 0)
        u_sh = pltpu.roll(u, sh, 0)
        u = jnp.where(keep, a * u_sh + u, u)
        a = jnp.where(keep, a * a_sh, a)
        sh *= 2
    a_ref[...] = a
    u_ref[...] = u

    def body(g, h):
        r0 = pl.multiple_of(g * SUBLANES, SUBLANES)
        hr = a_ref[pl.ds(r0, SUBLANES), :] * h + u_ref[pl.ds(r0, SUBLANES), :]
        u_ref[pl.ds(r0, SUBLANES), :] = hr
        return hr[SUBLANES - 1:SUBLANES, :]

    h_last = lax.fori_loop(0, tt // SUBLANES, body, hc_ref[...])
    hc_ref[...] = h_last
    hl_ref[0] = h_last
    o_ref[0] = (u_ref[...] * jax.nn.gelu(gb_ref[0])).astype(o_ref.dtype)


def _rglru(z3, past8, h0, cw, cb, wa_bd, ba, wi_bd, bi, lam, xb_col_block, tt):
    b, s, _ = z3.shape
    d = cw.shape[1]
    width = cw.shape[0]
    vec = pl.BlockSpec((1, d), lambda bi_, t: (0, 0))
    mat = pl.BlockSpec((d, d), lambda bi_, t: (0, 0))
    return pl.pallas_call(
        functools.partial(_rglru_kernel, tt=tt, width=width),
        grid=(b, s // tt),
        in_specs=[
            pl.BlockSpec((1, tt, d), lambda bi_, t: (bi_, t, xb_col_block)),
            pl.BlockSpec((1, tt, d), lambda bi_, t: (bi_, t, xb_col_block + 1)),
            pl.BlockSpec((1, SUBLANES, d), lambda bi_, t: (bi_, 0, 0)),
            pl.BlockSpec((1, 1, d), lambda bi_, t: (bi_, 0, 0)),
            pl.BlockSpec((width, d), lambda bi_, t: (0, 0)),
            vec, mat, vec, mat, vec, vec,
        ],
        out_specs=[
            pl.BlockSpec((1, tt, d), lambda bi_, t: (bi_, t, 0)),
            pl.BlockSpec((1, 1, d), lambda bi_, t: (bi_, 0, 0)),
        ],
        out_shape=[jax.ShapeDtypeStruct((b, s, d), BF16), jax.ShapeDtypeStruct((b, 1, d), F32)],
        scratch_shapes=[
            pltpu.VMEM((tt + 2 * SUBLANES, d), F32),
            pltpu.VMEM((tt, d), F32),
            pltpu.VMEM((tt, d), F32),
            pltpu.VMEM((1, d), F32),
        ],
        compiler_params=_params(("parallel", "arbitrary")),
        name="rglru",
    )(z3, z3, past8, h0.reshape(b, 1, d), cw, cb.reshape(1, d), wa_bd, ba.reshape(1, d),
      wi_bd, bi.reshape(1, d), lam.reshape(1, d))


def _out_proj_kernel(x_ref, oa_ref, om_ref, ob_ref, wa_ref, wm_ref, wb_ref, o_ref):
    acc = jnp.dot(oa_ref[...], wa_ref[...], preferred_element_type=F32)
    acc = acc + jnp.dot(om_ref[...], wm_ref[...], preferred_element_type=F32)
    acc = acc + jnp.dot(ob_ref[...], wb_ref[...], preferred_element_type=F32)
    o_ref[...] = x_ref[...] + acc


def _out_proj(x, oa, om, ob, wa, wm, wb, tm):
    m, d = x.shape
    row = lambda c: pl.BlockSpec((tm, c), lambda i: (i, 0))
    full = lambda a: pl.BlockSpec(a.shape, lambda i: (0, 0))
    return pl.pallas_call(
        _out_proj_kernel,
        grid=(m // tm,),
        in_specs=[row(d), row(oa.shape[1]), row(om.shape[1]), row(ob.shape[1]), full(wa), full(wm), full(wb)],
        out_specs=row(d),
        out_shape=jax.ShapeDtypeStruct((m, d), F32),
        compiler_params=_params(("parallel",)),
        name="out_proj",
    )(x, oa, om, ob, wa, wm, wb)


def _conv_ffn_kernel(x_ref, g_ref, hv_ref, hg_ref, wv_ref, wg_ref, cwv_ref, cwg_ref, cbv_ref, cbg_ref, wd_ref,
                     o_ref, fv_ref, fg_ref, xn_ref, bv_ref, bg_ref, carv_ref, carg_ref,
                     *, nseq, seq_rows, tiles_per_seq, width):
    i = pl.program_id(0)
    c = pl.program_id(1)
    pad = SUBLANES
    tf = wv_ref.shape[1]
    sub = min(tf, MXU_COLS)

    @pl.when(c == 0)
    def _():
        x = x_ref[...]
        xn_ref[...] = _rms(x, g_ref[...]).astype(BF16)
        o_ref[...] = x

    for hist_ref, car_ref, buf_ref in ((hv_ref, carv_ref, bv_ref), (hg_ref, carg_ref, bg_ref)):
        if tiles_per_seq == 1:
            buf_ref[:, 0:pad, :] = hist_ref[...]
        else:
            first = i % tiles_per_seq == 0

            @pl.when(first)
            def _(hist_ref=hist_ref, buf_ref=buf_ref):
                buf_ref[:, 0:pad, :] = hist_ref[...]

            @pl.when(jnp.logical_not(first))
            def _(car_ref=car_ref, buf_ref=buf_ref):
                buf_ref[:, 0:pad, :] = car_ref[c]

    def conv(w_ref, buf_ref, cw_ref, cb_ref, cs):
        up = jnp.dot(xn_ref[...], w_ref[:, cs], preferred_element_type=F32).reshape(nseq, seq_rows, sub)
        buf_ref[:, pad:pad + seq_rows, cs] = up
        y = cb_ref[:, cs] + cw_ref[width - 1:width, cs] * up
        for k in range(1, width):
            y = y + cw_ref[width - 1 - k:width - k, cs] * buf_ref[:, pad - k:pad - k + seq_rows, cs]
        return y

    for s in range(tf // sub):
        cs = slice(s * sub, (s + 1) * sub)
        val = conv(wv_ref, bv_ref, cwv_ref, cbv_ref, cs)
        gate = conv(wg_ref, bg_ref, cwg_ref, cbg_ref, cs)
        h = (jax.nn.gelu(gate) * val).astype(BF16).reshape(nseq * seq_rows, sub)
        o_ref[...] += jnp.dot(h, wd_ref[cs, :], preferred_element_type=F32)

    for car_ref, buf_ref, f_ref in ((carv_ref, bv_ref, fv_ref), (carg_ref, bg_ref, fg_ref)):
        last = buf_ref[:, seq_rows:seq_rows + pad, :]
        f_ref[...] = last
        if tiles_per_seq > 1:
            car_ref[c] = last


def _conv_ffn(x, g, hist8, w_up, cw, cb, w_down, nseq, seq_rows, tiles_per_seq, tf):
    m, d = x.shape
    d_ff = w_down.shape[0]
    width = cw.shape[0]
    tm = nseq * seq_rows
    nt = m // tm
    nc = d_ff // tf
    seq_blk = lambda i: i // tiles_per_seq
    kern = functools.partial(_conv_ffn_kernel, nseq=nseq, seq_rows=seq_rows,
                             tiles_per_seq=tiles_per_seq, width=width)
    car_shape = (nc, nseq, SUBLANES, tf) if tiles_per_seq > 1 else (1, 1, SUBLANES, LANES)
    return pl.pallas_call(
        kern,
        grid=(nt, nc),
        in_specs=[
            pl.BlockSpec((tm, d), lambda i, c: (i, 0), pipeline_mode=pl.Buffered(1)),
            pl.BlockSpec((1, d), lambda i, c: (0, 0)),
            pl.BlockSpec((nseq, SUBLANES, tf), lambda i, c: (seq_blk(i), 0, c)),
            pl.BlockSpec((nseq, SUBLANES, tf), lambda i, c: (seq_blk(i), 0, nc + c)),
            pl.BlockSpec((d, tf), lambda i, c: (0, c)),
            pl.BlockSpec((d, tf), lambda i, c: (0, nc + c)),
            pl.BlockSpec((width, tf), lambda i, c: (0, c)),
            pl.BlockSpec((width, tf), lambda i, c: (0, nc + c)),
            pl.BlockSpec((1, tf), lambda i, c: (0, c)),
            pl.BlockSpec((1, tf), lambda i, c: (0, nc + c)),
            pl.BlockSpec((tf, d), lambda i, c: (c, 0)),
        ],
        out_specs=[
            pl.BlockSpec((tm, d), lambda i, c: (i, 0)),
            pl.BlockSpec((nseq, SUBLANES, tf), lambda i, c: (i, 0, c)),
            pl.BlockSpec((nseq, SUBLANES, tf), lambda i, c: (i, 0, c)),
        ],
        out_shape=[
            jax.ShapeDtypeStruct((m, d), F32),
            jax.ShapeDtypeStruct((nt * nseq, SUBLANES, d_ff), F32),
            jax.ShapeDtypeStruct((nt * nseq, SUBLANES, d_ff), F32),
        ],
        scratch_shapes=[
            pltpu.VMEM((tm, d), BF16),
            pltpu.VMEM((nseq, seq_rows + 2 * SUBLANES, tf), F32),
            pltpu.VMEM((nseq, seq_rows + 2 * SUBLANES, tf), F32),
            pltpu.VMEM(car_shape, F32),
            pltpu.VMEM(car_shape, F32),
        ],
        compiler_params=_params(("arbitrary", "arbitrary")),
        name="conv_ffn",
    )(x, g.reshape(1, d), hist8, hist8, w_up, w_up, cw, cw, cb.reshape(1, 2 * d_ff), cb.reshape(1, 2 * d_ff), w_down)


def _pad_rows_front(a, rows):
    return jnp.pad(a, ((0, 0), (rows - a.shape[1], 0), (0, 0)))


def _block_diag(w):
    n, bi, bo = w.shape
    eye = jnp.eye(n, dtype=w.dtype)
    return (w[:, :, None, :] * eye[:, None, :, None]).reshape(n * bi, n * bo)


def _trunk(x3, band_fn, mem_k, mem_v, b_conv_past, b_h0, f_conv_past, bias, p, *, tm, tq_mem, tt, ffn_tiles):
    b, t, d = x3.shape
    n_heads_a, n_heads_m, d_rnn = p["n_heads_a"], p["n_heads_m"], p["d_rnn"]
    d_a, d_m = n_heads_a * HEAD_DIM, n_heads_m * HEAD_DIM
    x2 = x3.reshape(b * t, d)
    tn = d_m
    z = _norm_matmul(x2, p["g_attn"], p["w_in"], p["in_gains"], p["in_norm_ranges"], tm, tn)
    d_in = z.shape[1]
    z3 = z.reshape(b, t, d_in)

    o_a = band_fn(z3, bias)
    o_m = _mem_attn(z3, mem_k.reshape(b, -1, d_m), mem_v.reshape(b, -1, d_m), n_heads_m, 3 * d_a // d_m, tq_mem)
    o_b, h_last = _rglru(z3, _pad_rows_front(b_conv_past, SUBLANES), b_h0, p["conv_b_w"], p["conv_b_b"],
                         p["wa_bd"], p["b_rg_a"], p["wi_bd"], p["b_rg_i"], p["lru_lambda"],
                         (3 * d_a + d_m) // d_rnn, tt)
    x1 = _out_proj(x2, o_a.reshape(b * t, d_a), o_m.reshape(b * t, d_m), o_b.reshape(b * t, d_rnn),
                   p["wo_a"], p["wo_m"], p["wo_b"], min(tm, 512))

    nseq, seq_rows, tiles_per_seq = ffn_tiles
    y, fv, fg = _conv_ffn(x1, p["g_ffn"], _pad_rows_front(f_conv_past, SUBLANES), p["w_up"], p["conv_f_w"],
                          p["conv_f_b"], p["w_down"], nseq, seq_rows, tiles_per_seq, p["tf"])
    wf = p["conv_f_w"].shape[0]
    f_new = jnp.concatenate([fv, fg], axis=-1).reshape(b, -1, SUBLANES, fv.shape[-1] * 2)[:, -1, SUBLANES - (wf - 1):]

    k_a = z3[:, :, d_a:2 * d_a].reshape(b, t, n_heads_a, HEAD_DIM)
    v_a = z3[:, :, 2 * d_a:3 * d_a].reshape(b, t, n_heads_a, HEAD_DIM)
    wb = p["conv_b_w"].shape[0]
    xb = z3[:, :, 3 * d_a + d_m:3 * d_a + d_m + d_rnn]
    b_conv_new = jnp.concatenate([b_conv_past, xb], axis=1)[:, -(wb - 1):] if t < wb - 1 else xb[:, t - (wb - 1):]
    return y.reshape(b, t, d), k_a, v_a, b_conv_new, h_last.reshape(b, d_rnn), f_new


def kernel(x_prompt, x_sample, cache_a_k, cache_a_v, cache_mem_k, cache_mem_v, cache_b_conv, state_b_h, cache_f_conv, mem_prompt, g_attn, w_in, g_q_a, g_k_a, rel_table, g_q_m, g_k_m, g_mem, w_mem_kv, conv_b_w, conv_b_b, w_rg_a, b_rg_a, w_rg_i, b_rg_i, lru_lambda, w_out, g_ffn, w_up, conv_f_w, conv_f_b, w_down):
    depth = w_in.shape[0]
    bp, sp, d = x_prompt.shape
    bs, ts, _ = x_sample.shape
    n_heads_a = cache_a_k.shape[3]
    n_heads_m = cache_mem_k.shape[3]
    n_mem = mem_prompt.shape[1]
    d_a, d_m = n_heads_a * HEAD_DIM, n_heads_m * HEAD_DIM
    d_rnn = conv_b_w.shape[2]
    d_ff = w_down.shape[1]
    keep = min(WINDOW, sp)
    ones = lambda n: jnp.ones((n,), F32)

    xp, xs = x_prompt, x_sample
    outs = [[] for _ in range(12)]
    for l in range(depth):
        tn = d_m
        p = dict(
            n_heads_a=n_heads_a, n_heads_m=n_heads_m, d_rnn=d_rnn, tf=512,
            g_attn=g_attn[l], w_in=w_in[l].astype(BF16),
            in_gains=jnp.concatenate([jnp.tile(g_q_a[l], n_heads_a), jnp.tile(g_k_a[l], n_heads_a), ones(d_a),
                                      jnp.tile(g_q_m[l], n_heads_m), ones(2 * d_rnn)]),
            in_norm_ranges=((0, 2 * d_a // tn), (3 * d_a // tn, (3 * d_a + d_m) // tn)),
            conv_b_w=conv_b_w[l], conv_b_b=conv_b_b[l],
            wa_bd=_block_diag(w_rg_a[l]).astype(BF16), b_rg_a=b_rg_a[l],
            wi_bd=_block_diag(w_rg_i[l]).astype(BF16), b_rg_i=b_rg_i[l], lru_lambda=lru_lambda[l],
            wo_a=w_out[l, :d_a].astype(BF16), wo_m=w_out[l, d_a:d_a + d_m].astype(BF16),
            wo_b=w_out[l, d_a + d_m:].astype(BF16),
            g_ffn=g_ffn[l], w_up=w_up[l].astype(BF16), conv_f_w=conv_f_w[l], conv_f_b=conv_f_b[l],
            w_down=w_down[l].astype(BF16),
        )
        bias = _rel_bias(rel_table[l])

        mem_gains = jnp.concatenate([jnp.tile(g_k_m[l], n_heads_m), ones(d_m)])
        kv = _norm_matmul(mem_prompt.reshape(bp * n_mem, d), g_mem[l], w_mem_kv[l].astype(BF16), mem_gains,
                          ((0, 1),), 512, d_m)
        mem_k = kv[:, :d_m].reshape(bp, n_mem, n_heads_m, HEAD_DIM)
        mem_v = kv[:, d_m:].reshape(bp, n_mem, n_heads_m, HEAD_DIM)

        ffn_tm = 1024
        xp, k_a, v_a, bc, bh, fc = _trunk(
            xp, functools.partial(_band_attn, n_heads=n_heads_a, tq=WINDOW), mem_k, mem_v,
            jnp.zeros((bp, conv_b_w.shape[1] - 1, d_rnn), F32), jnp.zeros((bp, d_rnn), F32),
            jnp.zeros((bp, conv_f_w.shape[1] - 1, 2 * d_ff), F32), bias, p,
            tm=1024, tq_mem=512, tt=512, ffn_tiles=(1, ffn_tm, sp // ffn_tm))
        for lst, v in zip(outs[:7], (k_a[:, sp - keep:], v_a[:, sp - keep:], mem_k, mem_v, bc, bh, fc)):
            lst.append(v)

        band_s = functools.partial(
            _band_attn_sample, cache_k=cache_a_k[l].reshape(bs, -1, d_a), cache_v=cache_a_v[l].reshape(bs, -1, d_a),
            n_heads=n_heads_a)
        xs, k_a, v_a, bc, bh, fc = _trunk(
            xs, lambda z3, bias_: band_s(z3, bias=bias_), cache_mem_k[l], cache_mem_v[l],
            cache_b_conv[l], state_b_h[l], cache_f_conv[l], bias, p,
            tm=bs * ts, tq_mem=ts, tt=ts, ffn_tiles=(bs, ts, 1))
        for lst, v in zip(outs[7:], (k_a, v_a, bc, bh, fc)):
            lst.append(v)

    stacked = [jnp.stack(o) for o in outs]
    return (xp, xs, *stacked)
```

```python
import functools

import jax
import jax.numpy as jnp
from jax import lax
from jax.experimental import pallas as pl
from jax.experimental.pallas import tpu as pltpu

EPS = 1e-6
HEAD_DIM = 128
CHUNK = 64
LEFT_CHUNKS = 8
WINDOW = LEFT_CHUNKS * CHUNK
REL_CLIP = 256
LRU_C = 8.0
NEG = -1e30
SCALE = HEAD_DIM ** -0.5

LANES = 128
SUBLANES = 8
MXU_COLS = 256
VMEM_LIMIT = 56 * 2 ** 20

QSUB = 2 * CHUNK
KSPAN = WINDOW + QSUB

BF16 = jnp.bfloat16
F32 = jnp.float32


def _params(sem):
    return pltpu.CompilerParams(dimension_semantics=sem, vmem_limit_bytes=VMEM_LIMIT)


def _rms(x, g):
    ms = jnp.mean(x * x, axis=-1, keepdims=True)
    return x * lax.rsqrt(ms + EPS) * g


def _norm_matmul_kernel(x_ref, g_ref, w_ref, hg_ref, o_ref, xn_ref, *, norm_ranges, tn):
    j = pl.program_id(1)

    @pl.when(j == 0)
    def _():
        xn_ref[...] = _rms(x_ref[...], g_ref[...]).astype(BF16)

    z = jnp.dot(xn_ref[...], w_ref[...], preferred_element_type=F32)
    cond = None
    for lo, hi in norm_ranges:
        c = jnp.logical_and(j >= lo, j < hi)
        cond = c if cond is None else jnp.logical_or(cond, c)

    @pl.when(cond)
    def _():
        for h in range(tn // HEAD_DIM):
            cs = slice(h * HEAD_DIM, (h + 1) * HEAD_DIM)
            o_ref[:, cs] = _rms(z[:, cs], hg_ref[:, cs])

    @pl.when(jnp.logical_not(cond))
    def _():
        o_ref[...] = z


def _norm_matmul(x, g, w_bf16, head_gains, norm_ranges, tm, tn):
    m, k = x.shape
    n = w_bf16.shape[1]
    kern = functools.partial(_norm_matmul_kernel, norm_ranges=norm_ranges, tn=tn)
    return pl.pallas_call(
        kern,
        grid=(m // tm, n // tn),
        in_specs=[
            pl.BlockSpec((tm, k), lambda i, j: (i, 0)),
            pl.BlockSpec((1, k), lambda i, j: (0, 0)),
            pl.BlockSpec((k, tn), lambda i, j: (0, j)),
            pl.BlockSpec((1, tn), lambda i, j: (0, j)),
        ],
        out_specs=pl.BlockSpec((tm, tn), lambda i, j: (i, j)),
        out_shape=jax.ShapeDtypeStruct((m, n), F32),
        scratch_shapes=[pltpu.VMEM((tm, k), BF16)],
        compiler_params=_params(("parallel", "arbitrary")),
        name="norm_matmul",
    )(x, g.reshape(1, k), w_bf16, head_gains.reshape(1, n))


def _in_proj_kernel(x_ref, g_ref, w_ref, gq_ref, gk_ref, gqm_ref, qkv_ref, qm_ref, xg_ref, ak_ref, av_ref,
                    xn_ref, kv_ref, *, d_a, d_m, d_rnn, group, tiles_per_seq):
    i = pl.program_id(0)
    xn_ref[...] = _rms(x_ref[...], g_ref[...]).astype(BF16)

    def cols(c0):
        return jnp.dot(xn_ref[...], w_ref[:, c0:c0 + group], preferred_element_type=F32)

    def heads(z, c0):
        return [(slice(c0 + h * HEAD_DIM, c0 + (h + 1) * HEAD_DIM), z[:, h * HEAD_DIM:(h + 1) * HEAD_DIM])
                for h in range(group // HEAD_DIM)]

    for c0 in range(0, d_a, group):
        for cs, zh in heads(cols(c0), c0):
            qkv_ref[:, cs] = _rms(zh, gq_ref[...]).astype(BF16)
    for c0 in range(d_a, 2 * d_a, group):
        for cs, zh in heads(cols(c0), c0):
            kh = _rms(zh, gk_ref[...])
            qkv_ref[:, cs] = kh.astype(BF16)
            kv_ref[:, slice(cs.start - d_a, cs.stop - d_a)] = kh
    for c0 in range(2 * d_a, 3 * d_a, group):
        for cs, zh in heads(cols(c0), c0):
            qkv_ref[:, cs] = zh.astype(BF16)
            kv_ref[:, slice(cs.start - d_a, cs.stop - d_a)] = zh
    for c0 in range(0, d_m, group):
        for cs, zh in heads(cols(3 * d_a + c0), c0):
            qm_ref[:, cs] = _rms(zh, gqm_ref[...]).astype(BF16)
    for c0 in range(0, 2 * d_rnn, group):
        xg_ref[:, c0:c0 + group] = cols(3 * d_a + d_m + c0)

    @pl.when(i % tiles_per_seq == tiles_per_seq - 1)
    def _():
        nseq, rows, n_heads = ak_ref.shape[0], ak_ref.shape[1], ak_ref.shape[2]
        for h in range(n_heads):
            cs = slice(h * HEAD_DIM, (h + 1) * HEAD_DIM)
            ak_ref[:, :, h, :] = kv_ref[:, cs].reshape(nseq, rows, HEAD_DIM)
            av_ref[:, :, h, :] = kv_ref[:, slice(d_a + cs.start, d_a + cs.stop)].reshape(nseq, rows, HEAD_DIM)


def _in_proj(x, g, w_bf16, gq, gk, gqm, n_heads_a, n_heads_m, d_rnn, nseq, rows, tiles_per_seq):
    m, k = x.shape
    d_in = w_bf16.shape[1]
    d_a, d_m = n_heads_a * HEAD_DIM, n_heads_m * HEAD_DIM
    tm = nseq * rows
    nt = m // tm
    n_cache_seq = nt * nseq // tiles_per_seq
    kern = functools.partial(_in_proj_kernel, d_a=d_a, d_m=d_m, d_rnn=d_rnn, group=4 * HEAD_DIM,
                             tiles_per_seq=tiles_per_seq)
    row = lambda c: pl.BlockSpec((tm, c), lambda i: (i, 0))
    vec = pl.BlockSpec((1, HEAD_DIM), lambda i: (0, 0))
    cache = pl.BlockSpec((nseq, rows, n_heads_a, HEAD_DIM), lambda i: (i // tiles_per_seq, 0, 0, 0))
    return pl.pallas_call(
        kern,
        grid=(nt,),
        in_specs=[
            row(k),
            pl.BlockSpec((1, k), lambda i: (0, 0)),
            pl.BlockSpec((k, d_in), lambda i: (0, 0), pipeline_mode=pl.Buffered(1)),
            vec, vec, vec,
        ],
        out_specs=[row(3 * d_a), row(d_m), row(2 * d_rnn), cache, cache],
        out_shape=[
            jax.ShapeDtypeStruct((m, 3 * d_a), BF16),
            jax.ShapeDtypeStruct((m, d_m), BF16),
            jax.ShapeDtypeStruct((m, 2 * d_rnn), F32),
            jax.ShapeDtypeStruct((n_cache_seq, rows, n_heads_a, HEAD_DIM), F32),
            jax.ShapeDtypeStruct((n_cache_seq, rows, n_heads_a, HEAD_DIM), F32),
        ],
        scratch_shapes=[pltpu.VMEM((tm, k), BF16), pltpu.VMEM((tm, 2 * d_a), F32)],
        compiler_params=_params(("arbitrary",)),
        name="in_proj",
    )(x, g.reshape(1, k), w_bf16, gq.reshape(1, HEAD_DIM), gk.reshape(1, HEAD_DIM), gqm.reshape(1, HEAD_DIM))


def _rel_bias_kernel(tab_ref, o_ref, *, n_heads, tpad):
    t = tab_ref[...]
    hi = t.astype(BF16)
    r1 = t - hi.astype(F32)
    mid = r1.astype(BF16)
    lo = (r1 - mid.astype(F32)).astype(BF16)
    tix = lax.broadcasted_iota(jnp.int32, (tpad, KSPAN), 0)
    j = lax.broadcasted_iota(jnp.int32, (1, KSPAN), 1)
    for i in range(QSUB):
        jj = j - CHUNK * (i // CHUNK)
        valid = jnp.logical_and(jj >= 0, jj < WINDOW + CHUNK)
        idx = jnp.clip((i % CHUNK) + WINDOW - jj, -REL_CLIP, REL_CLIP) + REL_CLIP
        onehot = jnp.where(tix == idx, 1.0, 0.0).astype(BF16)
        val = (jnp.dot(hi, onehot, preferred_element_type=F32)
               + jnp.dot(mid, onehot, preferred_element_type=F32)
               + jnp.dot(lo, onehot, preferred_element_type=F32))
        o_ref[i] = jnp.where(valid, val, NEG)


def _rel_bias(rel_table):
    n_heads, tlen = rel_table.shape
    tpad = -(-tlen // LANES) * LANES
    tab = jnp.pad(rel_table, ((0, 0), (0, tpad - tlen)))
    out = pl.pallas_call(
        functools.partial(_rel_bias_kernel, n_heads=n_heads, tpad=tpad),
        out_shape=jax.ShapeDtypeStruct((QSUB, n_heads, KSPAN), F32),
        compiler_params=_params(None),
        name="rel_bias",
    )(tab)
    return jnp.transpose(out, (1, 0, 2))


def _band_attn_kernel(q_ref, kp_ref, kc_ref, vp_ref, vc_ref, bias_ref, o_ref, kcat_ref, vcat_ref,
                      *, n_heads, tq):
    t = pl.program_id(1)
    kcat_ref[0:WINDOW, :] = kp_ref[0]
    kcat_ref[WINDOW:WINDOW + tq, :] = kc_ref[0]
    vcat_ref[0:WINDOW, :] = vp_ref[0]
    vcat_ref[WINDOW:WINDOW + tq, :] = vc_ref[0]

    def run(first_tile):
        def body(s, carry):
            r0 = pl.multiple_of(s * QSUB, QSUB)
            if first_tile:
                kidx = lax.broadcasted_iota(jnp.int32, (1, KSPAN), 1)
                before_start = kidx + r0 < WINDOW
            for h in range(n_heads):
                cs = slice(h * HEAD_DIM, (h + 1) * HEAD_DIM)
                q = q_ref[0, pl.ds(r0, QSUB), cs]
                k = kcat_ref[pl.ds(r0, KSPAN), cs]
                v = vcat_ref[pl.ds(r0, KSPAN), cs]
                sc = lax.dot_general(q, k, (((1,), (1,)), ((), ())), preferred_element_type=F32) + bias_ref[h]
                if first_tile:
                    sc = jnp.where(before_start, NEG, sc)
                m = jnp.max(sc, axis=-1, keepdims=True)
                p = jnp.exp(sc - m)
                l = jnp.sum(p, axis=-1, keepdims=True)
                o = jnp.dot(p.astype(BF16), v, preferred_element_type=F32) / l
                o_ref[0, pl.ds(r0, QSUB), cs] = o.astype(o_ref.dtype)
            return carry

        lax.fori_loop(0, tq // QSUB, body, 0)

    @pl.when(t == 0)
    def _():
        run(True)

    @pl.when(t > 0)
    def _():
        run(False)


def _band_attn(qkv3, bias, n_heads, tq):
    b, s, _ = qkv3.shape
    d_a = n_heads * HEAD_DIM
    assert tq == WINDOW
    prev = lambda bi, t: (bi, jnp.maximum(t - 1, 0))
    return pl.pallas_call(
        functools.partial(_band_attn_kernel, n_heads=n_heads, tq=tq),
        grid=(b, s // tq),
        in_specs=[
            pl.BlockSpec((1, tq, d_a), lambda bi, t: (bi, t, 0)),
            pl.BlockSpec((1, WINDOW, d_a), lambda bi, t: prev(bi, t) + (1,)),
            pl.BlockSpec((1, tq, d_a), lambda bi, t: (bi, t, 1)),
            pl.BlockSpec((1, WINDOW, d_a), lambda bi, t: prev(bi, t) + (2,)),
            pl.BlockSpec((1, tq, d_a), lambda bi, t: (bi, t, 2)),
            pl.BlockSpec((n_heads, QSUB, KSPAN), lambda bi, t: (0, 0, 0)),
        ],
        out_specs=pl.BlockSpec((1, tq, d_a), lambda bi, t: (bi, t, 0)),
        out_shape=jax.ShapeDtypeStruct((b, s, d_a), BF16),
        scratch_shapes=[pltpu.VMEM((WINDOW + tq, d_a), BF16), pltpu.VMEM((WINDOW + tq, d_a), BF16)],
        compiler_params=_params(("parallel", "parallel")),
        name="band_attn",
    )(qkv3, qkv3, qkv3, qkv3, qkv3, bias)


def _band_attn_sample_kernel(q_ref, kn_ref, vn_ref, ck_ref, cv_ref, bias_ref, o_ref, *, n_heads, t, p_len):
    for h in range(n_heads):
        cs = slice(h * HEAD_DIM, (h + 1) * HEAD_DIM)
        q = q_ref[0, :, cs]
        ck = ck_ref[0, :, h, :].astype(BF16)
        cv = cv_ref[0, :, h, :].astype(BF16)
        kn = kn_ref[0, :, cs]
        vn = vn_ref[0, :, cs]
        dn = (((1,), (1,)), ((), ()))
        s1 = lax.dot_general(q, ck, dn, preferred_element_type=F32) + bias_ref[h, 0:t, 0:p_len]
        s2 = lax.dot_general(q, kn, dn, preferred_element_type=F32) + bias_ref[h, 0:t, p_len:p_len + t]
        m = jnp.maximum(jnp.max(s1, axis=-1, keepdims=True), jnp.max(s2, axis=-1, keepdims=True))
        p1 = jnp.exp(s1 - m)
        p2 = jnp.exp(s2 - m)
        l = jnp.sum(p1, axis=-1, keepdims=True) + jnp.sum(p2, axis=-1, keepdims=True)
        o = (jnp.dot(p1.astype(BF16), cv, preferred_element_type=F32)
             + jnp.dot(p2.astype(BF16), vn, preferred_element_type=F32)) / l
        o_ref[0, :, cs] = o.astype(o_ref.dtype)


def _band_attn_sample(qkv3, cache_k, cache_v, bias, n_heads):
    b, t, _ = qkv3.shape
    d_a = n_heads * HEAD_DIM
    p_len = cache_k.shape[1]
    assert p_len == WINDOW and t <= CHUNK
    cache = pl.BlockSpec((1, p_len, n_heads, HEAD_DIM), lambda bi: (bi, 0, 0, 0))
    return pl.pallas_call(
        functools.partial(_band_attn_sample_kernel, n_heads=n_heads, t=t, p_len=p_len),
        grid=(b,),
        in_specs=[
            pl.BlockSpec((1, t, d_a), lambda bi: (bi, 0, 0)),
            pl.BlockSpec((1, t, d_a), lambda bi: (bi, 0, 1)),
            pl.BlockSpec((1, t, d_a), lambda bi: (bi, 0, 2)),
            cache, cache,
            pl.BlockSpec((n_heads, QSUB, KSPAN), lambda bi: (0, 0, 0)),
        ],
        out_specs=pl.BlockSpec((1, t, d_a), lambda bi: (bi, 0, 0)),
        out_shape=jax.ShapeDtypeStruct((b, t, d_a), BF16),
        compiler_params=_params(("parallel",)),
        name="band_attn_sample",
    )(qkv3, qkv3, qkv3, cache_k, cache_v, bias)


def _mem_attn_kernel(q_ref, mk_ref, mv_ref, o_ref, *, n_heads, head_axis):
    for h in range(n_heads):
        cs = slice(h * HEAD_DIM, (h + 1) * HEAD_DIM)
        q = q_ref[0, :, cs]
        if head_axis:
            k = mk_ref[0, :, h, :].astype(BF16)
            v = mv_ref[0, :, h, :].astype(BF16)
        else:
            k = mk_ref[0, :, cs].astype(BF16)
            v = mv_ref[0, :, cs].astype(BF16)
        sc = lax.dot_general(q, k, (((1,), (1,)), ((), ())), preferred_element_type=F32)
        m = jnp.max(sc, axis=-1, keepdims=True)
        p = jnp.exp(sc - m)
        l = jnp.sum(p, axis=-1, keepdims=True)
        o = jnp.dot(p.astype(BF16), v, preferred_element_type=F32) / l
        o_ref[0, :, cs] = o.astype(o_ref.dtype)


def _mem_attn(qm3, mem_k, mem_v, n_heads, tm):
    b, s, d_m = qm3.shape
    n_mem = mem_k.shape[1]
    head_axis = mem_k.ndim == 4
    if head_axis:
        kspec = vspec = pl.BlockSpec((1, n_mem, n_heads, HEAD_DIM), lambda bi, t: (bi, 0, 0, 0))
    else:
        kspec = pl.BlockSpec((1, n_mem, d_m), lambda bi, t: (bi, 0, 0))
        vspec = pl.BlockSpec((1, n_mem, d_m), lambda bi, t: (bi, 0, 1))
    return pl.pallas_call(
        functools.partial(_mem_attn_kernel, n_heads=n_heads, head_axis=head_axis),
        grid=(b, s // tm),
        in_specs=[pl.BlockSpec((1, tm, d_m), lambda bi, t: (bi, t, 0)), kspec, vspec],
        out_specs=pl.BlockSpec((1, tm, d_m), lambda bi, t: (bi, t, 0)),
        out_shape=jax.ShapeDtypeStruct((b, s, d_m), BF16),
        compiler_params=_params(("parallel", "parallel")),
        name="mem_attn",
    )(qm3, mem_k, mem_v)


def _rglru_kernel(xb_ref, gb_ref, past_ref, h0_ref, cw_ref, cb_ref, wa_ref, ba_ref, wi_ref, bi_ref, lam_ref,
                  o_ref, hl_ref, buf_ref, a_ref, u_ref, hc_ref, *, tt, width):
    t = pl.program_id(1)
    pad = SUBLANES

    @pl.when(t == 0)
    def _():
        buf_ref[0:pad, :] = past_ref[0]
        hc_ref[...] = h0_ref[0]

    buf_ref[pad:pad + tt, :] = xb_ref[0]
    xc = cb_ref[...] + cw_ref[width - 1:width, :] * xb_ref[0]
    for k in range(1, width):
        xc = xc + cw_ref[width - 1 - k:width - k, :] * buf_ref[pad - k:pad - k + tt, :]
    buf_ref[0:pad, :] = buf_ref[tt:tt + pad, :]

    xcb = xc.astype(BF16)
    r = jax.nn.sigmoid(jnp.dot(xcb, wa_ref[...], preferred_element_type=F32) + ba_ref[...])
    i = jax.nn.sigmoid(jnp.dot(xcb, wi_ref[...], preferred_element_type=F32) + bi_ref[...])
    log_a = -LRU_C * r * jax.nn.softplus(-lam_ref[...])
    a = jnp.exp(log_a)
    th = jnp.tanh(log_a)
    u = jnp.sqrt(-2.0 * th / (1.0 - th)) * (i * xc)

    rowm = lax.broadcasted_iota(jnp.int32, (tt, 1), 0) % SUBLANES
    sh = 1
    while sh < SUBLANES:
        keep = rowm >= sh
        a_sh = pltpu.roll(a, sh, 0)
        u_sh = pltpu.roll(u, sh, 0)
        u = jnp.where(keep, a * u_sh + u, u)
        a = jnp.where(keep, a * a_sh, a)
        sh *= 2
    a_ref[...] = a
    u_ref[...] = u

    def body(g, h):
        r0 = pl.multiple_of(g * SUBLANES, SUBLANES)
        hr = a_ref[pl.ds(r0, SUBLANES), :] * h + u_ref[pl.ds(r0, SUBLANES), :]
        u_ref[pl.ds(r0, SUBLANES), :] = hr
        return hr[SUBLANES - 1:SUBLANES, :]

    h_last = lax.fori_loop(0, tt // SUBLANES, body, hc_ref[...])
    hc_ref[...] = h_last
    hl_ref[0] = h_last
    o_ref[0] = (u_ref[...] * jax.nn.gelu(gb_ref[0])).astype(o_ref.dtype)


def _rglru(xg3, past8, h0, cw, cb, wa_bd, ba, wi_bd, bi, lam, tt):
    b, s, _ = xg3.shape
    d = cw.shape[1]
    width = cw.shape[0]
    vec = pl.BlockSpec((1, d), lambda bi_, t: (0, 0))
    mat = pl.BlockSpec((d, d), lambda bi_, t: (0, 0))
    return pl.pallas_call(
        functools.partial(_rglru_kernel, tt=tt, width=width),
        grid=(b, s // tt),
        in_specs=[
            pl.BlockSpec((1, tt, d), lambda bi_, t: (bi_, t, 0)),
            pl.BlockSpec((1, tt, d), lambda bi_, t: (bi_, t, 1)),
            pl.BlockSpec((1, SUBLANES, d), lambda bi_, t: (bi_, 0, 0)),
            pl.BlockSpec((1, 1, d), lambda bi_, t: (bi_, 0, 0)),
            pl.BlockSpec((width, d), lambda bi_, t: (0, 0)),
            vec, mat, vec, mat, vec, vec,
        ],
        out_specs=[
            pl.BlockSpec((1, tt, d), lambda bi_, t: (bi_, t, 0)),
            pl.BlockSpec((1, 1, d), lambda bi_, t: (bi_, 0, 0)),
        ],
        out_shape=[jax.ShapeDtypeStruct((b, s, d), BF16), jax.ShapeDtypeStruct((b, 1, d), F32)],
        scratch_shapes=[
            pltpu.VMEM((tt + 2 * SUBLANES, d), F32),
            pltpu.VMEM((tt, d), F32),
            pltpu.VMEM((tt, d), F32),
            pltpu.VMEM((1, d), F32),
        ],
        compiler_params=_params(("parallel", "arbitrary")),
        name="rglru",
    )(xg3, xg3, past8, h0.reshape(b, 1, d), cw, cb.reshape(1, d), wa_bd, ba.reshape(1, d),
      wi_bd, bi.reshape(1, d), lam.reshape(1, d))


def _out_proj_kernel(x_ref, oa_ref, om_ref, ob_ref, wa_ref, wm_ref, wb_ref, o_ref):
    acc = jnp.dot(oa_ref[...], wa_ref[...], preferred_element_type=F32)
    acc = acc + jnp.dot(om_ref[...], wm_ref[...], preferred_element_type=F32)
    acc = acc + jnp.dot(ob_ref[...], wb_ref[...], preferred_element_type=F32)
    o_ref[...] = x_ref[...] + acc


def _out_proj(x, oa, om, ob, wa, wm, wb, tm):
    m, d = x.shape
    row = lambda c: pl.BlockSpec((tm, c), lambda i: (i, 0))
    full = lambda a: pl.BlockSpec(a.shape, lambda i: (0, 0))
    return pl.pallas_call(
        _out_proj_kernel,
        grid=(m // tm,),
        in_specs=[row(d), row(oa.shape[1]), row(om.shape[1]), row(ob.shape[1]), full(wa), full(wm), full(wb)],
        out_specs=row(d),
        out_shape=jax.ShapeDtypeStruct((m, d), F32),
        compiler_params=_params(("parallel",)),
        name="out_proj",
    )(x, oa, om, ob, wa, wm, wb)


def _conv_ffn_kernel(x_ref, g_ref, hv_ref, hg_ref, wv_ref, wg_ref, cwv_ref, cwg_ref, cbv_ref, cbg_ref, wd_ref,
                     o_ref, fv_ref, fg_ref, xn_ref, bv_ref, bg_ref, carv_ref, carg_ref,
                     *, nseq, seq_rows, tiles_per_seq, width):
    i = pl.program_id(0)
    c = pl.program_id(1)
    pad = SUBLANES
    tf = wv_ref.shape[1]
    sub = min(tf, MXU_COLS)

    @pl.when(c == 0)
    def _():
        x = x_ref[...]
        xn_ref[...] = _rms(x, g_ref[...]).astype(BF16)
        o_ref[...] = x

    for hist_ref, car_ref, buf_ref in ((hv_ref, carv_ref, bv_ref), (hg_ref, carg_ref, bg_ref)):
        if tiles_per_seq == 1:
            buf_ref[:, 0:pad, :] = hist_ref[...]
        else:
            first = i % tiles_per_seq == 0

            @pl.when(first)
            def _(hist_ref=hist_ref, buf_ref=buf_ref):
                buf_ref[:, 0:pad, :] = hist_ref[...]

            @pl.when(jnp.logical_not(first))
            def _(car_ref=car_ref, buf_ref=buf_ref):
                buf_ref[:, 0:pad, :] = car_ref[c]

    def conv(w_ref, buf_ref, cw_ref, cb_ref, cs):
        up = jnp.dot(xn_ref[...], w_ref[:, cs], preferred_element_type=F32).reshape(nseq, seq_rows, sub)
        buf_ref[:, pad:pad + seq_rows, cs] = up
        y = cb_ref[:, cs] + cw_ref[width - 1:width, cs] * up
        for k in range(1, width):
            y = y + cw_ref[width - 1 - k:width - k, cs] * buf_ref[:, pad - k:pad - k + seq_rows, cs]
        return y

    for s in range(tf // sub):
        cs = slice(s * sub, (s + 1) * sub)
        val = conv(wv_ref, bv_ref, cwv_ref, cbv_ref, cs)
        gate = conv(wg_ref, bg_ref, cwg_ref, cbg_ref, cs)
        h = (jax.nn.gelu(gate) * val).astype(BF16).reshape(nseq * seq_rows, sub)
        o_ref[...] += jnp.dot(h, wd_ref[cs, :], preferred_element_type=F32)

    for car_ref, buf_ref, f_ref in ((carv_ref, bv_ref, fv_ref), (carg_ref, bg_ref, fg_ref)):
        last = buf_ref[:, seq_rows:seq_rows + pad, :]
        f_ref[...] = last
        if tiles_per_seq > 1:
            car_ref[c] = last


def _conv_ffn(x, g, hist8, w_up, cw, cb, w_down, nseq, seq_rows, tiles_per_seq, tf):
    m, d = x.shape
    d_ff = w_down.shape[0]
    width = cw.shape[0]
    tm = nseq * seq_rows
    nt = m // tm
    nc = d_ff // tf
    seq_blk = lambda i: i // tiles_per_seq
    kern = functools.partial(_conv_ffn_kernel, nseq=nseq, seq_rows=seq_rows,
                             tiles_per_seq=tiles_per_seq, width=width)
    car_shape = (nc, nseq, SUBLANES, tf) if tiles_per_seq > 1 else (1, 1, SUBLANES, LANES)
    return pl.pallas_call(
        kern,
        grid=(nt, nc),
        in_specs=[
            pl.BlockSpec((tm, d), lambda i, c: (i, 0), pipeline_mode=pl.Buffered(1)),
            pl.BlockSpec((1, d), lambda i, c: (0, 0)),
            pl.BlockSpec((nseq, SUBLANES, tf), lambda i, c: (seq_blk(i), 0, c)),
            pl.BlockSpec((nseq, SUBLANES, tf), lambda i, c: (seq_blk(i), 0, nc + c)),
            pl.BlockSpec((d, tf), lambda i, c: (0, c)),
            pl.BlockSpec((d, tf), lambda i, c: (0, nc + c)),
            pl.BlockSpec((width, tf), lambda i, c: (0, c)),
            pl.BlockSpec((width, tf), lambda i, c: (0, nc + c)),
            pl.BlockSpec((1, tf), lambda i, c: (0, c)),
            pl.BlockSpec((1, tf), lambda i, c: (0, nc + c)),
            pl.BlockSpec((tf, d), lambda i, c: (c, 0)),
        ],
        out_specs=[
            pl.BlockSpec((tm, d), lambda i, c: (i, 0)),
            pl.BlockSpec((nseq, SUBLANES, tf), lambda i, c: (i, 0, c)),
            pl.BlockSpec((nseq, SUBLANES, tf), lambda i, c: (i, 0, c)),
        ],
        out_shape=[
            jax.ShapeDtypeStruct((m, d), F32),
            jax.ShapeDtypeStruct((nt * nseq, SUBLANES, d_ff), F32),
            jax.ShapeDtypeStruct((nt * nseq, SUBLANES, d_ff), F32),
        ],
        scratch_shapes=[
            pltpu.VMEM((tm, d), BF16),
            pltpu.VMEM((nseq, seq_rows + 2 * SUBLANES, tf), F32),
            pltpu.VMEM((nseq, seq_rows + 2 * SUBLANES, tf), F32),
            pltpu.VMEM(car_shape, F32),
            pltpu.VMEM(car_shape, F32),
        ],
        compiler_params=_params(("arbitrary", "arbitrary")),
        name="conv_ffn",
    )(x, g.reshape(1, d), hist8, hist8, w_up, w_up, cw, cw, cb.reshape(1, 2 * d_ff), cb.reshape(1, 2 * d_ff), w_down)


def _pad_rows_front(a, rows):
    return jnp.pad(a, ((0, 0), (rows - a.shape[1], 0), (0, 0)))


def _block_diag(w):
    n, bi, bo = w.shape
    eye = jnp.eye(n, dtype=w.dtype)
    return (w[:, :, None, :] * eye[:, None, :, None]).reshape(n * bi, n * bo)


def _trunk(x3, band_fn, mem_k, mem_v, b_conv_past, b_h0, f_conv_past, bias, p, *, in_tiles, tq_mem, tt, ffn_tiles):
    b, t, d = x3.shape
    n_heads_a, n_heads_m, d_rnn = p["n_heads_a"], p["n_heads_m"], p["d_rnn"]
    d_a, d_m = n_heads_a * HEAD_DIM, n_heads_m * HEAD_DIM
    x2 = x3.reshape(b * t, d)
    nseq_in, rows_in, tiles_in = in_tiles
    qkv, qm, xg, k_a, v_a = _in_proj(x2, p["g_attn"], p["w_in"], p["g_q_a"], p["g_k_a"], p["g_q_m"],
                                     n_heads_a, n_heads_m, d_rnn, nseq_in, rows_in, tiles_in)
    xg3 = xg.reshape(b, t, 2 * d_rnn)

    o_a = band_fn(qkv.reshape(b, t, 3 * d_a), bias)
    o_m = _mem_attn(qm.reshape(b, t, d_m), mem_k, mem_v, n_heads_m, tq_mem)
    o_b, h_last = _rglru(xg3, _pad_rows_front(b_conv_past, SUBLANES), b_h0, p["conv_b_w"], p["conv_b_b"],
                         p["wa_bd"], p["b_rg_a"], p["wi_bd"], p["b_rg_i"], p["lru_lambda"], tt)
    x1 = _out_proj(x2, o_a.reshape(b * t, d_a), o_m.reshape(b * t, d_m), o_b.reshape(b * t, d_rnn),
                   p["wo_a"], p["wo_m"], p["wo_b"], 512)

    nseq, seq_rows, tiles_per_seq = ffn_tiles
    y, fv, fg = _conv_ffn(x1, p["g_ffn"], _pad_rows_front(f_conv_past, SUBLANES), p["w_up"], p["conv_f_w"],
                          p["conv_f_b"], p["w_down"], nseq, seq_rows, tiles_per_seq, p["tf"])
    wf = p["conv_f_w"].shape[0]
    f_new = jnp.concatenate([fv, fg], axis=-1).reshape(b, -1, SUBLANES, fv.shape[-1] * 2)[:, -1, SUBLANES - (wf - 1):]

    wb = p["conv_b_w"].shape[0]
    xb = xg3[:, :, :d_rnn]
    b_conv_new = jnp.concatenate([b_conv_past, xb], axis=1)[:, -(wb - 1):] if t < wb - 1 else xb[:, t - (wb - 1):]
    return y.reshape(b, t, d), k_a, v_a, b_conv_new, h_last.reshape(b, d_rnn), f_new


def kernel(x_prompt, x_sample, cache_a_k, cache_a_v, cache_mem_k, cache_mem_v, cache_b_conv, state_b_h, cache_f_conv, mem_prompt, g_attn, w_in, g_q_a, g_k_a, rel_table, g_q_m, g_k_m, g_mem, w_mem_kv, conv_b_w, conv_b_b, w_rg_a, b_rg_a, w_rg_i, b_rg_i, lru_lambda, w_out, g_ffn, w_up, conv_f_w, conv_f_b, w_down):
    depth = w_in.shape[0]
    bp, sp, d = x_prompt.shape
    bs, ts, _ = x_sample.shape
    n_heads_a = cache_a_k.shape[3]
    n_heads_m = cache_mem_k.shape[3]
    n_mem = mem_prompt.shape[1]
    d_a, d_m = n_heads_a * HEAD_DIM, n_heads_m * HEAD_DIM
    d_rnn = conv_b_w.shape[2]
    d_ff = w_down.shape[1]
    keep = min(WINDOW, sp)
    ones = lambda n: jnp.ones((n,), F32)

    xp, xs = x_prompt, x_sample
    outs = [[] for _ in range(12)]
    for l in range(depth):
        p = dict(
            n_heads_a=n_heads_a, n_heads_m=n_heads_m, d_rnn=d_rnn, tf=512,
            g_attn=g_attn[l], w_in=w_in[l].astype(BF16),
            g_q_a=g_q_a[l] * SCALE, g_k_a=g_k_a[l], g_q_m=g_q_m[l] * SCALE,
            conv_b_w=conv_b_w[l], conv_b_b=conv_b_b[l],
            wa_bd=_block_diag(w_rg_a[l]).astype(BF16), b_rg_a=b_rg_a[l],
            wi_bd=_block_diag(w_rg_i[l]).astype(BF16), b_rg_i=b_rg_i[l], lru_lambda=lru_lambda[l],
            wo_a=w_out[l, :d_a].astype(BF16), wo_m=w_out[l, d_a:d_a + d_m].astype(BF16),
            wo_b=w_out[l, d_a + d_m:].astype(BF16),
            g_ffn=g_ffn[l], w_up=w_up[l].astype(BF16), conv_f_w=conv_f_w[l], conv_f_b=conv_f_b[l],
            w_down=w_down[l].astype(BF16),
        )
        bias = _rel_bias(rel_table[l])

        mem_gains = jnp.concatenate([jnp.tile(g_k_m[l], n_heads_m), ones(d_m)])
        kv = _norm_matmul(mem_prompt.reshape(bp * n_mem, d), g_mem[l], w_mem_kv[l].astype(BF16), mem_gains,
                          ((0, 1),), 512, d_m)
        kv3 = kv.reshape(bp, n_mem, 2 * d_m)

        ffn_tm = 1024
        xp, k_a, v_a, bc, bh, fc = _trunk(
            xp, functools.partial(_band_attn, n_heads=n_heads_a, tq=WINDOW), kv3, kv3,
            jnp.zeros((bp, conv_b_w.shape[1] - 1, d_rnn), F32), jnp.zeros((bp, d_rnn), F32),
            jnp.zeros((bp, conv_f_w.shape[1] - 1, 2 * d_ff), F32), bias, p,
            in_tiles=(1, keep, sp // keep), tq_mem=512, tt=512, ffn_tiles=(1, ffn_tm, sp // ffn_tm))
        mem_k = kv3[:, :, :d_m].reshape(bp, n_mem, n_heads_m, HEAD_DIM)
        mem_v = kv3[:, :, d_m:].reshape(bp, n_mem, n_heads_m, HEAD_DIM)
        for lst, v in zip(outs[:7], (k_a, v_a, mem_k, mem_v, bc, bh, fc)):
            lst.append(v)

        band_s = functools.partial(_band_attn_sample, cache_k=cache_a_k[l], cache_v=cache_a_v[l], n_heads=n_heads_a)
        xs, k_a, v_a, bc, bh, fc = _trunk(
            xs, lambda qkv3, bias_: band_s(qkv3, bias=bias_), cache_mem_k[l], cache_mem_v[l],
            cache_b_conv[l], state_b_h[l], cache_f_conv[l], bias, p,
            in_tiles=(bs, ts, 1), tq_mem=ts, tt=ts, ffn_tiles=(bs, ts, 1))
        for lst, v in zip(outs[7:], (k_a, v_a, bc, bh, fc)):
            lst.append(v)

    stacked = [jnp.stack(o) for o in outs]
    return (xp, xs, *stacked)
```

```python
import functools

import jax
import jax.numpy as jnp
from jax import lax
from jax.experimental import pallas as pl
from jax.experimental.pallas import tpu as pltpu

EPS = 1e-6
HEAD_DIM = 128
CHUNK = 64
LEFT_CHUNKS = 8
WINDOW = LEFT_CHUNKS * CHUNK
REL_CLIP = 256
LRU_C = 8.0
NEG = -1e30
SCALE = HEAD_DIM ** -0.5

LANES = 128
SUBLANES = 8
MXU_COLS = 256
VMEM_LIMIT = 56 * 2 ** 20

QSUB = 2 * CHUNK
KSPAN = WINDOW + QSUB

BF16 = jnp.bfloat16
F32 = jnp.float32


def _params(sem):
    return pltpu.CompilerParams(dimension_semantics=sem, vmem_limit_bytes=VMEM_LIMIT)


def _rms(x, g):
    ms = jnp.mean(x * x, axis=-1, keepdims=True)
    return x * lax.rsqrt(ms + EPS) * g


def _norm_matmul_kernel(x_ref, g_ref, w_ref, hg_ref, o_ref, xn_ref, *, norm_ranges, tn):
    j = pl.program_id(1)

    @pl.when(j == 0)
    def _():
        xn_ref[...] = _rms(x_ref[...], g_ref[...]).astype(BF16)

    z = jnp.dot(xn_ref[...], w_ref[...], preferred_element_type=F32)
    cond = None
    for lo, hi in norm_ranges:
        c = jnp.logical_and(j >= lo, j < hi)
        cond = c if cond is None else jnp.logical_or(cond, c)

    @pl.when(cond)
    def _():
        for h in range(tn // HEAD_DIM):
            cs = slice(h * HEAD_DIM, (h + 1) * HEAD_DIM)
            o_ref[:, cs] = _rms(z[:, cs], hg_ref[:, cs])

    @pl.when(jnp.logical_not(cond))
    def _():
        o_ref[...] = z


def _norm_matmul(x, g, w_bf16, head_gains, norm_ranges, tm, tn):
    m, k = x.shape
    n = w_bf16.shape[1]
    kern = functools.partial(_norm_matmul_kernel, norm_ranges=norm_ranges, tn=tn)
    return pl.pallas_call(
        kern,
        grid=(m // tm, n // tn),
        in_specs=[
            pl.BlockSpec((tm, k), lambda i, j: (i, 0)),
            pl.BlockSpec((1, k), lambda i, j: (0, 0)),
            pl.BlockSpec((k, tn), lambda i, j: (0, j)),
            pl.BlockSpec((1, tn), lambda i, j: (0, j)),
        ],
        out_specs=pl.BlockSpec((tm, tn), lambda i, j: (i, j)),
        out_shape=jax.ShapeDtypeStruct((m, n), F32),
        scratch_shapes=[pltpu.VMEM((tm, k), BF16)],
        compiler_params=_params(("parallel", "arbitrary")),
        name="norm_matmul",
    )(x, g.reshape(1, k), w_bf16, head_gains.reshape(1, n))


def _in_proj_kernel(x_ref, g_ref, w_ref, gq_ref, gk_ref, gqm_ref, qkv_ref, qm_ref, xg_ref, ak_ref, av_ref,
                    xn_ref, kv_ref, *, d_a, d_m, d_rnn, group, tiles_per_seq):
    i = pl.program_id(0)
    xn_ref[...] = _rms(x_ref[...], g_ref[...]).astype(BF16)

    def cols(c0):
        return jnp.dot(xn_ref[...], w_ref[:, c0:c0 + group], preferred_element_type=F32)

    def heads(z, c0):
        return [(slice(c0 + h * HEAD_DIM, c0 + (h + 1) * HEAD_DIM), z[:, h * HEAD_DIM:(h + 1) * HEAD_DIM])
                for h in range(group // HEAD_DIM)]

    for c0 in range(0, d_a, group):
        for cs, zh in heads(cols(c0), c0):
            qkv_ref[:, cs] = _rms(zh, gq_ref[...]).astype(BF16)
    for c0 in range(d_a, 2 * d_a, group):
        for cs, zh in heads(cols(c0), c0):
            kh = _rms(zh, gk_ref[...])
            qkv_ref[:, cs] = kh.astype(BF16)
            kv_ref[:, slice(cs.start - d_a, cs.stop - d_a)] = kh
    for c0 in range(2 * d_a, 3 * d_a, group):
        for cs, zh in heads(cols(c0), c0):
            qkv_ref[:, cs] = zh.astype(BF16)
            kv_ref[:, slice(cs.start - d_a, cs.stop - d_a)] = zh
    for c0 in range(0, d_m, group):
        for cs, zh in heads(cols(3 * d_a + c0), c0):
            qm_ref[:, cs] = _rms(zh, gqm_ref[...]).astype(BF16)
    for c0 in range(0, 2 * d_rnn, group):
        xg_ref[:, c0:c0 + group] = cols(3 * d_a + d_m + c0)

    @pl.when(i % tiles_per_seq == tiles_per_seq - 1)
    def _():
        nseq, rows, n_heads = ak_ref.shape[0], ak_ref.shape[1], ak_ref.shape[2]
        for h in range(n_heads):
            cs = slice(h * HEAD_DIM, (h + 1) * HEAD_DIM)
            ak_ref[:, :, h, :] = kv_ref[:, cs].reshape(nseq, rows, HEAD_DIM)
            av_ref[:, :, h, :] = kv_ref[:, slice(d_a + cs.start, d_a + cs.stop)].reshape(nseq, rows, HEAD_DIM)


def _in_proj(x, g, w_bf16, gq, gk, gqm, n_heads_a, n_heads_m, d_rnn, nseq, rows, tiles_per_seq):
    m, k = x.shape
    d_in = w_bf16.shape[1]
    d_a, d_m = n_heads_a * HEAD_DIM, n_heads_m * HEAD_DIM
    tm = nseq * rows
    nt = m // tm
    n_cache_seq = nt * nseq // tiles_per_seq
    kern = functools.partial(_in_proj_kernel, d_a=d_a, d_m=d_m, d_rnn=d_rnn, group=4 * HEAD_DIM,
                             tiles_per_seq=tiles_per_seq)
    row = lambda c: pl.BlockSpec((tm, c), lambda i: (i, 0))
    vec = pl.BlockSpec((1, HEAD_DIM), lambda i: (0, 0))
    cache = pl.BlockSpec((nseq, rows, n_heads_a, HEAD_DIM), lambda i: (i // tiles_per_seq, 0, 0, 0))
    return pl.pallas_call(
        kern,
        grid=(nt,),
        in_specs=[
            row(k),
            pl.BlockSpec((1, k), lambda i: (0, 0)),
            pl.BlockSpec((k, d_in), lambda i: (0, 0), pipeline_mode=pl.Buffered(1)),
            vec, vec, vec,
        ],
        out_specs=[row(3 * d_a), row(d_m), row(2 * d_rnn), cache, cache],
        out_shape=[
            jax.ShapeDtypeStruct((m, 3 * d_a), BF16),
            jax.ShapeDtypeStruct((m, d_m), BF16),
            jax.ShapeDtypeStruct((m, 2 * d_rnn), F32),
            jax.ShapeDtypeStruct((n_cache_seq, rows, n_heads_a, HEAD_DIM), F32),
            jax.ShapeDtypeStruct((n_cache_seq, rows, n_heads_a, HEAD_DIM), F32),
        ],
        scratch_shapes=[pltpu.VMEM((tm, k), BF16), pltpu.VMEM((tm, 2 * d_a), F32)],
        compiler_params=_params(("arbitrary",)),
        name="in_proj",
    )(x, g.reshape(1, k), w_bf16, gq.reshape(1, HEAD_DIM), gk.reshape(1, HEAD_DIM), gqm.reshape(1, HEAD_DIM))


def _rel_bias_kernel(tab_ref, o_ref, *, n_heads, tpad):
    t = tab_ref[...]
    hi = t.astype(BF16)
    r1 = t - hi.astype(F32)
    mid = r1.astype(BF16)
    lo = (r1 - mid.astype(F32)).astype(BF16)
    tix = lax.broadcasted_iota(jnp.int32, (tpad, KSPAN), 0)
    j = lax.broadcasted_iota(jnp.int32, (1, KSPAN), 1)
    for i in range(QSUB):
        jj = j - CHUNK * (i // CHUNK)
        valid = jnp.logical_and(jj >= 0, jj < WINDOW + CHUNK)
        idx = jnp.clip((i % CHUNK) + WINDOW - jj, -REL_CLIP, REL_CLIP) + REL_CLIP
        onehot = jnp.where(tix == idx, 1.0, 0.0).astype(BF16)
        val = (jnp.dot(hi, onehot, preferred_element_type=F32)
               + jnp.dot(mid, onehot, preferred_element_type=F32)
               + jnp.dot(lo, onehot, preferred_element_type=F32))
        o_ref[i] = jnp.where(valid, val, NEG)


def _rel_bias(rel_table):
    n_heads, tlen = rel_table.shape
    tpad = -(-tlen // LANES) * LANES
    tab = jnp.pad(rel_table, ((0, 0), (0, tpad - tlen)))
    out = pl.pallas_call(
        functools.partial(_rel_bias_kernel, n_heads=n_heads, tpad=tpad),
        out_shape=jax.ShapeDtypeStruct((QSUB, n_heads, KSPAN), F32),
        compiler_params=_params(None),
        name="rel_bias",
    )(tab)
    return jnp.transpose(out, (1, 0, 2))


def _band_attn_kernel(q_ref, kp_ref, kc_ref, vp_ref, vc_ref, bias_ref, o_ref, kcat_ref, vcat_ref,
                      *, n_heads, tq):
    t = pl.program_id(1)
    kcat_ref[0:WINDOW, :] = kp_ref[0]
    kcat_ref[WINDOW:WINDOW + tq, :] = kc_ref[0]
    vcat_ref[0:WINDOW, :] = vp_ref[0]
    vcat_ref[WINDOW:WINDOW + tq, :] = vc_ref[0]

    def run(first_tile):
        def body(s, carry):
            r0 = pl.multiple_of(s * QSUB, QSUB)
            if first_tile:
                kidx = lax.broadcasted_iota(jnp.int32, (1, KSPAN), 1)
                before_start = kidx + r0 < WINDOW
            heads = [slice(h * HEAD_DIM, (h + 1) * HEAD_DIM) for h in range(n_heads)]
            scores = []
            for cs in heads:
                q = q_ref[0, pl.ds(r0, QSUB), cs]
                k = kcat_ref[pl.ds(r0, KSPAN), cs]
                scores.append(lax.dot_general(q, k, (((1,), (1,)), ((), ())), preferred_element_type=F32))
            for h, cs in enumerate(heads):
                sc = scores[h] + bias_ref[h]
                if first_tile:
                    sc = jnp.where(before_start, NEG, sc)
                m = jnp.max(sc, axis=-1, keepdims=True)
                p = jnp.exp(sc - m)
                l = jnp.sum(p, axis=-1, keepdims=True)
                v = vcat_ref[pl.ds(r0, KSPAN), cs]
                o = jnp.dot(p.astype(BF16), v, preferred_element_type=F32) / l
                o_ref[0, pl.ds(r0, QSUB), cs] = o.astype(o_ref.dtype)
            return carry

        lax.fori_loop(0, tq // QSUB, body, 0)

    @pl.when(t == 0)
    def _():
        run(True)

    @pl.when(t > 0)
    def _():
        run(False)


def _band_attn(qkv3, bias, n_heads, tq):
    b, s, _ = qkv3.shape
    d_a = n_heads * HEAD_DIM
    assert tq == WINDOW
    prev = lambda bi, t: (bi, jnp.maximum(t - 1, 0))
    return pl.pallas_call(
        functools.partial(_band_attn_kernel, n_heads=n_heads, tq=tq),
        grid=(b, s // tq),
        in_specs=[
            pl.BlockSpec((1, tq, d_a), lambda bi, t: (bi, t, 0)),
            pl.BlockSpec((1, WINDOW, d_a), lambda bi, t: prev(bi, t) + (1,)),
            pl.BlockSpec((1, tq, d_a), lambda bi, t: (bi, t, 1)),
            pl.BlockSpec((1, WINDOW, d_a), lambda bi, t: prev(bi, t) + (2,)),
            pl.BlockSpec((1, tq, d_a), lambda bi, t: (bi, t, 2)),
            pl.BlockSpec((n_heads, QSUB, KSPAN), lambda bi, t: (0, 0, 0)),
        ],
        out_specs=pl.BlockSpec((1, tq, d_a), lambda bi, t: (bi, t, 0)),
        out_shape=jax.ShapeDtypeStruct((b, s, d_a), BF16),
        scratch_shapes=[pltpu.VMEM((WINDOW + tq, d_a), BF16), pltpu.VMEM((WINDOW + tq, d_a), BF16)],
        compiler_params=_params(("parallel", "parallel")),
        name="band_attn",
    )(qkv3, qkv3, qkv3, qkv3, qkv3, bias)


def _band_attn_sample_kernel(q_ref, kn_ref, vn_ref, ck_ref, cv_ref, bias_ref, o_ref, *, n_heads, t, p_len):
    heads = [slice(h * HEAD_DIM, (h + 1) * HEAD_DIM) for h in range(n_heads)]
    dn = (((1,), (1,)), ((), ()))
    scores = []
    for h, cs in enumerate(heads):
        q = q_ref[0, :, cs]
        ck = ck_ref[0, :, h, :].astype(BF16)
        scores.append((lax.dot_general(q, ck, dn, preferred_element_type=F32),
                       lax.dot_general(q, kn_ref[0, :, cs], dn, preferred_element_type=F32)))
    for h, cs in enumerate(heads):
        cv = cv_ref[0, :, h, :].astype(BF16)
        vn = vn_ref[0, :, cs]
        s1 = scores[h][0] + bias_ref[h, 0:t, 0:p_len]
        s2 = scores[h][1] + bias_ref[h, 0:t, p_len:p_len + t]
        m = jnp.maximum(jnp.max(s1, axis=-1, keepdims=True), jnp.max(s2, axis=-1, keepdims=True))
        p1 = jnp.exp(s1 - m)
        p2 = jnp.exp(s2 - m)
        l = jnp.sum(p1, axis=-1, keepdims=True) + jnp.sum(p2, axis=-1, keepdims=True)
        o = (jnp.dot(p1.astype(BF16), cv, preferred_element_type=F32)
             + jnp.dot(p2.astype(BF16), vn, preferred_element_type=F32)) / l
        o_ref[0, :, cs] = o.astype(o_ref.dtype)


def _band_attn_sample(qkv3, cache_k, cache_v, bias, n_heads):
    b, t, _ = qkv3.shape
    d_a = n_heads * HEAD_DIM
    p_len = cache_k.shape[1]
    assert p_len == WINDOW and t <= CHUNK
    cache = pl.BlockSpec((1, p_len, n_heads, HEAD_DIM), lambda bi: (bi, 0, 0, 0))
    return pl.pallas_call(
        functools.partial(_band_attn_sample_kernel, n_heads=n_heads, t=t, p_len=p_len),
        grid=(b,),
        in_specs=[
            pl.BlockSpec((1, t, d_a), lambda bi: (bi, 0, 0)),
            pl.BlockSpec((1, t, d_a), lambda bi: (bi, 0, 1)),
            pl.BlockSpec((1, t, d_a), lambda bi: (bi, 0, 2)),
            cache, cache,
            pl.BlockSpec((n_heads, QSUB, KSPAN), lambda bi: (0, 0, 0)),
        ],
        out_specs=pl.BlockSpec((1, t, d_a), lambda bi: (bi, 0, 0)),
        out_shape=jax.ShapeDtypeStruct((b, t, d_a), BF16),
        compiler_params=_params(("parallel",)),
        name="band_attn_sample",
    )(qkv3, qkv3, qkv3, cache_k, cache_v, bias)


def _mem_attn_kernel(q_ref, mk_ref, mv_ref, o_ref, *, n_heads, head_axis):
    heads = [slice(h * HEAD_DIM, (h + 1) * HEAD_DIM) for h in range(n_heads)]
    load = (lambda ref, h, cs: ref[0, :, h, :]) if head_axis else (lambda ref, h, cs: ref[0, :, cs])
    scores = [lax.dot_general(q_ref[0, :, cs], load(mk_ref, h, cs).astype(BF16), (((1,), (1,)), ((), ())),
                              preferred_element_type=F32) for h, cs in enumerate(heads)]
    for h, cs in enumerate(heads):
        v = load(mv_ref, h, cs).astype(BF16)
        sc = scores[h]
        m = jnp.max(sc, axis=-1, keepdims=True)
        p = jnp.exp(sc - m)
        l = jnp.sum(p, axis=-1, keepdims=True)
        o = jnp.dot(p.astype(BF16), v, preferred_element_type=F32) / l
        o_ref[0, :, cs] = o.astype(o_ref.dtype)


def _mem_attn(qm3, mem_k, mem_v, n_heads, tm):
    b, s, d_m = qm3.shape
    n_mem = mem_k.shape[1]
    head_axis = mem_k.ndim == 4
    if head_axis:
        kspec = vspec = pl.BlockSpec((1, n_mem, n_heads, HEAD_DIM), lambda bi, t: (bi, 0, 0, 0))
    else:
        kspec = pl.BlockSpec((1, n_mem, d_m), lambda bi, t: (bi, 0, 0))
        vspec = pl.BlockSpec((1, n_mem, d_m), lambda bi, t: (bi, 0, 1))
    return pl.pallas_call(
        functools.partial(_mem_attn_kernel, n_heads=n_heads, head_axis=head_axis),
        grid=(b, s // tm),
        in_specs=[pl.BlockSpec((1, tm, d_m), lambda bi, t: (bi, t, 0)), kspec, vspec],
        out_specs=pl.BlockSpec((1, tm, d_m), lambda bi, t: (bi, t, 0)),
        out_shape=jax.ShapeDtypeStruct((b, s, d_m), BF16),
        compiler_params=_params(("parallel", "parallel")),
        name="mem_attn",
    )(qm3, mem_k, mem_v)


def _rglru_kernel(xb_ref, gb_ref, past_ref, h0_ref, cw_ref, cb_ref, wa_ref, ba_ref, wi_ref, bi_ref, lam_ref,
                  o_ref, hl_ref, buf_ref, a_ref, u_ref, hc_ref, *, tt, width):
    t = pl.program_id(1)
    pad = SUBLANES

    @pl.when(t == 0)
    def _():
        buf_ref[0:pad, :] = past_ref[0]
        hc_ref[...] = h0_ref[0]

    buf_ref[pad:pad + tt, :] = xb_ref[0]
    xc = cb_ref[...] + cw_ref[width - 1:width, :] * xb_ref[0]
    for k in range(1, width):
        xc = xc + cw_ref[width - 1 - k:width - k, :] * buf_ref[pad - k:pad - k + tt, :]
    buf_ref[0:pad, :] = buf_ref[tt:tt + pad, :]

    xcb = xc.astype(BF16)
    r = jax.nn.sigmoid(jnp.dot(xcb, wa_ref[...], preferred_element_type=F32) + ba_ref[...])
    i = jax.nn.sigmoid(jnp.dot(xcb, wi_ref[...], preferred_element_type=F32) + bi_ref[...])
    log_a = -LRU_C * r * jax.nn.softplus(-lam_ref[...])
    a = jnp.exp(log_a)
    th = jnp.tanh(log_a)
    u = jnp.sqrt(-2.0 * th / (1.0 - th)) * (i * xc)

    rowm = lax.broadcasted_iota(jnp.int32, (tt, 1), 0) % SUBLANES
    sh = 1
    while sh < SUBLANES:
        keep = rowm >= sh
        a_sh = pltpu.roll(a, sh, 0)
        u_sh = pltpu.roll(u, sh, 0)
        u = jnp.where(keep, a * u_sh + u, u)
        a = jnp.where(keep, a * a_sh, a)
        sh *= 2
    a_ref[...] = a
    u_ref[...] = u

    def body(g, h):
        r0 = pl.multiple_of(g * SUBLANES, SUBLANES)
        hr = a_ref[pl.ds(r0, SUBLANES), :] * h + u_ref[pl.ds(r0, SUBLANES), :]
        u_ref[pl.ds(r0, SUBLANES), :] = hr
        return hr[SUBLANES - 1:SUBLANES, :]

    h_last = lax.fori_loop(0, tt // SUBLANES, body, hc_ref[...])
    hc_ref[...] = h_last
    hl_ref[0] = h_last
    o_ref[0] = (u_ref[...] * jax.nn.gelu(gb_ref[0])).astype(o_ref.dtype)


def _rglru(xg3, past8, h0, cw, cb, wa_bd, ba, wi_bd, bi, lam, tt):
    b, s, _ = xg3.shape
    d = cw.shape[1]
    width = cw.shape[0]
    vec = pl.BlockSpec((1, d), lambda bi_, t: (0, 0))
    mat = pl.BlockSpec((d, d), lambda bi_, t: (0, 0))
    return pl.pallas_call(
        functools.partial(_rglru_kernel, tt=tt, width=width),
        grid=(b, s // tt),
        in_specs=[
            pl.BlockSpec((1, tt, d), lambda bi_, t: (bi_, t, 0)),
            pl.BlockSpec((1, tt, d), lambda bi_, t: (bi_, t, 1)),
            pl.BlockSpec((1, SUBLANES, d), lambda bi_, t: (bi_, 0, 0)),
            pl.BlockSpec((1, 1, d), lambda bi_, t: (bi_, 0, 0)),
            pl.BlockSpec((width, d), lambda bi_, t: (0, 0)),
            vec, mat, vec, mat, vec, vec,
        ],
        out_specs=[
            pl.BlockSpec((1, tt, d), lambda bi_, t: (bi_, t, 0)),
            pl.BlockSpec((1, 1, d), lambda bi_, t: (bi_, 0, 0)),
        ],
        out_shape=[jax.ShapeDtypeStruct((b, s, d), BF16), jax.ShapeDtypeStruct((b, 1, d), F32)],
        scratch_shapes=[
            pltpu.VMEM((tt + 2 * SUBLANES, d), F32),
            pltpu.VMEM((tt, d), F32),
            pltpu.VMEM((tt, d), F32),
            pltpu.VMEM((1, d), F32),
        ],
        compiler_params=_params(("parallel", "arbitrary")),
        name="rglru",
    )(xg3, xg3, past8, h0.reshape(b, 1, d), cw, cb.reshape(1, d), wa_bd, ba.reshape(1, d),
      wi_bd, bi.reshape(1, d), lam.reshape(1, d))


def _out_proj_kernel(x_ref, oa_ref, om_ref, ob_ref, wa_ref, wm_ref, wb_ref, o_ref):
    acc = jnp.dot(oa_ref[...], wa_ref[...], preferred_element_type=F32)
    acc = acc + jnp.dot(om_ref[...], wm_ref[...], preferred_element_type=F32)
    acc = acc + jnp.dot(ob_ref[...], wb_ref[...], preferred_element_type=F32)
    o_ref[...] = x_ref[...] + acc


def _out_proj(x, oa, om, ob, wa, wm, wb, tm):
    m, d = x.shape
    row = lambda c: pl.BlockSpec((tm, c), lambda i: (i, 0))
    full = lambda a: pl.BlockSpec(a.shape, lambda i: (0, 0))
    return pl.pallas_call(
        _out_proj_kernel,
        grid=(m // tm,),
        in_specs=[row(d), row(oa.shape[1]), row(om.shape[1]), row(ob.shape[1]), full(wa), full(wm), full(wb)],
        out_specs=row(d),
        out_shape=jax.ShapeDtypeStruct((m, d), F32),
        compiler_params=_params(("parallel",)),
        name="out_proj",
    )(x, oa, om, ob, wa, wm, wb)


def _conv_ffn_kernel(x_ref, g_ref, hv_ref, hg_ref, wv_ref, wg_ref, cwv_ref, cwg_ref, cbv_ref, cbg_ref, wd_ref,
                     o_ref, fv_ref, fg_ref, xn_ref, bv_ref, bg_ref, carv_ref, carg_ref,
                     *, nseq, seq_rows, tiles_per_seq, width):
    i = pl.program_id(0)
    c = pl.program_id(1)
    pad = SUBLANES
    tf = wv_ref.shape[1]
    sub = min(tf, MXU_COLS)

    @pl.when(c == 0)
    def _():
        x = x_ref[...]
        xn_ref[...] = _rms(x, g_ref[...]).astype(BF16)
        o_ref[...] = x

    for hist_ref, car_ref, buf_ref in ((hv_ref, carv_ref, bv_ref), (hg_ref, carg_ref, bg_ref)):
        if tiles_per_seq == 1:
            buf_ref[:, 0:pad, :] = hist_ref[...]
        else:
            first = i % tiles_per_seq == 0

            @pl.when(first)
            def _(hist_ref=hist_ref, buf_ref=buf_ref):
                buf_ref[:, 0:pad, :] = hist_ref[...]

            @pl.when(jnp.logical_not(first))
            def _(car_ref=car_ref, buf_ref=buf_ref):
                buf_ref[:, 0:pad, :] = car_ref[c]

    def up_proj(cs):
        for w_ref, buf_ref in ((wv_ref, bv_ref), (wg_ref, bg_ref)):
            up = jnp.dot(xn_ref[...], w_ref[:, cs], preferred_element_type=F32)
            buf_ref[:, pad:pad + seq_rows, cs] = up.reshape(nseq, seq_rows, sub)

    def conv(buf_ref, cw_ref, cb_ref, cs):
        y = cb_ref[:, cs]
        for k in range(width):
            y = y + cw_ref[width - 1 - k:width - k, cs] * buf_ref[:, pad - k:pad - k + seq_rows, cs]
        return y

    subs = [slice(s * sub, (s + 1) * sub) for s in range(tf // sub)]
    up_proj(subs[0])
    for s, cs in enumerate(subs):
        if s + 1 < len(subs):
            up_proj(subs[s + 1])
        val = conv(bv_ref, cwv_ref, cbv_ref, cs)
        gate = conv(bg_ref, cwg_ref, cbg_ref, cs)
        h = (jax.nn.gelu(gate) * val).astype(BF16).reshape(nseq * seq_rows, sub)
        o_ref[...] += jnp.dot(h, wd_ref[cs, :], preferred_element_type=F32)

    for car_ref, buf_ref, f_ref in ((carv_ref, bv_ref, fv_ref), (carg_ref, bg_ref, fg_ref)):
        last = buf_ref[:, seq_rows:seq_rows + pad, :]
        f_ref[...] = last
        if tiles_per_seq > 1:
            car_ref[c] = last


def _conv_ffn(x, g, hist8, w_up, cw, cb, w_down, nseq, seq_rows, tiles_per_seq, tf):
    m, d = x.shape
    d_ff = w_down.shape[0]
    width = cw.shape[0]
    tm = nseq * seq_rows
    nt = m // tm
    nc = d_ff // tf
    seq_blk = lambda i: i // tiles_per_seq
    kern = functools.partial(_conv_ffn_kernel, nseq=nseq, seq_rows=seq_rows,
                             tiles_per_seq=tiles_per_seq, width=width)
    car_shape = (nc, nseq, SUBLANES, tf) if tiles_per_seq > 1 else (1, 1, SUBLANES, LANES)
    return pl.pallas_call(
        kern,
        grid=(nt, nc),
        in_specs=[
            pl.BlockSpec((tm, d), lambda i, c: (i, 0), pipeline_mode=pl.Buffered(1)),
            pl.BlockSpec((1, d), lambda i, c: (0, 0)),
            pl.BlockSpec((nseq, SUBLANES, tf), lambda i, c: (seq_blk(i), 0, c)),
            pl.BlockSpec((nseq, SUBLANES, tf), lambda i, c: (seq_blk(i), 0, nc + c)),
            pl.BlockSpec((d, tf), lambda i, c: (0, c)),
            pl.BlockSpec((d, tf), lambda i, c: (0, nc + c)),
            pl.BlockSpec((width, tf), lambda i, c: (0, c)),
            pl.BlockSpec((width, tf), lambda i, c: (0, nc + c)),
            pl.BlockSpec((1, tf), lambda i, c: (0, c)),
            pl.BlockSpec((1, tf), lambda i, c: (0, nc + c)),
            pl.BlockSpec((tf, d), lambda i, c: (c, 0)),
        ],
        out_specs=[
            pl.BlockSpec((tm, d), lambda i, c: (i, 0)),
            pl.BlockSpec((nseq, SUBLANES, tf), lambda i, c: (i, 0, c)),
            pl.BlockSpec((nseq, SUBLANES, tf), lambda i, c: (i, 0, c)),
        ],
        out_shape=[
            jax.ShapeDtypeStruct((m, d), F32),
            jax.ShapeDtypeStruct((nt * nseq, SUBLANES, d_ff), F32),
            jax.ShapeDtypeStruct((nt * nseq, SUBLANES, d_ff), F32),
        ],
        scratch_shapes=[
            pltpu.VMEM((tm, d), BF16),
            pltpu.VMEM((nseq, seq_rows + 2 * SUBLANES, tf), F32),
            pltpu.VMEM((nseq, seq_rows + 2 * SUBLANES, tf), F32),
            pltpu.VMEM(car_shape, F32),
            pltpu.VMEM(car_shape, F32),
        ],
        compiler_params=_params(("arbitrary", "arbitrary")),
        name="conv_ffn",
    )(x, g.reshape(1, d), hist8, hist8, w_up, w_up, cw, cw, cb.reshape(1, 2 * d_ff), cb.reshape(1, 2 * d_ff), w_down)


def _pad_rows_front(a, rows):
    return jnp.pad(a, ((0, 0), (rows - a.shape[1], 0), (0, 0)))


def _block_diag(w):
    n, bi, bo = w.shape
    eye = jnp.eye(n, dtype=w.dtype)
    return (w[:, :, None, :] * eye[:, None, :, None]).reshape(n * bi, n * bo)


def _trunk(x3, band_fn, mem_k, mem_v, b_conv_past, b_h0, f_conv_past, bias, p, *, in_tiles, tq_mem, tt, ffn_tiles):
    b, t, d = x3.shape
    n_heads_a, n_heads_m, d_rnn = p["n_heads_a"], p["n_heads_m"], p["d_rnn"]
    d_a, d_m = n_heads_a * HEAD_DIM, n_heads_m * HEAD_DIM
    x2 = x3.reshape(b * t, d)
    nseq_in, rows_in, tiles_in = in_tiles
    qkv, qm, xg, k_a, v_a = _in_proj(x2, p["g_attn"], p["w_in"], p["g_q_a"], p["g_k_a"], p["g_q_m"],
                                     n_heads_a, n_heads_m, d_rnn, nseq_in, rows_in, tiles_in)
    xg3 = xg.reshape(b, t, 2 * d_rnn)

    o_a = band_fn(qkv.reshape(b, t, 3 * d_a), bias)
    o_m = _mem_attn(qm.reshape(b, t, d_m), mem_k, mem_v, n_heads_m, tq_mem)
    o_b, h_last = _rglru(xg3, _pad_rows_front(b_conv_past, SUBLANES), b_h0, p["conv_b_w"], p["conv_b_b"],
                         p["wa_bd"], p["b_rg_a"], p["wi_bd"], p["b_rg_i"], p["lru_lambda"], tt)
    x1 = _out_proj(x2, o_a.reshape(b * t, d_a), o_m.reshape(b * t, d_m), o_b.reshape(b * t, d_rnn),
                   p["wo_a"], p["wo_m"], p["wo_b"], 512)

    nseq, seq_rows, tiles_per_seq = ffn_tiles
    y, fv, fg = _conv_ffn(x1, p["g_ffn"], _pad_rows_front(f_conv_past, SUBLANES), p["w_up"], p["conv_f_w"],
                          p["conv_f_b"], p["w_down"], nseq, seq_rows, tiles_per_seq, p["tf"])
    wf = p["conv_f_w"].shape[0]
    f_new = jnp.concatenate([fv, fg], axis=-1).reshape(b, -1, SUBLANES, fv.shape[-1] * 2)[:, -1, SUBLANES - (wf - 1):]

    wb = p["conv_b_w"].shape[0]
    xb = xg3[:, :, :d_rnn]
    b_conv_new = jnp.concatenate([b_conv_past, xb], axis=1)[:, -(wb - 1):] if t < wb - 1 else xb[:, t - (wb - 1):]
    return y.reshape(b, t, d), k_a, v_a, b_conv_new, h_last.reshape(b, d_rnn), f_new


def kernel(x_prompt, x_sample, cache_a_k, cache_a_v, cache_mem_k, cache_mem_v, cache_b_conv, state_b_h, cache_f_conv, mem_prompt, g_attn, w_in, g_q_a, g_k_a, rel_table, g_q_m, g_k_m, g_mem, w_mem_kv, conv_b_w, conv_b_b, w_rg_a, b_rg_a, w_rg_i, b_rg_i, lru_lambda, w_out, g_ffn, w_up, conv_f_w, conv_f_b, w_down):
    depth = w_in.shape[0]
    bp, sp, d = x_prompt.shape
    bs, ts, _ = x_sample.shape
    n_heads_a = cache_a_k.shape[3]
    n_heads_m = cache_mem_k.shape[3]
    n_mem = mem_prompt.shape[1]
    d_a, d_m = n_heads_a * HEAD_DIM, n_heads_m * HEAD_DIM
    d_rnn = conv_b_w.shape[2]
    d_ff = w_down.shape[1]
    keep = min(WINDOW, sp)
    ones = lambda n: jnp.ones((n,), F32)

    xp, xs = x_prompt, x_sample
    outs = [[] for _ in range(12)]
    for l in range(depth):
        p = dict(
            n_heads_a=n_heads_a, n_heads_m=n_heads_m, d_rnn=d_rnn, tf=512,
            g_attn=g_attn[l], w_in=w_in[l].astype(BF16),
            g_q_a=g_q_a[l] * SCALE, g_k_a=g_k_a[l], g_q_m=g_q_m[l] * SCALE,
            conv_b_w=conv_b_w[l], conv_b_b=conv_b_b[l],
            wa_bd=_block_diag(w_rg_a[l]).astype(BF16), b_rg_a=b_rg_a[l],
            wi_bd=_block_diag(w_rg_i[l]).astype(BF16), b_rg_i=b_rg_i[l], lru_lambda=lru_lambda[l],
            wo_a=w_out[l, :d_a].astype(BF16), wo_m=w_out[l, d_a:d_a + d_m].astype(BF16),
            wo_b=w_out[l, d_a + d_m:].astype(BF16),
            g_ffn=g_ffn[l], w_up=w_up[l].astype(BF16), conv_f_w=conv_f_w[l], conv_f_b=conv_f_b[l],
            w_down=w_down[l].astype(BF16),
        )
        bias = _rel_bias(rel_table[l])

        mem_gains = jnp.concatenate([jnp.tile(g_k_m[l], n_heads_m), ones(d_m)])
        kv = _norm_matmul(mem_prompt.reshape(bp * n_mem, d), g_mem[l], w_mem_kv[l].astype(BF16), mem_gains,
                          ((0, 1),), 512, d_m)
        kv3 = kv.reshape(bp, n_mem, 2 * d_m)

        ffn_tm = 1024
        xp, k_a, v_a, bc, bh, fc = _trunk(
            xp, functools.partial(_band_attn, n_heads=n_heads_a, tq=WINDOW), kv3, kv3,
            jnp.zeros((bp, conv_b_w.shape[1] - 1, d_rnn), F32), jnp.zeros((bp, d_rnn), F32),
            jnp.zeros((bp, conv_f_w.shape[1] - 1, 2 * d_ff), F32), bias, p,
            in_tiles=(1, keep, sp // keep), tq_mem=512, tt=512, ffn_tiles=(1, ffn_tm, sp // ffn_tm))
        mem_k = kv3[:, :, :d_m].reshape(bp, n_mem, n_heads_m, HEAD_DIM)
        mem_v = kv3[:, :, d_m:].reshape(bp, n_mem, n_heads_m, HEAD_DIM)
        for lst, v in zip(outs[:7], (k_a, v_a, mem_k, mem_v, bc, bh, fc)):
            lst.append(v)

        band_s = functools.partial(_band_attn_sample, cache_k=cache_a_k[l], cache_v=cache_a_v[l], n_heads=n_heads_a)
        xs, k_a, v_a, bc, bh, fc = _trunk(
            xs, lambda qkv3, bias_: band_s(qkv3, bias=bias_), cache_mem_k[l], cache_mem_v[l],
            cache_b_conv[l], state_b_h[l], cache_f_conv[l], bias, p,
            in_tiles=(bs, ts, 1), tq_mem=ts, tt=ts, ffn_tiles=(bs, ts, 1))
        for lst, v in zip(outs[7:], (k_a, v_a, bc, bh, fc)):
            lst.append(v)

    stacked = [jnp.stack(o) for o in outs]
    return (xp, xs, *stacked)
```

```python
import functools

import jax
import jax.numpy as jnp
from jax import lax
from jax.experimental import pallas as pl
from jax.experimental.pallas import tpu as pltpu

EPS = 1e-6
HEAD_DIM = 128
CHUNK = 64
LEFT_CHUNKS = 8
WINDOW = LEFT_CHUNKS * CHUNK
REL_CLIP = 256
LRU_C = 8.0
NEG = -1e30
SCALE = HEAD_DIM ** -0.5

LANES = 128
SUBLANES = 8
MXU_COLS = 256
VMEM_LIMIT = 56 * 2 ** 20

QSUB = 4 * CHUNK
KSPAN = WINDOW + QSUB

BF16 = jnp.bfloat16
F32 = jnp.float32


def _params(sem):
    return pltpu.CompilerParams(dimension_semantics=sem, vmem_limit_bytes=VMEM_LIMIT)


def _rms(x, g):
    ms = jnp.mean(x * x, axis=-1, keepdims=True)
    return x * lax.rsqrt(ms + EPS) * g


def _norm_matmul_kernel(x_ref, g_ref, w_ref, hg_ref, o_ref, xn_ref, *, norm_ranges, tn):
    j = pl.program_id(1)

    @pl.when(j == 0)
    def _():
        xn_ref[...] = _rms(x_ref[...], g_ref[...]).astype(BF16)

    z = jnp.dot(xn_ref[...], w_ref[...], preferred_element_type=F32)
    cond = None
    for lo, hi in norm_ranges:
        c = jnp.logical_and(j >= lo, j < hi)
        cond = c if cond is None else jnp.logical_or(cond, c)

    @pl.when(cond)
    def _():
        for h in range(tn // HEAD_DIM):
            cs = slice(h * HEAD_DIM, (h + 1) * HEAD_DIM)
            o_ref[:, cs] = _rms(z[:, cs], hg_ref[:, cs])

    @pl.when(jnp.logical_not(cond))
    def _():
        o_ref[...] = z


def _norm_matmul(x, g, w_bf16, head_gains, norm_ranges, tm, tn):
    m, k = x.shape
    n = w_bf16.shape[1]
    kern = functools.partial(_norm_matmul_kernel, norm_ranges=norm_ranges, tn=tn)
    return pl.pallas_call(
        kern,
        grid=(m // tm, n // tn),
        in_specs=[
            pl.BlockSpec((tm, k), lambda i, j: (i, 0)),
            pl.BlockSpec((1, k), lambda i, j: (0, 0)),
            pl.BlockSpec((k, tn), lambda i, j: (0, j)),
            pl.BlockSpec((1, tn), lambda i, j: (0, j)),
        ],
        out_specs=pl.BlockSpec((tm, tn), lambda i, j: (i, j)),
        out_shape=jax.ShapeDtypeStruct((m, n), F32),
        scratch_shapes=[pltpu.VMEM((tm, k), BF16)],
        compiler_params=_params(("parallel", "arbitrary")),
        name="norm_matmul",
    )(x, g.reshape(1, k), w_bf16, head_gains.reshape(1, n))


def _in_proj_kernel(x_ref, g_ref, w_ref, gq_ref, gk_ref, gqm_ref, qkv_ref, qm_ref, xg_ref, ak_ref, av_ref,
                    xn_ref, kv_ref, *, d_a, d_m, d_rnn, group, tiles_per_seq):
    i = pl.program_id(0)
    xn_ref[...] = _rms(x_ref[...], g_ref[...]).astype(BF16)

    def cols(c0):
        return jnp.dot(xn_ref[...], w_ref[:, c0:c0 + group], preferred_element_type=F32)

    def heads(z, c0):
        return [(slice(c0 + h * HEAD_DIM, c0 + (h + 1) * HEAD_DIM), z[:, h * HEAD_DIM:(h + 1) * HEAD_DIM])
                for h in range(group // HEAD_DIM)]

    for c0 in range(0, d_a, group):
        for cs, zh in heads(cols(c0), c0):
            qkv_ref[:, cs] = _rms(zh, gq_ref[...]).astype(BF16)
    for c0 in range(d_a, 2 * d_a, group):
        for cs, zh in heads(cols(c0), c0):
            kh = _rms(zh, gk_ref[...])
            qkv_ref[:, cs] = kh.astype(BF16)
            kv_ref[:, slice(cs.start - d_a, cs.stop - d_a)] = kh
    for c0 in range(2 * d_a, 3 * d_a, group):
        for cs, zh in heads(cols(c0), c0):
            qkv_ref[:, cs] = zh.astype(BF16)
            kv_ref[:, slice(cs.start - d_a, cs.stop - d_a)] = zh
    for c0 in range(0, d_m, group):
        for cs, zh in heads(cols(3 * d_a + c0), c0):
            qm_ref[:, cs] = _rms(zh, gqm_ref[...]).astype(BF16)
    for c0 in range(0, 2 * d_rnn, group):
        xg_ref[:, c0:c0 + group] = cols(3 * d_a + d_m + c0)

    @pl.when(i % tiles_per_seq == tiles_per_seq - 1)
    def _():
        nseq, rows, n_heads = ak_ref.shape[0], ak_ref.shape[1], ak_ref.shape[2]
        for h in range(n_heads):
            cs = slice(h * HEAD_DIM, (h + 1) * HEAD_DIM)
            ak_ref[:, :, h, :] = kv_ref[:, cs].reshape(nseq, rows, HEAD_DIM)
            av_ref[:, :, h, :] = kv_ref[:, slice(d_a + cs.start, d_a + cs.stop)].reshape(nseq, rows, HEAD_DIM)


def _in_proj(x, g, w_bf16, gq, gk, gqm, n_heads_a, n_heads_m, d_rnn, nseq, rows, tiles_per_seq):
    m, k = x.shape
    d_in = w_bf16.shape[1]
    d_a, d_m = n_heads_a * HEAD_DIM, n_heads_m * HEAD_DIM
    tm = nseq * rows
    nt = m // tm
    n_cache_seq = nt * nseq // tiles_per_seq
    kern = functools.partial(_in_proj_kernel, d_a=d_a, d_m=d_m, d_rnn=d_rnn, group=4 * HEAD_DIM,
                             tiles_per_seq=tiles_per_seq)
    row = lambda c: pl.BlockSpec((tm, c), lambda i: (i, 0))
    vec = pl.BlockSpec((1, HEAD_DIM), lambda i: (0, 0))
    cache = pl.BlockSpec((nseq, rows, n_heads_a, HEAD_DIM), lambda i: (i // tiles_per_seq, 0, 0, 0))
    return pl.pallas_call(
        kern,
        grid=(nt,),
        in_specs=[
            row(k),
            pl.BlockSpec((1, k), lambda i: (0, 0)),
            pl.BlockSpec((k, d_in), lambda i: (0, 0), pipeline_mode=pl.Buffered(1)),
            vec, vec, vec,
        ],
        out_specs=[row(3 * d_a), row(d_m), row(2 * d_rnn), cache, cache],
        out_shape=[
            jax.ShapeDtypeStruct((m, 3 * d_a), BF16),
            jax.ShapeDtypeStruct((m, d_m), BF16),
            jax.ShapeDtypeStruct((m, 2 * d_rnn), F32),
            jax.ShapeDtypeStruct((n_cache_seq, rows, n_heads_a, HEAD_DIM), F32),
            jax.ShapeDtypeStruct((n_cache_seq, rows, n_heads_a, HEAD_DIM), F32),
        ],
        scratch_shapes=[pltpu.VMEM((tm, k), BF16), pltpu.VMEM((tm, 2 * d_a), F32)],
        compiler_params=_params(("arbitrary",)),
        name="in_proj",
    )(x, g.reshape(1, k), w_bf16, gq.reshape(1, HEAD_DIM), gk.reshape(1, HEAD_DIM), gqm.reshape(1, HEAD_DIM))


def _rel_bias_kernel(tab_ref, o_ref, *, n_heads, tpad):
    t = tab_ref[...]
    hi = t.astype(BF16)
    r1 = t - hi.astype(F32)
    mid = r1.astype(BF16)
    lo = (r1 - mid.astype(F32)).astype(BF16)
    tix = lax.broadcasted_iota(jnp.int32, (tpad, KSPAN), 0)
    jj = lax.broadcasted_iota(jnp.int32, (1, KSPAN), 1)
    valid = jj < WINDOW + CHUNK
    for i in range(CHUNK):
        idx = jnp.clip(i + WINDOW - jj, -REL_CLIP, REL_CLIP) + REL_CLIP
        onehot = jnp.where(tix == idx, 1.0, 0.0).astype(BF16)
        val = (jnp.dot(hi, onehot, preferred_element_type=F32)
               + jnp.dot(mid, onehot, preferred_element_type=F32)
               + jnp.dot(lo, onehot, preferred_element_type=F32))
        row = jnp.where(valid, val, NEG)
        for ci in range(QSUB // CHUNK):
            o_ref[ci * CHUNK + i] = row if ci == 0 else pltpu.roll(row, ci * CHUNK, 1)


def _rel_bias(rel_table):
    n_heads, tlen = rel_table.shape
    tpad = -(-tlen // LANES) * LANES
    tab = jnp.pad(rel_table, ((0, 0), (0, tpad - tlen)))
    out = pl.pallas_call(
        functools.partial(_rel_bias_kernel, n_heads=n_heads, tpad=tpad),
        out_shape=jax.ShapeDtypeStruct((QSUB, n_heads, KSPAN), F32),
        compiler_params=_params(None),
        name="rel_bias",
    )(tab)
    return jnp.transpose(out, (1, 0, 2))


def _band_attn_kernel(q_ref, kp_ref, kc_ref, vp_ref, vc_ref, bias_ref, o_ref, kcat_ref, vcat_ref,
                      *, n_heads, tq):
    t = pl.program_id(1)
    kcat_ref[0:WINDOW, :] = kp_ref[0]
    kcat_ref[WINDOW:WINDOW + tq, :] = kc_ref[0]
    vcat_ref[0:WINDOW, :] = vp_ref[0]
    vcat_ref[WINDOW:WINDOW + tq, :] = vc_ref[0]

    def run(first_tile):
        def body(s, carry):
            r0 = pl.multiple_of(s * QSUB, QSUB)
            if first_tile:
                kidx = lax.broadcasted_iota(jnp.int32, (1, KSPAN), 1)
                before_start = kidx + r0 < WINDOW
            heads = [slice(h * HEAD_DIM, (h + 1) * HEAD_DIM) for h in range(n_heads)]
            scores = []
            for cs in heads:
                q = q_ref[0, pl.ds(r0, QSUB), cs]
                k = kcat_ref[pl.ds(r0, KSPAN), cs]
                scores.append(lax.dot_general(q, k, (((1,), (1,)), ((), ())), preferred_element_type=F32))
            for h, cs in enumerate(heads):
                sc = scores[h] + bias_ref[h]
                if first_tile:
                    sc = jnp.where(before_start, NEG, sc)
                m = jnp.max(sc, axis=-1, keepdims=True)
                p = jnp.exp(sc - m)
                l = jnp.sum(p, axis=-1, keepdims=True)
                v = vcat_ref[pl.ds(r0, KSPAN), cs]
                o = jnp.dot(p.astype(BF16), v, preferred_element_type=F32) / l
                o_ref[0, pl.ds(r0, QSUB), cs] = o.astype(o_ref.dtype)
            return carry

        lax.fori_loop(0, tq // QSUB, body, 0)

    @pl.when(t == 0)
    def _():
        run(True)

    @pl.when(t > 0)
    def _():
        run(False)


def _band_attn(qkv3, bias, n_heads, tq):
    b, s, _ = qkv3.shape
    d_a = n_heads * HEAD_DIM
    assert tq == WINDOW
    prev = lambda bi, t: (bi, jnp.maximum(t - 1, 0))
    return pl.pallas_call(
        functools.partial(_band_attn_kernel, n_heads=n_heads, tq=tq),
        grid=(b, s // tq),
        in_specs=[
            pl.BlockSpec((1, tq, d_a), lambda bi, t: (bi, t, 0)),
            pl.BlockSpec((1, WINDOW, d_a), lambda bi, t: prev(bi, t) + (1,)),
            pl.BlockSpec((1, tq, d_a), lambda bi, t: (bi, t, 1)),
            pl.BlockSpec((1, WINDOW, d_a), lambda bi, t: prev(bi, t) + (2,)),
            pl.BlockSpec((1, tq, d_a), lambda bi, t: (bi, t, 2)),
            pl.BlockSpec((n_heads, QSUB, KSPAN), lambda bi, t: (0, 0, 0)),
        ],
        out_specs=pl.BlockSpec((1, tq, d_a), lambda bi, t: (bi, t, 0)),
        out_shape=jax.ShapeDtypeStruct((b, s, d_a), BF16),
        scratch_shapes=[pltpu.VMEM((WINDOW + tq, d_a), BF16), pltpu.VMEM((WINDOW + tq, d_a), BF16)],
        compiler_params=_params(("parallel", "parallel")),
        name="band_attn",
    )(qkv3, qkv3, qkv3, qkv3, qkv3, bias)


def _band_attn_sample_kernel(q_ref, kn_ref, vn_ref, ck_ref, cv_ref, bias_ref, o_ref, *, n_heads, t, p_len):
    heads = [slice(h * HEAD_DIM, (h + 1) * HEAD_DIM) for h in range(n_heads)]
    dn = (((1,), (1,)), ((), ()))
    scores = []
    for h, cs in enumerate(heads):
        q = q_ref[0, :, cs]
        ck = ck_ref[0, :, h, :].astype(BF16)
        scores.append((lax.dot_general(q, ck, dn, preferred_element_type=F32),
                       lax.dot_general(q, kn_ref[0, :, cs], dn, preferred_element_type=F32)))
    for h, cs in enumerate(heads):
        cv = cv_ref[0, :, h, :].astype(BF16)
        vn = vn_ref[0, :, cs]
        s1 = scores[h][0] + bias_ref[h, 0:t, 0:p_len]
        s2 = scores[h][1] + bias_ref[h, 0:t, p_len:p_len + t]
        m = jnp.maximum(jnp.max(s1, axis=-1, keepdims=True), jnp.max(s2, axis=-1, keepdims=True))
        p1 = jnp.exp(s1 - m)
        p2 = jnp.exp(s2 - m)
        l = jnp.sum(p1, axis=-1, keepdims=True) + jnp.sum(p2, axis=-1, keepdims=True)
        o = (jnp.dot(p1.astype(BF16), cv, preferred_element_type=F32)
             + jnp.dot(p2.astype(BF16), vn, preferred_element_type=F32)) / l
        o_ref[0, :, cs] = o.astype(o_ref.dtype)


def _band_attn_sample(qkv3, cache_k, cache_v, bias, n_heads):
    b, t, _ = qkv3.shape
    d_a = n_heads * HEAD_DIM
    p_len = cache_k.shape[1]
    assert p_len == WINDOW and t <= CHUNK
    cache = pl.BlockSpec((1, p_len, n_heads, HEAD_DIM), lambda bi: (bi, 0, 0, 0))
    return pl.pallas_call(
        functools.partial(_band_attn_sample_kernel, n_heads=n_heads, t=t, p_len=p_len),
        grid=(b,),
        in_specs=[
            pl.BlockSpec((1, t, d_a), lambda bi: (bi, 0, 0)),
            pl.BlockSpec((1, t, d_a), lambda bi: (bi, 0, 1)),
            pl.BlockSpec((1, t, d_a), lambda bi: (bi, 0, 2)),
            cache, cache,
            pl.BlockSpec((n_heads, QSUB, KSPAN), lambda bi: (0, 0, 0)),
        ],
        out_specs=pl.BlockSpec((1, t, d_a), lambda bi: (bi, 0, 0)),
        out_shape=jax.ShapeDtypeStruct((b, t, d_a), BF16),
        compiler_params=_params(("parallel",)),
        name="band_attn_sample",
    )(qkv3, qkv3, qkv3, cache_k, cache_v, bias)


def _mem_attn_kernel(q_ref, mk_ref, mv_ref, o_ref, *, n_heads, head_axis):
    heads = [slice(h * HEAD_DIM, (h + 1) * HEAD_DIM) for h in range(n_heads)]
    load = (lambda ref, h, cs: ref[0, :, h, :]) if head_axis else (lambda ref, h, cs: ref[0, :, cs])
    scores = [lax.dot_general(q_ref[0, :, cs], load(mk_ref, h, cs).astype(BF16), (((1,), (1,)), ((), ())),
                              preferred_element_type=F32) for h, cs in enumerate(heads)]
    for h, cs in enumerate(heads):
        v = load(mv_ref, h, cs).astype(BF16)
        sc = scores[h]
        m = jnp.max(sc, axis=-1, keepdims=True)
        p = jnp.exp(sc - m)
        l = jnp.sum(p, axis=-1, keepdims=True)
        o = jnp.dot(p.astype(BF16), v, preferred_element_type=F32) / l
        o_ref[0, :, cs] = o.astype(o_ref.dtype)


def _mem_attn(qm3, mem_k, mem_v, n_heads, tm):
    b, s, d_m = qm3.shape
    n_mem = mem_k.shape[1]
    head_axis = mem_k.ndim == 4
    if head_axis:
        kspec = vspec = pl.BlockSpec((1, n_mem, n_heads, HEAD_DIM), lambda bi, t: (bi, 0, 0, 0))
    else:
        kspec = pl.BlockSpec((1, n_mem, d_m), lambda bi, t: (bi, 0, 0))
        vspec = pl.BlockSpec((1, n_mem, d_m), lambda bi, t: (bi, 0, 1))
    return pl.pallas_call(
        functools.partial(_mem_attn_kernel, n_heads=n_heads, head_axis=head_axis),
        grid=(b, s // tm),
        in_specs=[pl.BlockSpec((1, tm, d_m), lambda bi, t: (bi, t, 0)), kspec, vspec],
        out_specs=pl.BlockSpec((1, tm, d_m), lambda bi, t: (bi, t, 0)),
        out_shape=jax.ShapeDtypeStruct((b, s, d_m), BF16),
        compiler_params=_params(("parallel", "parallel")),
        name="mem_attn",
    )(qm3, mem_k, mem_v)


def _rglru_kernel(xb_ref, gb_ref, past_ref, h0_ref, cw_ref, cb_ref, wa_ref, ba_ref, wi_ref, bi_ref, lam_ref,
                  o_ref, hl_ref, buf_ref, a_ref, u_ref, hc_ref, *, tt, width):
    t = pl.program_id(1)
    pad = SUBLANES

    @pl.when(t == 0)
    def _():
        buf_ref[0:pad, :] = past_ref[0]
        hc_ref[...] = h0_ref[0]

    buf_ref[pad:pad + tt, :] = xb_ref[0]
    xc = cb_ref[...] + cw_ref[width - 1:width, :] * xb_ref[0]
    for k in range(1, width):
        xc = xc + cw_ref[width - 1 - k:width - k, :] * buf_ref[pad - k:pad - k + tt, :]
    buf_ref[0:pad, :] = buf_ref[tt:tt + pad, :]

    xcb = xc.astype(BF16)
    def sigmoid(z):
        return 0.5 * jnp.tanh(0.5 * z) + 0.5

    r = sigmoid(jnp.dot(xcb, wa_ref[...], preferred_element_type=F32) + ba_ref[...])
    i = sigmoid(jnp.dot(xcb, wi_ref[...], preferred_element_type=F32) + bi_ref[...])
    log_a = -LRU_C * r * jax.nn.softplus(-lam_ref[...])
    a = jnp.exp(log_a)
    th = jnp.tanh(log_a)
    u = jnp.sqrt(-2.0 * th / (1.0 - th)) * (i * xc)

    rowm = lax.broadcasted_iota(jnp.int32, (tt, 1), 0) % SUBLANES
    sh = 1
    while sh < SUBLANES:
        keep = rowm >= sh
        a_sh = pltpu.roll(a, sh, 0)
        u_sh = pltpu.roll(u, sh, 0)
        u = jnp.where(keep, a * u_sh + u, u)
        a = jnp.where(keep, a * a_sh, a)
        sh *= 2
    a_ref[...] = a
    u_ref[...] = u

    def body(g, h):
        r0 = pl.multiple_of(g * SUBLANES, SUBLANES)
        hr = a_ref[pl.ds(r0, SUBLANES), :] * h + u_ref[pl.ds(r0, SUBLANES), :]
        u_ref[pl.ds(r0, SUBLANES), :] = hr
        return hr[SUBLANES - 1:SUBLANES, :]

    h_last = lax.fori_loop(0, tt // SUBLANES, body, hc_ref[...])
    hc_ref[...] = h_last
    hl_ref[0] = h_last
    o_ref[0] = (u_ref[...] * jax.nn.gelu(gb_ref[0])).astype(o_ref.dtype)


def _rglru(xg3, past8, h0, cw, cb, wa_bd, ba, wi_bd, bi, lam, tt):
    b, s, _ = xg3.shape
    d = cw.shape[1]
    width = cw.shape[0]
    vec = pl.BlockSpec((1, d), lambda bi_, t: (0, 0))
    mat = pl.BlockSpec((d, d), lambda bi_, t: (0, 0))
    return pl.pallas_call(
        functools.partial(_rglru_kernel, tt=tt, width=width),
        grid=(b, s // tt),
        in_specs=[
            pl.BlockSpec((1, tt, d), lambda bi_, t: (bi_, t, 0)),
            pl.BlockSpec((1, tt, d), lambda bi_, t: (bi_, t, 1)),
            pl.BlockSpec((1, SUBLANES, d), lambda bi_, t: (bi_, 0, 0)),
            pl.BlockSpec((1, 1, d), lambda bi_, t: (bi_, 0, 0)),
            pl.BlockSpec((width, d), lambda bi_, t: (0, 0)),
            vec, mat, vec, mat, vec, vec,
        ],
        out_specs=[
            pl.BlockSpec((1, tt, d), lambda bi_, t: (bi_, t, 0)),
            pl.BlockSpec((1, 1, d), lambda bi_, t: (bi_, 0, 0)),
        ],
        out_shape=[jax.ShapeDtypeStruct((b, s, d), BF16), jax.ShapeDtypeStruct((b, 1, d), F32)],
        scratch_shapes=[
            pltpu.VMEM((tt + 2 * SUBLANES, d), F32),
            pltpu.VMEM((tt, d), F32),
            pltpu.VMEM((tt, d), F32),
            pltpu.VMEM((1, d), F32),
        ],
        compiler_params=_params(("parallel", "arbitrary")),
        name="rglru",
    )(xg3, xg3, past8, h0.reshape(b, 1, d), cw, cb.reshape(1, d), wa_bd, ba.reshape(1, d),
      wi_bd, bi.reshape(1, d), lam.reshape(1, d))


def _out_proj_kernel(x_ref, oa_ref, om_ref, ob_ref, wa_ref, wm_ref, wb_ref, o_ref):
    acc = jnp.dot(oa_ref[...], wa_ref[...], preferred_element_type=F32)
    acc = acc + jnp.dot(om_ref[...], wm_ref[...], preferred_element_type=F32)
    acc = acc + jnp.dot(ob_ref[...], wb_ref[...], preferred_element_type=F32)
    o_ref[...] = x_ref[...] + acc


def _out_proj(x, oa, om, ob, wa, wm, wb, tm):
    m, d = x.shape
    row = lambda c: pl.BlockSpec((tm, c), lambda i: (i, 0))
    full = lambda a: pl.BlockSpec(a.shape, lambda i: (0, 0))
    return pl.pallas_call(
        _out_proj_kernel,
        grid=(m // tm,),
        in_specs=[row(d), row(oa.shape[1]), row(om.shape[1]), row(ob.shape[1]), full(wa), full(wm), full(wb)],
        out_specs=row(d),
        out_shape=jax.ShapeDtypeStruct((m, d), F32),
        compiler_params=_params(("parallel",)),
        name="out_proj",
    )(x, oa, om, ob, wa, wm, wb)


def _conv_ffn_kernel(x_ref, g_ref, hist_ref, wv_ref, wg_ref, cw_ref, cb_ref, wd_ref,
                     o_ref, f_ref, xn_ref, bv_ref, bg_ref, car_ref,
                     *, nseq, seq_rows, tiles_per_seq, width):
    i = pl.program_id(0)
    c = pl.program_id(1)
    nc = pl.num_programs(1)
    pad = SUBLANES
    tf = wv_ref.shape[1]
    sub = min(tf, MXU_COLS)
    streams = ((c, wv_ref, bv_ref), (nc + c, wg_ref, bg_ref))

    @pl.when(c == 0)
    def _():
        x = x_ref[...]
        xn_ref[...] = _rms(x, g_ref[...]).astype(BF16)
        o_ref[...] = x

    for j, _, buf_ref in streams:
        if tiles_per_seq == 1:
            buf_ref[:, 0:pad, :] = hist_ref[:, j]
        else:
            first = i % tiles_per_seq == 0

            @pl.when(first)
            def _(j=j, buf_ref=buf_ref):
                buf_ref[:, 0:pad, :] = hist_ref[:, j]

            @pl.when(jnp.logical_not(first))
            def _(j=j, buf_ref=buf_ref):
                buf_ref[:, 0:pad, :] = car_ref[j]

    def up_proj(cs):
        for _, w_ref, buf_ref in streams:
            up = jnp.dot(xn_ref[...], w_ref[:, cs], preferred_element_type=F32)
            buf_ref[:, pad:pad + seq_rows, cs] = up.reshape(nseq, seq_rows, sub)

    def conv(j, buf_ref, cs):
        y = cb_ref[j, :, cs]
        for k in range(width):
            y = y + cw_ref[j, width - 1 - k:width - k, cs] * buf_ref[:, pad - k:pad - k + seq_rows, cs]
        return y

    subs = [slice(s * sub, (s + 1) * sub) for s in range(tf // sub)]
    up_proj(subs[0])
    for s, cs in enumerate(subs):
        if s + 1 < len(subs):
            up_proj(subs[s + 1])
        val = conv(streams[0][0], bv_ref, cs)
        gate = conv(streams[1][0], bg_ref, cs)
        h = (jax.nn.gelu(gate) * val).astype(BF16).reshape(nseq * seq_rows, sub)
        o_ref[...] += jnp.dot(h, wd_ref[cs, :], preferred_element_type=F32)

    for j, _, buf_ref in streams:
        last = buf_ref[:, seq_rows:seq_rows + pad, :]
        f_ref[0, j] = last
        if tiles_per_seq > 1:
            car_ref[j] = last


def _conv_ffn(x, g, f_conv_past, w_up, cw, cb, w_down, nseq, seq_rows, tiles_per_seq, tf):
    m, d = x.shape
    d_ff = w_down.shape[0]
    width = cw.shape[0]
    tm = nseq * seq_rows
    nt = m // tm
    nc = d_ff // tf
    nb = f_conv_past.shape[0]
    hist = jnp.pad(f_conv_past, ((0, 0), (SUBLANES - (width - 1), 0), (0, 0)))
    hist = hist.reshape(nb, SUBLANES, 2 * nc, tf).transpose(0, 2, 1, 3)
    cw3 = cw.reshape(width, 2 * nc, tf).transpose(1, 0, 2)
    cb3 = cb.reshape(2 * nc, 1, tf)
    kern = functools.partial(_conv_ffn_kernel, nseq=nseq, seq_rows=seq_rows,
                             tiles_per_seq=tiles_per_seq, width=width)
    car_shape = (2 * nc, nseq, SUBLANES, tf) if tiles_per_seq > 1 else (1, 1, SUBLANES, LANES)
    y, f = pl.pallas_call(
        kern,
        grid=(nt, nc),
        in_specs=[
            pl.BlockSpec((tm, d), lambda i, c: (i, 0), pipeline_mode=pl.Buffered(1)),
            pl.BlockSpec((1, d), lambda i, c: (0, 0)),
            pl.BlockSpec((nseq, 2 * nc, SUBLANES, tf), lambda i, c: (i // tiles_per_seq, 0, 0, 0)),
            pl.BlockSpec((d, tf), lambda i, c: (0, c)),
            pl.BlockSpec((d, tf), lambda i, c: (0, nc + c)),
            pl.BlockSpec((2 * nc, width, tf), lambda i, c: (0, 0, 0)),
            pl.BlockSpec((2 * nc, 1, tf), lambda i, c: (0, 0, 0)),
            pl.BlockSpec((tf, d), lambda i, c: (c, 0)),
        ],
        out_specs=[
            pl.BlockSpec((tm, d), lambda i, c: (i, 0)),
            pl.BlockSpec((1, 2 * nc, nseq, SUBLANES, tf), lambda i, c: (i, 0, 0, 0, 0)),
        ],
        out_shape=[
            jax.ShapeDtypeStruct((m, d), F32),
            jax.ShapeDtypeStruct((nt, 2 * nc, nseq, SUBLANES, tf), F32),
        ],
        scratch_shapes=[
            pltpu.VMEM((tm, d), BF16),
            pltpu.VMEM((nseq, seq_rows + 2 * SUBLANES, tf), F32),
            pltpu.VMEM((nseq, seq_rows + 2 * SUBLANES, tf), F32),
            pltpu.VMEM(car_shape, F32),
        ],
        compiler_params=_params(("arbitrary", "arbitrary")),
        name="conv_ffn",
    )(x, g.reshape(1, d), hist, w_up, w_up, cw3, cb3, w_down)
    f = f.reshape(nt // tiles_per_seq, tiles_per_seq, 2 * nc, nseq, SUBLANES, tf)[:, -1]
    f = f.transpose(0, 2, 3, 1, 4).reshape(nb, SUBLANES, 2 * d_ff)
    return y, f[:, SUBLANES - (width - 1):]


def _pad_rows_front(a, rows):
    return jnp.pad(a, ((0, 0), (rows - a.shape[1], 0), (0, 0)))


def _block_diag(w):
    n, bi, bo = w.shape
    eye = jnp.eye(n, dtype=w.dtype)
    return (w[:, :, None, :] * eye[:, None, :, None]).reshape(n * bi, n * bo)


def _trunk(x3, band_fn, mem_k, mem_v, b_conv_past, b_h0, f_conv_past, bias, p, *, in_tiles, tq_mem, tt, ffn_tiles):
    b, t, d = x3.shape
    n_heads_a, n_heads_m, d_rnn = p["n_heads_a"], p["n_heads_m"], p["d_rnn"]
    d_a, d_m = n_heads_a * HEAD_DIM, n_heads_m * HEAD_DIM
    x2 = x3.reshape(b * t, d)
    nseq_in, rows_in, tiles_in = in_tiles
    qkv, qm, xg, k_a, v_a = _in_proj(x2, p["g_attn"], p["w_in"], p["g_q_a"], p["g_k_a"], p["g_q_m"],
                                     n_heads_a, n_heads_m, d_rnn, nseq_in, rows_in, tiles_in)
    xg3 = xg.reshape(b, t, 2 * d_rnn)

    o_a = band_fn(qkv.reshape(b, t, 3 * d_a), bias)
    o_m = _mem_attn(qm.reshape(b, t, d_m), mem_k, mem_v, n_heads_m, tq_mem)
    o_b, h_last = _rglru(xg3, _pad_rows_front(b_conv_past, SUBLANES), b_h0, p["conv_b_w"], p["conv_b_b"],
                         p["wa_bd"], p["b_rg_a"], p["wi_bd"], p["b_rg_i"], p["lru_lambda"], tt)
    x1 = _out_proj(x2, o_a.reshape(b * t, d_a), o_m.reshape(b * t, d_m), o_b.reshape(b * t, d_rnn),
                   p["wo_a"], p["wo_m"], p["wo_b"], 512)

    nseq, seq_rows, tiles_per_seq = ffn_tiles
    y, f_new = _conv_ffn(x1, p["g_ffn"], f_conv_past, p["w_up"], p["conv_f_w"], p["conv_f_b"], p["w_down"],
                         nseq, seq_rows, tiles_per_seq, p["tf"])

    wb = p["conv_b_w"].shape[0]
    xb = xg3[:, :, :d_rnn]
    b_conv_new = jnp.concatenate([b_conv_past, xb], axis=1)[:, -(wb - 1):] if t < wb - 1 else xb[:, t - (wb - 1):]
    return y.reshape(b, t, d), k_a, v_a, b_conv_new, h_last.reshape(b, d_rnn), f_new


def kernel(x_prompt, x_sample, cache_a_k, cache_a_v, cache_mem_k, cache_mem_v, cache_b_conv, state_b_h, cache_f_conv, mem_prompt, g_attn, w_in, g_q_a, g_k_a, rel_table, g_q_m, g_k_m, g_mem, w_mem_kv, conv_b_w, conv_b_b, w_rg_a, b_rg_a, w_rg_i, b_rg_i, lru_lambda, w_out, g_ffn, w_up, conv_f_w, conv_f_b, w_down):
    depth = w_in.shape[0]
    bp, sp, d = x_prompt.shape
    bs, ts, _ = x_sample.shape
    n_heads_a = cache_a_k.shape[3]
    n_heads_m = cache_mem_k.shape[3]
    n_mem = mem_prompt.shape[1]
    d_a, d_m = n_heads_a * HEAD_DIM, n_heads_m * HEAD_DIM
    d_rnn = conv_b_w.shape[2]
    d_ff = w_down.shape[1]
    keep = min(WINDOW, sp)
    ones = lambda n: jnp.ones((n,), F32)

    xp, xs = x_prompt, x_sample
    outs = [[] for _ in range(12)]
    for l in range(depth):
        p = dict(
            n_heads_a=n_heads_a, n_heads_m=n_heads_m, d_rnn=d_rnn, tf=512,
            g_attn=g_attn[l], w_in=w_in[l].astype(BF16),
            g_q_a=g_q_a[l] * SCALE, g_k_a=g_k_a[l], g_q_m=g_q_m[l] * SCALE,
            conv_b_w=conv_b_w[l], conv_b_b=conv_b_b[l],
            wa_bd=_block_diag(w_rg_a[l]).astype(BF16), b_rg_a=b_rg_a[l],
            wi_bd=_block_diag(w_rg_i[l]).astype(BF16), b_rg_i=b_rg_i[l], lru_lambda=lru_lambda[l],
            wo_a=w_out[l, :d_a].astype(BF16), wo_m=w_out[l, d_a:d_a + d_m].astype(BF16),
            wo_b=w_out[l, d_a + d_m:].astype(BF16),
            g_ffn=g_ffn[l], w_up=w_up[l].astype(BF16), conv_f_w=conv_f_w[l], conv_f_b=conv_f_b[l],
            w_down=w_down[l].astype(BF16),
        )
        bias = _rel_bias(rel_table[l])

        mem_gains = jnp.concatenate([jnp.tile(g_k_m[l], n_heads_m), ones(d_m)])
        kv = _norm_matmul(mem_prompt.reshape(bp * n_mem, d), g_mem[l], w_mem_kv[l].astype(BF16), mem_gains,
                          ((0, 1),), 512, d_m)
        kv3 = kv.reshape(bp, n_mem, 2 * d_m)

        ffn_tm = 1024
        xp, k_a, v_a, bc, bh, fc = _trunk(
            xp, functools.partial(_band_attn, n_heads=n_heads_a, tq=WINDOW), kv3, kv3,
            jnp.zeros((bp, conv_b_w.shape[1] - 1, d_rnn), F32), jnp.zeros((bp, d_rnn), F32),
            jnp.zeros((bp, conv_f_w.shape[1] - 1, 2 * d_ff), F32), bias, p,
            in_tiles=(1, keep, sp // keep), tq_mem=512, tt=512, ffn_tiles=(1, ffn_tm, sp // ffn_tm))
        mem_k = kv3[:, :, :d_m].reshape(bp, n_mem, n_heads_m, HEAD_DIM)
        mem_v = kv3[:, :, d_m:].reshape(bp, n_mem, n_heads_m, HEAD_DIM)
        for lst, v in zip(outs[:7], (k_a, v_a, mem_k, mem_v, bc, bh, fc)):
            lst.append(v)

        band_s = functools.partial(_band_attn_sample, cache_k=cache_a_k[l], cache_v=cache_a_v[l], n_heads=n_heads_a)
        xs, k_a, v_a, bc, bh, fc = _trunk(
            xs, lambda qkv3, bias_: band_s(qkv3, bias=bias_), cache_mem_k[l], cache_mem_v[l],
            cache_b_conv[l], state_b_h[l], cache_f_conv[l], bias, p,
            in_tiles=(bs, ts, 1), tq_mem=ts, tt=ts, ffn_tiles=(bs, ts, 1))
        for lst, v in zip(outs[7:], (k_a, v_a, bc, bh, fc)):
            lst.append(v)

    stacked = [jnp.stack(o) for o in outs]
    return (xp, xs, *stacked)
```

```python
import functools

import jax
import jax.numpy as jnp
from jax import lax
from jax.experimental import pallas as pl
from jax.experimental.pallas import tpu as pltpu

EPS = 1e-6
HEAD_DIM = 128
CHUNK = 64
LEFT_CHUNKS = 8
WINDOW = LEFT_CHUNKS * CHUNK
REL_CLIP = 256
LRU_C = 8.0
NEG = -1e30
SCALE = HEAD_DIM ** -0.5

LANES = 128
SUBLANES = 8
MXU_COLS = 256
VMEM_LIMIT = 56 * 2 ** 20

QSUB = 4 * CHUNK
KSPAN = WINDOW + QSUB
FFN_COLS = 512

BF16 = jnp.bfloat16
F32 = jnp.float32


def _params(sem):
    return pltpu.CompilerParams(dimension_semantics=sem, vmem_limit_bytes=VMEM_LIMIT)


def _rms(x, g):
    ms = jnp.mean(x * x, axis=-1, keepdims=True)
    return x * lax.rsqrt(ms + EPS) * g


def _norm_matmul_kernel(x_ref, g_ref, w_ref, hg_ref, o_ref, xn_ref, *, norm_ranges, tn):
    j = pl.program_id(1)

    @pl.when(j == 0)
    def _():
        xn_ref[...] = _rms(x_ref[...], g_ref[...]).astype(BF16)

    z = jnp.dot(xn_ref[...], w_ref[...], preferred_element_type=F32)
    cond = None
    for lo, hi in norm_ranges:
        c = jnp.logical_and(j >= lo, j < hi)
        cond = c if cond is None else jnp.logical_or(cond, c)

    @pl.when(cond)
    def _():
        for h in range(tn // HEAD_DIM):
            cs = slice(h * HEAD_DIM, (h + 1) * HEAD_DIM)
            o_ref[:, cs] = _rms(z[:, cs], hg_ref[:, cs])

    @pl.when(jnp.logical_not(cond))
    def _():
        o_ref[...] = z


def _norm_matmul(x, g, w_bf16, head_gains, norm_ranges, tm, tn):
    m, k = x.shape
    n = w_bf16.shape[1]
    kern = functools.partial(_norm_matmul_kernel, norm_ranges=norm_ranges, tn=tn)
    return pl.pallas_call(
        kern,
        grid=(m // tm, n // tn),
        in_specs=[
            pl.BlockSpec((tm, k), lambda i, j: (i, 0)),
            pl.BlockSpec((1, k), lambda i, j: (0, 0)),
            pl.BlockSpec((k, tn), lambda i, j: (0, j)),
            pl.BlockSpec((1, tn), lambda i, j: (0, j)),
        ],
        out_specs=pl.BlockSpec((tm, tn), lambda i, j: (i, j)),
        out_shape=jax.ShapeDtypeStruct((m, n), F32),
        scratch_shapes=[pltpu.VMEM((tm, k), BF16)],
        compiler_params=_params(("parallel", "arbitrary")),
        name="norm_matmul",
    )(x, g.reshape(1, k), w_bf16, head_gains.reshape(1, n))


def _in_proj_kernel(x_ref, g_ref, w_ref, gq_ref, gk_ref, gqm_ref, qkv_ref, qm_ref, xg_ref, ak_ref, av_ref,
                    xn_ref, kv_ref, *, d_a, d_m, d_rnn, group, tiles_per_seq):
    i = pl.program_id(0)
    xn_ref[...] = _rms(x_ref[...], g_ref[...]).astype(BF16)

    def cols(c0):
        return jnp.dot(xn_ref[...], w_ref[:, c0:c0 + group], preferred_element_type=F32)

    def heads(z, c0):
        return [(slice(c0 + h * HEAD_DIM, c0 + (h + 1) * HEAD_DIM), z[:, h * HEAD_DIM:(h + 1) * HEAD_DIM])
                for h in range(group // HEAD_DIM)]

    for c0 in range(0, d_a, group):
        for cs, zh in heads(cols(c0), c0):
            qkv_ref[:, cs] = _rms(zh, gq_ref[...]).astype(BF16)
    for c0 in range(d_a, 2 * d_a, group):
        for cs, zh in heads(cols(c0), c0):
            kh = _rms(zh, gk_ref[...])
            qkv_ref[:, cs] = kh.astype(BF16)
            kv_ref[:, slice(cs.start - d_a, cs.stop - d_a)] = kh
    for c0 in range(2 * d_a, 3 * d_a, group):
        for cs, zh in heads(cols(c0), c0):
            qkv_ref[:, cs] = zh.astype(BF16)
            kv_ref[:, slice(cs.start - d_a, cs.stop - d_a)] = zh
    for c0 in range(0, d_m, group):
        for cs, zh in heads(cols(3 * d_a + c0), c0):
            qm_ref[:, cs] = _rms(zh, gqm_ref[...]).astype(BF16)
    for c0 in range(0, 2 * d_rnn, group):
        xg_ref[:, c0:c0 + group] = cols(3 * d_a + d_m + c0)

    @pl.when(i % tiles_per_seq == tiles_per_seq - 1)
    def _():
        nseq, rows, n_heads = ak_ref.shape[0], ak_ref.shape[1], ak_ref.shape[2]
        for h in range(n_heads):
            cs = slice(h * HEAD_DIM, (h + 1) * HEAD_DIM)
            ak_ref[:, :, h, :] = kv_ref[:, cs].reshape(nseq, rows, HEAD_DIM)
            av_ref[:, :, h, :] = kv_ref[:, slice(d_a + cs.start, d_a + cs.stop)].reshape(nseq, rows, HEAD_DIM)


def _in_proj(x, g, w_bf16, gq, gk, gqm, n_heads_a, n_heads_m, d_rnn, nseq, rows, tiles_per_seq):
    m, k = x.shape
    d_in = w_bf16.shape[1]
    d_a, d_m = n_heads_a * HEAD_DIM, n_heads_m * HEAD_DIM
    tm = nseq * rows
    nt = m // tm
    n_cache_seq = nt * nseq // tiles_per_seq
    kern = functools.partial(_in_proj_kernel, d_a=d_a, d_m=d_m, d_rnn=d_rnn, group=4 * HEAD_DIM,
                             tiles_per_seq=tiles_per_seq)
    row = lambda c: pl.BlockSpec((tm, c), lambda i: (i, 0))
    vec = pl.BlockSpec((1, HEAD_DIM), lambda i: (0, 0))
    cache = pl.BlockSpec((nseq, rows, n_heads_a, HEAD_DIM), lambda i: (i // tiles_per_seq, 0, 0, 0))
    return pl.pallas_call(
        kern,
        grid=(nt,),
        in_specs=[
            row(k),
            pl.BlockSpec((1, k), lambda i: (0, 0)),
            pl.BlockSpec((k, d_in), lambda i: (0, 0), pipeline_mode=pl.Buffered(1)),
            vec, vec, vec,
        ],
        out_specs=[row(3 * d_a), row(d_m), row(2 * d_rnn), cache, cache],
        out_shape=[
            jax.ShapeDtypeStruct((m, 3 * d_a), BF16),
            jax.ShapeDtypeStruct((m, d_m), BF16),
            jax.ShapeDtypeStruct((m, 2 * d_rnn), F32),
            jax.ShapeDtypeStruct((n_cache_seq, rows, n_heads_a, HEAD_DIM), F32),
            jax.ShapeDtypeStruct((n_cache_seq, rows, n_heads_a, HEAD_DIM), F32),
        ],
        scratch_shapes=[pltpu.VMEM((tm, k), BF16), pltpu.VMEM((tm, 2 * d_a), F32)],
        compiler_params=_params(("arbitrary",)),
        name="in_proj",
    )(x, g.reshape(1, k), w_bf16, gq.reshape(1, HEAD_DIM), gk.reshape(1, HEAD_DIM), gqm.reshape(1, HEAD_DIM))


def _rel_bias_kernel(tab_ref, o_ref, *, n_heads, tpad):
    t = tab_ref[...]
    hi = t.astype(BF16)
    r1 = t - hi.astype(F32)
    mid = r1.astype(BF16)
    lo = (r1 - mid.astype(F32)).astype(BF16)
    tix = lax.broadcasted_iota(jnp.int32, (tpad, KSPAN), 0)
    jj = lax.broadcasted_iota(jnp.int32, (1, KSPAN), 1)
    valid = jj < WINDOW + CHUNK
    for i in range(CHUNK):
        idx = jnp.clip(i + WINDOW - jj, -REL_CLIP, REL_CLIP) + REL_CLIP
        onehot = jnp.where(tix == idx, 1.0, 0.0).astype(BF16)
        val = (jnp.dot(hi, onehot, preferred_element_type=F32)
               + jnp.dot(mid, onehot, preferred_element_type=F32)
               + jnp.dot(lo, onehot, preferred_element_type=F32))
        row = jnp.where(valid, val, NEG)
        for ci in range(QSUB // CHUNK):
            o_ref[ci * CHUNK + i] = row if ci == 0 else pltpu.roll(row, ci * CHUNK, 1)


def _rel_bias(rel_table):
    n_heads, tlen = rel_table.shape
    tpad = -(-tlen // LANES) * LANES
    tab = jnp.pad(rel_table, ((0, 0), (0, tpad - tlen)))
    out = pl.pallas_call(
        functools.partial(_rel_bias_kernel, n_heads=n_heads, tpad=tpad),
        out_shape=jax.ShapeDtypeStruct((QSUB, n_heads, KSPAN), F32),
        compiler_params=_params(None),
        name="rel_bias",
    )(tab)
    return jnp.transpose(out, (1, 0, 2))


def _band_attn_kernel(q_ref, kp_ref, kc_ref, vp_ref, vc_ref, bias_ref, o_ref, kcat_ref, vcat_ref,
                      *, n_heads, tq):
    t = pl.program_id(1)
    kcat_ref[0:WINDOW, :] = kp_ref[0]
    kcat_ref[WINDOW:WINDOW + tq, :] = kc_ref[0]
    vcat_ref[0:WINDOW, :] = vp_ref[0]
    vcat_ref[WINDOW:WINDOW + tq, :] = vc_ref[0]

    def run(first_tile):
        def body(s, carry):
            r0 = pl.multiple_of(s * QSUB, QSUB)
            if first_tile:
                kidx = lax.broadcasted_iota(jnp.int32, (1, KSPAN), 1)
                before_start = kidx + r0 < WINDOW
            heads = [slice(h * HEAD_DIM, (h + 1) * HEAD_DIM) for h in range(n_heads)]
            scores = []
            for cs in heads:
                q = q_ref[0, pl.ds(r0, QSUB), cs]
                k = kcat_ref[pl.ds(r0, KSPAN), cs]
                scores.append(lax.dot_general(q, k, (((1,), (1,)), ((), ())), preferred_element_type=F32))
            for h, cs in enumerate(heads):
                sc = scores[h] + bias_ref[h]
                if first_tile:
                    sc = jnp.where(before_start, NEG, sc)
                m = jnp.max(sc, axis=-1, keepdims=True)
                p = jnp.exp(sc - m)
                l = jnp.sum(p, axis=-1, keepdims=True)
                v = vcat_ref[pl.ds(r0, KSPAN), cs]
                o = jnp.dot(p.astype(BF16), v, preferred_element_type=F32) / l
                o_ref[0, pl.ds(r0, QSUB), cs] = o.astype(o_ref.dtype)
            return carry

        lax.fori_loop(0, tq // QSUB, body, 0)

    @pl.when(t == 0)
    def _():
        run(True)

    @pl.when(t > 0)
    def _():
        run(False)


def _band_attn(qkv3, bias, n_heads, tq):
    b, s, _ = qkv3.shape
    d_a = n_heads * HEAD_DIM
    assert tq == WINDOW
    prev = lambda bi, t: (bi, jnp.maximum(t - 1, 0))
    return pl.pallas_call(
        functools.partial(_band_attn_kernel, n_heads=n_heads, tq=tq),
        grid=(b, s // tq),
        in_specs=[
            pl.BlockSpec((1, tq, d_a), lambda bi, t: (bi, t, 0)),
            pl.BlockSpec((1, WINDOW, d_a), lambda bi, t: prev(bi, t) + (1,)),
            pl.BlockSpec((1, tq, d_a), lambda bi, t: (bi, t, 1)),
            pl.BlockSpec((1, WINDOW, d_a), lambda bi, t: prev(bi, t) + (2,)),
            pl.BlockSpec((1, tq, d_a), lambda bi, t: (bi, t, 2)),
            pl.BlockSpec((n_heads, QSUB, KSPAN), lambda bi, t: (0, 0, 0)),
        ],
        out_specs=pl.BlockSpec((1, tq, d_a), lambda bi, t: (bi, t, 0)),
        out_shape=jax.ShapeDtypeStruct((b, s, d_a), BF16),
        scratch_shapes=[pltpu.VMEM((WINDOW + tq, d_a), BF16), pltpu.VMEM((WINDOW + tq, d_a), BF16)],
        compiler_params=_params(("parallel", "parallel")),
        name="band_attn",
    )(qkv3, qkv3, qkv3, qkv3, qkv3, bias)


def _band_attn_sample_kernel(q_ref, kn_ref, vn_ref, ck_hbm, cv_hbm, bias_ref, o_ref, kbuf, vbuf, sem,
                             *, n_heads, t, p_len):
    b = pl.program_id(0)
    slot = b % 2

    def cache_copies(seq, sl):
        cps = []
        for h in range(n_heads):
            cps.append(pltpu.make_async_copy(ck_hbm.at[seq, :, h, :], kbuf.at[sl, h], sem.at[0, sl, h]))
            cps.append(pltpu.make_async_copy(cv_hbm.at[seq, :, h, :], vbuf.at[sl, h], sem.at[1, sl, h]))
        return cps

    @pl.when(b == 0)
    def _():
        for cp in cache_copies(0, 0):
            cp.start()

    @pl.when(b + 1 < pl.num_programs(0))
    def _():
        for cp in cache_copies(b + 1, 1 - slot):
            cp.start()

    for cp in cache_copies(b, slot):
        cp.wait()

    heads = [slice(h * HEAD_DIM, (h + 1) * HEAD_DIM) for h in range(n_heads)]
    dn = (((1,), (1,)), ((), ()))
    scores = []
    for h, cs in enumerate(heads):
        q = q_ref[0, :, cs]
        ck = kbuf[slot, h].astype(BF16)
        scores.append((lax.dot_general(q, ck, dn, preferred_element_type=F32),
                       lax.dot_general(q, kn_ref[0, :, cs], dn, preferred_element_type=F32)))
    for h, cs in enumerate(heads):
        cv = vbuf[slot, h].astype(BF16)
        vn = vn_ref[0, :, cs]
        s1 = scores[h][0] + bias_ref[h, 0:t, 0:p_len]
        s2 = scores[h][1] + bias_ref[h, 0:t, p_len:p_len + t]
        m = jnp.maximum(jnp.max(s1, axis=-1, keepdims=True), jnp.max(s2, axis=-1, keepdims=True))
        p1 = jnp.exp(s1 - m)
        p2 = jnp.exp(s2 - m)
        l = jnp.sum(p1, axis=-1, keepdims=True) + jnp.sum(p2, axis=-1, keepdims=True)
        o = (jnp.dot(p1.astype(BF16), cv, preferred_element_type=F32)
             + jnp.dot(p2.astype(BF16), vn, preferred_element_type=F32)) / l
        o_ref[0, :, cs] = o.astype(o_ref.dtype)


def _band_attn_sample(qkv3, cache_k, cache_v, bias, n_heads):
    b, t, _ = qkv3.shape
    d_a = n_heads * HEAD_DIM
    p_len = cache_k.shape[1]
    assert p_len == WINDOW and t <= CHUNK
    cache = pl.BlockSpec(memory_space=pl.ANY)
    return pl.pallas_call(
        functools.partial(_band_attn_sample_kernel, n_heads=n_heads, t=t, p_len=p_len),
        grid=(b,),
        in_specs=[
            pl.BlockSpec((1, t, d_a), lambda bi: (bi, 0, 0)),
            pl.BlockSpec((1, t, d_a), lambda bi: (bi, 0, 1)),
            pl.BlockSpec((1, t, d_a), lambda bi: (bi, 0, 2)),
            cache, cache,
            pl.BlockSpec((n_heads, QSUB, KSPAN), lambda bi: (0, 0, 0)),
        ],
        out_specs=pl.BlockSpec((1, t, d_a), lambda bi: (bi, 0, 0)),
        out_shape=jax.ShapeDtypeStruct((b, t, d_a), BF16),
        scratch_shapes=[
            pltpu.VMEM((2, n_heads, p_len, HEAD_DIM), cache_k.dtype),
            pltpu.VMEM((2, n_heads, p_len, HEAD_DIM), cache_v.dtype),
            pltpu.SemaphoreType.DMA((2, 2, n_heads)),
        ],
        compiler_params=_params(("arbitrary",)),
        name="band_attn_sample",
    )(qkv3, qkv3, qkv3, cache_k, cache_v, bias)


def _mem_attn_kernel(q_ref, mk_ref, mv_ref, o_ref, *, n_heads, head_axis):
    heads = [slice(h * HEAD_DIM, (h + 1) * HEAD_DIM) for h in range(n_heads)]
    load = (lambda ref, h, cs: ref[0, :, h, :]) if head_axis else (lambda ref, h, cs: ref[0, :, cs])
    scores = [lax.dot_general(q_ref[0, :, cs], load(mk_ref, h, cs).astype(BF16), (((1,), (1,)), ((), ())),
                              preferred_element_type=F32) for h, cs in enumerate(heads)]
    for h, cs in enumerate(heads):
        v = load(mv_ref, h, cs).astype(BF16)
        sc = scores[h]
        m = jnp.max(sc, axis=-1, keepdims=True)
        p = jnp.exp(sc - m)
        l = jnp.sum(p, axis=-1, keepdims=True)
        o = jnp.dot(p.astype(BF16), v, preferred_element_type=F32) / l
        o_ref[0, :, cs] = o.astype(o_ref.dtype)


def _mem_attn(qm3, mem_k, mem_v, n_heads, tm):
    b, s, d_m = qm3.shape
    n_mem = mem_k.shape[1]
    head_axis = mem_k.ndim == 4
    if head_axis:
        kspec = vspec = pl.BlockSpec((1, n_mem, n_heads, HEAD_DIM), lambda bi, t: (bi, 0, 0, 0))
    else:
        kspec = pl.BlockSpec((1, n_mem, d_m), lambda bi, t: (bi, 0, 0))
        vspec = pl.BlockSpec((1, n_mem, d_m), lambda bi, t: (bi, 0, 1))
    return pl.pallas_call(
        functools.partial(_mem_attn_kernel, n_heads=n_heads, head_axis=head_axis),
        grid=(b, s // tm),
        in_specs=[pl.BlockSpec((1, tm, d_m), lambda bi, t: (bi, t, 0)), kspec, vspec],
        out_specs=pl.BlockSpec((1, tm, d_m), lambda bi, t: (bi, t, 0)),
        out_shape=jax.ShapeDtypeStruct((b, s, d_m), BF16),
        compiler_params=_params(("parallel", "parallel")),
        name="mem_attn",
    )(qm3, mem_k, mem_v)


def _rglru_kernel(xb_ref, gb_ref, past_ref, h0_ref, cw_ref, cb_ref, wa_ref, ba_ref, wi_ref, bi_ref, lam_ref,
                  o_ref, hl_ref, buf_ref, a_ref, u_ref, hc_ref, *, tt, width):
    t = pl.program_id(1)
    pad = SUBLANES

    @pl.when(t == 0)
    def _():
        buf_ref[0:pad, :] = past_ref[0]
        hc_ref[...] = h0_ref[0]

    buf_ref[pad:pad + tt, :] = xb_ref[0]
    xc = cb_ref[...] + cw_ref[width - 1:width, :] * xb_ref[0]
    for k in range(1, width):
        xc = xc + cw_ref[width - 1 - k:width - k, :] * buf_ref[pad - k:pad - k + tt, :]
    buf_ref[0:pad, :] = buf_ref[tt:tt + pad, :]

    xcb = xc.astype(BF16)
    def sigmoid(z):
        return 0.5 * jnp.tanh(0.5 * z) + 0.5

    r = sigmoid(jnp.dot(xcb, wa_ref[...], preferred_element_type=F32) + ba_ref[...])
    i = sigmoid(jnp.dot(xcb, wi_ref[...], preferred_element_type=F32) + bi_ref[...])
    log_a = -LRU_C * r * jax.nn.softplus(-lam_ref[...])
    a = jnp.exp(log_a)
    th = jnp.tanh(log_a)
    n = -2.0 * th
    dd = 1.0 - th
    u = jnp.where(n > 0.0, n * lax.rsqrt(n * dd), 0.0) * (i * xc)

    d = a.shape[-1]
    a = a.reshape(tt // SUBLANES, SUBLANES, d)
    u = u.reshape(tt // SUBLANES, SUBLANES, d)
    rowm = lax.broadcasted_iota(jnp.int32, (1, SUBLANES, 1), 1)
    sh = 1
    while sh < SUBLANES:
        keep = rowm >= sh
        a_sh = pltpu.roll(a, sh, 1)
        u_sh = pltpu.roll(u, sh, 1)
        u = jnp.where(keep, a * u_sh + u, u)
        a = jnp.where(keep, a * a_sh, a)
        sh *= 2
    a_ref[...] = a.reshape(tt, d)
    u_ref[...] = u.reshape(tt, d)

    def body(g, h):
        r0 = pl.multiple_of(g * SUBLANES, SUBLANES)
        hr = a_ref[pl.ds(r0, SUBLANES), :] * h + u_ref[pl.ds(r0, SUBLANES), :]
        u_ref[pl.ds(r0, SUBLANES), :] = hr
        return hr[SUBLANES - 1:SUBLANES, :]

    h_last = lax.fori_loop(0, tt // SUBLANES, body, hc_ref[...])
    hc_ref[...] = h_last
    hl_ref[0] = h_last
    o_ref[0] = (u_ref[...] * jax.nn.gelu(gb_ref[0])).astype(o_ref.dtype)


def _rglru(xg3, past8, h0, cw, cb, wa_bd, ba, wi_bd, bi, lam, tt):
    b, s, _ = xg3.shape
    d = cw.shape[1]
    width = cw.shape[0]
    vec = pl.BlockSpec((1, d), lambda bi_, t: (0, 0))
    mat = pl.BlockSpec((d, d), lambda bi_, t: (0, 0))
    return pl.pallas_call(
        functools.partial(_rglru_kernel, tt=tt, width=width),
        grid=(b, s // tt),
        in_specs=[
            pl.BlockSpec((1, tt, d), lambda bi_, t: (bi_, t, 0)),
            pl.BlockSpec((1, tt, d), lambda bi_, t: (bi_, t, 1)),
            pl.BlockSpec((1, SUBLANES, d), lambda bi_, t: (bi_, 0, 0)),
            pl.BlockSpec((1, 1, d), lambda bi_, t: (bi_, 0, 0)),
            pl.BlockSpec((width, d), lambda bi_, t: (0, 0)),
            vec, mat, vec, mat, vec, vec,
        ],
        out_specs=[
            pl.BlockSpec((1, tt, d), lambda bi_, t: (bi_, t, 0)),
            pl.BlockSpec((1, 1, d), lambda bi_, t: (bi_, 0, 0)),
        ],
        out_shape=[jax.ShapeDtypeStruct((b, s, d), BF16), jax.ShapeDtypeStruct((b, 1, d), F32)],
        scratch_shapes=[
            pltpu.VMEM((tt + 2 * SUBLANES, d), F32),
            pltpu.VMEM((tt, d), F32),
            pltpu.VMEM((tt, d), F32),
            pltpu.VMEM((1, d), F32),
        ],
        compiler_params=_params(("parallel", "arbitrary")),
        name="rglru",
    )(xg3, xg3, past8, h0.reshape(b, 1, d), cw, cb.reshape(1, d), wa_bd, ba.reshape(1, d),
      wi_bd, bi.reshape(1, d), lam.reshape(1, d))


def _out_proj_kernel(x_ref, oa_ref, om_ref, ob_ref, wa_ref, wm_ref, wb_ref, o_ref):
    acc = jnp.dot(oa_ref[...], wa_ref[...], preferred_element_type=F32)
    acc = acc + jnp.dot(om_ref[...], wm_ref[...], preferred_element_type=F32)
    acc = acc + jnp.dot(ob_ref[...], wb_ref[...], preferred_element_type=F32)
    o_ref[...] = x_ref[...] + acc


def _out_proj(x, oa, om, ob, wa, wm, wb, tm):
    m, d = x.shape
    row = lambda c: pl.BlockSpec((tm, c), lambda i: (i, 0))
    full = lambda a: pl.BlockSpec(a.shape, lambda i: (0, 0))
    return pl.pallas_call(
        _out_proj_kernel,
        grid=(m // tm,),
        in_specs=[row(d), row(oa.shape[1]), row(om.shape[1]), row(ob.shape[1]), full(wa), full(wm), full(wb)],
        out_specs=row(d),
        out_shape=jax.ShapeDtypeStruct((m, d), F32),
        compiler_params=_params(("parallel",)),
        name="out_proj",
    )(x, oa, om, ob, wa, wm, wb)


def _conv_ffn_kernel(x_ref, g_ref, hist_ref, wv_ref, wg_ref, cw_ref, cb_ref, wd_ref,
                     o_ref, f_ref, xn_ref, bv_ref, bg_ref, car_ref,
                     *, nseq, seq_rows, tiles_per_seq, width):
    i = pl.program_id(0)
    c = pl.program_id(1)
    nc = pl.num_programs(1)
    pad = SUBLANES
    tf = wv_ref.shape[1]
    sub = min(tf, MXU_COLS)
    streams = ((c, wv_ref, bv_ref), (nc + c, wg_ref, bg_ref))

    @pl.when(c == 0)
    def _():
        x = x_ref[...]
        xn_ref[...] = _rms(x, g_ref[...]).astype(BF16)
        o_ref[...] = x

    for j, _, buf_ref in streams:
        if tiles_per_seq == 1:
            buf_ref[:, 0:pad, :] = hist_ref[:, j]
        else:
            first = i % tiles_per_seq == 0

            @pl.when(first)
            def _(j=j, buf_ref=buf_ref):
                buf_ref[:, 0:pad, :] = hist_ref[:, j]

            @pl.when(jnp.logical_not(first))
            def _(j=j, buf_ref=buf_ref):
                buf_ref[:, 0:pad, :] = car_ref[j]

    def up_proj(cs):
        for _, w_ref, buf_ref in streams:
            up = jnp.dot(xn_ref[...], w_ref[:, cs], preferred_element_type=F32)
            buf_ref[:, pad:pad + seq_rows, cs] = up.reshape(nseq, seq_rows, sub)

    def conv(j, buf_ref, cs):
        y = cb_ref[j, :, cs]
        for k in range(width):
            y = y + cw_ref[j, width - 1 - k:width - k, cs] * buf_ref[:, pad - k:pad - k + seq_rows, cs]
        return y

    subs = [slice(s * sub, (s + 1) * sub) for s in range(tf // sub)]
    up_proj(subs[0])
    for s, cs in enumerate(subs):
        if s + 1 < len(subs):
            up_proj(subs[s + 1])
        val = conv(streams[0][0], bv_ref, cs)
        gate = conv(streams[1][0], bg_ref, cs)
        h = (jax.nn.gelu(gate) * val).astype(BF16).reshape(nseq * seq_rows, sub)
        o_ref[...] += jnp.dot(h, wd_ref[cs, :], preferred_element_type=F32)

    for j, _, buf_ref in streams:
        last = buf_ref[:, seq_rows:seq_rows + pad, :]
        f_ref[0, j] = last
        if tiles_per_seq > 1:
            car_ref[j] = last


def _conv_ffn(x, g, f_conv_past, w_up, cw, cb, w_down, nseq, seq_rows, tiles_per_seq, tf):
    m, d = x.shape
    d_ff = w_down.shape[0]
    width = cw.shape[0]
    tm = nseq * seq_rows
    nt = m // tm
    nc = d_ff // tf
    nb = f_conv_past.shape[0]
    hist = jnp.pad(f_conv_past, ((0, 0), (SUBLANES - (width - 1), 0), (0, 0)))
    hist = hist.reshape(nb, SUBLANES, 2 * nc, tf).transpose(0, 2, 1, 3)
    cw3 = cw.reshape(width, 2 * nc, tf).transpose(1, 0, 2)
    cb3 = cb.reshape(2 * nc, 1, tf)
    kern = functools.partial(_conv_ffn_kernel, nseq=nseq, seq_rows=seq_rows,
                             tiles_per_seq=tiles_per_seq, width=width)
    car_shape = (2 * nc, nseq, SUBLANES, tf) if tiles_per_seq > 1 else (1, 1, SUBLANES, LANES)
    y, f = pl.pallas_call(
        kern,
        grid=(nt, nc),
        in_specs=[
            pl.BlockSpec((tm, d), lambda i, c: (i, 0), pipeline_mode=pl.Buffered(1)),
            pl.BlockSpec((1, d), lambda i, c: (0, 0)),
            pl.BlockSpec((nseq, 2 * nc, SUBLANES, tf), lambda i, c: (i // tiles_per_seq, 0, 0, 0)),
            pl.BlockSpec((None, d, tf), lambda i, c: (c, 0, 0)),
            pl.BlockSpec((None, d, tf), lambda i, c: (nc + c, 0, 0)),
            pl.BlockSpec((2 * nc, width, tf), lambda i, c: (0, 0, 0)),
            pl.BlockSpec((2 * nc, 1, tf), lambda i, c: (0, 0, 0)),
            pl.BlockSpec((tf, d), lambda i, c: (c, 0)),
        ],
        out_specs=[
            pl.BlockSpec((tm, d), lambda i, c: (i, 0)),
            pl.BlockSpec((1, 2 * nc, nseq, SUBLANES, tf), lambda i, c: (i, 0, 0, 0, 0)),
        ],
        out_shape=[
            jax.ShapeDtypeStruct((m, d), F32),
            jax.ShapeDtypeStruct((nt, 2 * nc, nseq, SUBLANES, tf), F32),
        ],
        scratch_shapes=[
            pltpu.VMEM((tm, d), BF16),
            pltpu.VMEM((nseq, seq_rows + 2 * SUBLANES, tf), F32),
            pltpu.VMEM((nseq, seq_rows + 2 * SUBLANES, tf), F32),
            pltpu.VMEM(car_shape, F32),
        ],
        compiler_params=_params(("arbitrary", "arbitrary")),
        name="conv_ffn",
    )(x, g.reshape(1, d), hist, w_up, w_up, cw3, cb3, w_down)
    f = f.reshape(nt // tiles_per_seq, tiles_per_seq, 2 * nc, nseq, SUBLANES, tf)[:, -1]
    f = f.transpose(0, 2, 3, 1, 4).reshape(nb, SUBLANES, 2 * d_ff)
    return y, f[:, SUBLANES - (width - 1):]


def _pad_rows_front(a, rows):
    return jnp.pad(a, ((0, 0), (rows - a.shape[1], 0), (0, 0)))


def _block_diag(w):
    n, bi, bo = w.shape
    eye = jnp.eye(n, dtype=w.dtype)
    return (w[:, :, None, :] * eye[:, None, :, None]).reshape(n * bi, n * bo)


def _trunk(x3, band_fn, mem_k, mem_v, b_conv_past, b_h0, f_conv_past, bias, p, *, in_tiles, tq_mem, tt, ffn_tiles):
    b, t, d = x3.shape
    n_heads_a, n_heads_m, d_rnn = p["n_heads_a"], p["n_heads_m"], p["d_rnn"]
    d_a, d_m = n_heads_a * HEAD_DIM, n_heads_m * HEAD_DIM
    x2 = x3.reshape(b * t, d)
    nseq_in, rows_in, tiles_in = in_tiles
    qkv, qm, xg, k_a, v_a = _in_proj(x2, p["g_attn"], p["w_in"], p["g_q_a"], p["g_k_a"], p["g_q_m"],
                                     n_heads_a, n_heads_m, d_rnn, nseq_in, rows_in, tiles_in)
    xg3 = xg.reshape(b, t, 2 * d_rnn)

    o_a = band_fn(qkv.reshape(b, t, 3 * d_a), bias)
    o_m = _mem_attn(qm.reshape(b, t, d_m), mem_k, mem_v, n_heads_m, tq_mem)
    o_b, h_last = _rglru(xg3, _pad_rows_front(b_conv_past, SUBLANES), b_h0, p["conv_b_w"], p["conv_b_b"],
                         p["wa_bd"], p["b_rg_a"], p["wi_bd"], p["b_rg_i"], p["lru_lambda"], tt)
    x1 = _out_proj(x2, o_a.reshape(b * t, d_a), o_m.reshape(b * t, d_m), o_b.reshape(b * t, d_rnn),
                   p["wo_a"], p["wo_m"], p["wo_b"], 512)

    nseq, seq_rows, tiles_per_seq = ffn_tiles
    y, f_new = _conv_ffn(x1, p["g_ffn"], f_conv_past, p["w_up"], p["conv_f_w"], p["conv_f_b"], p["w_down"],
                         nseq, seq_rows, tiles_per_seq, p["tf"])

    wb = p["conv_b_w"].shape[0]
    xb = xg3[:, :, :d_rnn]
    b_conv_new = jnp.concatenate([b_conv_past, xb], axis=1)[:, -(wb - 1):] if t < wb - 1 else xb[:, t - (wb - 1):]
    return y.reshape(b, t, d), k_a, v_a, b_conv_new, h_last.reshape(b, d_rnn), f_new


def kernel(x_prompt, x_sample, cache_a_k, cache_a_v, cache_mem_k, cache_mem_v, cache_b_conv, state_b_h, cache_f_conv, mem_prompt, g_attn, w_in, g_q_a, g_k_a, rel_table, g_q_m, g_k_m, g_mem, w_mem_kv, conv_b_w, conv_b_b, w_rg_a, b_rg_a, w_rg_i, b_rg_i, lru_lambda, w_out, g_ffn, w_up, conv_f_w, conv_f_b, w_down):
    depth = w_in.shape[0]
    bp, sp, d = x_prompt.shape
    bs, ts, _ = x_sample.shape
    n_heads_a = cache_a_k.shape[3]
    n_heads_m = cache_mem_k.shape[3]
    n_mem = mem_prompt.shape[1]
    d_a, d_m = n_heads_a * HEAD_DIM, n_heads_m * HEAD_DIM
    d_rnn = conv_b_w.shape[2]
    d_ff = w_down.shape[1]
    keep = min(WINDOW, sp)
    ones = lambda n: jnp.ones((n,), F32)

    xp, xs = x_prompt, x_sample
    outs = [[] for _ in range(12)]
    for l in range(depth):
        p = dict(
            n_heads_a=n_heads_a, n_heads_m=n_heads_m, d_rnn=d_rnn, tf=FFN_COLS,
            g_attn=g_attn[l], w_in=w_in[l].astype(BF16),
            g_q_a=g_q_a[l] * SCALE, g_k_a=g_k_a[l], g_q_m=g_q_m[l] * SCALE,
            conv_b_w=conv_b_w[l], conv_b_b=conv_b_b[l],
            wa_bd=_block_diag(w_rg_a[l]).astype(BF16), b_rg_a=b_rg_a[l],
            wi_bd=_block_diag(w_rg_i[l]).astype(BF16), b_rg_i=b_rg_i[l], lru_lambda=lru_lambda[l],
            wo_a=w_out[l, :d_a].astype(BF16), wo_m=w_out[l, d_a:d_a + d_m].astype(BF16),
            wo_b=w_out[l, d_a + d_m:].astype(BF16),
            g_ffn=g_ffn[l],
            w_up=w_up[l].astype(BF16).reshape(d, 2 * d_ff // FFN_COLS, FFN_COLS).transpose(1, 0, 2), conv_f_w=conv_f_w[l], conv_f_b=conv_f_b[l],
            w_down=w_down[l].astype(BF16),
        )
        bias = _rel_bias(rel_table[l])

        mem_gains = jnp.concatenate([jnp.tile(g_k_m[l], n_heads_m), ones(d_m)])
        kv = _norm_matmul(mem_prompt.reshape(bp * n_mem, d), g_mem[l], w_mem_kv[l].astype(BF16), mem_gains,
                          ((0, 1),), 512, d_m)
        kv3 = kv.reshape(bp, n_mem, 2 * d_m)

        ffn_tm = 1024
        xp, k_a, v_a, bc, bh, fc = _trunk(
            xp, functools.partial(_band_attn, n_heads=n_heads_a, tq=WINDOW), kv3, kv3,
            jnp.zeros((bp, conv_b_w.shape[1] - 1, d_rnn), F32), jnp.zeros((bp, d_rnn), F32),
            jnp.zeros((bp, conv_f_w.shape[1] - 1, 2 * d_ff), F32), bias, p,
            in_tiles=(1, keep, sp // keep), tq_mem=512, tt=512, ffn_tiles=(1, ffn_tm, sp // ffn_tm))
        mem_k = kv3[:, :, :d_m].reshape(bp, n_mem, n_heads_m, HEAD_DIM)
        mem_v = kv3[:, :, d_m:].reshape(bp, n_mem, n_heads_m, HEAD_DIM)
        for lst, v in zip(outs[:7], (k_a, v_a, mem_k, mem_v, bc, bh, fc)):
            lst.append(v)

        band_s = functools.partial(_band_attn_sample, cache_k=cache_a_k[l], cache_v=cache_a_v[l], n_heads=n_heads_a)
        xs, k_a, v_a, bc, bh, fc = _trunk(
            xs, lambda qkv3, bias_: band_s(qkv3, bias=bias_), cache_mem_k[l], cache_mem_v[l],
            cache_b_conv[l], state_b_h[l], cache_f_conv[l], bias, p,
            in_tiles=(bs, ts, 1), tq_mem=ts, tt=ts, ffn_tiles=(bs, ts, 1))
        for lst, v in zip(outs[7:], (k_a, v_a, bc, bh, fc)):
            lst.append(v)

    stacked = [jnp.stack(o) for o in outs]
    return (xp, xs, *stacked)
```

```python
import functools

import jax
import jax.numpy as jnp
from jax import lax
from jax.experimental import pallas as pl
from jax.experimental.pallas import tpu as pltpu

EPS = 1e-6
HEAD_DIM = 128
CHUNK = 64
LEFT_CHUNKS = 8
WINDOW = LEFT_CHUNKS * CHUNK
REL_CLIP = 256
LRU_C = 8.0
NEG = -1e30
SCALE = HEAD_DIM ** -0.5

LANES = 128
SUBLANES = 8
MXU_COLS = 256
VMEM_LIMIT = 56 * 2 ** 20

QSUB = 4 * CHUNK
KSPAN = WINDOW + QSUB
FFN_COLS = 512

BF16 = jnp.bfloat16
F32 = jnp.float32


def _params(sem):
    return pltpu.CompilerParams(dimension_semantics=sem, vmem_limit_bytes=VMEM_LIMIT)


def _rms(x, g):
    ms = jnp.mean(x * x, axis=-1, keepdims=True)
    return x * lax.rsqrt(ms + EPS) * g


def _norm_matmul_kernel(x_ref, g_ref, w_ref, hg_ref, o_ref, xn_ref, *, norm_ranges, tn):
    j = pl.program_id(1)

    @pl.when(j == 0)
    def _():
        xn_ref[...] = _rms(x_ref[...], g_ref[...]).astype(BF16)

    z = jnp.dot(xn_ref[...], w_ref[...], preferred_element_type=F32)
    cond = None
    for lo, hi in norm_ranges:
        c = jnp.logical_and(j >= lo, j < hi)
        cond = c if cond is None else jnp.logical_or(cond, c)

    @pl.when(cond)
    def _():
        for h in range(tn // HEAD_DIM):
            cs = slice(h * HEAD_DIM, (h + 1) * HEAD_DIM)
            o_ref[:, cs] = _rms(z[:, cs], hg_ref[:, cs])

    @pl.when(jnp.logical_not(cond))
    def _():
        o_ref[...] = z


def _norm_matmul(x, g, w_bf16, head_gains, norm_ranges, tm, tn):
    m, k = x.shape
    n = w_bf16.shape[1]
    kern = functools.partial(_norm_matmul_kernel, norm_ranges=norm_ranges, tn=tn)
    return pl.pallas_call(
        kern,
        grid=(m // tm, n // tn),
        in_specs=[
            pl.BlockSpec((tm, k), lambda i, j: (i, 0)),
            pl.BlockSpec((1, k), lambda i, j: (0, 0)),
            pl.BlockSpec((k, tn), lambda i, j: (0, j)),
            pl.BlockSpec((1, tn), lambda i, j: (0, j)),
        ],
        out_specs=pl.BlockSpec((tm, tn), lambda i, j: (i, j)),
        out_shape=jax.ShapeDtypeStruct((m, n), F32),
        scratch_shapes=[pltpu.VMEM((tm, k), BF16)],
        compiler_params=_params(("parallel", "arbitrary")),
        name="norm_matmul",
    )(x, g.reshape(1, k), w_bf16, head_gains.reshape(1, n))


def _in_proj_kernel(x_ref, g_ref, w_ref, gq_ref, gk_ref, gqm_ref, qkv_ref, qm_ref, xg_ref, ak_ref, av_ref,
                    xn_ref, kv_ref, *, d_a, d_m, d_rnn, group, tiles_per_seq):
    i = pl.program_id(0)
    xn_ref[...] = _rms(x_ref[...], g_ref[...]).astype(BF16)

    def cols(c0):
        return jnp.dot(xn_ref[...], w_ref[:, c0:c0 + group], preferred_element_type=F32)

    def heads(z, c0):
        return [(slice(c0 + h * HEAD_DIM, c0 + (h + 1) * HEAD_DIM), z[:, h * HEAD_DIM:(h + 1) * HEAD_DIM])
                for h in range(group // HEAD_DIM)]

    for c0 in range(0, d_a, group):
        for cs, zh in heads(cols(c0), c0):
            qkv_ref[:, cs] = _rms(zh, gq_ref[...]).astype(BF16)
    for c0 in range(d_a, 2 * d_a, group):
        for cs, zh in heads(cols(c0), c0):
            kh = _rms(zh, gk_ref[...])
            qkv_ref[:, cs] = kh.astype(BF16)
            kv_ref[:, slice(cs.start - d_a, cs.stop - d_a)] = kh
    for c0 in range(2 * d_a, 3 * d_a, group):
        for cs, zh in heads(cols(c0), c0):
            qkv_ref[:, cs] = zh.astype(BF16)
            kv_ref[:, slice(cs.start - d_a, cs.stop - d_a)] = zh
    for c0 in range(0, d_m, group):
        for cs, zh in heads(cols(3 * d_a + c0), c0):
            qm_ref[:, cs] = _rms(zh, gqm_ref[...]).astype(BF16)
    for c0 in range(0, 2 * d_rnn, group):
        xg_ref[:, c0:c0 + group] = cols(3 * d_a + d_m + c0)

    @pl.when(i % tiles_per_seq == tiles_per_seq - 1)
    def _():
        nseq, rows, n_heads = ak_ref.shape[0], ak_ref.shape[1], ak_ref.shape[2]
        for h in range(n_heads):
            cs = slice(h * HEAD_DIM, (h + 1) * HEAD_DIM)
            ak_ref[:, :, h, :] = kv_ref[:, cs].reshape(nseq, rows, HEAD_DIM)
            av_ref[:, :, h, :] = kv_ref[:, slice(d_a + cs.start, d_a + cs.stop)].reshape(nseq, rows, HEAD_DIM)


def _in_proj(x, g, w_bf16, gq, gk, gqm, n_heads_a, n_heads_m, d_rnn, nseq, rows, tiles_per_seq):
    m, k = x.shape
    d_in = w_bf16.shape[1]
    d_a, d_m = n_heads_a * HEAD_DIM, n_heads_m * HEAD_DIM
    tm = nseq * rows
    nt = m // tm
    n_cache_seq = nt * nseq // tiles_per_seq
    kern = functools.partial(_in_proj_kernel, d_a=d_a, d_m=d_m, d_rnn=d_rnn, group=4 * HEAD_DIM,
                             tiles_per_seq=tiles_per_seq)
    row = lambda c: pl.BlockSpec((tm, c), lambda i: (i, 0))
    vec = pl.BlockSpec((1, HEAD_DIM), lambda i: (0, 0))
    cache = pl.BlockSpec((nseq, rows, n_heads_a, HEAD_DIM), lambda i: (i // tiles_per_seq, 0, 0, 0))
    return pl.pallas_call(
        kern,
        grid=(nt,),
        in_specs=[
            row(k),
            pl.BlockSpec((1, k), lambda i: (0, 0)),
            pl.BlockSpec((k, d_in), lambda i: (0, 0), pipeline_mode=pl.Buffered(1)),
            vec, vec, vec,
        ],
        out_specs=[row(3 * d_a), row(d_m), row(2 * d_rnn), cache, cache],
        out_shape=[
            jax.ShapeDtypeStruct((m, 3 * d_a), BF16),
            jax.ShapeDtypeStruct((m, d_m), BF16),
            jax.ShapeDtypeStruct((m, 2 * d_rnn), F32),
            jax.ShapeDtypeStruct((n_cache_seq, rows, n_heads_a, HEAD_DIM), F32),
            jax.ShapeDtypeStruct((n_cache_seq, rows, n_heads_a, HEAD_DIM), F32),
        ],
        scratch_shapes=[pltpu.VMEM((tm, k), BF16), pltpu.VMEM((tm, 2 * d_a), F32)],
        compiler_params=_params(("arbitrary",)),
        name="in_proj",
    )(x, g.reshape(1, k), w_bf16, gq.reshape(1, HEAD_DIM), gk.reshape(1, HEAD_DIM), gqm.reshape(1, HEAD_DIM))


def _rel_bias_kernel(tab_ref, o_ref, *, n_heads, tpad):
    t = tab_ref[...]
    hi = t.astype(BF16)
    r1 = t - hi.astype(F32)
    mid = r1.astype(BF16)
    lo = (r1 - mid.astype(F32)).astype(BF16)
    tix = lax.broadcasted_iota(jnp.int32, (tpad, KSPAN), 0)
    jj = lax.broadcasted_iota(jnp.int32, (1, KSPAN), 1)
    valid = jj < WINDOW + CHUNK
    for i in range(CHUNK):
        idx = jnp.clip(i + WINDOW - jj, -REL_CLIP, REL_CLIP) + REL_CLIP
        onehot = jnp.where(tix == idx, 1.0, 0.0).astype(BF16)
        val = (jnp.dot(hi, onehot, preferred_element_type=F32)
               + jnp.dot(mid, onehot, preferred_element_type=F32)
               + jnp.dot(lo, onehot, preferred_element_type=F32))
        row = jnp.where(valid, val, NEG)
        for ci in range(QSUB // CHUNK):
            o_ref[ci * CHUNK + i] = row if ci == 0 else pltpu.roll(row, ci * CHUNK, 1)


def _rel_bias(rel_table):
    n_heads, tlen = rel_table.shape
    tpad = -(-tlen // LANES) * LANES
    tab = jnp.pad(rel_table, ((0, 0), (0, tpad - tlen)))
    out = pl.pallas_call(
        functools.partial(_rel_bias_kernel, n_heads=n_heads, tpad=tpad),
        out_shape=jax.ShapeDtypeStruct((QSUB, n_heads, KSPAN), F32),
        compiler_params=_params(None),
        name="rel_bias",
    )(tab)
    return jnp.transpose(out, (1, 0, 2))


def _band_attn_kernel(q_ref, kp_ref, kc_ref, vp_ref, vc_ref, bias_ref, o_ref, kcat_ref, vcat_ref,
                      *, n_heads, tq):
    t = pl.program_id(1)
    kcat_ref[0:WINDOW, :] = kp_ref[0]
    kcat_ref[WINDOW:WINDOW + tq, :] = kc_ref[0]
    vcat_ref[0:WINDOW, :] = vp_ref[0]
    vcat_ref[WINDOW:WINDOW + tq, :] = vc_ref[0]

    def run(first_tile):
        def body(s, carry):
            r0 = pl.multiple_of(s * QSUB, QSUB)
            if first_tile:
                kidx = lax.broadcasted_iota(jnp.int32, (1, KSPAN), 1)
                before_start = kidx + r0 < WINDOW
            heads = [slice(h * HEAD_DIM, (h + 1) * HEAD_DIM) for h in range(n_heads)]
            scores = []
            for cs in heads:
                q = q_ref[0, pl.ds(r0, QSUB), cs]
                k = kcat_ref[pl.ds(r0, KSPAN), cs]
                scores.append(lax.dot_general(q, k, (((1,), (1,)), ((), ())), preferred_element_type=F32))
            for h, cs in enumerate(heads):
                sc = scores[h] + bias_ref[h]
                if first_tile:
                    sc = jnp.where(before_start, NEG, sc)
                m = jnp.max(sc, axis=-1, keepdims=True)
                p = jnp.exp(sc - m)
                l = jnp.sum(p, axis=-1, keepdims=True)
                v = vcat_ref[pl.ds(r0, KSPAN), cs]
                o = jnp.dot(p.astype(BF16), v, preferred_element_type=F32) / l
                o_ref[0, pl.ds(r0, QSUB), cs] = o.astype(o_ref.dtype)
            return carry

        lax.fori_loop(0, tq // QSUB, body, 0)

    @pl.when(t == 0)
    def _():
        run(True)

    @pl.when(t > 0)
    def _():
        run(False)


def _band_attn(qkv3, bias, n_heads, tq):
    b, s, _ = qkv3.shape
    d_a = n_heads * HEAD_DIM
    assert tq == WINDOW
    prev = lambda bi, t: (bi, jnp.maximum(t - 1, 0))
    return pl.pallas_call(
        functools.partial(_band_attn_kernel, n_heads=n_heads, tq=tq),
        grid=(b, s // tq),
        in_specs=[
            pl.BlockSpec((1, tq, d_a), lambda bi, t: (bi, t, 0)),
            pl.BlockSpec((1, WINDOW, d_a), lambda bi, t: prev(bi, t) + (1,)),
            pl.BlockSpec((1, tq, d_a), lambda bi, t: (bi, t, 1)),
            pl.BlockSpec((1, WINDOW, d_a), lambda bi, t: prev(bi, t) + (2,)),
            pl.BlockSpec((1, tq, d_a), lambda bi, t: (bi, t, 2)),
            pl.BlockSpec((n_heads, QSUB, KSPAN), lambda bi, t: (0, 0, 0)),
        ],
        out_specs=pl.BlockSpec((1, tq, d_a), lambda bi, t: (bi, t, 0)),
        out_shape=jax.ShapeDtypeStruct((b, s, d_a), BF16),
        scratch_shapes=[pltpu.VMEM((WINDOW + tq, d_a), BF16), pltpu.VMEM((WINDOW + tq, d_a), BF16)],
        compiler_params=_params(("parallel", "parallel")),
        name="band_attn",
    )(qkv3, qkv3, qkv3, qkv3, qkv3, bias)


def _band_attn_sample_kernel(q_ref, kn_ref, vn_ref, ck_hbm, cv_hbm, bias_ref, o_ref, kbuf, vbuf, sem,
                             *, n_heads, t, p_len):
    b = pl.program_id(0)
    slot = b % 2

    def cache_copies(seq, sl):
        cps = []
        for h in range(n_heads):
            cps.append(pltpu.make_async_copy(ck_hbm.at[seq, :, h, :], kbuf.at[sl, h], sem.at[0, sl, h]))
            cps.append(pltpu.make_async_copy(cv_hbm.at[seq, :, h, :], vbuf.at[sl, h], sem.at[1, sl, h]))
        return cps

    @pl.when(b == 0)
    def _():
        for cp in cache_copies(0, 0):
            cp.start()

    @pl.when(b + 1 < pl.num_programs(0))
    def _():
        for cp in cache_copies(b + 1, 1 - slot):
            cp.start()

    for cp in cache_copies(b, slot):
        cp.wait()

    heads = [slice(h * HEAD_DIM, (h + 1) * HEAD_DIM) for h in range(n_heads)]
    dn = (((1,), (1,)), ((), ()))
    scores = []
    for h, cs in enumerate(heads):
        q = q_ref[0, :, cs]
        ck = kbuf[slot, h].astype(BF16)
        scores.append((lax.dot_general(q, ck, dn, preferred_element_type=F32),
                       lax.dot_general(q, kn_ref[0, :, cs], dn, preferred_element_type=F32)))
    for h, cs in enumerate(heads):
        cv = vbuf[slot, h].astype(BF16)
        vn = vn_ref[0, :, cs]
        s1 = scores[h][0] + bias_ref[h, 0:t, 0:p_len]
        s2 = scores[h][1] + bias_ref[h, 0:t, p_len:p_len + t]
        m = jnp.maximum(jnp.max(s1, axis=-1, keepdims=True), jnp.max(s2, axis=-1, keepdims=True))
        p1 = jnp.exp(s1 - m)
        p2 = jnp.exp(s2 - m)
        l = jnp.sum(p1, axis=-1, keepdims=True) + jnp.sum(p2, axis=-1, keepdims=True)
        o = (jnp.dot(p1.astype(BF16), cv, preferred_element_type=F32)
             + jnp.dot(p2.astype(BF16), vn, preferred_element_type=F32)) / l
        o_ref[0, :, cs] = o.astype(o_ref.dtype)


def _band_attn_sample(qkv3, cache_k, cache_v, bias, n_heads):
    b, t, _ = qkv3.shape
    d_a = n_heads * HEAD_DIM
    p_len = cache_k.shape[1]
    assert p_len == WINDOW and t <= CHUNK
    cache = pl.BlockSpec(memory_space=pl.ANY)
    return pl.pallas_call(
        functools.partial(_band_attn_sample_kernel, n_heads=n_heads, t=t, p_len=p_len),
        grid=(b,),
        in_specs=[
            pl.BlockSpec((1, t, d_a), lambda bi: (bi, 0, 0)),
            pl.BlockSpec((1, t, d_a), lambda bi: (bi, 0, 1)),
            pl.BlockSpec((1, t, d_a), lambda bi: (bi, 0, 2)),
            cache, cache,
            pl.BlockSpec((n_heads, QSUB, KSPAN), lambda bi: (0, 0, 0)),
        ],
        out_specs=pl.BlockSpec((1, t, d_a), lambda bi: (bi, 0, 0)),
        out_shape=jax.ShapeDtypeStruct((b, t, d_a), BF16),
        scratch_shapes=[
            pltpu.VMEM((2, n_heads, p_len, HEAD_DIM), cache_k.dtype),
            pltpu.VMEM((2, n_heads, p_len, HEAD_DIM), cache_v.dtype),
            pltpu.SemaphoreType.DMA((2, 2, n_heads)),
        ],
        compiler_params=_params(("arbitrary",)),
        name="band_attn_sample",
    )(qkv3, qkv3, qkv3, cache_k, cache_v, bias)


def _mem_attn_kernel(q_ref, mk_ref, mv_ref, o_ref, *, n_heads, head_axis):
    heads = [slice(h * HEAD_DIM, (h + 1) * HEAD_DIM) for h in range(n_heads)]
    load = (lambda ref, h, cs: ref[0, :, h, :]) if head_axis else (lambda ref, h, cs: ref[0, :, cs])
    scores = [lax.dot_general(q_ref[0, :, cs], load(mk_ref, h, cs).astype(BF16), (((1,), (1,)), ((), ())),
                              preferred_element_type=F32) for h, cs in enumerate(heads)]
    for h, cs in enumerate(heads):
        v = load(mv_ref, h, cs).astype(BF16)
        sc = scores[h]
        m = jnp.max(sc, axis=-1, keepdims=True)
        p = jnp.exp(sc - m)
        l = jnp.sum(p, axis=-1, keepdims=True)
        o = jnp.dot(p.astype(BF16), v, preferred_element_type=F32) / l
        o_ref[0, :, cs] = o.astype(o_ref.dtype)


def _mem_attn(qm3, mem_k, mem_v, n_heads, tm):
    b, s, d_m = qm3.shape
    n_mem = mem_k.shape[1]
    head_axis = mem_k.ndim == 4
    if head_axis:
        kspec = vspec = pl.BlockSpec((1, n_mem, n_heads, HEAD_DIM), lambda bi, t: (bi, 0, 0, 0))
    else:
        kspec = pl.BlockSpec((1, n_mem, d_m), lambda bi, t: (bi, 0, 0))
        vspec = pl.BlockSpec((1, n_mem, d_m), lambda bi, t: (bi, 0, 1))
    return pl.pallas_call(
        functools.partial(_mem_attn_kernel, n_heads=n_heads, head_axis=head_axis),
        grid=(b, s // tm),
        in_specs=[pl.BlockSpec((1, tm, d_m), lambda bi, t: (bi, t, 0)), kspec, vspec],
        out_specs=pl.BlockSpec((1, tm, d_m), lambda bi, t: (bi, t, 0)),
        out_shape=jax.ShapeDtypeStruct((b, s, d_m), BF16),
        compiler_params=_params(("parallel", "parallel")),
        name="mem_attn",
    )(qm3, mem_k, mem_v)


def _rglru_kernel(xb_ref, gb_ref, past_ref, h0_ref, cw_ref, cb_ref, wa_ref, ba_ref, wi_ref, bi_ref, lam_ref,
                  o_ref, hl_ref, buf_ref, a_ref, u_ref, hc_ref, *, tt, width):
    t = pl.program_id(1)
    pad = SUBLANES

    @pl.when(t == 0)
    def _():
        buf_ref[0:pad, :] = past_ref[0]
        hc_ref[...] = h0_ref[0]

    buf_ref[pad:pad + tt, :] = xb_ref[0]
    xc = cb_ref[...] + cw_ref[width - 1:width, :] * xb_ref[0]
    for k in range(1, width):
        xc = xc + cw_ref[width - 1 - k:width - k, :] * buf_ref[pad - k:pad - k + tt, :]
    buf_ref[0:pad, :] = buf_ref[tt:tt + pad, :]

    xcb = xc.astype(BF16)
    def sigmoid(z):
        return 0.5 * jnp.tanh(0.5 * z) + 0.5

    r = sigmoid(jnp.dot(xcb, wa_ref[...], preferred_element_type=F32) + ba_ref[...])
    i = sigmoid(jnp.dot(xcb, wi_ref[...], preferred_element_type=F32) + bi_ref[...])
    log_a = -LRU_C * r * jax.nn.softplus(-lam_ref[...])
    a = jnp.exp(log_a)
    th = jnp.tanh(log_a)
    n = -2.0 * th
    dd = 1.0 - th
    u = jnp.where(n > 0.0, n * lax.rsqrt(n * dd), 0.0) * (i * xc)

    d = a.shape[-1]
    a = a.reshape(tt // SUBLANES, SUBLANES, d)
    u = u.reshape(tt // SUBLANES, SUBLANES, d)
    rowm = lax.broadcasted_iota(jnp.int32, (1, SUBLANES, 1), 1)
    sh = 1
    while sh < SUBLANES:
        keep = rowm >= sh
        a_sh = pltpu.roll(a, sh, 1)
        u_sh = pltpu.roll(u, sh, 1)
        u = jnp.where(keep, a * u_sh + u, u)
        a = jnp.where(keep, a * a_sh, a)
        sh *= 2
    a_ref[...] = a.reshape(tt, d)
    u_ref[...] = u.reshape(tt, d)

    def body(g, h):
        r0 = pl.multiple_of(g * SUBLANES, SUBLANES)
        hr = a_ref[pl.ds(r0, SUBLANES), :] * h + u_ref[pl.ds(r0, SUBLANES), :]
        u_ref[pl.ds(r0, SUBLANES), :] = hr
        return hr[SUBLANES - 1:SUBLANES, :]

    h_last = lax.fori_loop(0, tt // SUBLANES, body, hc_ref[...])
    hc_ref[...] = h_last
    hl_ref[0] = h_last
    o_ref[0] = (u_ref[...] * jax.nn.gelu(gb_ref[0])).astype(o_ref.dtype)


def _rglru(xg3, past8, h0, cw, cb, wa_bd, ba, wi_bd, bi, lam, tt):
    b, s, _ = xg3.shape
    d = cw.shape[1]
    width = cw.shape[0]
    vec = pl.BlockSpec((1, d), lambda bi_, t: (0, 0))
    mat = pl.BlockSpec((d, d), lambda bi_, t: (0, 0))
    return pl.pallas_call(
        functools.partial(_rglru_kernel, tt=tt, width=width),
        grid=(b, s // tt),
        in_specs=[
            pl.BlockSpec((1, tt, d), lambda bi_, t: (bi_, t, 0)),
            pl.BlockSpec((1, tt, d), lambda bi_, t: (bi_, t, 1)),
            pl.BlockSpec((1, SUBLANES, d), lambda bi_, t: (bi_, 0, 0)),
            pl.BlockSpec((1, 1, d), lambda bi_, t: (bi_, 0, 0)),
            pl.BlockSpec((width, d), lambda bi_, t: (0, 0)),
            vec, mat, vec, mat, vec, vec,
        ],
        out_specs=[
            pl.BlockSpec((1, tt, d), lambda bi_, t: (bi_, t, 0)),
            pl.BlockSpec((1, 1, d), lambda bi_, t: (bi_, 0, 0)),
        ],
        out_shape=[jax.ShapeDtypeStruct((b, s, d), BF16), jax.ShapeDtypeStruct((b, 1, d), F32)],
        scratch_shapes=[
            pltpu.VMEM((tt + 2 * SUBLANES, d), F32),
            pltpu.VMEM((tt, d), F32),
            pltpu.VMEM((tt, d), F32),
            pltpu.VMEM((1, d), F32),
        ],
        compiler_params=_params(("parallel", "arbitrary")),
        name="rglru",
    )(xg3, xg3, past8, h0.reshape(b, 1, d), cw, cb.reshape(1, d), wa_bd, ba.reshape(1, d),
      wi_bd, bi.reshape(1, d), lam.reshape(1, d))


def _out_proj_kernel(x_ref, oa_ref, om_ref, ob_ref, wa_ref, wm_ref, wb_ref, o_ref):
    acc = jnp.dot(oa_ref[...], wa_ref[...], preferred_element_type=F32)
    acc = acc + jnp.dot(om_ref[...], wm_ref[...], preferred_element_type=F32)
    acc = acc + jnp.dot(ob_ref[...], wb_ref[...], preferred_element_type=F32)
    o_ref[...] = x_ref[...] + acc


def _out_proj(x, oa, om, ob, wa, wm, wb, tm):
    m, d = x.shape
    row = lambda c: pl.BlockSpec((tm, c), lambda i: (i, 0))
    full = lambda a: pl.BlockSpec(a.shape, lambda i: (0, 0))
    return pl.pallas_call(
        _out_proj_kernel,
        grid=(m // tm,),
        in_specs=[row(d), row(oa.shape[1]), row(om.shape[1]), row(ob.shape[1]), full(wa), full(wm), full(wb)],
        out_specs=row(d),
        out_shape=jax.ShapeDtypeStruct((m, d), F32),
        compiler_params=_params(("parallel",)),
        name="out_proj",
    )(x, oa, om, ob, wa, wm, wb)


def _conv_ffn_kernel(x_hbm, g_ref, hist_ref, wv_ref, wg_ref, cw_ref, cb_ref, wd_ref,
                     o_hbm, f_ref, acc_ref, xn_ref, bv_ref, bg_ref, car_ref, sem_in, sem_out,
                     *, nseq, seq_rows, tiles_per_seq, width):
    i = pl.program_id(0)
    c = pl.program_id(1)
    nt = pl.num_programs(0)
    nc = pl.num_programs(1)
    pad = SUBLANES
    tm = nseq * seq_rows
    tf = wv_ref.shape[1]
    sub = min(tf, MXU_COLS)
    streams = ((c, wv_ref, bv_ref), (nc + c, wg_ref, bg_ref))
    slot = i % 2

    def fetch(tile, sl):
        return pltpu.make_async_copy(x_hbm.at[pl.ds(tile * tm, tm)], acc_ref.at[sl], sem_in.at[sl])

    def drain(tile, sl):
        return pltpu.make_async_copy(acc_ref.at[sl], o_hbm.at[pl.ds(tile * tm, tm)], sem_out.at[sl])

    @pl.when(jnp.logical_and(i == 0, c == 0))
    def _():
        fetch(0, 0).start()

    @pl.when(c == 0)
    def _():
        fetch(i, slot).wait()
        xn_ref[...] = _rms(acc_ref[slot], g_ref[...]).astype(BF16)

    @pl.when(c == 1)
    def _():
        @pl.when(i >= 1)
        def _():
            drain(i - 1, 1 - slot).wait()

        @pl.when(i + 1 < nt)
        def _():
            fetch(i + 1, 1 - slot).start()

    for j, _, buf_ref in streams:
        if tiles_per_seq == 1:
            buf_ref[:, 0:pad, :] = hist_ref[:, j]
        else:
            first = i % tiles_per_seq == 0

            @pl.when(first)
            def _(j=j, buf_ref=buf_ref):
                buf_ref[:, 0:pad, :] = hist_ref[:, j]

            @pl.when(jnp.logical_not(first))
            def _(j=j, buf_ref=buf_ref):
                buf_ref[:, 0:pad, :] = car_ref[j]

    def up_proj(cs):
        for _, w_ref, buf_ref in streams:
            up = jnp.dot(xn_ref[...], w_ref[:, cs], preferred_element_type=F32)
            buf_ref[:, pad:pad + seq_rows, cs] = up.reshape(nseq, seq_rows, sub)

    def conv(j, buf_ref, cs):
        y = cb_ref[j, :, cs]
        for k in range(width):
            y = y + cw_ref[j, width - 1 - k:width - k, cs] * buf_ref[:, pad - k:pad - k + seq_rows, cs]
        return y

    subs = [slice(s * sub, (s + 1) * sub) for s in range(tf // sub)]
    up_proj(subs[0])
    for s, cs in enumerate(subs):
        if s + 1 < len(subs):
            up_proj(subs[s + 1])
        val = conv(streams[0][0], bv_ref, cs)
        gate = conv(streams[1][0], bg_ref, cs)
        h = (jax.nn.gelu(gate) * val).astype(BF16).reshape(tm, sub)
        acc_ref[slot] += jnp.dot(h, wd_ref[cs, :], preferred_element_type=F32)

    for j, _, buf_ref in streams:
        last = buf_ref[:, seq_rows:seq_rows + pad, :]
        f_ref[0, j] = last
        if tiles_per_seq > 1:
            car_ref[j] = last

    @pl.when(c == nc - 1)
    def _():
        drain(i, slot).start()

        @pl.when(i == nt - 1)
        def _():
            drain(i, slot).wait()


def _conv_ffn(x, g, f_conv_past, w_up, cw, cb, w_down, nseq, seq_rows, tiles_per_seq, tf):
    m, d = x.shape
    d_ff = w_down.shape[0]
    width = cw.shape[0]
    tm = nseq * seq_rows
    nt = m // tm
    nc = d_ff // tf
    nb = f_conv_past.shape[0]
    assert nc >= 2 and m == nt * tm
    hist = jnp.pad(f_conv_past, ((0, 0), (SUBLANES - (width - 1), 0), (0, 0)))
    hist = hist.reshape(nb, SUBLANES, 2 * nc, tf).transpose(0, 2, 1, 3)
    cw3 = cw.reshape(width, 2 * nc, tf).transpose(1, 0, 2)
    cb3 = cb.reshape(2 * nc, 1, tf)
    kern = functools.partial(_conv_ffn_kernel, nseq=nseq, seq_rows=seq_rows,
                             tiles_per_seq=tiles_per_seq, width=width)
    car_shape = (2 * nc, nseq, SUBLANES, tf) if tiles_per_seq > 1 else (1, 1, SUBLANES, LANES)
    y, f = pl.pallas_call(
        kern,
        grid=(nt, nc),
        in_specs=[
            pl.BlockSpec(memory_space=pl.ANY),
            pl.BlockSpec((1, d), lambda i, c: (0, 0)),
            pl.BlockSpec((nseq, 2 * nc, SUBLANES, tf), lambda i, c: (i // tiles_per_seq, 0, 0, 0)),
            pl.BlockSpec((d, tf), lambda i, c: (0, c)),
            pl.BlockSpec((d, tf), lambda i, c: (0, nc + c)),
            pl.BlockSpec((2 * nc, width, tf), lambda i, c: (0, 0, 0)),
            pl.BlockSpec((2 * nc, 1, tf), lambda i, c: (0, 0, 0)),
            pl.BlockSpec((tf, d), lambda i, c: (c, 0)),
        ],
        out_specs=[
            pl.BlockSpec(memory_space=pl.ANY),
            pl.BlockSpec((1, 2 * nc, nseq, SUBLANES, tf), lambda i, c: (i, 0, 0, 0, 0)),
        ],
        out_shape=[
            jax.ShapeDtypeStruct((m, d), F32),
            jax.ShapeDtypeStruct((nt, 2 * nc, nseq, SUBLANES, tf), F32),
        ],
        scratch_shapes=[
            pltpu.VMEM((2, tm, d), F32),
            pltpu.VMEM((tm, d), BF16),
            pltpu.VMEM((nseq, seq_rows + 2 * SUBLANES, tf), F32),
            pltpu.VMEM((nseq, seq_rows + 2 * SUBLANES, tf), F32),
            pltpu.VMEM(car_shape, F32),
            pltpu.SemaphoreType.DMA((2,)),
            pltpu.SemaphoreType.DMA((2,)),
        ],
        compiler_params=_params(("arbitrary", "arbitrary")),
        name="conv_ffn",
    )(x, g.reshape(1, d), hist, w_up, w_up, cw3, cb3, w_down)
    f = f.reshape(nt // tiles_per_seq, tiles_per_seq, 2 * nc, nseq, SUBLANES, tf)[:, -1]
    f = f.transpose(0, 2, 3, 1, 4).reshape(nb, SUBLANES, 2 * d_ff)
    return y, f[:, SUBLANES - (width - 1):]


def _pad_rows_front(a, rows):
    return jnp.pad(a, ((0, 0), (rows - a.shape[1], 0), (0, 0)))


def _block_diag(w):
    n, bi, bo = w.shape
    eye = jnp.eye(n, dtype=w.dtype)
    return (w[:, :, None, :] * eye[:, None, :, None]).reshape(n * bi, n * bo)


def _trunk(x3, band_fn, mem_k, mem_v, b_conv_past, b_h0, f_conv_past, bias, p, *, in_tiles, tq_mem, tt, ffn_tiles):
    b, t, d = x3.shape
    n_heads_a, n_heads_m, d_rnn = p["n_heads_a"], p["n_heads_m"], p["d_rnn"]
    d_a, d_m = n_heads_a * HEAD_DIM, n_heads_m * HEAD_DIM
    x2 = x3.reshape(b * t, d)
    nseq_in, rows_in, tiles_in = in_tiles
    qkv, qm, xg, k_a, v_a = _in_proj(x2, p["g_attn"], p["w_in"], p["g_q_a"], p["g_k_a"], p["g_q_m"],
                                     n_heads_a, n_heads_m, d_rnn, nseq_in, rows_in, tiles_in)
    xg3 = xg.reshape(b, t, 2 * d_rnn)

    o_a = band_fn(qkv.reshape(b, t, 3 * d_a), bias)
    o_m = _mem_attn(qm.reshape(b, t, d_m), mem_k, mem_v, n_heads_m, tq_mem)
    o_b, h_last = _rglru(xg3, _pad_rows_front(b_conv_past, SUBLANES), b_h0, p["conv_b_w"], p["conv_b_b"],
                         p["wa_bd"], p["b_rg_a"], p["wi_bd"], p["b_rg_i"], p["lru_lambda"], tt)
    x1 = _out_proj(x2, o_a.reshape(b * t, d_a), o_m.reshape(b * t, d_m), o_b.reshape(b * t, d_rnn),
                   p["wo_a"], p["wo_m"], p["wo_b"], 512)

    nseq, seq_rows, tiles_per_seq = ffn_tiles
    y, f_new = _conv_ffn(x1, p["g_ffn"], f_conv_past, p["w_up"], p["conv_f_w"], p["conv_f_b"], p["w_down"],
                         nseq, seq_rows, tiles_per_seq, p["tf"])

    wb = p["conv_b_w"].shape[0]
    xb = xg3[:, :, :d_rnn]
    b_conv_new = jnp.concatenate([b_conv_past, xb], axis=1)[:, -(wb - 1):] if t < wb - 1 else xb[:, t - (wb - 1):]
    return y.reshape(b, t, d), k_a, v_a, b_conv_new, h_last.reshape(b, d_rnn), f_new


def kernel(x_prompt, x_sample, cache_a_k, cache_a_v, cache_mem_k, cache_mem_v, cache_b_conv, state_b_h, cache_f_conv, mem_prompt, g_attn, w_in, g_q_a, g_k_a, rel_table, g_q_m, g_k_m, g_mem, w_mem_kv, conv_b_w, conv_b_b, w_rg_a, b_rg_a, w_rg_i, b_rg_i, lru_lambda, w_out, g_ffn, w_up, conv_f_w, conv_f_b, w_down):
    depth = w_in.shape[0]
    bp, sp, d = x_prompt.shape
    bs, ts, _ = x_sample.shape
    n_heads_a = cache_a_k.shape[3]
    n_heads_m = cache_mem_k.shape[3]
    n_mem = mem_prompt.shape[1]
    d_a, d_m = n_heads_a * HEAD_DIM, n_heads_m * HEAD_DIM
    d_rnn = conv_b_w.shape[2]
    d_ff = w_down.shape[1]
    keep = min(WINDOW, sp)
    ones = lambda n: jnp.ones((n,), F32)

    xp, xs = x_prompt, x_sample
    outs = [[] for _ in range(12)]
    for l in range(depth):
        p = dict(
            n_heads_a=n_heads_a, n_heads_m=n_heads_m, d_rnn=d_rnn, tf=FFN_COLS,
            g_attn=g_attn[l], w_in=w_in[l].astype(BF16),
            g_q_a=g_q_a[l] * SCALE, g_k_a=g_k_a[l], g_q_m=g_q_m[l] * SCALE,
            conv_b_w=conv_b_w[l], conv_b_b=conv_b_b[l],
            wa_bd=_block_diag(w_rg_a[l]).astype(BF16), b_rg_a=b_rg_a[l],
            wi_bd=_block_diag(w_rg_i[l]).astype(BF16), b_rg_i=b_rg_i[l], lru_lambda=lru_lambda[l],
            wo_a=w_out[l, :d_a].astype(BF16), wo_m=w_out[l, d_a:d_a + d_m].astype(BF16),
            wo_b=w_out[l, d_a + d_m:].astype(BF16),
            g_ffn=g_ffn[l], w_up=w_up[l].astype(BF16), conv_f_w=conv_f_w[l], conv_f_b=conv_f_b[l],
            w_down=w_down[l].astype(BF16),
        )
        bias = _rel_bias(rel_table[l])

        mem_gains = jnp.concatenate([jnp.tile(g_k_m[l], n_heads_m), ones(d_m)])
        kv = _norm_matmul(mem_prompt.reshape(bp * n_mem, d), g_mem[l], w_mem_kv[l].astype(BF16), mem_gains,
                          ((0, 1),), 512, d_m)
        kv3 = kv.reshape(bp, n_mem, 2 * d_m)

        ffn_tm = 1024
        xp, k_a, v_a, bc, bh, fc = _trunk(
            xp, functools.partial(_band_attn, n_heads=n_heads_a, tq=WINDOW), kv3, kv3,
            jnp.zeros((bp, conv_b_w.shape[1] - 1, d_rnn), F32), jnp.zeros((bp, d_rnn), F32),
            jnp.zeros((bp, conv_f_w.shape[1] - 1, 2 * d_ff), F32), bias, p,
            in_tiles=(1, keep, sp // keep), tq_mem=512, tt=512, ffn_tiles=(1, ffn_tm, sp // ffn_tm))
        mem_k = kv3[:, :, :d_m].reshape(bp, n_mem, n_heads_m, HEAD_DIM)
        mem_v = kv3[:, :, d_m:].reshape(bp, n_mem, n_heads_m, HEAD_DIM)
        for lst, v in zip(outs[:7], (k_a, v_a, mem_k, mem_v, bc, bh, fc)):
            lst.append(v)

        band_s = functools.partial(_band_attn_sample, cache_k=cache_a_k[l], cache_v=cache_a_v[l], n_heads=n_heads_a)
        xs, k_a, v_a, bc, bh, fc = _trunk(
            xs, lambda qkv3, bias_: band_s(qkv3, bias=bias_), cache_mem_k[l], cache_mem_v[l],
            cache_b_conv[l], state_b_h[l], cache_f_conv[l], bias, p,
            in_tiles=(bs, ts, 1), tq_mem=ts, tt=ts, ffn_tiles=(bs, ts, 1))
        for lst, v in zip(outs[7:], (k_a, v_a, bc, bh, fc)):
            lst.append(v)

    stacked = [jnp.stack(o) for o in outs]
    return (xp, xs, *stacked)
```

```python
import functools

import jax
import jax.numpy as jnp
from jax import lax
from jax.experimental import pallas as pl
from jax.experimental.pallas import tpu as pltpu

EPS = 1e-6
HEAD_DIM = 128
CHUNK = 64
LEFT_CHUNKS = 8
WINDOW = LEFT_CHUNKS * CHUNK
REL_CLIP = 256
LRU_C = 8.0
NEG = -1e30
SCALE = HEAD_DIM ** -0.5

LANES = 128
SUBLANES = 8
MXU_COLS = 256
VMEM_LIMIT = 56 * 2 ** 20

QSUB = 4 * CHUNK
KSPAN = WINDOW + QSUB
FFN_COLS = 512

BF16 = jnp.bfloat16
F32 = jnp.float32


def _params(sem):
    return pltpu.CompilerParams(dimension_semantics=sem, vmem_limit_bytes=VMEM_LIMIT)


def _rms(x, g):
    ms = jnp.mean(x * x, axis=-1, keepdims=True)
    return x * lax.rsqrt(ms + EPS) * g


def _norm_matmul_kernel(x_ref, g_ref, w_ref, hg_ref, o_ref, xn_ref, *, norm_ranges, tn):
    j = pl.program_id(1)

    @pl.when(j == 0)
    def _():
        xn_ref[...] = _rms(x_ref[...], g_ref[...]).astype(BF16)

    z = jnp.dot(xn_ref[...], w_ref[...], preferred_element_type=F32)
    cond = None
    for lo, hi in norm_ranges:
        c = jnp.logical_and(j >= lo, j < hi)
        cond = c if cond is None else jnp.logical_or(cond, c)

    @pl.when(cond)
    def _():
        for h in range(tn // HEAD_DIM):
            cs = slice(h * HEAD_DIM, (h + 1) * HEAD_DIM)
            o_ref[:, cs] = _rms(z[:, cs], hg_ref[:, cs])

    @pl.when(jnp.logical_not(cond))
    def _():
        o_ref[...] = z


def _norm_matmul(x, g, w_bf16, head_gains, norm_ranges, tm, tn):
    m, k = x.shape
    n = w_bf16.shape[1]
    kern = functools.partial(_norm_matmul_kernel, norm_ranges=norm_ranges, tn=tn)
    return pl.pallas_call(
        kern,
        grid=(m // tm, n // tn),
        in_specs=[
            pl.BlockSpec((tm, k), lambda i, j: (i, 0)),
            pl.BlockSpec((1, k), lambda i, j: (0, 0)),
            pl.BlockSpec((k, tn), lambda i, j: (0, j)),
            pl.BlockSpec((1, tn), lambda i, j: (0, j)),
        ],
        out_specs=pl.BlockSpec((tm, tn), lambda i, j: (i, j)),
        out_shape=jax.ShapeDtypeStruct((m, n), F32),
        scratch_shapes=[pltpu.VMEM((tm, k), BF16)],
        compiler_params=_params(("parallel", "arbitrary")),
        name="norm_matmul",
    )(x, g.reshape(1, k), w_bf16, head_gains.reshape(1, n))


def _in_proj_kernel(x_ref, g_ref, w_ref, gq_ref, gk_ref, gqm_ref, q_ref, k_ref, v_ref, qm_ref, xg_ref, ak_ref, av_ref,
                    xn_ref, kv_ref, *, d_a, d_m, d_rnn, group, tiles_per_seq, transposed):
    i = pl.program_id(0)
    xn_ref[...] = _rms(x_ref[...], g_ref[...]).astype(BF16)

    def cols(c0):
        return jnp.dot(xn_ref[...], w_ref[:, c0:c0 + group], preferred_element_type=F32)

    def heads(z, c0):
        return [(slice(c0 + h * HEAD_DIM, c0 + (h + 1) * HEAD_DIM), z[:, h * HEAD_DIM:(h + 1) * HEAD_DIM])
                for h in range(group // HEAD_DIM)]

    def put(ref, cs, val):
        if transposed:
            ref[cs, :] = val.T.astype(BF16)
        else:
            ref[:, cs] = val.astype(BF16)

    for c0 in range(0, d_a, group):
        for cs, zh in heads(cols(c0), c0):
            q_ref[:, cs] = _rms(zh, gq_ref[...]).astype(BF16)
    for c0 in range(0, d_a, group):
        for cs, zh in heads(cols(d_a + c0), c0):
            kh = _rms(zh, gk_ref[...])
            put(k_ref, cs, kh)
            kv_ref[:, cs] = kh
    for c0 in range(0, d_a, group):
        for cs, zh in heads(cols(2 * d_a + c0), c0):
            v_ref[:, cs] = zh.astype(BF16)
            kv_ref[:, slice(d_a + cs.start, d_a + cs.stop)] = zh
    for c0 in range(0, d_m, group):
        for cs, zh in heads(cols(3 * d_a + c0), c0):
            qm_ref[:, cs] = _rms(zh, gqm_ref[...]).astype(BF16)
    for c0 in range(0, 2 * d_rnn, group):
        xg_ref[:, c0:c0 + group] = cols(3 * d_a + d_m + c0)

    @pl.when(i % tiles_per_seq == tiles_per_seq - 1)
    def _():
        nseq, rows, n_heads = ak_ref.shape[0], ak_ref.shape[1], ak_ref.shape[2]
        for h in range(n_heads):
            cs = slice(h * HEAD_DIM, (h + 1) * HEAD_DIM)
            ak_ref[:, :, h, :] = kv_ref[:, cs].reshape(nseq, rows, HEAD_DIM)
            av_ref[:, :, h, :] = kv_ref[:, slice(d_a + cs.start, d_a + cs.stop)].reshape(nseq, rows, HEAD_DIM)


def _in_proj(x, g, w_bf16, gq, gk, gqm, n_heads_a, n_heads_m, d_rnn, nseq, rows, tiles_per_seq, transposed):
    m, k = x.shape
    d_in = w_bf16.shape[1]
    d_a, d_m = n_heads_a * HEAD_DIM, n_heads_m * HEAD_DIM
    tm = nseq * rows
    nt = m // tm
    n_cache_seq = nt * nseq // tiles_per_seq
    kern = functools.partial(_in_proj_kernel, d_a=d_a, d_m=d_m, d_rnn=d_rnn, group=4 * HEAD_DIM,
                             tiles_per_seq=tiles_per_seq, transposed=transposed)
    row = lambda c: pl.BlockSpec((tm, c), lambda i: (i, 0))
    vec = pl.BlockSpec((1, HEAD_DIM), lambda i: (0, 0))
    cache = pl.BlockSpec((nseq, rows, n_heads_a, HEAD_DIM), lambda i: (i // tiles_per_seq, 0, 0, 0))
    k_spec = pl.BlockSpec((d_a, tm), lambda i: (0, i)) if transposed else row(d_a)
    k_shape = jax.ShapeDtypeStruct((d_a, m) if transposed else (m, d_a), BF16)
    return pl.pallas_call(
        kern,
        grid=(nt,),
        in_specs=[
            row(k),
            pl.BlockSpec((1, k), lambda i: (0, 0)),
            pl.BlockSpec((k, d_in), lambda i: (0, 0), pipeline_mode=pl.Buffered(1)),
            vec, vec, vec,
        ],
        out_specs=[row(d_a), k_spec, row(d_a), row(d_m), row(2 * d_rnn), cache, cache],
        out_shape=[
            jax.ShapeDtypeStruct((m, d_a), BF16),
            k_shape,
            jax.ShapeDtypeStruct((m, d_a), BF16),
            jax.ShapeDtypeStruct((m, d_m), BF16),
            jax.ShapeDtypeStruct((m, 2 * d_rnn), F32),
            jax.ShapeDtypeStruct((n_cache_seq, rows, n_heads_a, HEAD_DIM), F32),
            jax.ShapeDtypeStruct((n_cache_seq, rows, n_heads_a, HEAD_DIM), F32),
        ],
        scratch_shapes=[pltpu.VMEM((tm, k), BF16), pltpu.VMEM((tm, 2 * d_a), F32)],
        compiler_params=_params(("arbitrary",)),
        name="in_proj",
    )(x, g.reshape(1, k), w_bf16, gq.reshape(1, HEAD_DIM), gk.reshape(1, HEAD_DIM), gqm.reshape(1, HEAD_DIM))


def _rel_bias_kernel(tab_ref, o_ref, *, n_heads, tpad):
    t = tab_ref[...]
    hi = t.astype(BF16)
    r1 = t - hi.astype(F32)
    mid = r1.astype(BF16)
    lo = (r1 - mid.astype(F32)).astype(BF16)
    tix = lax.broadcasted_iota(jnp.int32, (tpad, KSPAN), 0)
    jj = lax.broadcasted_iota(jnp.int32, (1, KSPAN), 1)
    valid = jj < WINDOW + CHUNK
    for i in range(CHUNK):
        idx = jnp.clip(i + WINDOW - jj, -REL_CLIP, REL_CLIP) + REL_CLIP
        onehot = jnp.where(tix == idx, 1.0, 0.0).astype(BF16)
        val = (jnp.dot(hi, onehot, preferred_element_type=F32)
               + jnp.dot(mid, onehot, preferred_element_type=F32)
               + jnp.dot(lo, onehot, preferred_element_type=F32))
        row = jnp.where(valid, val, NEG)
        for ci in range(QSUB // CHUNK):
            o_ref[ci * CHUNK + i] = row if ci == 0 else pltpu.roll(row, ci * CHUNK, 1)


def _rel_bias(rel_table):
    n_heads, tlen = rel_table.shape
    tpad = -(-tlen // LANES) * LANES
    tab = jnp.pad(rel_table, ((0, 0), (0, tpad - tlen)))
    out = pl.pallas_call(
        functools.partial(_rel_bias_kernel, n_heads=n_heads, tpad=tpad),
        out_shape=jax.ShapeDtypeStruct((QSUB, n_heads, KSPAN), F32),
        compiler_params=_params(None),
        name="rel_bias",
    )(tab)
    return jnp.transpose(out, (1, 0, 2))


def _band_attn_kernel(q_ref, ktp_ref, ktc_ref, vp_ref, vc_ref, bias_ref, o_ref, ktcat_ref, vcat_ref,
                      *, n_heads, tq):
    t = pl.program_id(1)
    ktcat_ref[:, 0:WINDOW] = ktp_ref[...]
    ktcat_ref[:, WINDOW:WINDOW + tq] = ktc_ref[...]
    vcat_ref[0:WINDOW, :] = vp_ref[0]
    vcat_ref[WINDOW:WINDOW + tq, :] = vc_ref[0]

    def run(first_tile):
        for s in range(tq // QSUB):
            r0 = s * QSUB
            if first_tile:
                before_start = lax.broadcasted_iota(jnp.int32, (1, KSPAN), 1) + r0 < WINDOW
            for h in range(n_heads):
                cs = slice(h * HEAD_DIM, (h + 1) * HEAD_DIM)
                sc = jnp.dot(q_ref[0, r0:r0 + QSUB, cs], ktcat_ref[cs, r0:r0 + KSPAN], preferred_element_type=F32)
                sc = sc + bias_ref[h]
                if first_tile:
                    sc = jnp.where(before_start, NEG, sc)
                m = jnp.max(sc, axis=-1, keepdims=True)
                p = jnp.exp(sc - m)
                l = jnp.sum(p, axis=-1, keepdims=True)
                o = jnp.dot(p.astype(BF16), vcat_ref[r0:r0 + KSPAN, cs], preferred_element_type=F32) / l
                o_ref[0, r0:r0 + QSUB, cs] = o.astype(o_ref.dtype)

    @pl.when(t == 0)
    def _():
        run(True)

    @pl.when(t > 0)
    def _():
        run(False)


def _band_attn(q3, kt, v3, bias, n_heads, tq):
    b, s, d_a = q3.shape
    assert tq == WINDOW and s % tq == 0
    nt = s // tq
    cur = lambda bi, t: (bi, t, 0)
    prev = lambda bi, t: (bi, jnp.maximum(t - 1, 0), 0)
    return pl.pallas_call(
        functools.partial(_band_attn_kernel, n_heads=n_heads, tq=tq),
        grid=(b, nt),
        in_specs=[
            pl.BlockSpec((1, tq, d_a), cur),
            pl.BlockSpec((d_a, WINDOW), lambda bi, t: (0, bi * nt + jnp.maximum(t - 1, 0))),
            pl.BlockSpec((d_a, tq), lambda bi, t: (0, bi * nt + t)),
            pl.BlockSpec((1, WINDOW, d_a), prev),
            pl.BlockSpec((1, tq, d_a), cur),
            pl.BlockSpec((n_heads, QSUB, KSPAN), lambda bi, t: (0, 0, 0)),
        ],
        out_specs=pl.BlockSpec((1, tq, d_a), cur),
        out_shape=jax.ShapeDtypeStruct((b, s, d_a), BF16),
        scratch_shapes=[pltpu.VMEM((d_a, WINDOW + tq), BF16), pltpu.VMEM((WINDOW + tq, d_a), BF16)],
        compiler_params=_params(("parallel", "parallel")),
        name="band_attn",
    )(q3, kt, kt, v3, v3, bias)


def _band_attn_sample_kernel(q_ref, kn_ref, vn_ref, ck_hbm, cv_hbm, bias_ref, o_ref, kbuf, vbuf, sem,
                             *, n_heads, t, p_len):
    b = pl.program_id(0)
    slot = b % 2

    def cache_copies(seq, sl):
        cps = []
        for h in range(n_heads):
            cps.append(pltpu.make_async_copy(ck_hbm.at[seq, :, h, :], kbuf.at[sl, h], sem.at[0, sl, h]))
            cps.append(pltpu.make_async_copy(cv_hbm.at[seq, :, h, :], vbuf.at[sl, h], sem.at[1, sl, h]))
        return cps

    @pl.when(b == 0)
    def _():
        for cp in cache_copies(0, 0):
            cp.start()

    @pl.when(b + 1 < pl.num_programs(0))
    def _():
        for cp in cache_copies(b + 1, 1 - slot):
            cp.start()

    for cp in cache_copies(b, slot):
        cp.wait()

    heads = [slice(h * HEAD_DIM, (h + 1) * HEAD_DIM) for h in range(n_heads)]
    dn = (((1,), (1,)), ((), ()))
    scores = []
    for h, cs in enumerate(heads):
        q = q_ref[0, :, cs]
        ck = kbuf[slot, h].astype(BF16)
        scores.append((lax.dot_general(q, ck, dn, preferred_element_type=F32),
                       lax.dot_general(q, kn_ref[0, :, cs], dn, preferred_element_type=F32)))
    for h, cs in enumerate(heads):
        cv = vbuf[slot, h].astype(BF16)
        vn = vn_ref[0, :, cs]
        s1 = scores[h][0] + bias_ref[h, 0:t, 0:p_len]
        s2 = scores[h][1] + bias_ref[h, 0:t, p_len:p_len + t]
        m = jnp.maximum(jnp.max(s1, axis=-1, keepdims=True), jnp.max(s2, axis=-1, keepdims=True))
        p1 = jnp.exp(s1 - m)
        p2 = jnp.exp(s2 - m)
        l = jnp.sum(p1, axis=-1, keepdims=True) + jnp.sum(p2, axis=-1, keepdims=True)
        o = (jnp.dot(p1.astype(BF16), cv, preferred_element_type=F32)
             + jnp.dot(p2.astype(BF16), vn, preferred_element_type=F32)) / l
        o_ref[0, :, cs] = o.astype(o_ref.dtype)


def _band_attn_sample(q3, k3, v3, cache_k, cache_v, bias, n_heads):
    b, t, d_a = q3.shape
    p_len = cache_k.shape[1]
    assert p_len == WINDOW and t <= CHUNK
    cache = pl.BlockSpec(memory_space=pl.ANY)
    return pl.pallas_call(
        functools.partial(_band_attn_sample_kernel, n_heads=n_heads, t=t, p_len=p_len),
        grid=(b,),
        in_specs=[
            pl.BlockSpec((1, t, d_a), lambda bi: (bi, 0, 0)),
            pl.BlockSpec((1, t, d_a), lambda bi: (bi, 0, 0)),
            pl.BlockSpec((1, t, d_a), lambda bi: (bi, 0, 0)),
            cache, cache,
            pl.BlockSpec((n_heads, QSUB, KSPAN), lambda bi: (0, 0, 0)),
        ],
        out_specs=pl.BlockSpec((1, t, d_a), lambda bi: (bi, 0, 0)),
        out_shape=jax.ShapeDtypeStruct((b, t, d_a), BF16),
        scratch_shapes=[
            pltpu.VMEM((2, n_heads, p_len, HEAD_DIM), cache_k.dtype),
            pltpu.VMEM((2, n_heads, p_len, HEAD_DIM), cache_v.dtype),
            pltpu.SemaphoreType.DMA((2, 2, n_heads)),
        ],
        compiler_params=_params(("arbitrary",)),
        name="band_attn_sample",
    )(q3, k3, v3, cache_k, cache_v, bias)


def _mem_attn_kernel(q_ref, mk_ref, mv_ref, o_ref, *, n_heads, head_axis):
    heads = [slice(h * HEAD_DIM, (h + 1) * HEAD_DIM) for h in range(n_heads)]
    load = (lambda ref, h, cs: ref[0, :, h, :]) if head_axis else (lambda ref, h, cs: ref[0, :, cs])
    scores = [lax.dot_general(q_ref[0, :, cs], load(mk_ref, h, cs).astype(BF16), (((1,), (1,)), ((), ())),
                              preferred_element_type=F32) for h, cs in enumerate(heads)]
    for h, cs in enumerate(heads):
        v = load(mv_ref, h, cs).astype(BF16)
        sc = scores[h]
        m = jnp.max(sc, axis=-1, keepdims=True)
        p = jnp.exp(sc - m)
        l = jnp.sum(p, axis=-1, keepdims=True)
        o = jnp.dot(p.astype(BF16), v, preferred_element_type=F32) / l
        o_ref[0, :, cs] = o.astype(o_ref.dtype)


def _mem_attn(qm3, mem_k, mem_v, n_heads, tm):
    b, s, d_m = qm3.shape
    n_mem = mem_k.shape[1]
    head_axis = mem_k.ndim == 4
    if head_axis:
        kspec = vspec = pl.BlockSpec((1, n_mem, n_heads, HEAD_DIM), lambda bi, t: (bi, 0, 0, 0))
    else:
        kspec = pl.BlockSpec((1, n_mem, d_m), lambda bi, t: (bi, 0, 0))
        vspec = pl.BlockSpec((1, n_mem, d_m), lambda bi, t: (bi, 0, 1))
    return pl.pallas_call(
        functools.partial(_mem_attn_kernel, n_heads=n_heads, head_axis=head_axis),
        grid=(b, s // tm),
        in_specs=[pl.BlockSpec((1, tm, d_m), lambda bi, t: (bi, t, 0)), kspec, vspec],
        out_specs=pl.BlockSpec((1, tm, d_m), lambda bi, t: (bi, t, 0)),
        out_shape=jax.ShapeDtypeStruct((b, s, d_m), BF16),
        compiler_params=_params(("parallel", "parallel")),
        name="mem_attn",
    )(qm3, mem_k, mem_v)


def _rglru_kernel(xb_ref, gb_ref, past_ref, h0_ref, cw_ref, cb_ref, wa_ref, ba_ref, wi_ref, bi_ref, lam_ref,
                  o_ref, hl_ref, buf_ref, a_ref, u_ref, hc_ref, *, tt, width):
    t = pl.program_id(1)
    pad = SUBLANES

    @pl.when(t == 0)
    def _():
        buf_ref[0:pad, :] = past_ref[0]
        hc_ref[...] = h0_ref[0]

    buf_ref[pad:pad + tt, :] = xb_ref[0]
    xc = cb_ref[...] + cw_ref[width - 1:width, :] * xb_ref[0]
    for k in range(1, width):
        xc = xc + cw_ref[width - 1 - k:width - k, :] * buf_ref[pad - k:pad - k + tt, :]
    buf_ref[0:pad, :] = buf_ref[tt:tt + pad, :]

    xcb = xc.astype(BF16)
    def sigmoid(z):
        return 0.5 * jnp.tanh(0.5 * z) + 0.5

    r = sigmoid(jnp.dot(xcb, wa_ref[...], preferred_element_type=F32) + ba_ref[...])
    i = sigmoid(jnp.dot(xcb, wi_ref[...], preferred_element_type=F32) + bi_ref[...])
    log_a = -LRU_C * r * jax.nn.softplus(-lam_ref[...])
    a = jnp.exp(log_a)
    th = jnp.tanh(log_a)
    n = -2.0 * th
    dd = 1.0 - th
    u = jnp.where(n > 0.0, n * lax.rsqrt(n * dd), 0.0) * (i * xc)

    d = a.shape[-1]
    a = a.reshape(tt // SUBLANES, SUBLANES, d)
    u = u.reshape(tt // SUBLANES, SUBLANES, d)
    rowm = lax.broadcasted_iota(jnp.int32, (1, SUBLANES, 1), 1)
    sh = 1
    while sh < SUBLANES:
        keep = rowm >= sh
        a_sh = pltpu.roll(a, sh, 1)
        u_sh = pltpu.roll(u, sh, 1)
        u = jnp.where(keep, a * u_sh + u, u)
        a = jnp.where(keep, a * a_sh, a)
        sh *= 2
    a_ref[...] = a.reshape(tt, d)
    u_ref[...] = u.reshape(tt, d)

    def body(g, h):
        r0 = pl.multiple_of(g * SUBLANES, SUBLANES)
        hr = a_ref[pl.ds(r0, SUBLANES), :] * h + u_ref[pl.ds(r0, SUBLANES), :]
        u_ref[pl.ds(r0, SUBLANES), :] = hr
        return hr[SUBLANES - 1:SUBLANES, :]

    h_last = lax.fori_loop(0, tt // SUBLANES, body, hc_ref[...])
    hc_ref[...] = h_last
    hl_ref[0] = h_last
    o_ref[0] = (u_ref[...] * jax.nn.gelu(gb_ref[0])).astype(o_ref.dtype)


def _rglru(xg3, past8, h0, cw, cb, wa_bd, ba, wi_bd, bi, lam, tt):
    b, s, _ = xg3.shape
    d = cw.shape[1]
    width = cw.shape[0]
    vec = pl.BlockSpec((1, d), lambda bi_, t: (0, 0))
    mat = pl.BlockSpec((d, d), lambda bi_, t: (0, 0))
    return pl.pallas_call(
        functools.partial(_rglru_kernel, tt=tt, width=width),
        grid=(b, s // tt),
        in_specs=[
            pl.BlockSpec((1, tt, d), lambda bi_, t: (bi_, t, 0)),
            pl.BlockSpec((1, tt, d), lambda bi_, t: (bi_, t, 1)),
            pl.BlockSpec((1, SUBLANES, d), lambda bi_, t: (bi_, 0, 0)),
            pl.BlockSpec((1, 1, d), lambda bi_, t: (bi_, 0, 0)),
            pl.BlockSpec((width, d), lambda bi_, t: (0, 0)),
            vec, mat, vec, mat, vec, vec,
        ],
        out_specs=[
            pl.BlockSpec((1, tt, d), lambda bi_, t: (bi_, t, 0)),
            pl.BlockSpec((1, 1, d), lambda bi_, t: (bi_, 0, 0)),
        ],
        out_shape=[jax.ShapeDtypeStruct((b, s, d), BF16), jax.ShapeDtypeStruct((b, 1, d), F32)],
        scratch_shapes=[
            pltpu.VMEM((tt + 2 * SUBLANES, d), F32),
            pltpu.VMEM((tt, d), F32),
            pltpu.VMEM((tt, d), F32),
            pltpu.VMEM((1, d), F32),
        ],
        compiler_params=_params(("parallel", "arbitrary")),
        name="rglru",
    )(xg3, xg3, past8, h0.reshape(b, 1, d), cw, cb.reshape(1, d), wa_bd, ba.reshape(1, d),
      wi_bd, bi.reshape(1, d), lam.reshape(1, d))


def _out_proj_kernel(x_ref, oa_ref, om_ref, ob_ref, wa_ref, wm_ref, wb_ref, o_ref):
    acc = jnp.dot(oa_ref[...], wa_ref[...], preferred_element_type=F32)
    acc = acc + jnp.dot(om_ref[...], wm_ref[...], preferred_element_type=F32)
    acc = acc + jnp.dot(ob_ref[...], wb_ref[...], preferred_element_type=F32)
    o_ref[...] = x_ref[...] + acc


def _out_proj(x, oa, om, ob, wa, wm, wb, tm):
    m, d = x.shape
    row = lambda c: pl.BlockSpec((tm, c), lambda i: (i, 0))
    full = lambda a: pl.BlockSpec(a.shape, lambda i: (0, 0))
    return pl.pallas_call(
        _out_proj_kernel,
        grid=(m // tm,),
        in_specs=[row(d), row(oa.shape[1]), row(om.shape[1]), row(ob.shape[1]), full(wa), full(wm), full(wb)],
        out_specs=row(d),
        out_shape=jax.ShapeDtypeStruct((m, d), F32),
        compiler_params=_params(("parallel",)),
        name="out_proj",
    )(x, oa, om, ob, wa, wm, wb)


def _conv_ffn_kernel(x_hbm, g_ref, hist_ref, wv_ref, wg_ref, cw_ref, cb_ref, wd_ref,
                     o_hbm, f_ref, acc_ref, xn_ref, bv_ref, bg_ref, car_ref, sem_in, sem_out,
                     *, nseq, seq_rows, tiles_per_seq, width):
    i = pl.program_id(0)
    c = pl.program_id(1)
    nt = pl.num_programs(0)
    nc = pl.num_programs(1)
    pad = SUBLANES
    tm = nseq * seq_rows
    tf = wv_ref.shape[1]
    sub = min(tf, MXU_COLS)
    streams = ((c, wv_ref, bv_ref), (nc + c, wg_ref, bg_ref))
    slot = i % 2

    def fetch(tile, sl):
        return pltpu.make_async_copy(x_hbm.at[pl.ds(tile * tm, tm)], acc_ref.at[sl], sem_in.at[sl])

    def drain(tile, sl):
        return pltpu.make_async_copy(acc_ref.at[sl], o_hbm.at[pl.ds(tile * tm, tm)], sem_out.at[sl])

    @pl.when(jnp.logical_and(i == 0, c == 0))
    def _():
        fetch(0, 0).start()

    @pl.when(c == 0)
    def _():
        fetch(i, slot).wait()
        xn_ref[...] = _rms(acc_ref[slot], g_ref[...]).astype(BF16)

    @pl.when(c == 1)
    def _():
        @pl.when(i >= 1)
        def _():
            drain(i - 1, 1 - slot).wait()

        @pl.when(i + 1 < nt)
        def _():
            fetch(i + 1, 1 - slot).start()

    for j, _, buf_ref in streams:
        if tiles_per_seq == 1:
            buf_ref[:, 0:pad, :] = hist_ref[:, j]
        else:
            first = i % tiles_per_seq == 0

            @pl.when(first)
            def _(j=j, buf_ref=buf_ref):
                buf_ref[:, 0:pad, :] = hist_ref[:, j]

            @pl.when(jnp.logical_not(first))
            def _(j=j, buf_ref=buf_ref):
                buf_ref[:, 0:pad, :] = car_ref[j]

    def up_proj(cs):
        for _, w_ref, buf_ref in streams:
            up = jnp.dot(xn_ref[...], w_ref[:, cs], preferred_element_type=F32)
            buf_ref[:, pad:pad + seq_rows, cs] = up.reshape(nseq, seq_rows, sub)

    def conv(j, buf_ref, cs):
        y = cb_ref[j, :, cs]
        for k in range(width):
            y = y + cw_ref[j, width - 1 - k:width - k, cs] * buf_ref[:, pad - k:pad - k + seq_rows, cs]
        return y

    subs = [slice(s * sub, (s + 1) * sub) for s in range(tf // sub)]
    up_proj(subs[0])
    for s, cs in enumerate(subs):
        if s + 1 < len(subs):
            up_proj(subs[s + 1])
        val = conv(streams[0][0], bv_ref, cs)
        gate = conv(streams[1][0], bg_ref, cs)
        h = (jax.nn.gelu(gate) * val).astype(BF16).reshape(tm, sub)
        acc_ref[slot] += jnp.dot(h, wd_ref[cs, :], preferred_element_type=F32)

    for j, _, buf_ref in streams:
        last = buf_ref[:, seq_rows:seq_rows + pad, :]
        f_ref[0, j] = last
        if tiles_per_seq > 1:
            car_ref[j] = last

    @pl.when(c == nc - 1)
    def _():
        drain(i, slot).start()

        @pl.when(i == nt - 1)
        def _():
            drain(i, slot).wait()


def _conv_ffn(x, g, f_conv_past, w_up, cw, cb, w_down, nseq, seq_rows, tiles_per_seq, tf):
    m, d = x.shape
    d_ff = w_down.shape[0]
    width = cw.shape[0]
    tm = nseq * seq_rows
    nt = m // tm
    nc = d_ff // tf
    nb = f_conv_past.shape[0]
    assert nc >= 2 and m == nt * tm
    hist = jnp.pad(f_conv_past, ((0, 0), (SUBLANES - (width - 1), 0), (0, 0)))
    hist = hist.reshape(nb, SUBLANES, 2 * nc, tf).transpose(0, 2, 1, 3)
    cw3 = cw.reshape(width, 2 * nc, tf).transpose(1, 0, 2)
    cb3 = cb.reshape(2 * nc, 1, tf)
    kern = functools.partial(_conv_ffn_kernel, nseq=nseq, seq_rows=seq_rows,
                             tiles_per_seq=tiles_per_seq, width=width)
    car_shape = (2 * nc, nseq, SUBLANES, tf) if tiles_per_seq > 1 else (1, 1, SUBLANES, LANES)
    y, f = pl.pallas_call(
        kern,
        grid=(nt, nc),
        in_specs=[
            pl.BlockSpec(memory_space=pl.ANY),
            pl.BlockSpec((1, d), lambda i, c: (0, 0)),
            pl.BlockSpec((nseq, 2 * nc, SUBLANES, tf), lambda i, c: (i // tiles_per_seq, 0, 0, 0)),
            pl.BlockSpec((d, tf), lambda i, c: (0, c)),
            pl.BlockSpec((d, tf), lambda i, c: (0, nc + c)),
            pl.BlockSpec((2 * nc, width, tf), lambda i, c: (0, 0, 0)),
            pl.BlockSpec((2 * nc, 1, tf), lambda i, c: (0, 0, 0)),
            pl.BlockSpec((tf, d), lambda i, c: (c, 0)),
        ],
        out_specs=[
            pl.BlockSpec(memory_space=pl.ANY),
            pl.BlockSpec((1, 2 * nc, nseq, SUBLANES, tf), lambda i, c: (i, 0, 0, 0, 0)),
        ],
        out_shape=[
            jax.ShapeDtypeStruct((m, d), F32),
            jax.ShapeDtypeStruct((nt, 2 * nc, nseq, SUBLANES, tf), F32),
        ],
        scratch_shapes=[
            pltpu.VMEM((2, tm, d), F32),
            pltpu.VMEM((tm, d), BF16),
            pltpu.VMEM((nseq, seq_rows + 2 * SUBLANES, tf), F32),
            pltpu.VMEM((nseq, seq_rows + 2 * SUBLANES, tf), F32),
            pltpu.VMEM(car_shape, F32),
            pltpu.SemaphoreType.DMA((2,)),
            pltpu.SemaphoreType.DMA((2,)),
        ],
        compiler_params=_params(("arbitrary", "arbitrary")),
        name="conv_ffn",
    )(x, g.reshape(1, d), hist, w_up, w_up, cw3, cb3, w_down)
    f = f.reshape(nt // tiles_per_seq, tiles_per_seq, 2 * nc, nseq, SUBLANES, tf)[:, -1]
    f = f.transpose(0, 2, 3, 1, 4).reshape(nb, SUBLANES, 2 * d_ff)
    return y, f[:, SUBLANES - (width - 1):]


def _pad_rows_front(a, rows):
    return jnp.pad(a, ((0, 0), (rows - a.shape[1], 0), (0, 0)))


def _block_diag(w):
    n, bi, bo = w.shape
    eye = jnp.eye(n, dtype=w.dtype)
    return (w[:, :, None, :] * eye[:, None, :, None]).reshape(n * bi, n * bo)


def _trunk(x3, band_fn, mem_k, mem_v, b_conv_past, b_h0, f_conv_past, p, *, in_tiles, transposed, tq_mem, tt,
           ffn_tiles):
    b, t, d = x3.shape
    n_heads_a, n_heads_m, d_rnn = p["n_heads_a"], p["n_heads_m"], p["d_rnn"]
    d_a, d_m = n_heads_a * HEAD_DIM, n_heads_m * HEAD_DIM
    x2 = x3.reshape(b * t, d)
    nseq_in, rows_in, tiles_in = in_tiles
    q, k, v, qm, xg, k_a, v_a = _in_proj(x2, p["g_attn"], p["w_in"], p["g_q_a"], p["g_k_a"], p["g_q_m"],
                                         n_heads_a, n_heads_m, d_rnn, nseq_in, rows_in, tiles_in, transposed)
    xg3 = xg.reshape(b, t, 2 * d_rnn)

    o_a = band_fn(q, k, v)
    o_m = _mem_attn(qm.reshape(b, t, d_m), mem_k, mem_v, n_heads_m, tq_mem)
    o_b, h_last = _rglru(xg3, _pad_rows_front(b_conv_past, SUBLANES), b_h0, p["conv_b_w"], p["conv_b_b"],
                         p["wa_bd"], p["b_rg_a"], p["wi_bd"], p["b_rg_i"], p["lru_lambda"], tt)
    x1 = _out_proj(x2, o_a.reshape(b * t, d_a), o_m.reshape(b * t, d_m), o_b.reshape(b * t, d_rnn),
                   p["wo_a"], p["wo_m"], p["wo_b"], 512)

    nseq, seq_rows, tiles_per_seq = ffn_tiles
    y, f_new = _conv_ffn(x1, p["g_ffn"], f_conv_past, p["w_up"], p["conv_f_w"], p["conv_f_b"], p["w_down"],
                         nseq, seq_rows, tiles_per_seq, p["tf"])

    wb = p["conv_b_w"].shape[0]
    xb = xg3[:, :, :d_rnn]
    b_conv_new = jnp.concatenate([b_conv_past, xb], axis=1)[:, -(wb - 1):] if t < wb - 1 else xb[:, t - (wb - 1):]
    return y.reshape(b, t, d), k_a, v_a, b_conv_new, h_last.reshape(b, d_rnn), f_new


def kernel(x_prompt, x_sample, cache_a_k, cache_a_v, cache_mem_k, cache_mem_v, cache_b_conv, state_b_h, cache_f_conv, mem_prompt, g_attn, w_in, g_q_a, g_k_a, rel_table, g_q_m, g_k_m, g_mem, w_mem_kv, conv_b_w, conv_b_b, w_rg_a, b_rg_a, w_rg_i, b_rg_i, lru_lambda, w_out, g_ffn, w_up, conv_f_w, conv_f_b, w_down):
    depth = w_in.shape[0]
    bp, sp, d = x_prompt.shape
    bs, ts, _ = x_sample.shape
    n_heads_a = cache_a_k.shape[3]
    n_heads_m = cache_mem_k.shape[3]
    n_mem = mem_prompt.shape[1]
    d_a, d_m = n_heads_a * HEAD_DIM, n_heads_m * HEAD_DIM
    d_rnn = conv_b_w.shape[2]
    d_ff = w_down.shape[1]
    keep = min(WINDOW, sp)
    ones = lambda n: jnp.ones((n,), F32)

    xp, xs = x_prompt, x_sample
    outs = [[] for _ in range(12)]
    for l in range(depth):
        p = dict(
            n_heads_a=n_heads_a, n_heads_m=n_heads_m, d_rnn=d_rnn, tf=FFN_COLS,
            g_attn=g_attn[l], w_in=w_in[l].astype(BF16),
            g_q_a=g_q_a[l] * SCALE, g_k_a=g_k_a[l], g_q_m=g_q_m[l] * SCALE,
            conv_b_w=conv_b_w[l], conv_b_b=conv_b_b[l],
            wa_bd=_block_diag(w_rg_a[l]).astype(BF16), b_rg_a=b_rg_a[l],
            wi_bd=_block_diag(w_rg_i[l]).astype(BF16), b_rg_i=b_rg_i[l], lru_lambda=lru_lambda[l],
            wo_a=w_out[l, :d_a].astype(BF16), wo_m=w_out[l, d_a:d_a + d_m].astype(BF16),
            wo_b=w_out[l, d_a + d_m:].astype(BF16),
            g_ffn=g_ffn[l], w_up=w_up[l].astype(BF16), conv_f_w=conv_f_w[l], conv_f_b=conv_f_b[l],
            w_down=w_down[l].astype(BF16),
        )
        bias = _rel_bias(rel_table[l])

        mem_gains = jnp.concatenate([jnp.tile(g_k_m[l], n_heads_m), ones(d_m)])
        kv = _norm_matmul(mem_prompt.reshape(bp * n_mem, d), g_mem[l], w_mem_kv[l].astype(BF16), mem_gains,
                          ((0, 1),), 512, d_m)
        kv3 = kv.reshape(bp, n_mem, 2 * d_m)

        ffn_tm = 1024
        band_p = lambda q, kt, v: _band_attn(q.reshape(bp, sp, d_a), kt, v.reshape(bp, sp, d_a), bias,
                                             n_heads_a, WINDOW)
        xp, k_a, v_a, bc, bh, fc = _trunk(
            xp, band_p, kv3, kv3,
            jnp.zeros((bp, conv_b_w.shape[1] - 1, d_rnn), F32), jnp.zeros((bp, d_rnn), F32),
            jnp.zeros((bp, conv_f_w.shape[1] - 1, 2 * d_ff), F32), p,
            in_tiles=(1, keep, sp // keep), transposed=True, tq_mem=512, tt=512,
            ffn_tiles=(1, ffn_tm, sp // ffn_tm))
        mem_k = kv3[:, :, :d_m].reshape(bp, n_mem, n_heads_m, HEAD_DIM)
        mem_v = kv3[:, :, d_m:].reshape(bp, n_mem, n_heads_m, HEAD_DIM)
        for lst, v in zip(outs[:7], (k_a, v_a, mem_k, mem_v, bc, bh, fc)):
            lst.append(v)

        band_s = lambda q, k, v: _band_attn_sample(
            q.reshape(bs, ts, d_a), k.reshape(bs, ts, d_a), v.reshape(bs, ts, d_a),
            cache_a_k[l], cache_a_v[l], bias, n_heads_a)
        xs, k_a, v_a, bc, bh, fc = _trunk(
            xs, band_s, cache_mem_k[l], cache_mem_v[l],
            cache_b_conv[l], state_b_h[l], cache_f_conv[l], p,
            in_tiles=(bs, ts, 1), transposed=False, tq_mem=ts, tt=ts, ffn_tiles=(bs, ts, 1))
        for lst, v in zip(outs[7:], (k_a, v_a, bc, bh, fc)):
            lst.append(v)

    stacked = [jnp.stack(o) for o in outs]
    return (xp, xs, *stacked)
```

```python
import functools

import jax
import jax.numpy as jnp
from jax import lax
from jax.experimental import pallas as pl
from jax.experimental.pallas import tpu as pltpu

EPS = 1e-6
HEAD_DIM = 128
CHUNK = 64
LEFT_CHUNKS = 8
WINDOW = LEFT_CHUNKS * CHUNK
REL_CLIP = 256
LRU_C = 8.0
NEG = -1e30
SCALE = HEAD_DIM ** -0.5

LANES = 128
SUBLANES = 8
MXU_COLS = 256
VMEM_LIMIT = 56 * 2 ** 20

QSUB = 4 * CHUNK
KSPAN = WINDOW + QSUB
FFN_COLS = 512

BF16 = jnp.bfloat16
F32 = jnp.float32


def _params(sem):
    return pltpu.CompilerParams(dimension_semantics=sem, vmem_limit_bytes=VMEM_LIMIT)


def _rms(x, g):
    ms = jnp.mean(x * x, axis=-1, keepdims=True)
    return x * lax.rsqrt(ms + EPS) * g


def _norm_matmul_kernel(x_ref, g_ref, w_ref, hg_ref, o_ref, xn_ref, *, norm_ranges, tn):
    j = pl.program_id(1)

    @pl.when(j == 0)
    def _():
        xn_ref[...] = _rms(x_ref[...], g_ref[...]).astype(BF16)

    z = jnp.dot(xn_ref[...], w_ref[...], preferred_element_type=F32)
    cond = None
    for lo, hi in norm_ranges:
        c = jnp.logical_and(j >= lo, j < hi)
        cond = c if cond is None else jnp.logical_or(cond, c)

    @pl.when(cond)
    def _():
        for h in range(tn // HEAD_DIM):
            cs = slice(h * HEAD_DIM, (h + 1) * HEAD_DIM)
            o_ref[:, cs] = _rms(z[:, cs], hg_ref[:, cs])

    @pl.when(jnp.logical_not(cond))
    def _():
        o_ref[...] = z


def _norm_matmul(x, g, w_bf16, head_gains, norm_ranges, tm, tn):
    m, k = x.shape
    n = w_bf16.shape[1]
    kern = functools.partial(_norm_matmul_kernel, norm_ranges=norm_ranges, tn=tn)
    return pl.pallas_call(
        kern,
        grid=(m // tm, n // tn),
        in_specs=[
            pl.BlockSpec((tm, k), lambda i, j: (i, 0)),
            pl.BlockSpec((1, k), lambda i, j: (0, 0)),
            pl.BlockSpec((k, tn), lambda i, j: (0, j)),
            pl.BlockSpec((1, tn), lambda i, j: (0, j)),
        ],
        out_specs=pl.BlockSpec((tm, tn), lambda i, j: (i, j)),
        out_shape=jax.ShapeDtypeStruct((m, n), F32),
        scratch_shapes=[pltpu.VMEM((tm, k), BF16)],
        compiler_params=_params(("parallel", "arbitrary")),
        name="norm_matmul",
    )(x, g.reshape(1, k), w_bf16, head_gains.reshape(1, n))


def _in_proj_kernel(x_ref, g_ref, w_ref, gq_ref, gk_ref, gqm_ref, q_ref, k_ref, v_ref, qm_ref, xg_ref, ak_ref, av_ref,
                    xn_ref, kv_ref, *, d_a, d_m, d_rnn, group, tiles_per_seq, transposed):
    i = pl.program_id(0)
    xn_ref[...] = _rms(x_ref[...], g_ref[...]).astype(BF16)

    def cols(c0):
        return jnp.dot(xn_ref[...], w_ref[:, c0:c0 + group], preferred_element_type=F32)

    def heads(z, c0):
        return [(slice(c0 + h * HEAD_DIM, c0 + (h + 1) * HEAD_DIM), z[:, h * HEAD_DIM:(h + 1) * HEAD_DIM])
                for h in range(group // HEAD_DIM)]

    def put(ref, cs, val):
        if transposed:
            ref[cs, :] = val.T.astype(BF16)
        else:
            ref[:, cs] = val.astype(BF16)

    for c0 in range(0, d_a, group):
        for cs, zh in heads(cols(c0), c0):
            q_ref[:, cs] = _rms(zh, gq_ref[...]).astype(BF16)
    for c0 in range(0, d_a, group):
        for cs, zh in heads(cols(d_a + c0), c0):
            kh = _rms(zh, gk_ref[...])
            put(k_ref, cs, kh)
            kv_ref[:, cs] = kh
    for c0 in range(0, d_a, group):
        for cs, zh in heads(cols(2 * d_a + c0), c0):
            v_ref[:, cs] = zh.astype(BF16)
            kv_ref[:, slice(d_a + cs.start, d_a + cs.stop)] = zh
    for c0 in range(0, d_m, group):
        for cs, zh in heads(cols(3 * d_a + c0), c0):
            qm_ref[:, cs] = _rms(zh, gqm_ref[...]).astype(BF16)
    for c0 in range(0, 2 * d_rnn, group):
        xg_ref[:, c0:c0 + group] = cols(3 * d_a + d_m + c0)

    @pl.when(i % tiles_per_seq == tiles_per_seq - 1)
    def _():
        nseq, rows, n_heads = ak_ref.shape[0], ak_ref.shape[1], ak_ref.shape[2]
        for h in range(n_heads):
            cs = slice(h * HEAD_DIM, (h + 1) * HEAD_DIM)
            ak_ref[:, :, h, :] = kv_ref[:, cs].reshape(nseq, rows, HEAD_DIM)
            av_ref[:, :, h, :] = kv_ref[:, slice(d_a + cs.start, d_a + cs.stop)].reshape(nseq, rows, HEAD_DIM)


def _in_proj(x, g, w_bf16, gq, gk, gqm, n_heads_a, n_heads_m, d_rnn, nseq, rows, tiles_per_seq, transposed):
    m, k = x.shape
    d_in = w_bf16.shape[1]
    d_a, d_m = n_heads_a * HEAD_DIM, n_heads_m * HEAD_DIM
    tm = nseq * rows
    nt = m // tm
    n_cache_seq = nt * nseq // tiles_per_seq
    kern = functools.partial(_in_proj_kernel, d_a=d_a, d_m=d_m, d_rnn=d_rnn, group=4 * HEAD_DIM,
                             tiles_per_seq=tiles_per_seq, transposed=transposed)
    row = lambda c: pl.BlockSpec((tm, c), lambda i: (i, 0))
    vec = pl.BlockSpec((1, HEAD_DIM), lambda i: (0, 0))
    cache = pl.BlockSpec((nseq, rows, n_heads_a, HEAD_DIM), lambda i: (i // tiles_per_seq, 0, 0, 0))
    k_spec = pl.BlockSpec((d_a, tm), lambda i: (0, i)) if transposed else row(d_a)
    k_shape = jax.ShapeDtypeStruct((d_a, m) if transposed else (m, d_a), BF16)
    return pl.pallas_call(
        kern,
        grid=(nt,),
        in_specs=[
            row(k),
            pl.BlockSpec((1, k), lambda i: (0, 0)),
            pl.BlockSpec((k, d_in), lambda i: (0, 0), pipeline_mode=pl.Buffered(1)),
            vec, vec, vec,
        ],
        out_specs=[row(d_a), k_spec, row(d_a), row(d_m), row(2 * d_rnn), cache, cache],
        out_shape=[
            jax.ShapeDtypeStruct((m, d_a), BF16),
            k_shape,
            jax.ShapeDtypeStruct((m, d_a), BF16),
            jax.ShapeDtypeStruct((m, d_m), BF16),
            jax.ShapeDtypeStruct((m, 2 * d_rnn), F32),
            jax.ShapeDtypeStruct((n_cache_seq, rows, n_heads_a, HEAD_DIM), F32),
            jax.ShapeDtypeStruct((n_cache_seq, rows, n_heads_a, HEAD_DIM), F32),
        ],
        scratch_shapes=[pltpu.VMEM((tm, k), BF16), pltpu.VMEM((tm, 2 * d_a), F32)],
        compiler_params=_params(("arbitrary",)),
        name="in_proj",
    )(x, g.reshape(1, k), w_bf16, gq.reshape(1, HEAD_DIM), gk.reshape(1, HEAD_DIM), gqm.reshape(1, HEAD_DIM))


def _rel_bias_kernel(tab_ref, o_ref, *, n_heads, tpad):
    t = tab_ref[...]
    hi = t.astype(BF16)
    r1 = t - hi.astype(F32)
    mid = r1.astype(BF16)
    lo = (r1 - mid.astype(F32)).astype(BF16)
    tix = lax.broadcasted_iota(jnp.int32, (tpad, KSPAN), 0)
    jj = lax.broadcasted_iota(jnp.int32, (1, KSPAN), 1)
    valid = jj < WINDOW + CHUNK
    for i in range(CHUNK):
        idx = jnp.clip(i + WINDOW - jj, -REL_CLIP, REL_CLIP) + REL_CLIP
        onehot = jnp.where(tix == idx, 1.0, 0.0).astype(BF16)
        val = (jnp.dot(hi, onehot, preferred_element_type=F32)
               + jnp.dot(mid, onehot, preferred_element_type=F32)
               + jnp.dot(lo, onehot, preferred_element_type=F32))
        row = jnp.where(valid, val, NEG)
        for ci in range(QSUB // CHUNK):
            o_ref[ci * CHUNK + i] = row if ci == 0 else pltpu.roll(row, ci * CHUNK, 1)


def _rel_bias(rel_table):
    n_heads, tlen = rel_table.shape
    tpad = -(-tlen // LANES) * LANES
    tab = jnp.pad(rel_table, ((0, 0), (0, tpad - tlen)))
    out = pl.pallas_call(
        functools.partial(_rel_bias_kernel, n_heads=n_heads, tpad=tpad),
        out_shape=jax.ShapeDtypeStruct((QSUB, n_heads, KSPAN), F32),
        compiler_params=_params(None),
        name="rel_bias",
    )(tab)
    return jnp.transpose(out, (1, 0, 2))


def _band_attn_kernel(q_ref, ktp_ref, ktc_ref, vp_ref, vc_ref, bias_ref, o_ref, ktcat_ref, vcat_ref,
                      *, n_heads, tq):
    t = pl.program_id(1)
    ktcat_ref[:, 0:WINDOW] = ktp_ref[...]
    ktcat_ref[:, WINDOW:WINDOW + tq] = ktc_ref[...]
    vcat_ref[0:WINDOW, :] = vp_ref[0]
    vcat_ref[WINDOW:WINDOW + tq, :] = vc_ref[0]

    def run(first_tile):
        for s in range(tq // QSUB):
            r0 = s * QSUB
            if first_tile:
                before_start = lax.broadcasted_iota(jnp.int32, (1, KSPAN), 1) + r0 < WINDOW
            for h in range(n_heads):
                cs = slice(h * HEAD_DIM, (h + 1) * HEAD_DIM)
                sc = jnp.dot(q_ref[0, r0:r0 + QSUB, cs], ktcat_ref[cs, r0:r0 + KSPAN], preferred_element_type=F32)
                sc = sc + bias_ref[h]
                if first_tile:
                    sc = jnp.where(before_start, NEG, sc)
                m = jnp.max(sc, axis=-1, keepdims=True)
                p = jnp.exp(sc - m)
                l = jnp.sum(p, axis=-1, keepdims=True)
                o = jnp.dot(p.astype(BF16), vcat_ref[r0:r0 + KSPAN, cs], preferred_element_type=F32) / l
                o_ref[0, r0:r0 + QSUB, cs] = o.astype(o_ref.dtype)

    @pl.when(t == 0)
    def _():
        run(True)

    @pl.when(t > 0)
    def _():
        run(False)


def _band_attn(q3, kt, v3, bias, n_heads, tq):
    b, s, d_a = q3.shape
    assert tq == WINDOW and s % tq == 0
    nt = s // tq
    cur = lambda bi, t: (bi, t, 0)
    prev = lambda bi, t: (bi, jnp.maximum(t - 1, 0), 0)
    return pl.pallas_call(
        functools.partial(_band_attn_kernel, n_heads=n_heads, tq=tq),
        grid=(b, nt),
        in_specs=[
            pl.BlockSpec((1, tq, d_a), cur),
            pl.BlockSpec((d_a, WINDOW), lambda bi, t: (0, bi * nt + jnp.maximum(t - 1, 0))),
            pl.BlockSpec((d_a, tq), lambda bi, t: (0, bi * nt + t)),
            pl.BlockSpec((1, WINDOW, d_a), prev),
            pl.BlockSpec((1, tq, d_a), cur),
            pl.BlockSpec((n_heads, QSUB, KSPAN), lambda bi, t: (0, 0, 0)),
        ],
        out_specs=pl.BlockSpec((1, tq, d_a), cur),
        out_shape=jax.ShapeDtypeStruct((b, s, d_a), BF16),
        scratch_shapes=[pltpu.VMEM((d_a, WINDOW + tq), BF16), pltpu.VMEM((WINDOW + tq, d_a), BF16)],
        compiler_params=_params(("parallel", "parallel")),
        name="band_attn",
    )(q3, kt, kt, v3, v3, bias)


def _band_attn_sample_kernel(q_ref, kn_ref, vn_ref, ck_hbm, cv_hbm, bias_ref, o_ref, kbuf, vbuf, sem,
                             *, n_heads, t, p_len):
    b = pl.program_id(0)
    slot = b % 2

    def cache_copies(seq, sl):
        cps = []
        for h in range(n_heads):
            cps.append(pltpu.make_async_copy(ck_hbm.at[seq, :, h, :], kbuf.at[sl, h], sem.at[0, sl, h]))
            cps.append(pltpu.make_async_copy(cv_hbm.at[seq, :, h, :], vbuf.at[sl, h], sem.at[1, sl, h]))
        return cps

    @pl.when(b == 0)
    def _():
        for cp in cache_copies(0, 0):
            cp.start()

    @pl.when(b + 1 < pl.num_programs(0))
    def _():
        for cp in cache_copies(b + 1, 1 - slot):
            cp.start()

    for cp in cache_copies(b, slot):
        cp.wait()

    heads = [slice(h * HEAD_DIM, (h + 1) * HEAD_DIM) for h in range(n_heads)]
    dn = (((1,), (1,)), ((), ()))
    scores = []
    for h, cs in enumerate(heads):
        q = q_ref[0, :, cs]
        ck = kbuf[slot, h].astype(BF16)
        scores.append((lax.dot_general(q, ck, dn, preferred_element_type=F32),
                       lax.dot_general(q, kn_ref[0, :, cs], dn, preferred_element_type=F32)))
    for h, cs in enumerate(heads):
        cv = vbuf[slot, h].astype(BF16)
        vn = vn_ref[0, :, cs]
        s1 = scores[h][0] + bias_ref[h, 0:t, 0:p_len]
        s2 = scores[h][1] + bias_ref[h, 0:t, p_len:p_len + t]
        m = jnp.maximum(jnp.max(s1, axis=-1, keepdims=True), jnp.max(s2, axis=-1, keepdims=True))
        p1 = jnp.exp(s1 - m)
        p2 = jnp.exp(s2 - m)
        l = jnp.sum(p1, axis=-1, keepdims=True) + jnp.sum(p2, axis=-1, keepdims=True)
        o = (jnp.dot(p1.astype(BF16), cv, preferred_element_type=F32)
             + jnp.dot(p2.astype(BF16), vn, preferred_element_type=F32)) / l
        o_ref[0, :, cs] = o.astype(o_ref.dtype)


def _band_attn_sample(q3, k3, v3, cache_k, cache_v, bias, n_heads):
    b, t, d_a = q3.shape
    p_len = cache_k.shape[1]
    assert p_len == WINDOW and t <= CHUNK
    cache = pl.BlockSpec(memory_space=pl.ANY)
    return pl.pallas_call(
        functools.partial(_band_attn_sample_kernel, n_heads=n_heads, t=t, p_len=p_len),
        grid=(b,),
        in_specs=[
            pl.BlockSpec((1, t, d_a), lambda bi: (bi, 0, 0)),
            pl.BlockSpec((1, t, d_a), lambda bi: (bi, 0, 0)),
            pl.BlockSpec((1, t, d_a), lambda bi: (bi, 0, 0)),
            cache, cache,
            pl.BlockSpec((n_heads, QSUB, KSPAN), lambda bi: (0, 0, 0)),
        ],
        out_specs=pl.BlockSpec((1, t, d_a), lambda bi: (bi, 0, 0)),
        out_shape=jax.ShapeDtypeStruct((b, t, d_a), BF16),
        scratch_shapes=[
            pltpu.VMEM((2, n_heads, p_len, HEAD_DIM), cache_k.dtype),
            pltpu.VMEM((2, n_heads, p_len, HEAD_DIM), cache_v.dtype),
            pltpu.SemaphoreType.DMA((2, 2, n_heads)),
        ],
        compiler_params=_params(("arbitrary",)),
        name="band_attn_sample",
    )(q3, k3, v3, cache_k, cache_v, bias)


def _mem_attn_kernel(q_ref, mk_ref, mv_ref, o_ref, *, n_heads, head_axis):
    heads = [slice(h * HEAD_DIM, (h + 1) * HEAD_DIM) for h in range(n_heads)]
    load = (lambda ref, h, cs: ref[0, :, h, :]) if head_axis else (lambda ref, h, cs: ref[0, :, cs])
    scores = [lax.dot_general(q_ref[0, :, cs], load(mk_ref, h, cs).astype(BF16), (((1,), (1,)), ((), ())),
                              preferred_element_type=F32) for h, cs in enumerate(heads)]
    for h, cs in enumerate(heads):
        v = load(mv_ref, h, cs).astype(BF16)
        sc = scores[h]
        m = jnp.max(sc, axis=-1, keepdims=True)
        p = jnp.exp(sc - m)
        l = jnp.sum(p, axis=-1, keepdims=True)
        o = jnp.dot(p.astype(BF16), v, preferred_element_type=F32) / l
        o_ref[0, :, cs] = o.astype(o_ref.dtype)


def _mem_attn(qm3, mem_k, mem_v, n_heads, tm):
    b, s, d_m = qm3.shape
    n_mem = mem_k.shape[1]
    head_axis = mem_k.ndim == 4
    if head_axis:
        kspec = vspec = pl.BlockSpec((1, n_mem, n_heads, HEAD_DIM), lambda bi, t: (bi, 0, 0, 0))
    else:
        kspec = pl.BlockSpec((1, n_mem, d_m), lambda bi, t: (bi, 0, 0))
        vspec = pl.BlockSpec((1, n_mem, d_m), lambda bi, t: (bi, 0, 1))
    return pl.pallas_call(
        functools.partial(_mem_attn_kernel, n_heads=n_heads, head_axis=head_axis),
        grid=(b, s // tm),
        in_specs=[pl.BlockSpec((1, tm, d_m), lambda bi, t: (bi, t, 0)), kspec, vspec],
        out_specs=pl.BlockSpec((1, tm, d_m), lambda bi, t: (bi, t, 0)),
        out_shape=jax.ShapeDtypeStruct((b, s, d_m), BF16),
        compiler_params=_params(("parallel", "parallel")),
        name="mem_attn",
    )(qm3, mem_k, mem_v)


def _rglru_kernel(xb_ref, gb_ref, past_ref, h0_ref, cw_ref, cb_ref, wa_ref, ba_ref, wi_ref, bi_ref, lam_ref,
                  o_ref, hl_ref, buf_ref, a_ref, u_ref, hc_ref, *, tt, width):
    t = pl.program_id(1)
    pad = SUBLANES

    @pl.when(t == 0)
    def _():
        buf_ref[0:pad, :] = past_ref[0]
        hc_ref[...] = h0_ref[0]

    buf_ref[pad:pad + tt, :] = xb_ref[0]
    xc = cb_ref[...] + cw_ref[width - 1:width, :] * xb_ref[0]
    for k in range(1, width):
        xc = xc + cw_ref[width - 1 - k:width - k, :] * buf_ref[pad - k:pad - k + tt, :]
    buf_ref[0:pad, :] = buf_ref[tt:tt + pad, :]

    xcb = xc.astype(BF16)
    def sigmoid(z):
        return 0.5 * jnp.tanh(0.5 * z) + 0.5

    r = sigmoid(jnp.dot(xcb, wa_ref[...], preferred_element_type=F32) + ba_ref[...])
    i = sigmoid(jnp.dot(xcb, wi_ref[...], preferred_element_type=F32) + bi_ref[...])
    log_a = -LRU_C * r * jax.nn.softplus(-lam_ref[...])
    a = jnp.exp(log_a)
    th = jnp.tanh(log_a)
    n = -2.0 * th
    dd = 1.0 - th
    u = jnp.where(n > 0.0, n * lax.rsqrt(n * dd), 0.0) * (i * xc)

    d = a.shape[-1]
    a = a.reshape(tt // SUBLANES, SUBLANES, d)
    u = u.reshape(tt // SUBLANES, SUBLANES, d)
    rowm = lax.broadcasted_iota(jnp.int32, (1, SUBLANES, 1), 1)
    sh = 1
    while sh < SUBLANES:
        keep = rowm >= sh
        a_sh = pltpu.roll(a, sh, 1)
        u_sh = pltpu.roll(u, sh, 1)
        u = jnp.where(keep, a * u_sh + u, u)
        a = jnp.where(keep, a * a_sh, a)
        sh *= 2
    a_ref[...] = a.reshape(tt, d)
    u_ref[...] = u.reshape(tt, d)

    def body(g, h):
        r0 = pl.multiple_of(g * SUBLANES, SUBLANES)
        hr = a_ref[pl.ds(r0, SUBLANES), :] * h + u_ref[pl.ds(r0, SUBLANES), :]
        u_ref[pl.ds(r0, SUBLANES), :] = hr
        return hr[SUBLANES - 1:SUBLANES, :]

    h_last = lax.fori_loop(0, tt // SUBLANES, body, hc_ref[...])
    hc_ref[...] = h_last
    hl_ref[0] = h_last
    o_ref[0] = (u_ref[...] * jax.nn.gelu(gb_ref[0])).astype(o_ref.dtype)


def _rglru(xg3, past8, h0, cw, cb, wa_bd, ba, wi_bd, bi, lam, tt):
    b, s, _ = xg3.shape
    d = cw.shape[1]
    width = cw.shape[0]
    vec = pl.BlockSpec((1, d), lambda bi_, t: (0, 0))
    mat = pl.BlockSpec((d, d), lambda bi_, t: (0, 0))
    return pl.pallas_call(
        functools.partial(_rglru_kernel, tt=tt, width=width),
        grid=(b, s // tt),
        in_specs=[
            pl.BlockSpec((1, tt, d), lambda bi_, t: (bi_, t, 0)),
            pl.BlockSpec((1, tt, d), lambda bi_, t: (bi_, t, 1)),
            pl.BlockSpec((1, SUBLANES, d), lambda bi_, t: (bi_, 0, 0)),
            pl.BlockSpec((1, 1, d), lambda bi_, t: (bi_, 0, 0)),
            pl.BlockSpec((width, d), lambda bi_, t: (0, 0)),
            vec, mat, vec, mat, vec, vec,
        ],
        out_specs=[
            pl.BlockSpec((1, tt, d), lambda bi_, t: (bi_, t, 0)),
            pl.BlockSpec((1, 1, d), lambda bi_, t: (bi_, 0, 0)),
        ],
        out_shape=[jax.ShapeDtypeStruct((b, s, d), BF16), jax.ShapeDtypeStruct((b, 1, d), F32)],
        scratch_shapes=[
            pltpu.VMEM((tt + 2 * SUBLANES, d), F32),
            pltpu.VMEM((tt, d), F32),
            pltpu.VMEM((tt, d), F32),
            pltpu.VMEM((1, d), F32),
        ],
        compiler_params=_params(("parallel", "arbitrary")),
        name="rglru",
    )(xg3, xg3, past8, h0.reshape(b, 1, d), cw, cb.reshape(1, d), wa_bd, ba.reshape(1, d),
      wi_bd, bi.reshape(1, d), lam.reshape(1, d))


def _out_proj_kernel(x_ref, oa_ref, om_ref, ob_ref, wa_ref, wm_ref, wb_ref, g_ref, o_ref, xn_ref):
    acc = jnp.dot(oa_ref[...], wa_ref[...], preferred_element_type=F32)
    acc = acc + jnp.dot(om_ref[...], wm_ref[...], preferred_element_type=F32)
    acc = acc + jnp.dot(ob_ref[...], wb_ref[...], preferred_element_type=F32)
    x1 = x_ref[...] + acc
    o_ref[...] = x1
    xn_ref[...] = _rms(x1, g_ref[...]).astype(BF16)


def _out_proj(x, oa, om, ob, wa, wm, wb, g_next, tm):
    m, d = x.shape
    row = lambda c: pl.BlockSpec((tm, c), lambda i: (i, 0))
    full = lambda a: pl.BlockSpec(a.shape, lambda i: (0, 0))
    return pl.pallas_call(
        _out_proj_kernel,
        grid=(m // tm,),
        in_specs=[row(d), row(oa.shape[1]), row(om.shape[1]), row(ob.shape[1]), full(wa), full(wm), full(wb),
                  pl.BlockSpec((1, d), lambda i: (0, 0))],
        out_specs=[row(d), row(d)],
        out_shape=[jax.ShapeDtypeStruct((m, d), F32), jax.ShapeDtypeStruct((m, d), BF16)],
        compiler_params=_params(("parallel",)),
        name="out_proj",
    )(x, oa, om, ob, wa, wm, wb, g_next.reshape(1, d))


def _conv_ffn_kernel(x_hbm, xn_ref, hist_ref, wv_ref, wg_ref, cw_ref, cb_ref, wd_ref,
                     o_hbm, f_ref, acc_ref, bv_ref, bg_ref, car_ref, sem_in, sem_out,
                     *, nseq, seq_rows, tiles_per_seq, width):
    i = pl.program_id(0)
    c = pl.program_id(1)
    nt = pl.num_programs(0)
    nc = pl.num_programs(1)
    pad = SUBLANES
    tm = nseq * seq_rows
    tf = wv_ref.shape[1]
    sub = min(tf, MXU_COLS)
    streams = ((c, wv_ref, bv_ref), (nc + c, wg_ref, bg_ref))
    slot = i % 2

    def fetch(tile, sl):
        return pltpu.make_async_copy(x_hbm.at[pl.ds(tile * tm, tm)], acc_ref.at[sl], sem_in.at[sl])

    def drain(tile, sl):
        return pltpu.make_async_copy(acc_ref.at[sl], o_hbm.at[pl.ds(tile * tm, tm)], sem_out.at[sl])

    @pl.when(jnp.logical_and(i == 0, c == 0))
    def _():
        fetch(0, 0).start()

    @pl.when(c == 0)
    def _():
        fetch(i, slot).wait()

    @pl.when(c == 1)
    def _():
        @pl.when(i >= 1)
        def _():
            drain(i - 1, 1 - slot).wait()

        @pl.when(i + 1 < nt)
        def _():
            fetch(i + 1, 1 - slot).start()

    for j, _, buf_ref in streams:
        if tiles_per_seq == 1:
            buf_ref[:, 0:pad, :] = hist_ref[:, j]
        else:
            first = i % tiles_per_seq == 0

            @pl.when(first)
            def _(j=j, buf_ref=buf_ref):
                buf_ref[:, 0:pad, :] = hist_ref[:, j]

            @pl.when(jnp.logical_not(first))
            def _(j=j, buf_ref=buf_ref):
                buf_ref[:, 0:pad, :] = car_ref[j]

    def up_proj(cs):
        for _, w_ref, buf_ref in streams:
            up = jnp.dot(xn_ref[...], w_ref[:, cs], preferred_element_type=F32)
            buf_ref[:, pad:pad + seq_rows, cs] = up.reshape(nseq, seq_rows, sub)

    def conv(j, buf_ref, cs):
        y = cb_ref[j, :, cs]
        for k in range(width):
            y = y + cw_ref[j, width - 1 - k:width - k, cs] * buf_ref[:, pad - k:pad - k + seq_rows, cs]
        return y

    subs = [slice(s * sub, (s + 1) * sub) for s in range(tf // sub)]
    up_proj(subs[0])
    for s, cs in enumerate(subs):
        if s + 1 < len(subs):
            up_proj(subs[s + 1])
        val = conv(streams[0][0], bv_ref, cs)
        gate = conv(streams[1][0], bg_ref, cs)
        h = (jax.nn.gelu(gate) * val).astype(BF16).reshape(tm, sub)
        acc_ref[slot] += jnp.dot(h, wd_ref[cs, :], preferred_element_type=F32)

    for j, _, buf_ref in streams:
        last = buf_ref[:, seq_rows:seq_rows + pad, :]
        f_ref[0, j] = last
        if tiles_per_seq > 1:
            car_ref[j] = last

    @pl.when(c == nc - 1)
    def _():
        drain(i, slot).start()

        @pl.when(i == nt - 1)
        def _():
            drain(i, slot).wait()


def _conv_ffn(x, xn, f_conv_past, w_up, cw, cb, w_down, nseq, seq_rows, tiles_per_seq, tf):
    m, d = x.shape
    d_ff = w_down.shape[0]
    width = cw.shape[0]
    tm = nseq * seq_rows
    nt = m // tm
    nc = d_ff // tf
    nb = f_conv_past.shape[0]
    assert nc >= 2 and m == nt * tm
    hist = jnp.pad(f_conv_past, ((0, 0), (SUBLANES - (width - 1), 0), (0, 0)))
    hist = hist.reshape(nb, SUBLANES, 2 * nc, tf).transpose(0, 2, 1, 3)
    cw3 = cw.reshape(width, 2 * nc, tf).transpose(1, 0, 2)
    cb3 = cb.reshape(2 * nc, 1, tf)
    kern = functools.partial(_conv_ffn_kernel, nseq=nseq, seq_rows=seq_rows,
                             tiles_per_seq=tiles_per_seq, width=width)
    car_shape = (2 * nc, nseq, SUBLANES, tf) if tiles_per_seq > 1 else (1, 1, SUBLANES, LANES)
    y, f = pl.pallas_call(
        kern,
        grid=(nt, nc),
        in_specs=[
            pl.BlockSpec(memory_space=pl.ANY),
            pl.BlockSpec((tm, d), lambda i, c: (i, 0)),
            pl.BlockSpec((nseq, 2 * nc, SUBLANES, tf), lambda i, c: (i // tiles_per_seq, 0, 0, 0)),
            pl.BlockSpec((d, tf), lambda i, c: (0, c)),
            pl.BlockSpec((d, tf), lambda i, c: (0, nc + c)),
            pl.BlockSpec((2 * nc, width, tf), lambda i, c: (0, 0, 0)),
            pl.BlockSpec((2 * nc, 1, tf), lambda i, c: (0, 0, 0)),
            pl.BlockSpec((tf, d), lambda i, c: (c, 0)),
        ],
        out_specs=[
            pl.BlockSpec(memory_space=pl.ANY),
            pl.BlockSpec((1, 2 * nc, nseq, SUBLANES, tf), lambda i, c: (i, 0, 0, 0, 0)),
        ],
        out_shape=[
            jax.ShapeDtypeStruct((m, d), F32),
            jax.ShapeDtypeStruct((nt, 2 * nc, nseq, SUBLANES, tf), F32),
        ],
        scratch_shapes=[
            pltpu.VMEM((2, tm, d), F32),
            pltpu.VMEM((nseq, seq_rows + 2 * SUBLANES, tf), F32),
            pltpu.VMEM((nseq, seq_rows + 2 * SUBLANES, tf), F32),
            pltpu.VMEM(car_shape, F32),
            pltpu.SemaphoreType.DMA((2,)),
            pltpu.SemaphoreType.DMA((2,)),
        ],
        compiler_params=_params(("arbitrary", "arbitrary")),
        name="conv_ffn",
    )(x, xn, hist, w_up, w_up, cw3, cb3, w_down)
    f = f.reshape(nt // tiles_per_seq, tiles_per_seq, 2 * nc, nseq, SUBLANES, tf)[:, -1]
    f = f.transpose(0, 2, 3, 1, 4).reshape(nb, SUBLANES, 2 * d_ff)
    return y, f[:, SUBLANES - (width - 1):]


def _pad_rows_front(a, rows):
    return jnp.pad(a, ((0, 0), (rows - a.shape[1], 0), (0, 0)))


def _block_diag(w):
    n, bi, bo = w.shape
    eye = jnp.eye(n, dtype=w.dtype)
    return (w[:, :, None, :] * eye[:, None, :, None]).reshape(n * bi, n * bo)


def _trunk(x3, band_fn, mem_k, mem_v, b_conv_past, b_h0, f_conv_past, p, *, in_tiles, transposed, tq_mem, tt,
           ffn_tiles, ffn_cols):
    b, t, d = x3.shape
    n_heads_a, n_heads_m, d_rnn = p["n_heads_a"], p["n_heads_m"], p["d_rnn"]
    d_a, d_m = n_heads_a * HEAD_DIM, n_heads_m * HEAD_DIM
    x2 = x3.reshape(b * t, d)
    nseq_in, rows_in, tiles_in = in_tiles
    q, k, v, qm, xg, k_a, v_a = _in_proj(x2, p["g_attn"], p["w_in"], p["g_q_a"], p["g_k_a"], p["g_q_m"],
                                         n_heads_a, n_heads_m, d_rnn, nseq_in, rows_in, tiles_in, transposed)
    xg3 = xg.reshape(b, t, 2 * d_rnn)

    o_a = band_fn(q, k, v)
    o_m = _mem_attn(qm.reshape(b, t, d_m), mem_k, mem_v, n_heads_m, tq_mem)
    o_b, h_last = _rglru(xg3, _pad_rows_front(b_conv_past, SUBLANES), b_h0, p["conv_b_w"], p["conv_b_b"],
                         p["wa_bd"], p["b_rg_a"], p["wi_bd"], p["b_rg_i"], p["lru_lambda"], tt)
    x1, xn1 = _out_proj(x2, o_a.reshape(b * t, d_a), o_m.reshape(b * t, d_m), o_b.reshape(b * t, d_rnn),
                        p["wo_a"], p["wo_m"], p["wo_b"], p["g_ffn"], 512)

    nseq, seq_rows, tiles_per_seq = ffn_tiles
    y, f_new = _conv_ffn(x1, xn1, f_conv_past, p["w_up"], p["conv_f_w"], p["conv_f_b"], p["w_down"],
                         nseq, seq_rows, tiles_per_seq, ffn_cols)

    wb = p["conv_b_w"].shape[0]
    xb = xg3[:, :, :d_rnn]
    b_conv_new = jnp.concatenate([b_conv_past, xb], axis=1)[:, -(wb - 1):] if t < wb - 1 else xb[:, t - (wb - 1):]
    return y.reshape(b, t, d), k_a, v_a, b_conv_new, h_last.reshape(b, d_rnn), f_new


def kernel(x_prompt, x_sample, cache_a_k, cache_a_v, cache_mem_k, cache_mem_v, cache_b_conv, state_b_h, cache_f_conv, mem_prompt, g_attn, w_in, g_q_a, g_k_a, rel_table, g_q_m, g_k_m, g_mem, w_mem_kv, conv_b_w, conv_b_b, w_rg_a, b_rg_a, w_rg_i, b_rg_i, lru_lambda, w_out, g_ffn, w_up, conv_f_w, conv_f_b, w_down):
    depth = w_in.shape[0]
    bp, sp, d = x_prompt.shape
    bs, ts, _ = x_sample.shape
    n_heads_a = cache_a_k.shape[3]
    n_heads_m = cache_mem_k.shape[3]
    n_mem = mem_prompt.shape[1]
    d_a, d_m = n_heads_a * HEAD_DIM, n_heads_m * HEAD_DIM
    d_rnn = conv_b_w.shape[2]
    d_ff = w_down.shape[1]
    keep = min(WINDOW, sp)
    ones = lambda n: jnp.ones((n,), F32)

    xp, xs = x_prompt, x_sample
    outs = [[] for _ in range(12)]
    for l in range(depth):
        p = dict(
            n_heads_a=n_heads_a, n_heads_m=n_heads_m, d_rnn=d_rnn,
            g_attn=g_attn[l], w_in=w_in[l].astype(BF16),
            g_q_a=g_q_a[l] * SCALE, g_k_a=g_k_a[l], g_q_m=g_q_m[l] * SCALE,
            conv_b_w=conv_b_w[l], conv_b_b=conv_b_b[l],
            wa_bd=_block_diag(w_rg_a[l]).astype(BF16), b_rg_a=b_rg_a[l],
            wi_bd=_block_diag(w_rg_i[l]).astype(BF16), b_rg_i=b_rg_i[l], lru_lambda=lru_lambda[l],
            wo_a=w_out[l, :d_a].astype(BF16), wo_m=w_out[l, d_a:d_a + d_m].astype(BF16),
            wo_b=w_out[l, d_a + d_m:].astype(BF16),
            g_ffn=g_ffn[l], w_up=w_up[l].astype(BF16), conv_f_w=conv_f_w[l], conv_f_b=conv_f_b[l],
            w_down=w_down[l].astype(BF16),
        )
        bias = _rel_bias(rel_table[l])

        mem_gains = jnp.concatenate([jnp.tile(g_k_m[l], n_heads_m), ones(d_m)])
        kv = _norm_matmul(mem_prompt.reshape(bp * n_mem, d), g_mem[l], w_mem_kv[l].astype(BF16), mem_gains,
                          ((0, 1),), 512, d_m)
        kv3 = kv.reshape(bp, n_mem, 2 * d_m)

        ffn_tm = 1024
        band_p = lambda q, kt, v: _band_attn(q.reshape(bp, sp, d_a), kt, v.reshape(bp, sp, d_a), bias,
                                             n_heads_a, WINDOW)
        xp, k_a, v_a, bc, bh, fc = _trunk(
            xp, band_p, kv3, kv3,
            jnp.zeros((bp, conv_b_w.shape[1] - 1, d_rnn), F32), jnp.zeros((bp, d_rnn), F32),
            jnp.zeros((bp, conv_f_w.shape[1] - 1, 2 * d_ff), F32), p,
            in_tiles=(1, keep, sp // keep), transposed=True, tq_mem=512, tt=512,
            ffn_tiles=(1, ffn_tm, sp // ffn_tm), ffn_cols=FFN_COLS)
        mem_k = kv3[:, :, :d_m].reshape(bp, n_mem, n_heads_m, HEAD_DIM)
        mem_v = kv3[:, :, d_m:].reshape(bp, n_mem, n_heads_m, HEAD_DIM)
        for lst, v in zip(outs[:7], (k_a, v_a, mem_k, mem_v, bc, bh, fc)):
            lst.append(v)

        band_s = lambda q, k, v: _band_attn_sample(
            q.reshape(bs, ts, d_a), k.reshape(bs, ts, d_a), v.reshape(bs, ts, d_a),
            cache_a_k[l], cache_a_v[l], bias, n_heads_a)
        xs, k_a, v_a, bc, bh, fc = _trunk(
            xs, band_s, cache_mem_k[l], cache_mem_v[l],
            cache_b_conv[l], state_b_h[l], cache_f_conv[l], p,
            in_tiles=(bs, ts, 1), transposed=False, tq_mem=ts, tt=ts, ffn_tiles=(bs, ts, 1),
            ffn_cols=FFN_COLS)
        for lst, v in zip(outs[7:], (k_a, v_a, bc, bh, fc)):
            lst.append(v)

    stacked = [jnp.stack(o) for o in outs]
    return (xp, xs, *stacked)
```

```python
import functools

import jax
import jax.numpy as jnp
from jax import lax
from jax.experimental import pallas as pl
from jax.experimental.pallas import tpu as pltpu

EPS = 1e-6
HEAD_DIM = 128
CHUNK = 64
LEFT_CHUNKS = 8
WINDOW = LEFT_CHUNKS * CHUNK
REL_CLIP = 256
LRU_C = 8.0
NEG = -1e30
SCALE = HEAD_DIM ** -0.5

LANES = 128
SUBLANES = 8
MXU_COLS = 256
VMEM_LIMIT = 56 * 2 ** 20

QSUB = 4 * CHUNK
KSPAN = WINDOW + QSUB
OUT_ROWS = 512
FFN_COLS = 512

BF16 = jnp.bfloat16
F32 = jnp.float32


def _params(sem):
    return pltpu.CompilerParams(dimension_semantics=sem, vmem_limit_bytes=VMEM_LIMIT)


def _rms(x, g):
    ms = jnp.mean(x * x, axis=-1, keepdims=True)
    return x * lax.rsqrt(ms + EPS) * g


def _norm_matmul_kernel(x_ref, g_ref, w_ref, hg_ref, o_ref, xn_ref, *, norm_ranges, tn):
    j = pl.program_id(1)

    @pl.when(j == 0)
    def _():
        xn_ref[...] = _rms(x_ref[...], g_ref[...]).astype(BF16)

    z = jnp.dot(xn_ref[...], w_ref[...], preferred_element_type=F32)
    cond = None
    for lo, hi in norm_ranges:
        c = jnp.logical_and(j >= lo, j < hi)
        cond = c if cond is None else jnp.logical_or(cond, c)

    @pl.when(cond)
    def _():
        for h in range(tn // HEAD_DIM):
            cs = slice(h * HEAD_DIM, (h + 1) * HEAD_DIM)
            o_ref[:, cs] = _rms(z[:, cs], hg_ref[:, cs])

    @pl.when(jnp.logical_not(cond))
    def _():
        o_ref[...] = z


def _norm_matmul(x, g, w_bf16, head_gains, norm_ranges, tm, tn):
    m, k = x.shape
    n = w_bf16.shape[1]
    kern = functools.partial(_norm_matmul_kernel, norm_ranges=norm_ranges, tn=tn)
    return pl.pallas_call(
        kern,
        grid=(m // tm, n // tn),
        in_specs=[
            pl.BlockSpec((tm, k), lambda i, j: (i, 0)),
            pl.BlockSpec((1, k), lambda i, j: (0, 0)),
            pl.BlockSpec((k, tn), lambda i, j: (0, j)),
            pl.BlockSpec((1, tn), lambda i, j: (0, j)),
        ],
        out_specs=pl.BlockSpec((tm, tn), lambda i, j: (i, j)),
        out_shape=jax.ShapeDtypeStruct((m, n), F32),
        scratch_shapes=[pltpu.VMEM((tm, k), BF16)],
        compiler_params=_params(("parallel", "arbitrary")),
        name="norm_matmul",
    )(x, g.reshape(1, k), w_bf16, head_gains.reshape(1, n))


def _in_proj_kernel(x_ref, g_ref, w_ref, gq_ref, gk_ref, gqm_ref, q_ref, k_ref, v_ref, qm_ref, xg_ref, ak_ref, av_ref,
                    xn_ref, kv_ref, *, d_a, d_m, d_rnn, group, tiles_per_seq, transposed):
    i = pl.program_id(0)
    xn_ref[...] = _rms(x_ref[...], g_ref[...]).astype(BF16)

    def cols(c0):
        return jnp.dot(xn_ref[...], w_ref[:, c0:c0 + group], preferred_element_type=F32)

    def heads(z, c0):
        return [(slice(c0 + h * HEAD_DIM, c0 + (h + 1) * HEAD_DIM), z[:, h * HEAD_DIM:(h + 1) * HEAD_DIM])
                for h in range(group // HEAD_DIM)]

    def put(ref, cs, val):
        if transposed:
            ref[cs, :] = val.T.astype(BF16)
        else:
            ref[:, cs] = val.astype(BF16)

    for c0 in range(0, d_a, group):
        for cs, zh in heads(cols(c0), c0):
            q_ref[:, cs] = _rms(zh, gq_ref[...]).astype(BF16)
    for c0 in range(0, d_a, group):
        for cs, zh in heads(cols(d_a + c0), c0):
            kh = _rms(zh, gk_ref[...])
            put(k_ref, cs, kh)
            kv_ref[:, cs] = kh
    for c0 in range(0, d_a, group):
        for cs, zh in heads(cols(2 * d_a + c0), c0):
            v_ref[:, cs] = zh.astype(BF16)
            kv_ref[:, slice(d_a + cs.start, d_a + cs.stop)] = zh
    for c0 in range(0, d_m, group):
        for cs, zh in heads(cols(3 * d_a + c0), c0):
            qm_ref[:, cs] = _rms(zh, gqm_ref[...]).astype(BF16)
    for c0 in range(0, 2 * d_rnn, group):
        xg_ref[:, c0:c0 + group] = cols(3 * d_a + d_m + c0)

    @pl.when(i % tiles_per_seq == tiles_per_seq - 1)
    def _():
        nseq, rows, n_heads = ak_ref.shape[0], ak_ref.shape[1], ak_ref.shape[2]
        for h in range(n_heads):
            cs = slice(h * HEAD_DIM, (h + 1) * HEAD_DIM)
            ak_ref[:, :, h, :] = kv_ref[:, cs].reshape(nseq, rows, HEAD_DIM)
            av_ref[:, :, h, :] = kv_ref[:, slice(d_a + cs.start, d_a + cs.stop)].reshape(nseq, rows, HEAD_DIM)


def _in_proj(x, g, w_bf16, gq, gk, gqm, n_heads_a, n_heads_m, d_rnn, nseq, rows, tiles_per_seq, transposed):
    m, k = x.shape
    d_in = w_bf16.shape[1]
    d_a, d_m = n_heads_a * HEAD_DIM, n_heads_m * HEAD_DIM
    tm = nseq * rows
    nt = m // tm
    n_cache_seq = nt * nseq // tiles_per_seq
    kern = functools.partial(_in_proj_kernel, d_a=d_a, d_m=d_m, d_rnn=d_rnn, group=4 * HEAD_DIM,
                             tiles_per_seq=tiles_per_seq, transposed=transposed)
    row = lambda c: pl.BlockSpec((tm, c), lambda i: (i, 0))
    vec = pl.BlockSpec((1, HEAD_DIM), lambda i: (0, 0))
    cache = pl.BlockSpec((nseq, rows, n_heads_a, HEAD_DIM), lambda i: (i // tiles_per_seq, 0, 0, 0))
    k_spec = pl.BlockSpec((d_a, tm), lambda i: (0, i)) if transposed else row(d_a)
    k_shape = jax.ShapeDtypeStruct((d_a, m) if transposed else (m, d_a), BF16)
    return pl.pallas_call(
        kern,
        grid=(nt,),
        in_specs=[
            row(k),
            pl.BlockSpec((1, k), lambda i: (0, 0)),
            pl.BlockSpec((k, d_in), lambda i: (0, 0), pipeline_mode=pl.Buffered(1)),
            vec, vec, vec,
        ],
        out_specs=[row(d_a), k_spec, row(d_a), row(d_m), row(2 * d_rnn), cache, cache],
        out_shape=[
            jax.ShapeDtypeStruct((m, d_a), BF16),
            k_shape,
            jax.ShapeDtypeStruct((m, d_a), BF16),
            jax.ShapeDtypeStruct((m, d_m), BF16),
            jax.ShapeDtypeStruct((m, 2 * d_rnn), F32),
            jax.ShapeDtypeStruct((n_cache_seq, rows, n_heads_a, HEAD_DIM), F32),
            jax.ShapeDtypeStruct((n_cache_seq, rows, n_heads_a, HEAD_DIM), F32),
        ],
        scratch_shapes=[pltpu.VMEM((tm, k), BF16), pltpu.VMEM((tm, 2 * d_a), F32)],
        compiler_params=_params(("arbitrary",)),
        name="in_proj",
    )(x, g.reshape(1, k), w_bf16, gq.reshape(1, HEAD_DIM), gk.reshape(1, HEAD_DIM), gqm.reshape(1, HEAD_DIM))


def _rel_bias_kernel(tab_ref, o_ref, *, n_heads, tpad):
    t = tab_ref[...]
    hi = t.astype(BF16)
    r1 = t - hi.astype(F32)
    mid = r1.astype(BF16)
    lo = (r1 - mid.astype(F32)).astype(BF16)
    tix = lax.broadcasted_iota(jnp.int32, (tpad, KSPAN), 0)
    jj = lax.broadcasted_iota(jnp.int32, (1, KSPAN), 1)
    valid = jj < WINDOW + CHUNK
    for i in range(CHUNK):
        idx = jnp.clip(i + WINDOW - jj, -REL_CLIP, REL_CLIP) + REL_CLIP
        onehot = jnp.where(tix == idx, 1.0, 0.0).astype(BF16)
        val = (jnp.dot(hi, onehot, preferred_element_type=F32)
               + jnp.dot(mid, onehot, preferred_element_type=F32)
               + jnp.dot(lo, onehot, preferred_element_type=F32))
        row = jnp.where(valid, val, NEG)
        for ci in range(QSUB // CHUNK):
            o_ref[ci * CHUNK + i] = row if ci == 0 else pltpu.roll(row, ci * CHUNK, 1)


def _rel_bias(rel_table):
    n_heads, tlen = rel_table.shape
    tpad = -(-tlen // LANES) * LANES
    tab = jnp.pad(rel_table, ((0, 0), (0, tpad - tlen)))
    out = pl.pallas_call(
        functools.partial(_rel_bias_kernel, n_heads=n_heads, tpad=tpad),
        out_shape=jax.ShapeDtypeStruct((QSUB, n_heads, KSPAN), F32),
        compiler_params=_params(None),
        name="rel_bias",
    )(tab)
    return jnp.transpose(out, (1, 0, 2))


def _band_attn_kernel(q_ref, ktp_ref, ktc_ref, vp_ref, vc_ref, bias_ref, o_ref, ktcat_ref, vcat_ref,
                      *, n_heads, tq):
    t = pl.program_id(1)
    ktcat_ref[:, 0:WINDOW] = ktp_ref[...]
    ktcat_ref[:, WINDOW:WINDOW + tq] = ktc_ref[...]
    vcat_ref[0:WINDOW, :] = vp_ref[0]
    vcat_ref[WINDOW:WINDOW + tq, :] = vc_ref[0]

    def run(first_tile):
        for s in range(tq // QSUB):
            r0 = s * QSUB
            if first_tile:
                before_start = lax.broadcasted_iota(jnp.int32, (1, KSPAN), 1) + r0 < WINDOW
            for h in range(n_heads):
                cs = slice(h * HEAD_DIM, (h + 1) * HEAD_DIM)
                sc = jnp.dot(q_ref[0, r0:r0 + QSUB, cs], ktcat_ref[cs, r0:r0 + KSPAN], preferred_element_type=F32)
                sc = sc + bias_ref[h]
                if first_tile:
                    sc = jnp.where(before_start, NEG, sc)
                m = jnp.max(sc, axis=-1, keepdims=True)
                p = jnp.exp(sc - m)
                l = jnp.sum(p, axis=-1, keepdims=True)
                o = jnp.dot(p.astype(BF16), vcat_ref[r0:r0 + KSPAN, cs], preferred_element_type=F32) / l
                o_ref[0, r0:r0 + QSUB, cs] = o.astype(o_ref.dtype)

    @pl.when(t == 0)
    def _():
        run(True)

    @pl.when(t > 0)
    def _():
        run(False)


def _band_attn(q3, kt, v3, bias, n_heads, tq):
    b, s, d_a = q3.shape
    assert tq == WINDOW and s % tq == 0
    nt = s // tq
    cur = lambda bi, t: (bi, t, 0)
    prev = lambda bi, t: (bi, jnp.maximum(t - 1, 0), 0)
    return pl.pallas_call(
        functools.partial(_band_attn_kernel, n_heads=n_heads, tq=tq),
        grid=(b, nt),
        in_specs=[
            pl.BlockSpec((1, tq, d_a), cur),
            pl.BlockSpec((d_a, WINDOW), lambda bi, t: (0, bi * nt + jnp.maximum(t - 1, 0))),
            pl.BlockSpec((d_a, tq), lambda bi, t: (0, bi * nt + t)),
            pl.BlockSpec((1, WINDOW, d_a), prev),
            pl.BlockSpec((1, tq, d_a), cur),
            pl.BlockSpec((n_heads, QSUB, KSPAN), lambda bi, t: (0, 0, 0)),
        ],
        out_specs=pl.BlockSpec((1, tq, d_a), cur),
        out_shape=jax.ShapeDtypeStruct((b, s, d_a), BF16),
        scratch_shapes=[pltpu.VMEM((d_a, WINDOW + tq), BF16), pltpu.VMEM((WINDOW + tq, d_a), BF16)],
        compiler_params=_params(("parallel", "parallel")),
        name="band_attn",
    )(q3, kt, kt, v3, v3, bias)


def _band_attn_sample_kernel(q_ref, kn_ref, vn_ref, ck_hbm, cv_hbm, bias_ref, o_ref, kbuf, vbuf, sem,
                             *, n_heads, t, p_len):
    b = pl.program_id(0)
    slot = b % 2

    def cache_copies(seq, sl):
        cps = []
        for h in range(n_heads):
            cps.append(pltpu.make_async_copy(ck_hbm.at[seq, :, h, :], kbuf.at[sl, h], sem.at[0, sl, h]))
            cps.append(pltpu.make_async_copy(cv_hbm.at[seq, :, h, :], vbuf.at[sl, h], sem.at[1, sl, h]))
        return cps

    @pl.when(b == 0)
    def _():
        for cp in cache_copies(0, 0):
            cp.start()

    @pl.when(b + 1 < pl.num_programs(0))
    def _():
        for cp in cache_copies(b + 1, 1 - slot):
            cp.start()

    for cp in cache_copies(b, slot):
        cp.wait()

    heads = [slice(h * HEAD_DIM, (h + 1) * HEAD_DIM) for h in range(n_heads)]
    dn = (((1,), (1,)), ((), ()))
    scores = []
    for h, cs in enumerate(heads):
        q = q_ref[0, :, cs]
        ck = kbuf[slot, h].astype(BF16)
        scores.append((lax.dot_general(q, ck, dn, preferred_element_type=F32),
                       lax.dot_general(q, kn_ref[0, :, cs], dn, preferred_element_type=F32)))
    for h, cs in enumerate(heads):
        cv = vbuf[slot, h].astype(BF16)
        vn = vn_ref[0, :, cs]
        s1 = scores[h][0] + bias_ref[h, 0:t, 0:p_len]
        s2 = scores[h][1] + bias_ref[h, 0:t, p_len:p_len + t]
        m = jnp.maximum(jnp.max(s1, axis=-1, keepdims=True), jnp.max(s2, axis=-1, keepdims=True))
        p1 = jnp.exp(s1 - m)
        p2 = jnp.exp(s2 - m)
        l = jnp.sum(p1, axis=-1, keepdims=True) + jnp.sum(p2, axis=-1, keepdims=True)
        o = (jnp.dot(p1.astype(BF16), cv, preferred_element_type=F32)
             + jnp.dot(p2.astype(BF16), vn, preferred_element_type=F32)) / l
        o_ref[0, :, cs] = o.astype(o_ref.dtype)


def _band_attn_sample(q3, k3, v3, cache_k, cache_v, bias, n_heads):
    b, t, d_a = q3.shape
    p_len = cache_k.shape[1]
    assert p_len == WINDOW and t <= CHUNK
    cache = pl.BlockSpec(memory_space=pl.ANY)
    return pl.pallas_call(
        functools.partial(_band_attn_sample_kernel, n_heads=n_heads, t=t, p_len=p_len),
        grid=(b,),
        in_specs=[
            pl.BlockSpec((1, t, d_a), lambda bi: (bi, 0, 0)),
            pl.BlockSpec((1, t, d_a), lambda bi: (bi, 0, 0)),
            pl.BlockSpec((1, t, d_a), lambda bi: (bi, 0, 0)),
            cache, cache,
            pl.BlockSpec((n_heads, QSUB, KSPAN), lambda bi: (0, 0, 0)),
        ],
        out_specs=pl.BlockSpec((1, t, d_a), lambda bi: (bi, 0, 0)),
        out_shape=jax.ShapeDtypeStruct((b, t, d_a), BF16),
        scratch_shapes=[
            pltpu.VMEM((2, n_heads, p_len, HEAD_DIM), cache_k.dtype),
            pltpu.VMEM((2, n_heads, p_len, HEAD_DIM), cache_v.dtype),
            pltpu.SemaphoreType.DMA((2, 2, n_heads)),
        ],
        compiler_params=_params(("arbitrary",)),
        name="band_attn_sample",
    )(q3, k3, v3, cache_k, cache_v, bias)


def _mem_attn_kernel(q_ref, mk_ref, mv_ref, o_ref, *, n_heads, head_axis):
    heads = [slice(h * HEAD_DIM, (h + 1) * HEAD_DIM) for h in range(n_heads)]
    load = (lambda ref, h, cs: ref[0, :, h, :]) if head_axis else (lambda ref, h, cs: ref[0, :, cs])
    scores = [lax.dot_general(q_ref[0, :, cs], load(mk_ref, h, cs).astype(BF16), (((1,), (1,)), ((), ())),
                              preferred_element_type=F32) for h, cs in enumerate(heads)]
    for h, cs in enumerate(heads):
        v = load(mv_ref, h, cs).astype(BF16)
        sc = scores[h]
        m = jnp.max(sc, axis=-1, keepdims=True)
        p = jnp.exp(sc - m)
        l = jnp.sum(p, axis=-1, keepdims=True)
        o = jnp.dot(p.astype(BF16), v, preferred_element_type=F32) / l
        o_ref[0, :, cs] = o.astype(o_ref.dtype)


def _mem_attn(qm3, mem_k, mem_v, n_heads, tm):
    b, s, d_m = qm3.shape
    n_mem = mem_k.shape[1]
    head_axis = mem_k.ndim == 4
    if head_axis:
        kspec = vspec = pl.BlockSpec((1, n_mem, n_heads, HEAD_DIM), lambda bi, t: (bi, 0, 0, 0))
    else:
        kspec = pl.BlockSpec((1, n_mem, d_m), lambda bi, t: (bi, 0, 0))
        vspec = pl.BlockSpec((1, n_mem, d_m), lambda bi, t: (bi, 0, 1))
    return pl.pallas_call(
        functools.partial(_mem_attn_kernel, n_heads=n_heads, head_axis=head_axis),
        grid=(b, s // tm),
        in_specs=[pl.BlockSpec((1, tm, d_m), lambda bi, t: (bi, t, 0)), kspec, vspec],
        out_specs=pl.BlockSpec((1, tm, d_m), lambda bi, t: (bi, t, 0)),
        out_shape=jax.ShapeDtypeStruct((b, s, d_m), BF16),
        compiler_params=_params(("parallel", "parallel")),
        name="mem_attn",
    )(qm3, mem_k, mem_v)


def _rglru_core(xb_ref, gb_ref, past_ref, h0_ref, cw_ref, cb_ref, wa_ref, ba_ref, wi_ref, bi_ref, lam_ref,
                buf_ref, a_ref, u_ref, hc_ref, tt, width, after_gates=None):
    t = pl.program_id(1)
    pad = SUBLANES

    @pl.when(t == 0)
    def _():
        buf_ref[0:pad, :] = past_ref[0]
        hc_ref[...] = h0_ref[0]

    buf_ref[pad:pad + tt, :] = xb_ref[0]
    xc = cb_ref[...] + cw_ref[width - 1:width, :] * xb_ref[0]
    for k in range(1, width):
        xc = xc + cw_ref[width - 1 - k:width - k, :] * buf_ref[pad - k:pad - k + tt, :]
    buf_ref[0:pad, :] = buf_ref[tt:tt + pad, :]

    xcb = xc.astype(BF16)
    def sigmoid(z):
        return 0.5 * jnp.tanh(0.5 * z) + 0.5

    r = sigmoid(jnp.dot(xcb, wa_ref[...], preferred_element_type=F32) + ba_ref[...])
    i = sigmoid(jnp.dot(xcb, wi_ref[...], preferred_element_type=F32) + bi_ref[...])
    if after_gates is not None:
        after_gates()
    log_a = -LRU_C * r * jax.nn.softplus(-lam_ref[...])
    a = jnp.exp(log_a)
    th = jnp.tanh(log_a)
    n = -2.0 * th
    dd = 1.0 - th
    u = jnp.where(n > 0.0, n * lax.rsqrt(n * dd), 0.0) * (i * xc)

    d = a.shape[-1]
    a = a.reshape(tt // SUBLANES, SUBLANES, d)
    u = u.reshape(tt // SUBLANES, SUBLANES, d)
    rowm = lax.broadcasted_iota(jnp.int32, (1, SUBLANES, 1), 1)
    sh = 1
    while sh < SUBLANES:
        keep = rowm >= sh
        a_sh = pltpu.roll(a, sh, 1)
        u_sh = pltpu.roll(u, sh, 1)
        u = jnp.where(keep, a * u_sh + u, u)
        a = jnp.where(keep, a * a_sh, a)
        sh *= 2
    a_ref[...] = a.reshape(tt, d)
    u_ref[...] = u.reshape(tt, d)

    def body(g, h):
        r0 = pl.multiple_of(g * SUBLANES, SUBLANES)
        hr = a_ref[pl.ds(r0, SUBLANES), :] * h + u_ref[pl.ds(r0, SUBLANES), :]
        u_ref[pl.ds(r0, SUBLANES), :] = hr
        return hr[SUBLANES - 1:SUBLANES, :]

    h_last = lax.fori_loop(0, tt // SUBLANES, body, hc_ref[...])
    hc_ref[...] = h_last
    return h_last, (u_ref[...] * jax.nn.gelu(gb_ref[0])).astype(BF16)


def _rglru_kernel(xb_ref, gb_ref, past_ref, h0_ref, cw_ref, cb_ref, wa_ref, ba_ref, wi_ref, bi_ref, lam_ref,
                  o_ref, hl_ref, buf_ref, a_ref, u_ref, hc_ref, *, tt, width):
    h_last, ob = _rglru_core(xb_ref, gb_ref, past_ref, h0_ref, cw_ref, cb_ref, wa_ref, ba_ref, wi_ref, bi_ref,
                             lam_ref, buf_ref, a_ref, u_ref, hc_ref, tt, width)
    hl_ref[0] = h_last
    o_ref[0] = ob


def _rglru_out_proj_kernel(x_ref, oa_ref, om_ref, xb_ref, gb_ref, past_ref, h0_ref, cw_ref, cb_ref, wa_ref, ba_ref,
                           wi_ref, bi_ref, lam_ref, woam_ref, wob_ref, g_ref,
                           o_ref, xn_ref, hl_ref, buf_ref, a_ref, u_ref, hc_ref, *, tt, width):
    def attention_share():
        lhs = jnp.concatenate([oa_ref[0], om_ref[0]], axis=1)
        o_ref[0] = jnp.dot(lhs, woam_ref[...], preferred_element_type=F32)

    h_last, ob = _rglru_core(xb_ref, gb_ref, past_ref, h0_ref, cw_ref, cb_ref, wa_ref, ba_ref, wi_ref, bi_ref,
                             lam_ref, buf_ref, a_ref, u_ref, hc_ref, tt, width, after_gates=attention_share)
    hl_ref[0] = h_last
    x1 = x_ref[0] + o_ref[0] + jnp.dot(ob, wob_ref[...], preferred_element_type=F32)
    o_ref[0] = x1
    xn_ref[0] = _rms(x1, g_ref[...]).astype(BF16)


def _rglru(xg3, past8, h0, cw, cb, wa_bd, ba, wi_bd, bi, lam, tt):
    b, s, _ = xg3.shape
    d = cw.shape[1]
    width = cw.shape[0]
    vec = pl.BlockSpec((1, d), lambda bi_, t: (0, 0))
    mat = pl.BlockSpec((d, d), lambda bi_, t: (0, 0))
    return pl.pallas_call(
        functools.partial(_rglru_kernel, tt=tt, width=width),
        grid=(b, s // tt),
        in_specs=[
            pl.BlockSpec((1, tt, d), lambda bi_, t: (bi_, t, 0)),
            pl.BlockSpec((1, tt, d), lambda bi_, t: (bi_, t, 1)),
            pl.BlockSpec((1, SUBLANES, d), lambda bi_, t: (bi_, 0, 0)),
            pl.BlockSpec((1, 1, d), lambda bi_, t: (bi_, 0, 0)),
            pl.BlockSpec((width, d), lambda bi_, t: (0, 0)),
            vec, mat, vec, mat, vec, vec,
        ],
        out_specs=[
            pl.BlockSpec((1, tt, d), lambda bi_, t: (bi_, t, 0)),
            pl.BlockSpec((1, 1, d), lambda bi_, t: (bi_, 0, 0)),
        ],
        out_shape=[jax.ShapeDtypeStruct((b, s, d), BF16), jax.ShapeDtypeStruct((b, 1, d), F32)],
        scratch_shapes=[
            pltpu.VMEM((tt + 2 * SUBLANES, d), F32),
            pltpu.VMEM((tt, d), F32),
            pltpu.VMEM((tt, d), F32),
            pltpu.VMEM((1, d), F32),
        ],
        compiler_params=_params(("parallel", "arbitrary")),
        name="rglru",
    )(xg3, xg3, past8, h0.reshape(b, 1, d), cw, cb.reshape(1, d), wa_bd, ba.reshape(1, d),
      wi_bd, bi.reshape(1, d), lam.reshape(1, d))


def _rglru_out_proj(x3, oa3, om3, xg3, past8, h0, cw, cb, wa_bd, ba, wi_bd, bi, lam, woam, wob, g_next, tt):
    b, s, dm = x3.shape
    d = cw.shape[1]
    width = cw.shape[0]
    vec = pl.BlockSpec((1, d), lambda bi_, t: (0, 0))
    mat = pl.BlockSpec((d, d), lambda bi_, t: (0, 0))
    row = lambda c, j=0: pl.BlockSpec((1, tt, c), lambda bi_, t: (bi_, t, j))
    full = lambda w: pl.BlockSpec(w.shape, lambda bi_, t: (0, 0), pipeline_mode=pl.Buffered(1))
    return pl.pallas_call(
        functools.partial(_rglru_out_proj_kernel, tt=tt, width=width),
        grid=(b, s // tt),
        in_specs=[
            row(dm), row(oa3.shape[2]), row(om3.shape[2]), row(d, 0), row(d, 1),
            pl.BlockSpec((1, SUBLANES, d), lambda bi_, t: (bi_, 0, 0)),
            pl.BlockSpec((1, 1, d), lambda bi_, t: (bi_, 0, 0)),
            pl.BlockSpec((width, d), lambda bi_, t: (0, 0)),
            vec, mat, vec, mat, vec, vec,
            full(woam), full(wob),
            pl.BlockSpec((1, dm), lambda bi_, t: (0, 0)),
        ],
        out_specs=[row(dm), row(dm), pl.BlockSpec((1, 1, d), lambda bi_, t: (bi_, 0, 0))],
        out_shape=[jax.ShapeDtypeStruct((b, s, dm), F32), jax.ShapeDtypeStruct((b, s, dm), BF16),
                   jax.ShapeDtypeStruct((b, 1, d), F32)],
        scratch_shapes=[
            pltpu.VMEM((tt + 2 * SUBLANES, d), F32),
            pltpu.VMEM((tt, d), F32),
            pltpu.VMEM((tt, d), F32),
            pltpu.VMEM((1, d), F32),
        ],
        compiler_params=_params(("parallel", "arbitrary")),
        name="rglru_out_proj",
    )(x3, oa3, om3, xg3, xg3, past8, h0.reshape(b, 1, d), cw, cb.reshape(1, d), wa_bd, ba.reshape(1, d),
      wi_bd, bi.reshape(1, d), lam.reshape(1, d), woam, wob, g_next.reshape(1, dm))


def _out_proj_kernel(x_ref, oa_ref, om_ref, ob_ref, wa_ref, wm_ref, wb_ref, g_ref, o_ref, xn_ref):
    acc = jnp.dot(oa_ref[...], wa_ref[...], preferred_element_type=F32)
    acc = acc + jnp.dot(om_ref[...], wm_ref[...], preferred_element_type=F32)
    acc = acc + jnp.dot(ob_ref[...], wb_ref[...], preferred_element_type=F32)
    x1 = x_ref[...] + acc
    o_ref[...] = x1
    xn_ref[...] = _rms(x1, g_ref[...]).astype(BF16)


def _out_proj(x, oa, om, ob, wa, wm, wb, g_next, tm):
    m, d = x.shape
    row = lambda c: pl.BlockSpec((tm, c), lambda i: (i, 0))
    full = lambda a: pl.BlockSpec(a.shape, lambda i: (0, 0))
    return pl.pallas_call(
        _out_proj_kernel,
        grid=(m // tm,),
        in_specs=[row(d), row(oa.shape[1]), row(om.shape[1]), row(ob.shape[1]), full(wa), full(wm), full(wb),
                  pl.BlockSpec((1, d), lambda i: (0, 0))],
        out_specs=[row(d), row(d)],
        out_shape=[jax.ShapeDtypeStruct((m, d), F32), jax.ShapeDtypeStruct((m, d), BF16)],
        compiler_params=_params(("parallel",)),
        name="out_proj",
    )(x, oa, om, ob, wa, wm, wb, g_next.reshape(1, d))


def _conv_ffn_kernel(x_hbm, xn_ref, hist_ref, wv_ref, wg_ref, cw_ref, cb_ref, wd_ref,
                     o_hbm, f_ref, acc_ref, bv_ref, bg_ref, car_ref, sem_in, sem_out,
                     *, nseq, seq_rows, tiles_per_seq, width):
    i = pl.program_id(0)
    c = pl.program_id(1)
    nt = pl.num_programs(0)
    nc = pl.num_programs(1)
    pad = SUBLANES
    tm = nseq * seq_rows
    tf = wv_ref.shape[1]
    sub = min(tf, MXU_COLS)
    streams = ((c, wv_ref, bv_ref), (nc + c, wg_ref, bg_ref))
    slot = i % 2

    def fetch(tile, sl):
        return pltpu.make_async_copy(x_hbm.at[pl.ds(tile * tm, tm)], acc_ref.at[sl], sem_in.at[sl])

    def drain(tile, sl):
        return pltpu.make_async_copy(acc_ref.at[sl], o_hbm.at[pl.ds(tile * tm, tm)], sem_out.at[sl])

    @pl.when(jnp.logical_and(i == 0, c == 0))
    def _():
        fetch(0, 0).start()

    @pl.when(c == 0)
    def _():
        fetch(i, slot).wait()

    @pl.when(c == 1)
    def _():
        @pl.when(i >= 1)
        def _():
            drain(i - 1, 1 - slot).wait()

        @pl.when(i + 1 < nt)
        def _():
            fetch(i + 1, 1 - slot).start()

    for j, _, buf_ref in streams:
        if tiles_per_seq == 1:
            buf_ref[:, 0:pad, :] = hist_ref[:, j]
        else:
            first = i % tiles_per_seq == 0

            @pl.when(first)
            def _(j=j, buf_ref=buf_ref):
                buf_ref[:, 0:pad, :] = hist_ref[:, j]

            @pl.when(jnp.logical_not(first))
            def _(j=j, buf_ref=buf_ref):
                buf_ref[:, 0:pad, :] = car_ref[j]

    def up_proj(cs):
        for _, w_ref, buf_ref in streams:
            up = jnp.dot(xn_ref[...], w_ref[:, cs], preferred_element_type=F32)
            buf_ref[:, pad:pad + seq_rows, cs] = up.reshape(nseq, seq_rows, sub)

    def conv(j, buf_ref, cs):
        y = cb_ref[j, :, cs]
        for k in range(width):
            y = y + cw_ref[j, width - 1 - k:width - k, cs] * buf_ref[:, pad - k:pad - k + seq_rows, cs]
        return y

    subs = [slice(s * sub, (s + 1) * sub) for s in range(tf // sub)]
    up_proj(subs[0])
    for s, cs in enumerate(subs):
        if s + 1 < len(subs):
            up_proj(subs[s + 1])
        val = conv(streams[0][0], bv_ref, cs)
        gate = conv(streams[1][0], bg_ref, cs)
        h = (jax.nn.gelu(gate) * val).astype(BF16).reshape(tm, sub)
        acc_ref[slot] += jnp.dot(h, wd_ref[cs, :], preferred_element_type=F32)

    for j, _, buf_ref in streams:
        last = buf_ref[:, seq_rows:seq_rows + pad, :]
        f_ref[0, j] = last
        if tiles_per_seq > 1:
            car_ref[j] = last

    @pl.when(c == nc - 1)
    def _():
        drain(i, slot).start()

        @pl.when(i == nt - 1)
        def _():
            drain(i, slot).wait()


def _conv_ffn(x, xn, f_conv_past, w_up, cw, cb, w_down, nseq, seq_rows, tiles_per_seq, tf):
    m, d = x.shape
    d_ff = w_down.shape[0]
    width = cw.shape[0]
    tm = nseq * seq_rows
    nt = m // tm
    nc = d_ff // tf
    nb = f_conv_past.shape[0]
    assert nc >= 2 and m == nt * tm
    hist = jnp.pad(f_conv_past, ((0, 0), (SUBLANES - (width - 1), 0), (0, 0)))
    hist = hist.reshape(nb, SUBLANES, 2 * nc, tf).transpose(0, 2, 1, 3)
    cw3 = cw.reshape(width, 2 * nc, tf).transpose(1, 0, 2)
    cb3 = cb.reshape(2 * nc, 1, tf)
    kern = functools.partial(_conv_ffn_kernel, nseq=nseq, seq_rows=seq_rows,
                             tiles_per_seq=tiles_per_seq, width=width)
    car_shape = (2 * nc, nseq, SUBLANES, tf) if tiles_per_seq > 1 else (1, 1, SUBLANES, LANES)
    y, f = pl.pallas_call(
        kern,
        grid=(nt, nc),
        in_specs=[
            pl.BlockSpec(memory_space=pl.ANY),
            pl.BlockSpec((tm, d), lambda i, c: (i, 0)),
            pl.BlockSpec((nseq, 2 * nc, SUBLANES, tf), lambda i, c: (i // tiles_per_seq, 0, 0, 0)),
            pl.BlockSpec((d, tf), lambda i, c: (0, c)),
            pl.BlockSpec((d, tf), lambda i, c: (0, nc + c)),
            pl.BlockSpec((2 * nc, width, tf), lambda i, c: (0, 0, 0)),
            pl.BlockSpec((2 * nc, 1, tf), lambda i, c: (0, 0, 0)),
            pl.BlockSpec((tf, d), lambda i, c: (c, 0)),
        ],
        out_specs=[
            pl.BlockSpec(memory_space=pl.ANY),
            pl.BlockSpec((1, 2 * nc, nseq, SUBLANES, tf), lambda i, c: (i, 0, 0, 0, 0)),
        ],
        out_shape=[
            jax.ShapeDtypeStruct((m, d), F32),
            jax.ShapeDtypeStruct((nt, 2 * nc, nseq, SUBLANES, tf), F32),
        ],
        scratch_shapes=[
            pltpu.VMEM((2, tm, d), F32),
            pltpu.VMEM((nseq, seq_rows + 2 * SUBLANES, tf), F32),
            pltpu.VMEM((nseq, seq_rows + 2 * SUBLANES, tf), F32),
            pltpu.VMEM(car_shape, F32),
            pltpu.SemaphoreType.DMA((2,)),
            pltpu.SemaphoreType.DMA((2,)),
        ],
        compiler_params=_params(("arbitrary", "arbitrary")),
        name="conv_ffn",
    )(x, xn, hist, w_up, w_up, cw3, cb3, w_down)
    f = f.reshape(nt // tiles_per_seq, tiles_per_seq, 2 * nc, nseq, SUBLANES, tf)[:, -1]
    f = f.transpose(0, 2, 3, 1, 4).reshape(nb, SUBLANES, 2 * d_ff)
    return y, f[:, SUBLANES - (width - 1):]


def _pad_rows_front(a, rows):
    return jnp.pad(a, ((0, 0), (rows - a.shape[1], 0), (0, 0)))


def _block_diag(w):
    n, bi, bo = w.shape
    eye = jnp.eye(n, dtype=w.dtype)
    return (w[:, :, None, :] * eye[:, None, :, None]).reshape(n * bi, n * bo)


def _trunk(x3, band_fn, mem_k, mem_v, b_conv_past, b_h0, f_conv_past, p, *, in_tiles, transposed, tq_mem, tt,
           ffn_tiles, ffn_cols):
    b, t, d = x3.shape
    n_heads_a, n_heads_m, d_rnn = p["n_heads_a"], p["n_heads_m"], p["d_rnn"]
    d_a, d_m = n_heads_a * HEAD_DIM, n_heads_m * HEAD_DIM
    x2 = x3.reshape(b * t, d)
    nseq_in, rows_in, tiles_in = in_tiles
    q, k, v, qm, xg, k_a, v_a = _in_proj(x2, p["g_attn"], p["w_in"], p["g_q_a"], p["g_k_a"], p["g_q_m"],
                                         n_heads_a, n_heads_m, d_rnn, nseq_in, rows_in, tiles_in, transposed)
    xg3 = xg.reshape(b, t, 2 * d_rnn)

    o_a = band_fn(q, k, v)
    o_m = _mem_attn(qm.reshape(b, t, d_m), mem_k, mem_v, n_heads_m, tq_mem)
    lru = (xg3, _pad_rows_front(b_conv_past, SUBLANES), b_h0, p["conv_b_w"], p["conv_b_b"],
           p["wa_bd"], p["b_rg_a"], p["wi_bd"], p["b_rg_i"], p["lru_lambda"])
    wo = (p["wo_a"], p["wo_m"], p["wo_b"], p["g_ffn"])
    if tt >= OUT_ROWS:
        x1, xn1, h_last = _rglru_out_proj(x3, o_a, o_m, *lru, p["wo_am"], p["wo_b"], p["g_ffn"], tt)
        x1, xn1 = x1.reshape(b * t, d), xn1.reshape(b * t, d)
    else:
        o_b, h_last = _rglru(*lru, tt)
        x1, xn1 = _out_proj(x2, o_a.reshape(b * t, d_a), o_m.reshape(b * t, d_m), o_b.reshape(b * t, d_rnn),
                            *wo, OUT_ROWS)

    nseq, seq_rows, tiles_per_seq = ffn_tiles
    y, f_new = _conv_ffn(x1, xn1, f_conv_past, p["w_up"], p["conv_f_w"], p["conv_f_b"], p["w_down"],
                         nseq, seq_rows, tiles_per_seq, ffn_cols)

    wb = p["conv_b_w"].shape[0]
    xb = xg3[:, :, :d_rnn]
    b_conv_new = jnp.concatenate([b_conv_past, xb], axis=1)[:, -(wb - 1):] if t < wb - 1 else xb[:, t - (wb - 1):]
    return y.reshape(b, t, d), k_a, v_a, b_conv_new, h_last.reshape(b, d_rnn), f_new


def kernel(x_prompt, x_sample, cache_a_k, cache_a_v, cache_mem_k, cache_mem_v, cache_b_conv, state_b_h, cache_f_conv, mem_prompt, g_attn, w_in, g_q_a, g_k_a, rel_table, g_q_m, g_k_m, g_mem, w_mem_kv, conv_b_w, conv_b_b, w_rg_a, b_rg_a, w_rg_i, b_rg_i, lru_lambda, w_out, g_ffn, w_up, conv_f_w, conv_f_b, w_down):
    depth = w_in.shape[0]
    bp, sp, d = x_prompt.shape
    bs, ts, _ = x_sample.shape
    n_heads_a = cache_a_k.shape[3]
    n_heads_m = cache_mem_k.shape[3]
    n_mem = mem_prompt.shape[1]
    d_a, d_m = n_heads_a * HEAD_DIM, n_heads_m * HEAD_DIM
    d_rnn = conv_b_w.shape[2]
    d_ff = w_down.shape[1]
    keep = min(WINDOW, sp)
    ones = lambda n: jnp.ones((n,), F32)

    xp, xs = x_prompt, x_sample
    outs = [[] for _ in range(12)]
    for l in range(depth):
        p = dict(
            n_heads_a=n_heads_a, n_heads_m=n_heads_m, d_rnn=d_rnn,
            g_attn=g_attn[l], w_in=w_in[l].astype(BF16),
            g_q_a=g_q_a[l] * SCALE, g_k_a=g_k_a[l], g_q_m=g_q_m[l] * SCALE,
            conv_b_w=conv_b_w[l], conv_b_b=conv_b_b[l],
            wa_bd=_block_diag(w_rg_a[l]).astype(BF16), b_rg_a=b_rg_a[l],
            wi_bd=_block_diag(w_rg_i[l]).astype(BF16), b_rg_i=b_rg_i[l], lru_lambda=lru_lambda[l],
            wo_a=w_out[l, :d_a].astype(BF16), wo_m=w_out[l, d_a:d_a + d_m].astype(BF16),
            wo_am=w_out[l, :d_a + d_m].astype(BF16),
            wo_b=w_out[l, d_a + d_m:].astype(BF16),
            g_ffn=g_ffn[l], w_up=w_up[l].astype(BF16), conv_f_w=conv_f_w[l], conv_f_b=conv_f_b[l],
            w_down=w_down[l].astype(BF16),
        )
        bias = _rel_bias(rel_table[l])

        mem_gains = jnp.concatenate([jnp.tile(g_k_m[l], n_heads_m), ones(d_m)])
        kv = _norm_matmul(mem_prompt.reshape(bp * n_mem, d), g_mem[l], w_mem_kv[l].astype(BF16), mem_gains,
                          ((0, 1),), 512, d_m)
        kv3 = kv.reshape(bp, n_mem, 2 * d_m)

        ffn_tm = 1024
        band_p = lambda q, kt, v: _band_attn(q.reshape(bp, sp, d_a), kt, v.reshape(bp, sp, d_a), bias,
                                             n_heads_a, WINDOW)
        xp, k_a, v_a, bc, bh, fc = _trunk(
            xp, band_p, kv3, kv3,
            jnp.zeros((bp, conv_b_w.shape[1] - 1, d_rnn), F32), jnp.zeros((bp, d_rnn), F32),
            jnp.zeros((bp, conv_f_w.shape[1] - 1, 2 * d_ff), F32), p,
            in_tiles=(1, keep, sp // keep), transposed=True, tq_mem=512, tt=512,
            ffn_tiles=(1, ffn_tm, sp // ffn_tm), ffn_cols=FFN_COLS)
        mem_k = kv3[:, :, :d_m].reshape(bp, n_mem, n_heads_m, HEAD_DIM)
        mem_v = kv3[:, :, d_m:].reshape(bp, n_mem, n_heads_m, HEAD_DIM)
        for lst, v in zip(outs[:7], (k_a, v_a, mem_k, mem_v, bc, bh, fc)):
            lst.append(v)

        band_s = lambda q, k, v: _band_attn_sample(
            q.reshape(bs, ts, d_a), k.reshape(bs, ts, d_a), v.reshape(bs, ts, d_a),
            cache_a_k[l], cache_a_v[l], bias, n_heads_a)
        xs, k_a, v_a, bc, bh, fc = _trunk(
            xs, band_s, cache_mem_k[l], cache_mem_v[l],
            cache_b_conv[l], state_b_h[l], cache_f_conv[l], p,
            in_tiles=(bs, ts, 1), transposed=False, tq_mem=ts, tt=ts, ffn_tiles=(bs, ts, 1),
            ffn_cols=FFN_COLS)
        for lst, v in zip(outs[7:], (k_a, v_a, bc, bh, fc)):
            lst.append(v)

    stacked = [jnp.stack(o) for o in outs]
    return (xp, xs, *stacked)
```

```python
import functools

import jax
import jax.numpy as jnp
from jax import lax
from jax.experimental import pallas as pl
from jax.experimental.pallas import tpu as pltpu

EPS = 1e-6
HEAD_DIM = 128
CHUNK = 64
LEFT_CHUNKS = 8
WINDOW = LEFT_CHUNKS * CHUNK
REL_CLIP = 256
LRU_C = 8.0
NEG = -1e30
SCALE = HEAD_DIM ** -0.5

LANES = 128
SUBLANES = 8
MXU_COLS = 256
VMEM_LIMIT = 56 * 2 ** 20

QSUB = 4 * CHUNK
KSPAN = WINDOW + QSUB
OUT_ROWS = 512
FFN_COLS = 512

BF16 = jnp.bfloat16
F32 = jnp.float32


def _params(sem):
    return pltpu.CompilerParams(dimension_semantics=sem, vmem_limit_bytes=VMEM_LIMIT)


def _rms(x, g):
    ms = jnp.mean(x * x, axis=-1, keepdims=True)
    return x * lax.rsqrt(ms + EPS) * g


def _norm_matmul_kernel(x_ref, g_ref, w_ref, hg_ref, o_ref, xn_ref, *, norm_ranges, tn):
    j = pl.program_id(1)

    @pl.when(j == 0)
    def _():
        xn_ref[...] = _rms(x_ref[...], g_ref[...]).astype(BF16)

    z = jnp.dot(xn_ref[...], w_ref[...], preferred_element_type=F32)
    cond = None
    for lo, hi in norm_ranges:
        c = jnp.logical_and(j >= lo, j < hi)
        cond = c if cond is None else jnp.logical_or(cond, c)

    @pl.when(cond)
    def _():
        for h in range(tn // HEAD_DIM):
            cs = slice(h * HEAD_DIM, (h + 1) * HEAD_DIM)
            o_ref[:, cs] = _rms(z[:, cs], hg_ref[:, cs])

    @pl.when(jnp.logical_not(cond))
    def _():
        o_ref[...] = z


def _norm_matmul(x, g, w_bf16, head_gains, norm_ranges, tm, tn):
    m, k = x.shape
    n = w_bf16.shape[1]
    kern = functools.partial(_norm_matmul_kernel, norm_ranges=norm_ranges, tn=tn)
    return pl.pallas_call(
        kern,
        grid=(m // tm, n // tn),
        in_specs=[
            pl.BlockSpec((tm, k), lambda i, j: (i, 0)),
            pl.BlockSpec((1, k), lambda i, j: (0, 0)),
            pl.BlockSpec((k, tn), lambda i, j: (0, j)),
            pl.BlockSpec((1, tn), lambda i, j: (0, j)),
        ],
        out_specs=pl.BlockSpec((tm, tn), lambda i, j: (i, j)),
        out_shape=jax.ShapeDtypeStruct((m, n), F32),
        scratch_shapes=[pltpu.VMEM((tm, k), BF16)],
        compiler_params=_params(("parallel", "arbitrary")),
        name="norm_matmul",
    )(x, g.reshape(1, k), w_bf16, head_gains.reshape(1, n))


def _in_proj_kernel(x_ref, g_ref, w_ref, gq_ref, gk_ref, gqm_ref, q_ref, k_ref, v_ref, qm_ref, xg_ref, ak_ref, av_ref,
                    xn_ref, kv_ref, *, d_a, d_m, d_rnn, group, tiles_per_seq, transposed):
    i = pl.program_id(0)
    xn_ref[...] = _rms(x_ref[...], g_ref[...]).astype(BF16)

    def cols(c0):
        return jnp.dot(xn_ref[...], w_ref[:, c0:c0 + group], preferred_element_type=F32)

    def heads(z, c0):
        return [(slice(c0 + h * HEAD_DIM, c0 + (h + 1) * HEAD_DIM), z[:, h * HEAD_DIM:(h + 1) * HEAD_DIM])
                for h in range(group // HEAD_DIM)]

    def put(ref, cs, val):
        if transposed:
            ref[cs, :] = val.T.astype(BF16)
        else:
            ref[:, cs] = val.astype(BF16)

    for c0 in range(0, d_a, group):
        for cs, zh in heads(cols(c0), c0):
            q_ref[:, cs] = _rms(zh, gq_ref[...]).astype(BF16)
    for c0 in range(0, d_a, group):
        for cs, zh in heads(cols(d_a + c0), c0):
            kh = _rms(zh, gk_ref[...])
            put(k_ref, cs, kh)
            kv_ref[:, cs] = kh
    for c0 in range(0, d_a, group):
        for cs, zh in heads(cols(2 * d_a + c0), c0):
            v_ref[:, cs] = zh.astype(BF16)
            kv_ref[:, slice(d_a + cs.start, d_a + cs.stop)] = zh
    for c0 in range(0, d_m, group):
        for cs, zh in heads(cols(3 * d_a + c0), c0):
            qm_ref[:, cs] = _rms(zh, gqm_ref[...]).astype(BF16)
    for c0 in range(0, 2 * d_rnn, group):
        xg_ref[:, c0:c0 + group] = cols(3 * d_a + d_m + c0)

    @pl.when(i % tiles_per_seq == tiles_per_seq - 1)
    def _():
        nseq, rows, n_heads = ak_ref.shape[0], ak_ref.shape[1], ak_ref.shape[2]
        for h in range(n_heads):
            cs = slice(h * HEAD_DIM, (h + 1) * HEAD_DIM)
            ak_ref[:, :, h, :] = kv_ref[:, cs].reshape(nseq, rows, HEAD_DIM)
            av_ref[:, :, h, :] = kv_ref[:, slice(d_a + cs.start, d_a + cs.stop)].reshape(nseq, rows, HEAD_DIM)


def _in_proj(x, g, w_bf16, gq, gk, gqm, n_heads_a, n_heads_m, d_rnn, nseq, rows, tiles_per_seq, transposed):
    m, k = x.shape
    d_in = w_bf16.shape[1]
    d_a, d_m = n_heads_a * HEAD_DIM, n_heads_m * HEAD_DIM
    tm = nseq * rows
    nt = m // tm
    n_cache_seq = nt * nseq // tiles_per_seq
    kern = functools.partial(_in_proj_kernel, d_a=d_a, d_m=d_m, d_rnn=d_rnn, group=4 * HEAD_DIM,
                             tiles_per_seq=tiles_per_seq, transposed=transposed)
    row = lambda c: pl.BlockSpec((tm, c), lambda i: (i, 0))
    vec = pl.BlockSpec((1, HEAD_DIM), lambda i: (0, 0))
    cache = pl.BlockSpec((nseq, rows, n_heads_a, HEAD_DIM), lambda i: (i // tiles_per_seq, 0, 0, 0))
    k_spec = pl.BlockSpec((d_a, tm), lambda i: (0, i)) if transposed else row(d_a)
    k_shape = jax.ShapeDtypeStruct((d_a, m) if transposed else (m, d_a), BF16)
    return pl.pallas_call(
        kern,
        grid=(nt,),
        in_specs=[
            row(k),
            pl.BlockSpec((1, k), lambda i: (0, 0)),
            pl.BlockSpec((k, d_in), lambda i: (0, 0), pipeline_mode=pl.Buffered(1)),
            vec, vec, vec,
        ],
        out_specs=[row(d_a), k_spec, row(d_a), row(d_m), row(2 * d_rnn), cache, cache],
        out_shape=[
            jax.ShapeDtypeStruct((m, d_a), BF16),
            k_shape,
            jax.ShapeDtypeStruct((m, d_a), BF16),
            jax.ShapeDtypeStruct((m, d_m), BF16),
            jax.ShapeDtypeStruct((m, 2 * d_rnn), F32),
            jax.ShapeDtypeStruct((n_cache_seq, rows, n_heads_a, HEAD_DIM), F32),
            jax.ShapeDtypeStruct((n_cache_seq, rows, n_heads_a, HEAD_DIM), F32),
        ],
        scratch_shapes=[pltpu.VMEM((tm, k), BF16), pltpu.VMEM((tm, 2 * d_a), F32)],
        compiler_params=_params(("arbitrary",)),
        name="in_proj",
    )(x, g.reshape(1, k), w_bf16, gq.reshape(1, HEAD_DIM), gk.reshape(1, HEAD_DIM), gqm.reshape(1, HEAD_DIM))


def _rel_bias_kernel(tab_ref, o_ref, *, tpad, wext):
    t = tab_ref[...]
    hi = t.astype(BF16)
    r1 = t - hi.astype(F32)
    mid = r1.astype(BF16)
    lo = (r1 - mid.astype(F32)).astype(BF16)
    m = lax.broadcasted_iota(jnp.int32, (1, wext), 1)
    x = jnp.where(m >= wext - CHUNK, m - wext, m)
    idx = jnp.clip(WINDOW - x, -REL_CLIP, REL_CLIP) + REL_CLIP
    tix = lax.broadcasted_iota(jnp.int32, (tpad, wext), 0)
    onehot = jnp.where(tix == idx, 1.0, 0.0).astype(BF16)
    v = (jnp.dot(hi, onehot, preferred_element_type=F32)
         + jnp.dot(mid, onehot, preferred_element_type=F32)
         + jnp.dot(lo, onehot, preferred_element_type=F32))
    j = lax.broadcasted_iota(jnp.int32, (1, KSPAN), 1)
    for ci in range(QSUB // CHUNK):
        jj = j - CHUNK * ci
        valid = jnp.logical_and(jj >= 0, jj < WINDOW + CHUNK)
        for ii in range(CHUNK):
            s = ci * CHUNK + ii
            row = (v if s == 0 else pltpu.roll(v, s, 1))[:, :KSPAN]
            o_ref[:, s, :] = jnp.where(valid, row, NEG)


def _rel_bias(rel_table):
    n_heads, tlen = rel_table.shape
    tpad = -(-tlen // LANES) * LANES
    wext = KSPAN + LANES
    assert wext - CHUNK >= WINDOW + CHUNK and tlen == 2 * REL_CLIP + 1
    tab = jnp.pad(rel_table, ((0, 0), (0, tpad - tlen)))
    return pl.pallas_call(
        functools.partial(_rel_bias_kernel, tpad=tpad, wext=wext),
        out_shape=jax.ShapeDtypeStruct((n_heads, QSUB, KSPAN), F32),
        compiler_params=_params(None),
        name="rel_bias",
    )(tab)


def _band_attn_kernel(q_ref, ktp_ref, ktc_ref, vp_ref, vc_ref, bias_ref, o_ref, ktcat_ref, vcat_ref,
                      *, n_heads, tq):
    t = pl.program_id(1)
    ktcat_ref[:, 0:WINDOW] = ktp_ref[...]
    ktcat_ref[:, WINDOW:WINDOW + tq] = ktc_ref[...]
    vcat_ref[0:WINDOW, :] = vp_ref[0]
    vcat_ref[WINDOW:WINDOW + tq, :] = vc_ref[0]

    def run(first_tile):
        for s in range(tq // QSUB):
            r0 = s * QSUB
            if first_tile:
                before_start = lax.broadcasted_iota(jnp.int32, (1, KSPAN), 1) + r0 < WINDOW
            for h in range(n_heads):
                cs = slice(h * HEAD_DIM, (h + 1) * HEAD_DIM)
                sc = jnp.dot(q_ref[0, r0:r0 + QSUB, cs], ktcat_ref[cs, r0:r0 + KSPAN], preferred_element_type=F32)
                sc = sc + bias_ref[h]
                if first_tile:
                    sc = jnp.where(before_start, NEG, sc)
                m = jnp.max(sc, axis=-1, keepdims=True)
                p = jnp.exp(sc - m)
                l = jnp.sum(p, axis=-1, keepdims=True)
                o = jnp.dot(p.astype(BF16), vcat_ref[r0:r0 + KSPAN, cs], preferred_element_type=F32) / l
                o_ref[0, r0:r0 + QSUB, cs] = o.astype(o_ref.dtype)

    @pl.when(t == 0)
    def _():
        run(True)

    @pl.when(t > 0)
    def _():
        run(False)


def _band_attn(q3, kt, v3, bias, n_heads, tq):
    b, s, d_a = q3.shape
    assert tq == WINDOW and s % tq == 0
    nt = s // tq
    cur = lambda bi, t: (bi, t, 0)
    prev = lambda bi, t: (bi, jnp.maximum(t - 1, 0), 0)
    return pl.pallas_call(
        functools.partial(_band_attn_kernel, n_heads=n_heads, tq=tq),
        grid=(b, nt),
        in_specs=[
            pl.BlockSpec((1, tq, d_a), cur),
            pl.BlockSpec((d_a, WINDOW), lambda bi, t: (0, bi * nt + jnp.maximum(t - 1, 0))),
            pl.BlockSpec((d_a, tq), lambda bi, t: (0, bi * nt + t)),
            pl.BlockSpec((1, WINDOW, d_a), prev),
            pl.BlockSpec((1, tq, d_a), cur),
            pl.BlockSpec((n_heads, QSUB, KSPAN), lambda bi, t: (0, 0, 0)),
        ],
        out_specs=pl.BlockSpec((1, tq, d_a), cur),
        out_shape=jax.ShapeDtypeStruct((b, s, d_a), BF16),
        scratch_shapes=[pltpu.VMEM((d_a, WINDOW + tq), BF16), pltpu.VMEM((WINDOW + tq, d_a), BF16)],
        compiler_params=_params(("parallel", "parallel")),
        name="band_attn",
    )(q3, kt, kt, v3, v3, bias)


def _band_attn_sample_kernel(q_ref, kn_ref, vn_ref, ck_hbm, cv_hbm, bias_ref, o_ref, kbuf, vbuf, sem,
                             *, n_heads, t, p_len):
    b = pl.program_id(0)
    slot = b % 2

    def cache_copies(seq, sl):
        cps = []
        for h in range(n_heads):
            cps.append(pltpu.make_async_copy(ck_hbm.at[seq, :, h, :], kbuf.at[sl, h], sem.at[0, sl, h]))
            cps.append(pltpu.make_async_copy(cv_hbm.at[seq, :, h, :], vbuf.at[sl, h], sem.at[1, sl, h]))
        return cps

    @pl.when(b == 0)
    def _():
        for cp in cache_copies(0, 0):
            cp.start()

    @pl.when(b + 1 < pl.num_programs(0))
    def _():
        for cp in cache_copies(b + 1, 1 - slot):
            cp.start()

    for cp in cache_copies(b, slot):
        cp.wait()

    heads = [slice(h * HEAD_DIM, (h + 1) * HEAD_DIM) for h in range(n_heads)]
    dn = (((1,), (1,)), ((), ()))
    scores = []
    for h, cs in enumerate(heads):
        q = q_ref[0, :, cs]
        ck = kbuf[slot, h].astype(BF16)
        scores.append((lax.dot_general(q, ck, dn, preferred_element_type=F32),
                       lax.dot_general(q, kn_ref[0, :, cs], dn, preferred_element_type=F32)))
    for h, cs in enumerate(heads):
        cv = vbuf[slot, h].astype(BF16)
        vn = vn_ref[0, :, cs]
        s1 = scores[h][0] + bias_ref[h, 0:t, 0:p_len]
        s2 = scores[h][1] + bias_ref[h, 0:t, p_len:p_len + t]
        m = jnp.maximum(jnp.max(s1, axis=-1, keepdims=True), jnp.max(s2, axis=-1, keepdims=True))
        p1 = jnp.exp(s1 - m)
        p2 = jnp.exp(s2 - m)
        l = jnp.sum(p1, axis=-1, keepdims=True) + jnp.sum(p2, axis=-1, keepdims=True)
        o = (jnp.dot(p1.astype(BF16), cv, preferred_element_type=F32)
             + jnp.dot(p2.astype(BF16), vn, preferred_element_type=F32)) / l
        o_ref[0, :, cs] = o.astype(o_ref.dtype)


def _band_attn_sample(q3, k3, v3, cache_k, cache_v, bias, n_heads):
    b, t, d_a = q3.shape
    p_len = cache_k.shape[1]
    assert p_len == WINDOW and t <= CHUNK
    cache = pl.BlockSpec(memory_space=pl.ANY)
    return pl.pallas_call(
        functools.partial(_band_attn_sample_kernel, n_heads=n_heads, t=t, p_len=p_len),
        grid=(b,),
        in_specs=[
            pl.BlockSpec((1, t, d_a), lambda bi: (bi, 0, 0)),
            pl.BlockSpec((1, t, d_a), lambda bi: (bi, 0, 0)),
            pl.BlockSpec((1, t, d_a), lambda bi: (bi, 0, 0)),
            cache, cache,
            pl.BlockSpec((n_heads, QSUB, KSPAN), lambda bi: (0, 0, 0)),
        ],
        out_specs=pl.BlockSpec((1, t, d_a), lambda bi: (bi, 0, 0)),
        out_shape=jax.ShapeDtypeStruct((b, t, d_a), BF16),
        scratch_shapes=[
            pltpu.VMEM((2, n_heads, p_len, HEAD_DIM), cache_k.dtype),
            pltpu.VMEM((2, n_heads, p_len, HEAD_DIM), cache_v.dtype),
            pltpu.SemaphoreType.DMA((2, 2, n_heads)),
        ],
        compiler_params=_params(("arbitrary",)),
        name="band_attn_sample",
    )(q3, k3, v3, cache_k, cache_v, bias)


def _mem_attn_kernel(q_ref, mk_ref, mv_ref, o_ref, *, n_heads, head_axis):
    heads = [slice(h * HEAD_DIM, (h + 1) * HEAD_DIM) for h in range(n_heads)]
    load = (lambda ref, h, cs: ref[0, :, h, :]) if head_axis else (lambda ref, h, cs: ref[0, :, cs])
    scores = [lax.dot_general(q_ref[0, :, cs], load(mk_ref, h, cs).astype(BF16), (((1,), (1,)), ((), ())),
                              preferred_element_type=F32) for h, cs in enumerate(heads)]
    for h, cs in enumerate(heads):
        v = load(mv_ref, h, cs).astype(BF16)
        sc = scores[h]
        m = jnp.max(sc, axis=-1, keepdims=True)
        p = jnp.exp(sc - m)
        l = jnp.sum(p, axis=-1, keepdims=True)
        o = jnp.dot(p.astype(BF16), v, preferred_element_type=F32) / l
        o_ref[0, :, cs] = o.astype(o_ref.dtype)


def _mem_attn(qm3, mem_k, mem_v, n_heads, tm):
    b, s, d_m = qm3.shape
    n_mem = mem_k.shape[1]
    head_axis = mem_k.ndim == 4
    if head_axis:
        kspec = vspec = pl.BlockSpec((1, n_mem, n_heads, HEAD_DIM), lambda bi, t: (bi, 0, 0, 0))
    else:
        kspec = pl.BlockSpec((1, n_mem, d_m), lambda bi, t: (bi, 0, 0))
        vspec = pl.BlockSpec((1, n_mem, d_m), lambda bi, t: (bi, 0, 1))
    return pl.pallas_call(
        functools.partial(_mem_attn_kernel, n_heads=n_heads, head_axis=head_axis),
        grid=(b, s // tm),
        in_specs=[pl.BlockSpec((1, tm, d_m), lambda bi, t: (bi, t, 0)), kspec, vspec],
        out_specs=pl.BlockSpec((1, tm, d_m), lambda bi, t: (bi, t, 0)),
        out_shape=jax.ShapeDtypeStruct((b, s, d_m), BF16),
        compiler_params=_params(("parallel", "parallel")),
        name="mem_attn",
    )(qm3, mem_k, mem_v)


def _rglru_core(xb_ref, gb_ref, past_ref, h0_ref, cw_ref, cb_ref, wa_ref, ba_ref, wi_ref, bi_ref, lam_ref,
                buf_ref, a_ref, u_ref, hc_ref, tt, width, after_gates=None):
    t = pl.program_id(1)
    pad = SUBLANES

    @pl.when(t == 0)
    def _():
        buf_ref[0:pad, :] = past_ref[0]
        hc_ref[...] = h0_ref[0]

    buf_ref[pad:pad + tt, :] = xb_ref[0]
    xc = cb_ref[...] + cw_ref[width - 1:width, :] * xb_ref[0]
    for k in range(1, width):
        xc = xc + cw_ref[width - 1 - k:width - k, :] * buf_ref[pad - k:pad - k + tt, :]
    buf_ref[0:pad, :] = buf_ref[tt:tt + pad, :]

    xcb = xc.astype(BF16)
    def sigmoid(z):
        return 0.5 * jnp.tanh(0.5 * z) + 0.5

    r = sigmoid(jnp.dot(xcb, wa_ref[...], preferred_element_type=F32) + ba_ref[...])
    i = sigmoid(jnp.dot(xcb, wi_ref[...], preferred_element_type=F32) + bi_ref[...])
    if after_gates is not None:
        after_gates()
    log_a = -LRU_C * r * jax.nn.softplus(-lam_ref[...])
    a = jnp.exp(log_a)
    th = jnp.tanh(log_a)
    n = -2.0 * th
    dd = 1.0 - th
    u = jnp.where(n > 0.0, n * lax.rsqrt(n * dd), 0.0) * (i * xc)

    d = a.shape[-1]
    a = a.reshape(tt // SUBLANES, SUBLANES, d)
    u = u.reshape(tt // SUBLANES, SUBLANES, d)
    rowm = lax.broadcasted_iota(jnp.int32, (1, SUBLANES, 1), 1)
    sh = 1
    while sh < SUBLANES:
        keep = rowm >= sh
        a_sh = pltpu.roll(a, sh, 1)
        u_sh = pltpu.roll(u, sh, 1)
        u = jnp.where(keep, a * u_sh + u, u)
        a = jnp.where(keep, a * a_sh, a)
        sh *= 2
    a_ref[...] = a.reshape(tt, d)
    u_ref[...] = u.reshape(tt, d)

    def body(g, h):
        r0 = pl.multiple_of(g * SUBLANES, SUBLANES)
        hr = a_ref[pl.ds(r0, SUBLANES), :] * h + u_ref[pl.ds(r0, SUBLANES), :]
        u_ref[pl.ds(r0, SUBLANES), :] = hr
        return hr[SUBLANES - 1:SUBLANES, :]

    h_last = lax.fori_loop(0, tt // SUBLANES, body, hc_ref[...])
    hc_ref[...] = h_last
    return h_last, (u_ref[...] * jax.nn.gelu(gb_ref[0])).astype(BF16)


def _rglru_kernel(xb_ref, gb_ref, past_ref, h0_ref, cw_ref, cb_ref, wa_ref, ba_ref, wi_ref, bi_ref, lam_ref,
                  o_ref, hl_ref, buf_ref, a_ref, u_ref, hc_ref, *, tt, width):
    h_last, ob = _rglru_core(xb_ref, gb_ref, past_ref, h0_ref, cw_ref, cb_ref, wa_ref, ba_ref, wi_ref, bi_ref,
                             lam_ref, buf_ref, a_ref, u_ref, hc_ref, tt, width)
    hl_ref[0] = h_last
    o_ref[0] = ob


def _rglru_out_proj_kernel(x_ref, oa_ref, om_ref, xb_ref, gb_ref, past_ref, h0_ref, cw_ref, cb_ref, wa_ref, ba_ref,
                           wi_ref, bi_ref, lam_ref, woam_ref, wob_ref, g_ref,
                           o_ref, xn_ref, hl_ref, buf_ref, a_ref, u_ref, hc_ref, *, tt, width):
    def attention_share():
        lhs = jnp.concatenate([oa_ref[0], om_ref[0]], axis=1)
        o_ref[0] = jnp.dot(lhs, woam_ref[...], preferred_element_type=F32)

    h_last, ob = _rglru_core(xb_ref, gb_ref, past_ref, h0_ref, cw_ref, cb_ref, wa_ref, ba_ref, wi_ref, bi_ref,
                             lam_ref, buf_ref, a_ref, u_ref, hc_ref, tt, width, after_gates=attention_share)
    hl_ref[0] = h_last
    x1 = x_ref[0] + o_ref[0] + jnp.dot(ob, wob_ref[...], preferred_element_type=F32)
    o_ref[0] = x1
    xn_ref[0] = _rms(x1, g_ref[...]).astype(BF16)


def _rglru(xg3, past8, h0, cw, cb, wa_bd, ba, wi_bd, bi, lam, tt):
    b, s, _ = xg3.shape
    d = cw.shape[1]
    width = cw.shape[0]
    vec = pl.BlockSpec((1, d), lambda bi_, t: (0, 0))
    mat = pl.BlockSpec((d, d), lambda bi_, t: (0, 0))
    return pl.pallas_call(
        functools.partial(_rglru_kernel, tt=tt, width=width),
        grid=(b, s // tt),
        in_specs=[
            pl.BlockSpec((1, tt, d), lambda bi_, t: (bi_, t, 0)),
            pl.BlockSpec((1, tt, d), lambda bi_, t: (bi_, t, 1)),
            pl.BlockSpec((1, SUBLANES, d), lambda bi_, t: (bi_, 0, 0)),
            pl.BlockSpec((1, 1, d), lambda bi_, t: (bi_, 0, 0)),
            pl.BlockSpec((width, d), lambda bi_, t: (0, 0)),
            vec, mat, vec, mat, vec, vec,
        ],
        out_specs=[
            pl.BlockSpec((1, tt, d), lambda bi_, t: (bi_, t, 0)),
            pl.BlockSpec((1, 1, d), lambda bi_, t: (bi_, 0, 0)),
        ],
        out_shape=[jax.ShapeDtypeStruct((b, s, d), BF16), jax.ShapeDtypeStruct((b, 1, d), F32)],
        scratch_shapes=[
            pltpu.VMEM((tt + 2 * SUBLANES, d), F32),
            pltpu.VMEM((tt, d), F32),
            pltpu.VMEM((tt, d), F32),
            pltpu.VMEM((1, d), F32),
        ],
        compiler_params=_params(("parallel", "arbitrary")),
        name="rglru",
    )(xg3, xg3, past8, h0.reshape(b, 1, d), cw, cb.reshape(1, d), wa_bd, ba.reshape(1, d),
      wi_bd, bi.reshape(1, d), lam.reshape(1, d))


def _rglru_out_proj(x3, oa3, om3, xg3, past8, h0, cw, cb, wa_bd, ba, wi_bd, bi, lam, woam, wob, g_next, tt):
    b, s, dm = x3.shape
    d = cw.shape[1]
    width = cw.shape[0]
    vec = pl.BlockSpec((1, d), lambda bi_, t: (0, 0))
    mat = pl.BlockSpec((d, d), lambda bi_, t: (0, 0))
    row = lambda c, j=0: pl.BlockSpec((1, tt, c), lambda bi_, t: (bi_, t, j))
    full = lambda w: pl.BlockSpec(w.shape, lambda bi_, t: (0, 0), pipeline_mode=pl.Buffered(1))
    return pl.pallas_call(
        functools.partial(_rglru_out_proj_kernel, tt=tt, width=width),
        grid=(b, s // tt),
        in_specs=[
            row(dm), row(oa3.shape[2]), row(om3.shape[2]), row(d, 0), row(d, 1),
            pl.BlockSpec((1, SUBLANES, d), lambda bi_, t: (bi_, 0, 0)),
            pl.BlockSpec((1, 1, d), lambda bi_, t: (bi_, 0, 0)),
            pl.BlockSpec((width, d), lambda bi_, t: (0, 0)),
            vec, mat, vec, mat, vec, vec,
            full(woam), full(wob),
            pl.BlockSpec((1, dm), lambda bi_, t: (0, 0)),
        ],
        out_specs=[row(dm), row(dm), pl.BlockSpec((1, 1, d), lambda bi_, t: (bi_, 0, 0))],
        out_shape=[jax.ShapeDtypeStruct((b, s, dm), F32), jax.ShapeDtypeStruct((b, s, dm), BF16),
                   jax.ShapeDtypeStruct((b, 1, d), F32)],
        scratch_shapes=[
            pltpu.VMEM((tt + 2 * SUBLANES, d), F32),
            pltpu.VMEM((tt, d), F32),
            pltpu.VMEM((tt, d), F32),
            pltpu.VMEM((1, d), F32),
        ],
        compiler_params=_params(("parallel", "arbitrary")),
        name="rglru_out_proj",
    )(x3, oa3, om3, xg3, xg3, past8, h0.reshape(b, 1, d), cw, cb.reshape(1, d), wa_bd, ba.reshape(1, d),
      wi_bd, bi.reshape(1, d), lam.reshape(1, d), woam, wob, g_next.reshape(1, dm))


def _out_proj_kernel(x_ref, oa_ref, om_ref, ob_ref, wam_ref, wb_ref, g_ref, o_ref, xn_ref):
    lhs = jnp.concatenate([oa_ref[...], om_ref[...]], axis=1)
    acc = jnp.dot(lhs, wam_ref[...], preferred_element_type=F32)
    acc = acc + jnp.dot(ob_ref[...], wb_ref[...], preferred_element_type=F32)
    x1 = x_ref[...] + acc
    o_ref[...] = x1
    xn_ref[...] = _rms(x1, g_ref[...]).astype(BF16)


def _out_proj(x, oa, om, ob, wam, wb, g_next, tm):
    m, d = x.shape
    row = lambda c: pl.BlockSpec((tm, c), lambda i: (i, 0))
    full = lambda a: pl.BlockSpec(a.shape, lambda i: (0, 0))
    return pl.pallas_call(
        _out_proj_kernel,
        grid=(m // tm,),
        in_specs=[row(d), row(oa.shape[1]), row(om.shape[1]), row(ob.shape[1]), full(wam), full(wb),
                  pl.BlockSpec((1, d), lambda i: (0, 0))],
        out_specs=[row(d), row(d)],
        out_shape=[jax.ShapeDtypeStruct((m, d), F32), jax.ShapeDtypeStruct((m, d), BF16)],
        compiler_params=_params(("parallel",)),
        name="out_proj",
    )(x, oa, om, ob, wam, wb, g_next.reshape(1, d))


def _conv_ffn_kernel(x_hbm, xn_ref, hist_ref, wv_ref, wg_ref, cw_ref, cb_ref, wd_ref,
                     o_hbm, f_ref, acc_ref, bv_ref, bg_ref, car_ref, sem_in, sem_out,
                     *, nseq, seq_rows, tiles_per_seq, width):
    i = pl.program_id(0)
    c = pl.program_id(1)
    nt = pl.num_programs(0)
    nc = pl.num_programs(1)
    pad = SUBLANES
    tm = nseq * seq_rows
    tf = wv_ref.shape[1]
    sub = min(tf, MXU_COLS)
    streams = ((c, wv_ref, bv_ref), (nc + c, wg_ref, bg_ref))
    slot = i % 2

    def fetch(tile, sl):
        return pltpu.make_async_copy(x_hbm.at[pl.ds(tile * tm, tm)], acc_ref.at[sl], sem_in.at[sl])

    def drain(tile, sl):
        return pltpu.make_async_copy(acc_ref.at[sl], o_hbm.at[pl.ds(tile * tm, tm)], sem_out.at[sl])

    @pl.when(jnp.logical_and(i == 0, c == 0))
    def _():
        fetch(0, 0).start()

    @pl.when(c == 0)
    def _():
        fetch(i, slot).wait()

    @pl.when(c == 1)
    def _():
        @pl.when(i >= 1)
        def _():
            drain(i - 1, 1 - slot).wait()

        @pl.when(i + 1 < nt)
        def _():
            fetch(i + 1, 1 - slot).start()

    for j, _, buf_ref in streams:
        if tiles_per_seq == 1:
            buf_ref[:, 0:pad, :] = hist_ref[:, j]
        else:
            first = i % tiles_per_seq == 0

            @pl.when(first)
            def _(j=j, buf_ref=buf_ref):
                buf_ref[:, 0:pad, :] = hist_ref[:, j]

            @pl.when(jnp.logical_not(first))
            def _(j=j, buf_ref=buf_ref):
                buf_ref[:, 0:pad, :] = car_ref[j]

    def up_proj(cs):
        for _, w_ref, buf_ref in streams:
            up = jnp.dot(xn_ref[...], w_ref[:, cs], preferred_element_type=F32)
            buf_ref[:, pad:pad + seq_rows, cs] = up.reshape(nseq, seq_rows, sub)

    def conv(j, buf_ref, cs):
        y = cb_ref[j, :, cs]
        for k in range(width):
            y = y + cw_ref[j, width - 1 - k:width - k, cs] * buf_ref[:, pad - k:pad - k + seq_rows, cs]
        return y

    subs = [slice(s * sub, (s + 1) * sub) for s in range(tf // sub)]
    up_proj(subs[0])
    for s, cs in enumerate(subs):
        if s + 1 < len(subs):
            up_proj(subs[s + 1])
        val = conv(streams[0][0], bv_ref, cs)
        gate = conv(streams[1][0], bg_ref, cs)
        h = (jax.nn.gelu(gate) * val).astype(BF16).reshape(tm, sub)
        acc_ref[slot] += jnp.dot(h, wd_ref[cs, :], preferred_element_type=F32)

    for j, _, buf_ref in streams:
        last = buf_ref[:, seq_rows:seq_rows + pad, :]
        f_ref[0, j] = last
        if tiles_per_seq > 1:
            car_ref[j] = last

    @pl.when(c == nc - 1)
    def _():
        drain(i, slot).start()

        @pl.when(i == nt - 1)
        def _():
            drain(i, slot).wait()


def _conv_ffn(x, xn, f_conv_past, w_up, cw, cb, w_down, nseq, seq_rows, tiles_per_seq, tf):
    m, d = x.shape
    d_ff = w_down.shape[0]
    width = cw.shape[0]
    tm = nseq * seq_rows
    nt = m // tm
    nc = d_ff // tf
    nb = f_conv_past.shape[0]
    assert nc >= 2 and m == nt * tm
    hist = jnp.pad(f_conv_past, ((0, 0), (SUBLANES - (width - 1), 0), (0, 0)))
    hist = hist.reshape(nb, SUBLANES, 2 * nc, tf).transpose(0, 2, 1, 3)
    cw3 = cw.reshape(width, 2 * nc, tf).transpose(1, 0, 2)
    cb3 = cb.reshape(2 * nc, 1, tf)
    kern = functools.partial(_conv_ffn_kernel, nseq=nseq, seq_rows=seq_rows,
                             tiles_per_seq=tiles_per_seq, width=width)
    car_shape = (2 * nc, nseq, SUBLANES, tf) if tiles_per_seq > 1 else (1, 1, SUBLANES, LANES)
    y, f = pl.pallas_call(
        kern,
        grid=(nt, nc),
        in_specs=[
            pl.BlockSpec(memory_space=pl.ANY),
            pl.BlockSpec((tm, d), lambda i, c: (i, 0)),
            pl.BlockSpec((nseq, 2 * nc, SUBLANES, tf), lambda i, c: (i // tiles_per_seq, 0, 0, 0)),
            pl.BlockSpec((d, tf), lambda i, c: (0, c)),
            pl.BlockSpec((d, tf), lambda i, c: (0, nc + c)),
            pl.BlockSpec((2 * nc, width, tf), lambda i, c: (0, 0, 0)),
            pl.BlockSpec((2 * nc, 1, tf), lambda i, c: (0, 0, 0)),
            pl.BlockSpec((tf, d), lambda i, c: (c, 0)),
        ],
        out_specs=[
            pl.BlockSpec(memory_space=pl.ANY),
            pl.BlockSpec((1, 2 * nc, nseq, SUBLANES, tf), lambda i, c: (i, 0, 0, 0, 0)),
        ],
        out_shape=[
            jax.ShapeDtypeStruct((m, d), F32),
            jax.ShapeDtypeStruct((nt, 2 * nc, nseq, SUBLANES, tf), F32),
        ],
        scratch_shapes=[
            pltpu.VMEM((2, tm, d), F32),
            pltpu.VMEM((nseq, seq_rows + 2 * SUBLANES, tf), F32),
            pltpu.VMEM((nseq, seq_rows + 2 * SUBLANES, tf), F32),
            pltpu.VMEM(car_shape, F32),
            pltpu.SemaphoreType.DMA((2,)),
            pltpu.SemaphoreType.DMA((2,)),
        ],
        compiler_params=_params(("arbitrary", "arbitrary")),
        name="conv_ffn",
    )(x, xn, hist, w_up, w_up, cw3, cb3, w_down)
    f = f.reshape(nt // tiles_per_seq, tiles_per_seq, 2 * nc, nseq, SUBLANES, tf)[:, -1]
    f = f.transpose(0, 2, 3, 1, 4).reshape(nb, SUBLANES, 2 * d_ff)
    return y, f[:, SUBLANES - (width - 1):]


def _pad_rows_front(a, rows):
    return jnp.pad(a, ((0, 0), (rows - a.shape[1], 0), (0, 0)))


def _block_diag(w):
    n, bi, bo = w.shape
    eye = jnp.eye(n, dtype=w.dtype)
    return (w[:, :, None, :] * eye[:, None, :, None]).reshape(n * bi, n * bo)


def _trunk(x3, band_fn, mem_k, mem_v, b_conv_past, b_h0, f_conv_past, p, *, in_tiles, transposed, tq_mem, tt,
           ffn_tiles, ffn_cols):
    b, t, d = x3.shape
    n_heads_a, n_heads_m, d_rnn = p["n_heads_a"], p["n_heads_m"], p["d_rnn"]
    d_a, d_m = n_heads_a * HEAD_DIM, n_heads_m * HEAD_DIM
    x2 = x3.reshape(b * t, d)
    nseq_in, rows_in, tiles_in = in_tiles
    q, k, v, qm, xg, k_a, v_a = _in_proj(x2, p["g_attn"], p["w_in"], p["g_q_a"], p["g_k_a"], p["g_q_m"],
                                         n_heads_a, n_heads_m, d_rnn, nseq_in, rows_in, tiles_in, transposed)
    xg3 = xg.reshape(b, t, 2 * d_rnn)

    o_a = band_fn(q, k, v)
    o_m = _mem_attn(qm.reshape(b, t, d_m), mem_k, mem_v, n_heads_m, tq_mem)
    lru = (xg3, _pad_rows_front(b_conv_past, SUBLANES), b_h0, p["conv_b_w"], p["conv_b_b"],
           p["wa_bd"], p["b_rg_a"], p["wi_bd"], p["b_rg_i"], p["lru_lambda"])
    wo = (p["wo_am"], p["wo_b"], p["g_ffn"])
    if tt >= OUT_ROWS:
        x1, xn1, h_last = _rglru_out_proj(x3, o_a, o_m, *lru, *wo, tt)
        x1, xn1 = x1.reshape(b * t, d), xn1.reshape(b * t, d)
    else:
        o_b, h_last = _rglru(*lru, tt)
        x1, xn1 = _out_proj(x2, o_a.reshape(b * t, d_a), o_m.reshape(b * t, d_m), o_b.reshape(b * t, d_rnn),
                            *wo, OUT_ROWS)

    nseq, seq_rows, tiles_per_seq = ffn_tiles
    y, f_new = _conv_ffn(x1, xn1, f_conv_past, p["w_up"], p["conv_f_w"], p["conv_f_b"], p["w_down"],
                         nseq, seq_rows, tiles_per_seq, ffn_cols)

    wb = p["conv_b_w"].shape[0]
    xb = xg3[:, :, :d_rnn]
    b_conv_new = jnp.concatenate([b_conv_past, xb], axis=1)[:, -(wb - 1):] if t < wb - 1 else xb[:, t - (wb - 1):]
    return y.reshape(b, t, d), k_a, v_a, b_conv_new, h_last.reshape(b, d_rnn), f_new


def kernel(x_prompt, x_sample, cache_a_k, cache_a_v, cache_mem_k, cache_mem_v, cache_b_conv, state_b_h, cache_f_conv, mem_prompt, g_attn, w_in, g_q_a, g_k_a, rel_table, g_q_m, g_k_m, g_mem, w_mem_kv, conv_b_w, conv_b_b, w_rg_a, b_rg_a, w_rg_i, b_rg_i, lru_lambda, w_out, g_ffn, w_up, conv_f_w, conv_f_b, w_down):
    depth = w_in.shape[0]
    bp, sp, d = x_prompt.shape
    bs, ts, _ = x_sample.shape
    n_heads_a = cache_a_k.shape[3]
    n_heads_m = cache_mem_k.shape[3]
    n_mem = mem_prompt.shape[1]
    d_a, d_m = n_heads_a * HEAD_DIM, n_heads_m * HEAD_DIM
    d_rnn = conv_b_w.shape[2]
    d_ff = w_down.shape[1]
    keep = min(WINDOW, sp)
    ones = lambda n: jnp.ones((n,), F32)

    xp, xs = x_prompt, x_sample
    outs = [[] for _ in range(12)]
    for l in range(depth):
        p = dict(
            n_heads_a=n_heads_a, n_heads_m=n_heads_m, d_rnn=d_rnn,
            g_attn=g_attn[l], w_in=w_in[l].astype(BF16),
            g_q_a=g_q_a[l] * SCALE, g_k_a=g_k_a[l], g_q_m=g_q_m[l] * SCALE,
            conv_b_w=conv_b_w[l], conv_b_b=conv_b_b[l],
            wa_bd=_block_diag(w_rg_a[l]).astype(BF16), b_rg_a=b_rg_a[l],
            wi_bd=_block_diag(w_rg_i[l]).astype(BF16), b_rg_i=b_rg_i[l], lru_lambda=lru_lambda[l],
            wo_am=w_out[l, :d_a + d_m].astype(BF16), wo_b=w_out[l, d_a + d_m:].astype(BF16),
            g_ffn=g_ffn[l], w_up=w_up[l].astype(BF16), conv_f_w=conv_f_w[l], conv_f_b=conv_f_b[l],
            w_down=w_down[l].astype(BF16),
        )
        bias = _rel_bias(rel_table[l])

        mem_gains = jnp.concatenate([jnp.tile(g_k_m[l], n_heads_m), ones(d_m)])
        kv = _norm_matmul(mem_prompt.reshape(bp * n_mem, d), g_mem[l], w_mem_kv[l].astype(BF16), mem_gains,
                          ((0, 1),), 512, d_m)
        kv3 = kv.reshape(bp, n_mem, 2 * d_m)

        ffn_tm = 1024
        band_p = lambda q, kt, v: _band_attn(q.reshape(bp, sp, d_a), kt, v.reshape(bp, sp, d_a), bias,
                                             n_heads_a, WINDOW)
        xp, k_a, v_a, bc, bh, fc = _trunk(
            xp, band_p, kv3, kv3,
            jnp.zeros((bp, conv_b_w.shape[1] - 1, d_rnn), F32), jnp.zeros((bp, d_rnn), F32),
            jnp.zeros((bp, conv_f_w.shape[1] - 1, 2 * d_ff), F32), p,
            in_tiles=(1, keep, sp // keep), transposed=True, tq_mem=512, tt=512,
            ffn_tiles=(1, ffn_tm, sp // ffn_tm), ffn_cols=FFN_COLS)
        mem_k = kv3[:, :, :d_m].reshape(bp, n_mem, n_heads_m, HEAD_DIM)
        mem_v = kv3[:, :, d_m:].reshape(bp, n_mem, n_heads_m, HEAD_DIM)
        for lst, v in zip(outs[:7], (k_a, v_a, mem_k, mem_v, bc, bh, fc)):
            lst.append(v)

        band_s = lambda q, k, v: _band_attn_sample(
            q.reshape(bs, ts, d_a), k.reshape(bs, ts, d_a), v.reshape(bs, ts, d_a),
            cache_a_k[l], cache_a_v[l], bias, n_heads_a)
        xs, k_a, v_a, bc, bh, fc = _trunk(
            xs, band_s, cache_mem_k[l], cache_mem_v[l],
            cache_b_conv[l], state_b_h[l], cache_f_conv[l], p,
            in_tiles=(bs, ts, 1), transposed=False, tq_mem=ts, tt=ts, ffn_tiles=(bs, ts, 1),
            ffn_cols=FFN_COLS)
        for lst, v in zip(outs[7:], (k_a, v_a, bc, bh, fc)):
            lst.append(v)

    stacked = [jnp.stack(o) for o in outs]
    return (xp, xs, *stacked)
```

```python
import functools

import jax
import jax.numpy as jnp
from jax import lax
from jax.experimental import pallas as pl
from jax.experimental.pallas import tpu as pltpu

EPS = 1e-6
HEAD_DIM = 128
CHUNK = 64
LEFT_CHUNKS = 8
WINDOW = LEFT_CHUNKS * CHUNK
REL_CLIP = 256
LRU_C = 8.0
NEG = -1e30
SCALE = HEAD_DIM ** -0.5

LANES = 128
SUBLANES = 8
MXU_COLS = 256
VMEM_LIMIT = 56 * 2 ** 20

QSUB = 4 * CHUNK
KSPAN = WINDOW + QSUB
OUT_ROWS = 512
FFN_ROWS = 1024
FFN_COLS = 512

BF16 = jnp.bfloat16
F32 = jnp.float32


def _params(sem):
    return pltpu.CompilerParams(dimension_semantics=sem, vmem_limit_bytes=VMEM_LIMIT)


def _rms(x, g):
    ms = jnp.mean(x * x, axis=-1, keepdims=True)
    return x * lax.rsqrt(ms + EPS) * g


def _norm_matmul_kernel(x_ref, g_ref, w_ref, hg_ref, o_ref, xn_ref, *, norm_ranges, tn):
    j = pl.program_id(1)

    @pl.when(j == 0)
    def _():
        xn_ref[...] = _rms(x_ref[...], g_ref[...]).astype(BF16)

    z = jnp.dot(xn_ref[...], w_ref[...], preferred_element_type=F32)
    cond = None
    for lo, hi in norm_ranges:
        c = jnp.logical_and(j >= lo, j < hi)
        cond = c if cond is None else jnp.logical_or(cond, c)

    @pl.when(cond)
    def _():
        for h in range(tn // HEAD_DIM):
            cs = slice(h * HEAD_DIM, (h + 1) * HEAD_DIM)
            o_ref[:, cs] = _rms(z[:, cs], hg_ref[:, cs])

    @pl.when(jnp.logical_not(cond))
    def _():
        o_ref[...] = z


def _norm_matmul(x, g, w_bf16, head_gains, norm_ranges, tm, tn):
    m, k = x.shape
    n = w_bf16.shape[1]
    kern = functools.partial(_norm_matmul_kernel, norm_ranges=norm_ranges, tn=tn)
    return pl.pallas_call(
        kern,
        grid=(m // tm, n // tn),
        in_specs=[
            pl.BlockSpec((tm, k), lambda i, j: (i, 0)),
            pl.BlockSpec((1, k), lambda i, j: (0, 0)),
            pl.BlockSpec((k, tn), lambda i, j: (0, j)),
            pl.BlockSpec((1, tn), lambda i, j: (0, j)),
        ],
        out_specs=pl.BlockSpec((tm, tn), lambda i, j: (i, j)),
        out_shape=jax.ShapeDtypeStruct((m, n), F32),
        scratch_shapes=[pltpu.VMEM((tm, k), BF16)],
        compiler_params=_params(("parallel", "arbitrary")),
        name="norm_matmul",
    )(x, g.reshape(1, k), w_bf16, head_gains.reshape(1, n))


def _in_proj_kernel(x_ref, g_ref, w_ref, gq_ref, gk_ref, gqm_ref, q_ref, k_ref, v_ref, qm_ref, xg_ref, ak_ref, av_ref,
                    xn_ref, kv_ref, *, d_a, d_m, d_rnn, group, tiles_per_seq, transposed):
    i = pl.program_id(0)
    xn_ref[...] = _rms(x_ref[...], g_ref[...]).astype(BF16)

    def cols(c0):
        return jnp.dot(xn_ref[...], w_ref[:, c0:c0 + group], preferred_element_type=F32)

    def heads(z, c0):
        return [(slice(c0 + h * HEAD_DIM, c0 + (h + 1) * HEAD_DIM), z[:, h * HEAD_DIM:(h + 1) * HEAD_DIM])
                for h in range(group // HEAD_DIM)]

    def put(ref, cs, val):
        if transposed:
            ref[cs, :] = val.T.astype(BF16)
        else:
            ref[:, cs] = val.astype(BF16)

    for c0 in range(0, d_a, group):
        for cs, zh in heads(cols(c0), c0):
            q_ref[:, cs] = _rms(zh, gq_ref[...]).astype(BF16)
    for c0 in range(0, d_a, group):
        for cs, zh in heads(cols(d_a + c0), c0):
            kh = _rms(zh, gk_ref[...])
            put(k_ref, cs, kh)
            kv_ref[:, cs] = kh
    for c0 in range(0, d_a, group):
        for cs, zh in heads(cols(2 * d_a + c0), c0):
            v_ref[:, cs] = zh.astype(BF16)
            kv_ref[:, slice(d_a + cs.start, d_a + cs.stop)] = zh
    for c0 in range(0, d_m, group):
        for cs, zh in heads(cols(3 * d_a + c0), c0):
            qm_ref[:, cs] = _rms(zh, gqm_ref[...]).astype(BF16)
    for c0 in range(0, 2 * d_rnn, group):
        xg_ref[:, c0:c0 + group] = cols(3 * d_a + d_m + c0)

    @pl.when(i % tiles_per_seq == tiles_per_seq - 1)
    def _():
        nseq, rows, n_heads = ak_ref.shape[0], ak_ref.shape[1], ak_ref.shape[2]
        for h in range(n_heads):
            cs = slice(h * HEAD_DIM, (h + 1) * HEAD_DIM)
            ak_ref[:, :, h, :] = kv_ref[:, cs].reshape(nseq, rows, HEAD_DIM)
            av_ref[:, :, h, :] = kv_ref[:, slice(d_a + cs.start, d_a + cs.stop)].reshape(nseq, rows, HEAD_DIM)


def _in_proj(x, g, w_bf16, gq, gk, gqm, n_heads_a, n_heads_m, d_rnn, nseq, rows, tiles_per_seq, transposed):
    m, k = x.shape
    d_in = w_bf16.shape[1]
    d_a, d_m = n_heads_a * HEAD_DIM, n_heads_m * HEAD_DIM
    tm = nseq * rows
    nt = m // tm
    n_cache_seq = nt * nseq // tiles_per_seq
    kern = functools.partial(_in_proj_kernel, d_a=d_a, d_m=d_m, d_rnn=d_rnn, group=4 * HEAD_DIM,
                             tiles_per_seq=tiles_per_seq, transposed=transposed)
    row = lambda c: pl.BlockSpec((tm, c), lambda i: (i, 0))
    vec = pl.BlockSpec((1, HEAD_DIM), lambda i: (0, 0))
    cache = pl.BlockSpec((nseq, rows, n_heads_a, HEAD_DIM), lambda i: (i // tiles_per_seq, 0, 0, 0))
    k_spec = pl.BlockSpec((d_a, tm), lambda i: (0, i)) if transposed else row(d_a)
    k_shape = jax.ShapeDtypeStruct((d_a, m) if transposed else (m, d_a), BF16)
    return pl.pallas_call(
        kern,
        grid=(nt,),
        in_specs=[
            row(k),
            pl.BlockSpec((1, k), lambda i: (0, 0)),
            pl.BlockSpec((k, d_in), lambda i: (0, 0), pipeline_mode=pl.Buffered(1)),
            vec, vec, vec,
        ],
        out_specs=[row(d_a), k_spec, row(d_a), row(d_m), row(2 * d_rnn), cache, cache],
        out_shape=[
            jax.ShapeDtypeStruct((m, d_a), BF16),
            k_shape,
            jax.ShapeDtypeStruct((m, d_a), BF16),
            jax.ShapeDtypeStruct((m, d_m), BF16),
            jax.ShapeDtypeStruct((m, 2 * d_rnn), F32),
            jax.ShapeDtypeStruct((n_cache_seq, rows, n_heads_a, HEAD_DIM), F32),
            jax.ShapeDtypeStruct((n_cache_seq, rows, n_heads_a, HEAD_DIM), F32),
        ],
        scratch_shapes=[pltpu.VMEM((tm, k), BF16), pltpu.VMEM((tm, 2 * d_a), F32)],
        compiler_params=_params(("arbitrary",)),
        name="in_proj",
    )(x, g.reshape(1, k), w_bf16, gq.reshape(1, HEAD_DIM), gk.reshape(1, HEAD_DIM), gqm.reshape(1, HEAD_DIM))


def _rel_bias_kernel(tab_ref, o_ref, *, tpad, wext):
    t = tab_ref[...]
    hi = t.astype(BF16)
    r1 = t - hi.astype(F32)
    mid = r1.astype(BF16)
    lo = (r1 - mid.astype(F32)).astype(BF16)
    m = lax.broadcasted_iota(jnp.int32, (1, wext), 1)
    x = jnp.where(m >= wext - CHUNK, m - wext, m)
    idx = jnp.clip(WINDOW - x, -REL_CLIP, REL_CLIP) + REL_CLIP
    tix = lax.broadcasted_iota(jnp.int32, (tpad, wext), 0)
    onehot = jnp.where(tix == idx, 1.0, 0.0).astype(BF16)
    v = (jnp.dot(hi, onehot, preferred_element_type=F32)
         + jnp.dot(mid, onehot, preferred_element_type=F32)
         + jnp.dot(lo, onehot, preferred_element_type=F32))
    j = lax.broadcasted_iota(jnp.int32, (1, KSPAN), 1)
    for ci in range(QSUB // CHUNK):
        jj = j - CHUNK * ci
        valid = jnp.logical_and(jj >= 0, jj < WINDOW + CHUNK)
        for ii in range(CHUNK):
            s = ci * CHUNK + ii
            row = (v if s == 0 else pltpu.roll(v, s, 1))[:, :KSPAN]
            o_ref[:, s, :] = jnp.where(valid, row, NEG)


def _rel_bias(rel_table):
    n_heads, tlen = rel_table.shape
    tpad = -(-tlen // LANES) * LANES
    wext = KSPAN + LANES
    assert wext - CHUNK >= WINDOW + CHUNK and tlen == 2 * REL_CLIP + 1
    tab = jnp.pad(rel_table, ((0, 0), (0, tpad - tlen)))
    return pl.pallas_call(
        functools.partial(_rel_bias_kernel, tpad=tpad, wext=wext),
        out_shape=jax.ShapeDtypeStruct((n_heads, QSUB, KSPAN), F32),
        compiler_params=_params(None),
        name="rel_bias",
    )(tab)


def _band_attn_kernel(q_ref, ktp_ref, ktc_ref, vp_ref, vc_ref, bias_ref, o_ref, ktcat_ref, vcat_ref,
                      *, n_heads, tq):
    t = pl.program_id(1)
    ktcat_ref[:, 0:WINDOW] = ktp_ref[...]
    ktcat_ref[:, WINDOW:WINDOW + tq] = ktc_ref[...]
    vcat_ref[0:WINDOW, :] = vp_ref[0]
    vcat_ref[WINDOW:WINDOW + tq, :] = vc_ref[0]

    def run(first_tile):
        for s in range(tq // QSUB):
            r0 = s * QSUB
            if first_tile:
                before_start = lax.broadcasted_iota(jnp.int32, (1, KSPAN), 1) + r0 < WINDOW
            for h in range(n_heads):
                cs = slice(h * HEAD_DIM, (h + 1) * HEAD_DIM)
                sc = jnp.dot(q_ref[0, r0:r0 + QSUB, cs], ktcat_ref[cs, r0:r0 + KSPAN], preferred_element_type=F32)
                sc = sc + bias_ref[h]
                if first_tile:
                    sc = jnp.where(before_start, NEG, sc)
                m = jnp.max(sc, axis=-1, keepdims=True)
                p = jnp.exp(sc - m)
                l = jnp.sum(p, axis=-1, keepdims=True)
                o = jnp.dot(p.astype(BF16), vcat_ref[r0:r0 + KSPAN, cs], preferred_element_type=F32) / l
                o_ref[0, r0:r0 + QSUB, cs] = o.astype(o_ref.dtype)

    @pl.when(t == 0)
    def _():
        run(True)

    @pl.when(t > 0)
    def _():
        run(False)


def _band_attn(q3, kt, v3, bias, n_heads, tq):
    b, s, d_a = q3.shape
    assert tq == WINDOW and s % tq == 0
    nt = s // tq
    cur = lambda bi, t: (bi, t, 0)
    prev = lambda bi, t: (bi, jnp.maximum(t - 1, 0), 0)
    return pl.pallas_call(
        functools.partial(_band_attn_kernel, n_heads=n_heads, tq=tq),
        grid=(b, nt),
        in_specs=[
            pl.BlockSpec((1, tq, d_a), cur),
            pl.BlockSpec((d_a, WINDOW), lambda bi, t: (0, bi * nt + jnp.maximum(t - 1, 0))),
            pl.BlockSpec((d_a, tq), lambda bi, t: (0, bi * nt + t)),
            pl.BlockSpec((1, WINDOW, d_a), prev),
            pl.BlockSpec((1, tq, d_a), cur),
            pl.BlockSpec((n_heads, QSUB, KSPAN), lambda bi, t: (0, 0, 0)),
        ],
        out_specs=pl.BlockSpec((1, tq, d_a), cur),
        out_shape=jax.ShapeDtypeStruct((b, s, d_a), BF16),
        scratch_shapes=[pltpu.VMEM((d_a, WINDOW + tq), BF16), pltpu.VMEM((WINDOW + tq, d_a), BF16)],
        compiler_params=_params(("parallel", "parallel")),
        name="band_attn",
    )(q3, kt, kt, v3, v3, bias)


def _band_attn_sample_kernel(q_ref, kn_ref, vn_ref, ck_hbm, cv_hbm, bias_ref, o_ref, kbuf, vbuf, sem,
                             *, n_heads, t, p_len):
    b = pl.program_id(0)
    slot = b % 2

    def cache_copies(seq, sl):
        cps = []
        for h in range(n_heads):
            cps.append(pltpu.make_async_copy(ck_hbm.at[seq, :, h, :], kbuf.at[sl, h], sem.at[0, sl, h]))
            cps.append(pltpu.make_async_copy(cv_hbm.at[seq, :, h, :], vbuf.at[sl, h], sem.at[1, sl, h]))
        return cps

    @pl.when(b == 0)
    def _():
        for cp in cache_copies(0, 0):
            cp.start()

    @pl.when(b + 1 < pl.num_programs(0))
    def _():
        for cp in cache_copies(b + 1, 1 - slot):
            cp.start()

    for cp in cache_copies(b, slot):
        cp.wait()

    heads = [slice(h * HEAD_DIM, (h + 1) * HEAD_DIM) for h in range(n_heads)]
    dn = (((1,), (1,)), ((), ()))
    scores = []
    for h, cs in enumerate(heads):
        q = q_ref[0, :, cs]
        ck = kbuf[slot, h].astype(BF16)
        scores.append((lax.dot_general(q, ck, dn, preferred_element_type=F32),
                       lax.dot_general(q, kn_ref[0, :, cs], dn, preferred_element_type=F32)))
    for h, cs in enumerate(heads):
        cv = vbuf[slot, h].astype(BF16)
        vn = vn_ref[0, :, cs]
        s1 = scores[h][0] + bias_ref[h, 0:t, 0:p_len]
        s2 = scores[h][1] + bias_ref[h, 0:t, p_len:p_len + t]
        m = jnp.maximum(jnp.max(s1, axis=-1, keepdims=True), jnp.max(s2, axis=-1, keepdims=True))
        p1 = jnp.exp(s1 - m)
        p2 = jnp.exp(s2 - m)
        l = jnp.sum(p1, axis=-1, keepdims=True) + jnp.sum(p2, axis=-1, keepdims=True)
        o = (jnp.dot(p1.astype(BF16), cv, preferred_element_type=F32)
             + jnp.dot(p2.astype(BF16), vn, preferred_element_type=F32)) / l
        o_ref[0, :, cs] = o.astype(o_ref.dtype)


def _band_attn_sample(q3, k3, v3, cache_k, cache_v, bias, n_heads):
    b, t, d_a = q3.shape
    p_len = cache_k.shape[1]
    assert p_len == WINDOW and t <= CHUNK
    cache = pl.BlockSpec(memory_space=pl.ANY)
    return pl.pallas_call(
        functools.partial(_band_attn_sample_kernel, n_heads=n_heads, t=t, p_len=p_len),
        grid=(b,),
        in_specs=[
            pl.BlockSpec((1, t, d_a), lambda bi: (bi, 0, 0)),
            pl.BlockSpec((1, t, d_a), lambda bi: (bi, 0, 0)),
            pl.BlockSpec((1, t, d_a), lambda bi: (bi, 0, 0)),
            cache, cache,
            pl.BlockSpec((n_heads, QSUB, KSPAN), lambda bi: (0, 0, 0)),
        ],
        out_specs=pl.BlockSpec((1, t, d_a), lambda bi: (bi, 0, 0)),
        out_shape=jax.ShapeDtypeStruct((b, t, d_a), BF16),
        scratch_shapes=[
            pltpu.VMEM((2, n_heads, p_len, HEAD_DIM), cache_k.dtype),
            pltpu.VMEM((2, n_heads, p_len, HEAD_DIM), cache_v.dtype),
            pltpu.SemaphoreType.DMA((2, 2, n_heads)),
        ],
        compiler_params=_params(("arbitrary",)),
        name="band_attn_sample",
    )(q3, k3, v3, cache_k, cache_v, bias)


def _mem_attn_kernel(q_ref, mk_ref, mv_ref, o_ref, *, n_heads, head_axis):
    heads = [slice(h * HEAD_DIM, (h + 1) * HEAD_DIM) for h in range(n_heads)]
    load = (lambda ref, h, cs: ref[0, :, h, :]) if head_axis else (lambda ref, h, cs: ref[0, :, cs])
    scores = [lax.dot_general(q_ref[0, :, cs], load(mk_ref, h, cs).astype(BF16), (((1,), (1,)), ((), ())),
                              preferred_element_type=F32) for h, cs in enumerate(heads)]
    for h, cs in enumerate(heads):
        v = load(mv_ref, h, cs).astype(BF16)
        sc = scores[h]
        m = jnp.max(sc, axis=-1, keepdims=True)
        p = jnp.exp(sc - m)
        l = jnp.sum(p, axis=-1, keepdims=True)
        o = jnp.dot(p.astype(BF16), v, preferred_element_type=F32) / l
        o_ref[0, :, cs] = o.astype(o_ref.dtype)


def _mem_attn(qm3, mem_k, mem_v, n_heads, tm):
    b, s, d_m = qm3.shape
    n_mem = mem_k.shape[1]
    head_axis = mem_k.ndim == 4
    if head_axis:
        kspec = vspec = pl.BlockSpec((1, n_mem, n_heads, HEAD_DIM), lambda bi, t: (bi, 0, 0, 0))
    else:
        kspec = pl.BlockSpec((1, n_mem, d_m), lambda bi, t: (bi, 0, 0))
        vspec = pl.BlockSpec((1, n_mem, d_m), lambda bi, t: (bi, 0, 1))
    return pl.pallas_call(
        functools.partial(_mem_attn_kernel, n_heads=n_heads, head_axis=head_axis),
        grid=(b, s // tm),
        in_specs=[pl.BlockSpec((1, tm, d_m), lambda bi, t: (bi, t, 0)), kspec, vspec],
        out_specs=pl.BlockSpec((1, tm, d_m), lambda bi, t: (bi, t, 0)),
        out_shape=jax.ShapeDtypeStruct((b, s, d_m), BF16),
        compiler_params=_params(("parallel", "parallel")),
        name="mem_attn",
    )(qm3, mem_k, mem_v)


def _rglru_core(xb_ref, gb_ref, past_ref, h0_ref, cw_ref, cb_ref, wa_ref, ba_ref, wi_ref, bi_ref, lam_ref,
                buf_ref, a_ref, u_ref, hc_ref, tt, width, after_gates=None):
    t = pl.program_id(1)
    pad = SUBLANES

    @pl.when(t == 0)
    def _():
        buf_ref[0:pad, :] = past_ref[0]
        hc_ref[...] = h0_ref[0]

    buf_ref[pad:pad + tt, :] = xb_ref[0]
    xc = cb_ref[...] + cw_ref[width - 1:width, :] * xb_ref[0]
    for k in range(1, width):
        xc = xc + cw_ref[width - 1 - k:width - k, :] * buf_ref[pad - k:pad - k + tt, :]
    buf_ref[0:pad, :] = buf_ref[tt:tt + pad, :]

    xcb = xc.astype(BF16)
    def sigmoid(z):
        return 0.5 * jnp.tanh(0.5 * z) + 0.5

    r = sigmoid(jnp.dot(xcb, wa_ref[...], preferred_element_type=F32) + ba_ref[...])
    i = sigmoid(jnp.dot(xcb, wi_ref[...], preferred_element_type=F32) + bi_ref[...])
    if after_gates is not None:
        after_gates()
    log_a = -LRU_C * r * jax.nn.softplus(-lam_ref[...])
    a = jnp.exp(log_a)
    th = jnp.tanh(log_a)
    n = -2.0 * th
    dd = 1.0 - th
    u = jnp.where(n > 0.0, n * lax.rsqrt(n * dd), 0.0) * (i * xc)

    d = a.shape[-1]
    a = a.reshape(tt // SUBLANES, SUBLANES, d)
    u = u.reshape(tt // SUBLANES, SUBLANES, d)
    rowm = lax.broadcasted_iota(jnp.int32, (1, SUBLANES, 1), 1)
    sh = 1
    while sh < SUBLANES:
        keep = rowm >= sh
        a_sh = pltpu.roll(a, sh, 1)
        u_sh = pltpu.roll(u, sh, 1)
        u = jnp.where(keep, a * u_sh + u, u)
        a = jnp.where(keep, a * a_sh, a)
        sh *= 2
    a_ref[...] = a.reshape(tt, d)
    u_ref[...] = u.reshape(tt, d)

    def body(g, h):
        r0 = pl.multiple_of(g * SUBLANES, SUBLANES)
        hr = a_ref[pl.ds(r0, SUBLANES), :] * h + u_ref[pl.ds(r0, SUBLANES), :]
        u_ref[pl.ds(r0, SUBLANES), :] = hr
        return hr[SUBLANES - 1:SUBLANES, :]

    h_last = lax.fori_loop(0, tt // SUBLANES, body, hc_ref[...])
    hc_ref[...] = h_last
    return h_last, (u_ref[...] * jax.nn.gelu(gb_ref[0])).astype(BF16)


def _rglru_kernel(xb_ref, gb_ref, past_ref, h0_ref, cw_ref, cb_ref, wa_ref, ba_ref, wi_ref, bi_ref, lam_ref,
                  o_ref, hl_ref, buf_ref, a_ref, u_ref, hc_ref, *, tt, width):
    h_last, ob = _rglru_core(xb_ref, gb_ref, past_ref, h0_ref, cw_ref, cb_ref, wa_ref, ba_ref, wi_ref, bi_ref,
                             lam_ref, buf_ref, a_ref, u_ref, hc_ref, tt, width)
    hl_ref[0] = h_last
    o_ref[0] = ob


def _rglru_out_proj_kernel(x_ref, oa_ref, om_ref, xb_ref, gb_ref, past_ref, h0_ref, cw_ref, cb_ref, wa_ref, ba_ref,
                           wi_ref, bi_ref, lam_ref, woam_ref, wob_ref, g_ref,
                           o_ref, xn_ref, hl_ref, buf_ref, a_ref, u_ref, hc_ref, *, tt, width):
    def attention_share():
        lhs = jnp.concatenate([oa_ref[0], om_ref[0]], axis=1)
        o_ref[0] = jnp.dot(lhs, woam_ref[...], preferred_element_type=F32)

    h_last, ob = _rglru_core(xb_ref, gb_ref, past_ref, h0_ref, cw_ref, cb_ref, wa_ref, ba_ref, wi_ref, bi_ref,
                             lam_ref, buf_ref, a_ref, u_ref, hc_ref, tt, width, after_gates=attention_share)
    hl_ref[0] = h_last
    x1 = x_ref[0] + o_ref[0] + jnp.dot(ob, wob_ref[...], preferred_element_type=F32)
    o_ref[0] = x1
    xn_ref[0] = _rms(x1, g_ref[...]).astype(BF16)


def _rglru(xg3, past8, h0, cw, cb, wa_bd, ba, wi_bd, bi, lam, tt):
    b, s, _ = xg3.shape
    d = cw.shape[1]
    width = cw.shape[0]
    vec = pl.BlockSpec((1, d), lambda bi_, t: (0, 0))
    mat = pl.BlockSpec((d, d), lambda bi_, t: (0, 0))
    return pl.pallas_call(
        functools.partial(_rglru_kernel, tt=tt, width=width),
        grid=(b, s // tt),
        in_specs=[
            pl.BlockSpec((1, tt, d), lambda bi_, t: (bi_, t, 0)),
            pl.BlockSpec((1, tt, d), lambda bi_, t: (bi_, t, 1)),
            pl.BlockSpec((1, SUBLANES, d), lambda bi_, t: (bi_, 0, 0)),
            pl.BlockSpec((1, 1, d), lambda bi_, t: (bi_, 0, 0)),
            pl.BlockSpec((width, d), lambda bi_, t: (0, 0)),
            vec, mat, vec, mat, vec, vec,
        ],
        out_specs=[
            pl.BlockSpec((1, tt, d), lambda bi_, t: (bi_, t, 0)),
            pl.BlockSpec((1, 1, d), lambda bi_, t: (bi_, 0, 0)),
        ],
        out_shape=[jax.ShapeDtypeStruct((b, s, d), BF16), jax.ShapeDtypeStruct((b, 1, d), F32)],
        scratch_shapes=[
            pltpu.VMEM((tt + 2 * SUBLANES, d), F32),
            pltpu.VMEM((tt, d), F32),
            pltpu.VMEM((tt, d), F32),
            pltpu.VMEM((1, d), F32),
        ],
        compiler_params=_params(("parallel", "arbitrary")),
        name="rglru",
    )(xg3, xg3, past8, h0.reshape(b, 1, d), cw, cb.reshape(1, d), wa_bd, ba.reshape(1, d),
      wi_bd, bi.reshape(1, d), lam.reshape(1, d))


def _rglru_out_proj(x3, oa3, om3, xg3, past8, h0, cw, cb, wa_bd, ba, wi_bd, bi, lam, woam, wob, g_next, tt):
    b, s, dm = x3.shape
    d = cw.shape[1]
    width = cw.shape[0]
    vec = pl.BlockSpec((1, d), lambda bi_, t: (0, 0))
    mat = pl.BlockSpec((d, d), lambda bi_, t: (0, 0))
    row = lambda c, j=0: pl.BlockSpec((1, tt, c), lambda bi_, t: (bi_, t, j))
    full = lambda w: pl.BlockSpec(w.shape, lambda bi_, t: (0, 0), pipeline_mode=pl.Buffered(1))
    return pl.pallas_call(
        functools.partial(_rglru_out_proj_kernel, tt=tt, width=width),
        grid=(b, s // tt),
        in_specs=[
            row(dm), row(oa3.shape[2]), row(om3.shape[2]), row(d, 0), row(d, 1),
            pl.BlockSpec((1, SUBLANES, d), lambda bi_, t: (bi_, 0, 0)),
            pl.BlockSpec((1, 1, d), lambda bi_, t: (bi_, 0, 0)),
            pl.BlockSpec((width, d), lambda bi_, t: (0, 0)),
            vec, mat, vec, mat, vec, vec,
            full(woam), full(wob),
            pl.BlockSpec((1, dm), lambda bi_, t: (0, 0)),
        ],
        out_specs=[row(dm), row(dm), pl.BlockSpec((1, 1, d), lambda bi_, t: (bi_, 0, 0))],
        out_shape=[jax.ShapeDtypeStruct((b, s, dm), F32), jax.ShapeDtypeStruct((b, s, dm), BF16),
                   jax.ShapeDtypeStruct((b, 1, d), F32)],
        scratch_shapes=[
            pltpu.VMEM((tt + 2 * SUBLANES, d), F32),
            pltpu.VMEM((tt, d), F32),
            pltpu.VMEM((tt, d), F32),
            pltpu.VMEM((1, d), F32),
        ],
        compiler_params=_params(("parallel", "arbitrary")),
        name="rglru_out_proj",
    )(x3, oa3, om3, xg3, xg3, past8, h0.reshape(b, 1, d), cw, cb.reshape(1, d), wa_bd, ba.reshape(1, d),
      wi_bd, bi.reshape(1, d), lam.reshape(1, d), woam, wob, g_next.reshape(1, dm))


def _out_proj_kernel(x_ref, oa_ref, om_ref, ob_ref, wam_ref, wb_ref, g_ref, o_ref, xn_ref):
    lhs = jnp.concatenate([oa_ref[...], om_ref[...]], axis=1)
    acc = jnp.dot(lhs, wam_ref[...], preferred_element_type=F32)
    acc = acc + jnp.dot(ob_ref[...], wb_ref[...], preferred_element_type=F32)
    x1 = x_ref[...] + acc
    o_ref[...] = x1
    xn_ref[...] = _rms(x1, g_ref[...]).astype(BF16)


def _out_proj(x, oa, om, ob, wam, wb, g_next, tm):
    m, d = x.shape
    row = lambda c: pl.BlockSpec((tm, c), lambda i: (i, 0))
    full = lambda a: pl.BlockSpec(a.shape, lambda i: (0, 0))
    return pl.pallas_call(
        _out_proj_kernel,
        grid=(m // tm,),
        in_specs=[row(d), row(oa.shape[1]), row(om.shape[1]), row(ob.shape[1]), full(wam), full(wb),
                  pl.BlockSpec((1, d), lambda i: (0, 0))],
        out_specs=[row(d), row(d)],
        out_shape=[jax.ShapeDtypeStruct((m, d), F32), jax.ShapeDtypeStruct((m, d), BF16)],
        compiler_params=_params(("parallel",)),
        name="out_proj",
    )(x, oa, om, ob, wam, wb, g_next.reshape(1, d))


def _conv_ffn_kernel(x_hbm, xn_ref, hist_ref, wv_ref, wg_ref, cw_ref, cb_ref, wd_ref,
                     o_hbm, f_ref, acc_ref, bv_ref, bg_ref, car_ref, sem_in, sem_out,
                     *, nseq, seq_rows, tiles_per_seq, width):
    i = pl.program_id(0)
    c = pl.program_id(1)
    nt = pl.num_programs(0)
    nc = pl.num_programs(1)
    pad = SUBLANES
    tm = nseq * seq_rows
    tf = wv_ref.shape[1]
    sub = min(tf, MXU_COLS)
    streams = ((c, wv_ref, bv_ref), (nc + c, wg_ref, bg_ref))
    cols = lambda j, start=0, size=tf: pl.ds(pl.multiple_of(j * tf + start, sub), size)
    slot = i % 2

    def fetch(tile, sl):
        return pltpu.make_async_copy(x_hbm.at[pl.ds(tile * tm, tm)], acc_ref.at[sl], sem_in.at[sl])

    def drain(tile, sl):
        return pltpu.make_async_copy(acc_ref.at[sl], o_hbm.at[pl.ds(tile * tm, tm)], sem_out.at[sl])

    @pl.when(jnp.logical_and(i == 0, c == 0))
    def _():
        fetch(0, 0).start()

    @pl.when(c == 0)
    def _():
        fetch(i, slot).wait()

    @pl.when(c == 1)
    def _():
        @pl.when(i >= 1)
        def _():
            drain(i - 1, 1 - slot).wait()

        @pl.when(i + 1 < nt)
        def _():
            fetch(i + 1, 1 - slot).start()

    for j, _, buf_ref in streams:
        if tiles_per_seq == 1:
            buf_ref[:, pad - (width - 1):pad, :] = hist_ref[:, :, cols(j)]
        else:
            first = i % tiles_per_seq == 0

            @pl.when(first)
            def _(j=j, buf_ref=buf_ref):
                buf_ref[:, pad - (width - 1):pad, :] = hist_ref[:, :, cols(j)]

            @pl.when(jnp.logical_not(first))
            def _(j=j, buf_ref=buf_ref):
                buf_ref[:, 0:pad, :] = car_ref[j]

    def up_proj(cs):
        for _, w_ref, buf_ref in streams:
            up = jnp.dot(xn_ref[...], w_ref[:, cs], preferred_element_type=F32)
            buf_ref[:, pad:pad + seq_rows, cs] = up.reshape(nseq, seq_rows, sub)

    def conv(j, buf_ref, cs):
        lanes = cols(j, cs.start, sub)
        y = cb_ref[:, lanes]
        for k in range(width):
            y = y + cw_ref[width - 1 - k:width - k, lanes] * buf_ref[:, pad - k:pad - k + seq_rows, cs]
        return y

    subs = [slice(s * sub, (s + 1) * sub) for s in range(tf // sub)]
    up_proj(subs[0])
    for s, cs in enumerate(subs):
        if s + 1 < len(subs):
            up_proj(subs[s + 1])
        val = conv(streams[0][0], bv_ref, cs)
        gate = conv(streams[1][0], bg_ref, cs)
        h = (jax.nn.gelu(gate) * val).astype(BF16).reshape(tm, sub)
        acc_ref[slot] += jnp.dot(h, wd_ref[cs, :], preferred_element_type=F32)

    for j, _, buf_ref in streams:
        f_ref[:, :, cols(j)] = buf_ref[:, pad + seq_rows - (width - 1):pad + seq_rows, :]
        if tiles_per_seq > 1:
            car_ref[j] = buf_ref[:, seq_rows:seq_rows + pad, :]

    @pl.when(c == nc - 1)
    def _():
        drain(i, slot).start()

        @pl.when(i == nt - 1)
        def _():
            drain(i, slot).wait()


def _conv_ffn(x, xn, f_conv_past, w_up, cw, cb, w_down, nseq, seq_rows, tiles_per_seq, tf):
    m, d = x.shape
    d_ff = w_down.shape[0]
    width = cw.shape[0]
    tm = nseq * seq_rows
    nt = m // tm
    nc = d_ff // tf
    nb = f_conv_past.shape[0]
    assert nc >= 2 and m == nt * tm
    assert f_conv_past.shape == (nb, width - 1, 2 * d_ff) and nb * tiles_per_seq == nt * nseq
    kern = functools.partial(_conv_ffn_kernel, nseq=nseq, seq_rows=seq_rows,
                             tiles_per_seq=tiles_per_seq, width=width)
    car_shape = (2 * nc, nseq, SUBLANES, tf) if tiles_per_seq > 1 else (1, 1, SUBLANES, LANES)
    y, f = pl.pallas_call(
        kern,
        grid=(nt, nc),
        in_specs=[
            pl.BlockSpec(memory_space=pl.ANY),
            pl.BlockSpec((tm, d), lambda i, c: (i, 0)),
            pl.BlockSpec((nseq, width - 1, 2 * d_ff), lambda i, c: (i // tiles_per_seq, 0, 0)),
            pl.BlockSpec((d, tf), lambda i, c: (0, c)),
            pl.BlockSpec((d, tf), lambda i, c: (0, nc + c)),
            pl.BlockSpec((width, 2 * d_ff), lambda i, c: (0, 0)),
            pl.BlockSpec((1, 2 * d_ff), lambda i, c: (0, 0)),
            pl.BlockSpec((tf, d), lambda i, c: (c, 0)),
        ],
        out_specs=[
            pl.BlockSpec(memory_space=pl.ANY),
            pl.BlockSpec((nseq, width - 1, 2 * d_ff), lambda i, c: (i, 0, 0)),
        ],
        out_shape=[
            jax.ShapeDtypeStruct((m, d), F32),
            jax.ShapeDtypeStruct((nt * nseq, width - 1, 2 * d_ff), F32),
        ],
        scratch_shapes=[
            pltpu.VMEM((2, tm, d), F32),
            pltpu.VMEM((nseq, seq_rows + 2 * SUBLANES, tf), F32),
            pltpu.VMEM((nseq, seq_rows + 2 * SUBLANES, tf), F32),
            pltpu.VMEM(car_shape, F32),
            pltpu.SemaphoreType.DMA((2,)),
            pltpu.SemaphoreType.DMA((2,)),
        ],
        compiler_params=_params(("arbitrary", "arbitrary")),
        name="conv_ffn",
    )(x, xn, f_conv_past, w_up, w_up, cw, cb.reshape(1, 2 * d_ff), w_down)
    f = f.reshape(nt // tiles_per_seq, tiles_per_seq, nseq, width - 1, 2 * d_ff)[:, -1]
    return y, f.reshape(nb, width - 1, 2 * d_ff)


def _pad_rows_front(a, rows):
    return jnp.pad(a, ((0, 0), (rows - a.shape[1], 0), (0, 0)))


def _block_diag(w):
    n, bi, bo = w.shape
    eye = jnp.eye(n, dtype=w.dtype)
    return (w[:, :, None, :] * eye[:, None, :, None]).reshape(n * bi, n * bo)


def _trunk(x3, band_fn, mem_k, mem_v, b_conv_past, b_h0, f_conv_past, p, *, in_tiles, transposed, tq_mem, tt,
           ffn_tiles, ffn_cols):
    b, t, d = x3.shape
    n_heads_a, n_heads_m, d_rnn = p["n_heads_a"], p["n_heads_m"], p["d_rnn"]
    d_a, d_m = n_heads_a * HEAD_DIM, n_heads_m * HEAD_DIM
    x2 = x3.reshape(b * t, d)
    nseq_in, rows_in, tiles_in = in_tiles
    q, k, v, qm, xg, k_a, v_a = _in_proj(x2, p["g_attn"], p["w_in"], p["g_q_a"], p["g_k_a"], p["g_q_m"],
                                         n_heads_a, n_heads_m, d_rnn, nseq_in, rows_in, tiles_in, transposed)
    xg3 = xg.reshape(b, t, 2 * d_rnn)

    o_a = band_fn(q, k, v)
    o_m = _mem_attn(qm.reshape(b, t, d_m), mem_k, mem_v, n_heads_m, tq_mem)
    lru = (xg3, _pad_rows_front(b_conv_past, SUBLANES), b_h0, p["conv_b_w"], p["conv_b_b"],
           p["wa_bd"], p["b_rg_a"], p["wi_bd"], p["b_rg_i"], p["lru_lambda"])
    wo = (p["wo_am"], p["wo_b"], p["g_ffn"])
    if tt >= OUT_ROWS:
        x1, xn1, h_last = _rglru_out_proj(x3, o_a, o_m, *lru, *wo, tt)
        x1, xn1 = x1.reshape(b * t, d), xn1.reshape(b * t, d)
    else:
        o_b, h_last = _rglru(*lru, tt)
        x1, xn1 = _out_proj(x2, o_a.reshape(b * t, d_a), o_m.reshape(b * t, d_m), o_b.reshape(b * t, d_rnn),
                            *wo, OUT_ROWS)

    nseq, seq_rows, tiles_per_seq = ffn_tiles
    y, f_new = _conv_ffn(x1, xn1, f_conv_past, p["w_up"], p["conv_f_w"], p["conv_f_b"], p["w_down"],
                         nseq, seq_rows, tiles_per_seq, ffn_cols)

    wb = p["conv_b_w"].shape[0]
    xb = xg3[:, :, :d_rnn]
    b_conv_new = jnp.concatenate([b_conv_past, xb], axis=1)[:, -(wb - 1):] if t < wb - 1 else xb[:, t - (wb - 1):]
    return y.reshape(b, t, d), k_a, v_a, b_conv_new, h_last.reshape(b, d_rnn), f_new


def kernel(x_prompt, x_sample, cache_a_k, cache_a_v, cache_mem_k, cache_mem_v, cache_b_conv, state_b_h, cache_f_conv, mem_prompt, g_attn, w_in, g_q_a, g_k_a, rel_table, g_q_m, g_k_m, g_mem, w_mem_kv, conv_b_w, conv_b_b, w_rg_a, b_rg_a, w_rg_i, b_rg_i, lru_lambda, w_out, g_ffn, w_up, conv_f_w, conv_f_b, w_down):
    depth = w_in.shape[0]
    bp, sp, d = x_prompt.shape
    bs, ts, _ = x_sample.shape
    n_heads_a = cache_a_k.shape[3]
    n_heads_m = cache_mem_k.shape[3]
    n_mem = mem_prompt.shape[1]
    d_a, d_m = n_heads_a * HEAD_DIM, n_heads_m * HEAD_DIM
    d_rnn = conv_b_w.shape[2]
    d_ff = w_down.shape[1]
    keep = min(WINDOW, sp)
    ones = lambda n: jnp.ones((n,), F32)

    xp, xs = x_prompt, x_sample
    outs = [[] for _ in range(12)]
    for l in range(depth):
        p = dict(
            n_heads_a=n_heads_a, n_heads_m=n_heads_m, d_rnn=d_rnn,
            g_attn=g_attn[l], w_in=w_in[l].astype(BF16),
            g_q_a=g_q_a[l] * SCALE, g_k_a=g_k_a[l], g_q_m=g_q_m[l] * SCALE,
            conv_b_w=conv_b_w[l], conv_b_b=conv_b_b[l],
            wa_bd=_block_diag(w_rg_a[l]).astype(BF16), b_rg_a=b_rg_a[l],
            wi_bd=_block_diag(w_rg_i[l]).astype(BF16), b_rg_i=b_rg_i[l], lru_lambda=lru_lambda[l],
            wo_am=w_out[l, :d_a + d_m].astype(BF16), wo_b=w_out[l, d_a + d_m:].astype(BF16),
            g_ffn=g_ffn[l], w_up=w_up[l].astype(BF16), conv_f_w=conv_f_w[l], conv_f_b=conv_f_b[l],
            w_down=w_down[l].astype(BF16),
        )
        bias = _rel_bias(rel_table[l])

        mem_gains = jnp.concatenate([jnp.tile(g_k_m[l], n_heads_m), ones(d_m)])
        kv = _norm_matmul(mem_prompt.reshape(bp * n_mem, d), g_mem[l], w_mem_kv[l].astype(BF16), mem_gains,
                          ((0, 1),), OUT_ROWS, d_m)
        kv3 = kv.reshape(bp, n_mem, 2 * d_m)

        band_p = lambda q, kt, v: _band_attn(q.reshape(bp, sp, d_a), kt, v.reshape(bp, sp, d_a), bias,
                                             n_heads_a, WINDOW)
        xp, k_a, v_a, bc, bh, fc = _trunk(
            xp, band_p, kv3, kv3,
            jnp.zeros((bp, conv_b_w.shape[1] - 1, d_rnn), F32), jnp.zeros((bp, d_rnn), F32),
            jnp.zeros((bp, conv_f_w.shape[1] - 1, 2 * d_ff), F32), p,
            in_tiles=(1, keep, sp // keep), transposed=True, tq_mem=OUT_ROWS, tt=OUT_ROWS,
            ffn_tiles=(1, FFN_ROWS, sp // FFN_ROWS), ffn_cols=FFN_COLS)
        mem_k = kv3[:, :, :d_m].reshape(bp, n_mem, n_heads_m, HEAD_DIM)
        mem_v = kv3[:, :, d_m:].reshape(bp, n_mem, n_heads_m, HEAD_DIM)
        for lst, v in zip(outs[:7], (k_a, v_a, mem_k, mem_v, bc, bh, fc)):
            lst.append(v)

        band_s = lambda q, k, v: _band_attn_sample(
            q.reshape(bs, ts, d_a), k.reshape(bs, ts, d_a), v.reshape(bs, ts, d_a),
            cache_a_k[l], cache_a_v[l], bias, n_heads_a)
        xs, k_a, v_a, bc, bh, fc = _trunk(
            xs, band_s, cache_mem_k[l], cache_mem_v[l],
            cache_b_conv[l], state_b_h[l], cache_f_conv[l], p,
            in_tiles=(bs, ts, 1), transposed=False, tq_mem=ts, tt=ts, ffn_tiles=(bs, ts, 1),
            ffn_cols=FFN_COLS)
        for lst, v in zip(outs[7:], (k_a, v_a, bc, bh, fc)):
            lst.append(v)

    stacked = [jnp.stack(o) for o in outs]
    return (xp, xs, *stacked)
```

```python
import functools

import jax
import jax.numpy as jnp
from jax import lax
from jax.experimental import pallas as pl
from jax.experimental.pallas import tpu as pltpu

EPS = 1e-6
HEAD_DIM = 128
CHUNK = 64
LEFT_CHUNKS = 8
WINDOW = LEFT_CHUNKS * CHUNK
REL_CLIP = 256
LRU_C = 8.0
NEG = -1e30
SCALE = HEAD_DIM ** -0.5

LANES = 128
SUBLANES = 8
MXU_COLS = 256
VMEM_LIMIT = 56 * 2 ** 20

QSUB = 4 * CHUNK
KSPAN = WINDOW + QSUB
OUT_ROWS = 512
FFN_ROWS = 1024
FFN_COLS = 512

BF16 = jnp.bfloat16
F32 = jnp.float32


def _params(sem):
    return pltpu.CompilerParams(dimension_semantics=sem, vmem_limit_bytes=VMEM_LIMIT)


def _rms(x, g):
    ms = jnp.mean(x * x, axis=-1, keepdims=True)
    return x * lax.rsqrt(ms + EPS) * g


def _norm_matmul_kernel(x_ref, g_ref, w_ref, hg_ref, o_ref, xn_ref, *, norm_ranges, tn):
    j = pl.program_id(1)

    @pl.when(j == 0)
    def _():
        xn_ref[...] = _rms(x_ref[...], g_ref[...]).astype(BF16)

    z = jnp.dot(xn_ref[...], w_ref[...], preferred_element_type=F32)
    cond = None
    for lo, hi in norm_ranges:
        c = jnp.logical_and(j >= lo, j < hi)
        cond = c if cond is None else jnp.logical_or(cond, c)

    @pl.when(cond)
    def _():
        for h in range(tn // HEAD_DIM):
            cs = slice(h * HEAD_DIM, (h + 1) * HEAD_DIM)
            o_ref[:, cs] = _rms(z[:, cs], hg_ref[:, cs])

    @pl.when(jnp.logical_not(cond))
    def _():
        o_ref[...] = z


def _norm_matmul(x, g, w_bf16, head_gains, norm_ranges, tm, tn):
    m, k = x.shape
    n = w_bf16.shape[1]
    kern = functools.partial(_norm_matmul_kernel, norm_ranges=norm_ranges, tn=tn)
    return pl.pallas_call(
        kern,
        grid=(m // tm, n // tn),
        in_specs=[
            pl.BlockSpec((tm, k), lambda i, j: (i, 0)),
            pl.BlockSpec((1, k), lambda i, j: (0, 0)),
            pl.BlockSpec((k, tn), lambda i, j: (0, j)),
            pl.BlockSpec((1, tn), lambda i, j: (0, j)),
        ],
        out_specs=pl.BlockSpec((tm, tn), lambda i, j: (i, j)),
        out_shape=jax.ShapeDtypeStruct((m, n), F32),
        scratch_shapes=[pltpu.VMEM((tm, k), BF16)],
        compiler_params=_params(("parallel", "arbitrary")),
        name="norm_matmul",
    )(x, g.reshape(1, k), w_bf16, head_gains.reshape(1, n))


def _in_proj_kernel(x_ref, g_ref, w_ref, gq_ref, gk_ref, gqm_ref, q_ref, k_ref, v_ref, qm_ref, xg_ref, ak_ref, av_ref,
                    xn_ref, kv_ref, *, d_a, d_m, d_rnn, group, tiles_per_seq, transposed):
    i = pl.program_id(0)
    xn_ref[...] = _rms(x_ref[...], g_ref[...]).astype(BF16)

    def cols(c0):
        return jnp.dot(xn_ref[...], w_ref[:, c0:c0 + group], preferred_element_type=F32)

    def heads(z, c0):
        return [(slice(c0 + h * HEAD_DIM, c0 + (h + 1) * HEAD_DIM), z[:, h * HEAD_DIM:(h + 1) * HEAD_DIM])
                for h in range(group // HEAD_DIM)]

    def put(ref, cs, val):
        if transposed:
            ref[cs, :] = val.T.astype(BF16)
        else:
            ref[:, cs] = val.astype(BF16)

    for c0 in range(0, d_a, group):
        for cs, zh in heads(cols(c0), c0):
            q_ref[:, cs] = _rms(zh, gq_ref[...]).astype(BF16)
    for c0 in range(0, d_a, group):
        for cs, zh in heads(cols(d_a + c0), c0):
            kh = _rms(zh, gk_ref[...])
            put(k_ref, cs, kh)
            kv_ref[:, cs] = kh
    for c0 in range(0, d_a, group):
        for cs, zh in heads(cols(2 * d_a + c0), c0):
            v_ref[:, cs] = zh.astype(BF16)
            kv_ref[:, slice(d_a + cs.start, d_a + cs.stop)] = zh
    for c0 in range(0, d_m, group):
        for cs, zh in heads(cols(3 * d_a + c0), c0):
            qm_ref[:, cs] = _rms(zh, gqm_ref[...]).astype(BF16)
    for c0 in range(0, 2 * d_rnn, group):
        xg_ref[:, c0:c0 + group] = cols(3 * d_a + d_m + c0)

    @pl.when(i % tiles_per_seq == tiles_per_seq - 1)
    def _():
        nseq, rows, n_heads = ak_ref.shape[0], ak_ref.shape[1], ak_ref.shape[2]
        for h in range(n_heads):
            cs = slice(h * HEAD_DIM, (h + 1) * HEAD_DIM)
            ak_ref[:, :, h, :] = kv_ref[:, cs].reshape(nseq, rows, HEAD_DIM)
            av_ref[:, :, h, :] = kv_ref[:, slice(d_a + cs.start, d_a + cs.stop)].reshape(nseq, rows, HEAD_DIM)


def _in_proj(x, g, w_bf16, gq, gk, gqm, n_heads_a, n_heads_m, d_rnn, nseq, rows, tiles_per_seq, transposed):
    m, k = x.shape
    d_in = w_bf16.shape[1]
    d_a, d_m = n_heads_a * HEAD_DIM, n_heads_m * HEAD_DIM
    tm = nseq * rows
    nt = m // tm
    n_cache_seq = nt * nseq // tiles_per_seq
    kern = functools.partial(_in_proj_kernel, d_a=d_a, d_m=d_m, d_rnn=d_rnn, group=4 * HEAD_DIM,
                             tiles_per_seq=tiles_per_seq, transposed=transposed)
    row = lambda c: pl.BlockSpec((tm, c), lambda i: (i, 0))
    vec = pl.BlockSpec((1, HEAD_DIM), lambda i: (0, 0))
    cache = pl.BlockSpec((nseq, rows, n_heads_a, HEAD_DIM), lambda i: (i // tiles_per_seq, 0, 0, 0))
    k_spec = pl.BlockSpec((d_a, tm), lambda i: (0, i)) if transposed else row(d_a)
    k_shape = jax.ShapeDtypeStruct((d_a, m) if transposed else (m, d_a), BF16)
    return pl.pallas_call(
        kern,
        grid=(nt,),
        in_specs=[
            row(k),
            pl.BlockSpec((1, k), lambda i: (0, 0)),
            pl.BlockSpec((k, d_in), lambda i: (0, 0), pipeline_mode=pl.Buffered(1)),
            vec, vec, vec,
        ],
        out_specs=[row(d_a), k_spec, row(d_a), row(d_m), row(2 * d_rnn), cache, cache],
        out_shape=[
            jax.ShapeDtypeStruct((m, d_a), BF16),
            k_shape,
            jax.ShapeDtypeStruct((m, d_a), BF16),
            jax.ShapeDtypeStruct((m, d_m), BF16),
            jax.ShapeDtypeStruct((m, 2 * d_rnn), F32),
            jax.ShapeDtypeStruct((n_cache_seq, rows, n_heads_a, HEAD_DIM), F32),
            jax.ShapeDtypeStruct((n_cache_seq, rows, n_heads_a, HEAD_DIM), F32),
        ],
        scratch_shapes=[pltpu.VMEM((tm, k), BF16), pltpu.VMEM((tm, 2 * d_a), F32)],
        compiler_params=_params(("arbitrary",)),
        name="in_proj",
    )(x, g.reshape(1, k), w_bf16, gq.reshape(1, HEAD_DIM), gk.reshape(1, HEAD_DIM), gqm.reshape(1, HEAD_DIM))


def _rel_bias_kernel(tab_ref, o_ref, *, tpad, wext):
    t = tab_ref[...]
    hi = t.astype(BF16)
    r1 = t - hi.astype(F32)
    mid = r1.astype(BF16)
    lo = (r1 - mid.astype(F32)).astype(BF16)
    m = lax.broadcasted_iota(jnp.int32, (1, wext), 1)
    x = jnp.where(m >= wext - CHUNK, m - wext, m)
    idx = jnp.clip(WINDOW - x, -REL_CLIP, REL_CLIP) + REL_CLIP
    tix = lax.broadcasted_iota(jnp.int32, (tpad, wext), 0)
    onehot = jnp.where(tix == idx, 1.0, 0.0).astype(BF16)
    v = (jnp.dot(hi, onehot, preferred_element_type=F32)
         + jnp.dot(mid, onehot, preferred_element_type=F32)
         + jnp.dot(lo, onehot, preferred_element_type=F32))
    j = lax.broadcasted_iota(jnp.int32, (1, KSPAN), 1)
    for ci in range(QSUB // CHUNK):
        jj = j - CHUNK * ci
        valid = jnp.logical_and(jj >= 0, jj < WINDOW + CHUNK)
        for ii in range(CHUNK):
            s = ci * CHUNK + ii
            row = (v if s == 0 else pltpu.roll(v, s, 1))[:, :KSPAN]
            o_ref[:, s, :] = jnp.where(valid, row, NEG)


def _rel_bias(rel_table):
    n_heads, tlen = rel_table.shape
    tpad = -(-tlen // LANES) * LANES
    wext = KSPAN + LANES
    assert wext - CHUNK >= WINDOW + CHUNK and tlen == 2 * REL_CLIP + 1
    tab = jnp.pad(rel_table, ((0, 0), (0, tpad - tlen)))
    return pl.pallas_call(
        functools.partial(_rel_bias_kernel, tpad=tpad, wext=wext),
        out_shape=jax.ShapeDtypeStruct((n_heads, QSUB, KSPAN), F32),
        compiler_params=_params(None),
        name="rel_bias",
    )(tab)


def _band_attn_kernel(q_ref, ktp_ref, ktc_ref, vp_ref, vc_ref, bias_ref, o_ref, ktcat_ref, vcat_ref,
                      *, n_heads, tq):
    t = pl.program_id(1)
    ktcat_ref[:, 0:WINDOW] = ktp_ref[...]
    ktcat_ref[:, WINDOW:WINDOW + tq] = ktc_ref[...]
    vcat_ref[0:WINDOW, :] = vp_ref[0]
    vcat_ref[WINDOW:WINDOW + tq, :] = vc_ref[0]

    def run(first_tile):
        for s in range(tq // QSUB):
            r0 = s * QSUB
            if first_tile:
                before_start = lax.broadcasted_iota(jnp.int32, (1, KSPAN), 1) + r0 < WINDOW
            for h in range(n_heads):
                cs = slice(h * HEAD_DIM, (h + 1) * HEAD_DIM)
                sc = jnp.dot(q_ref[0, r0:r0 + QSUB, cs], ktcat_ref[cs, r0:r0 + KSPAN], preferred_element_type=F32)
                sc = sc + bias_ref[h]
                if first_tile:
                    sc = jnp.where(before_start, NEG, sc)
                m = jnp.max(sc, axis=-1, keepdims=True)
                p = jnp.exp(sc - m)
                l = jnp.sum(p, axis=-1, keepdims=True)
                o = jnp.dot(p.astype(BF16), vcat_ref[r0:r0 + KSPAN, cs], preferred_element_type=F32) / l
                o_ref[0, r0:r0 + QSUB, cs] = o.astype(o_ref.dtype)

    @pl.when(t == 0)
    def _():
        run(True)

    @pl.when(t > 0)
    def _():
        run(False)


def _band_attn(q3, kt, v3, bias, n_heads, tq):
    b, s, d_a = q3.shape
    assert tq == WINDOW and s % tq == 0
    nt = s // tq
    cur = lambda bi, t: (bi, t, 0)
    prev = lambda bi, t: (bi, jnp.maximum(t - 1, 0), 0)
    return pl.pallas_call(
        functools.partial(_band_attn_kernel, n_heads=n_heads, tq=tq),
        grid=(b, nt),
        in_specs=[
            pl.BlockSpec((1, tq, d_a), cur),
            pl.BlockSpec((d_a, WINDOW), lambda bi, t: (0, bi * nt + jnp.maximum(t - 1, 0))),
            pl.BlockSpec((d_a, tq), lambda bi, t: (0, bi * nt + t)),
            pl.BlockSpec((1, WINDOW, d_a), prev),
            pl.BlockSpec((1, tq, d_a), cur),
            pl.BlockSpec((n_heads, QSUB, KSPAN), lambda bi, t: (0, 0, 0)),
        ],
        out_specs=pl.BlockSpec((1, tq, d_a), cur),
        out_shape=jax.ShapeDtypeStruct((b, s, d_a), BF16),
        scratch_shapes=[pltpu.VMEM((d_a, WINDOW + tq), BF16), pltpu.VMEM((WINDOW + tq, d_a), BF16)],
        compiler_params=_params(("parallel", "parallel")),
        name="band_attn",
    )(q3, kt, kt, v3, v3, bias)


def _band_attn_sample_kernel(q_ref, kn_ref, vn_ref, ck_hbm, cv_hbm, bias_ref, o_ref, kbuf, vbuf, sem,
                             *, n_heads, t, p_len):
    b = pl.program_id(0)
    slot = b % 2

    def cache_copies(seq, sl):
        cps = []
        for h in range(n_heads):
            cps.append(pltpu.make_async_copy(ck_hbm.at[seq, :, h, :], kbuf.at[sl, h], sem.at[0, sl, h]))
            cps.append(pltpu.make_async_copy(cv_hbm.at[seq, :, h, :], vbuf.at[sl, h], sem.at[1, sl, h]))
        return cps

    @pl.when(b == 0)
    def _():
        for cp in cache_copies(0, 0):
            cp.start()

    @pl.when(b + 1 < pl.num_programs(0))
    def _():
        for cp in cache_copies(b + 1, 1 - slot):
            cp.start()

    for cp in cache_copies(b, slot):
        cp.wait()

    heads = [slice(h * HEAD_DIM, (h + 1) * HEAD_DIM) for h in range(n_heads)]
    dn = (((1,), (1,)), ((), ()))
    scores = []
    for h, cs in enumerate(heads):
        q = q_ref[0, :, cs]
        ck = kbuf[slot, h].astype(BF16)
        scores.append((lax.dot_general(q, ck, dn, preferred_element_type=F32),
                       lax.dot_general(q, kn_ref[0, :, cs], dn, preferred_element_type=F32)))
    for h, cs in enumerate(heads):
        cv = vbuf[slot, h].astype(BF16)
        vn = vn_ref[0, :, cs]
        s1 = scores[h][0] + bias_ref[h, 0:t, 0:p_len]
        s2 = scores[h][1] + bias_ref[h, 0:t, p_len:p_len + t]
        m = jnp.maximum(jnp.max(s1, axis=-1, keepdims=True), jnp.max(s2, axis=-1, keepdims=True))
        p1 = jnp.exp(s1 - m)
        p2 = jnp.exp(s2 - m)
        l = jnp.sum(p1, axis=-1, keepdims=True) + jnp.sum(p2, axis=-1, keepdims=True)
        o = (jnp.dot(p1.astype(BF16), cv, preferred_element_type=F32)
             + jnp.dot(p2.astype(BF16), vn, preferred_element_type=F32)) / l
        o_ref[0, :, cs] = o.astype(o_ref.dtype)


def _band_attn_sample(q3, k3, v3, cache_k, cache_v, bias, n_heads):
    b, t, d_a = q3.shape
    p_len = cache_k.shape[1]
    assert p_len == WINDOW and t <= CHUNK
    cache = pl.BlockSpec(memory_space=pl.ANY)
    return pl.pallas_call(
        functools.partial(_band_attn_sample_kernel, n_heads=n_heads, t=t, p_len=p_len),
        grid=(b,),
        in_specs=[
            pl.BlockSpec((1, t, d_a), lambda bi: (bi, 0, 0)),
            pl.BlockSpec((1, t, d_a), lambda bi: (bi, 0, 0)),
            pl.BlockSpec((1, t, d_a), lambda bi: (bi, 0, 0)),
            cache, cache,
            pl.BlockSpec((n_heads, QSUB, KSPAN), lambda bi: (0, 0, 0)),
        ],
        out_specs=pl.BlockSpec((1, t, d_a), lambda bi: (bi, 0, 0)),
        out_shape=jax.ShapeDtypeStruct((b, t, d_a), BF16),
        scratch_shapes=[
            pltpu.VMEM((2, n_heads, p_len, HEAD_DIM), cache_k.dtype),
            pltpu.VMEM((2, n_heads, p_len, HEAD_DIM), cache_v.dtype),
            pltpu.SemaphoreType.DMA((2, 2, n_heads)),
        ],
        compiler_params=_params(("arbitrary",)),
        name="band_attn_sample",
    )(q3, k3, v3, cache_k, cache_v, bias)


def _mem_attn_kernel(q_ref, mk_ref, mv_ref, o_ref, *, n_heads, head_axis):
    heads = [slice(h * HEAD_DIM, (h + 1) * HEAD_DIM) for h in range(n_heads)]
    load = (lambda ref, h, cs: ref[0, :, h, :]) if head_axis else (lambda ref, h, cs: ref[0, :, cs])
    scores = [lax.dot_general(q_ref[0, :, cs], load(mk_ref, h, cs).astype(BF16), (((1,), (1,)), ((), ())),
                              preferred_element_type=F32) for h, cs in enumerate(heads)]
    for h, cs in enumerate(heads):
        v = load(mv_ref, h, cs).astype(BF16)
        sc = scores[h]
        m = jnp.max(sc, axis=-1, keepdims=True)
        p = jnp.exp(sc - m)
        l = jnp.sum(p, axis=-1, keepdims=True)
        o = jnp.dot(p.astype(BF16), v, preferred_element_type=F32) / l
        o_ref[0, :, cs] = o.astype(o_ref.dtype)


def _mem_attn(qm3, mem_k, mem_v, n_heads, tm):
    b, s, d_m = qm3.shape
    n_mem = mem_k.shape[1]
    head_axis = mem_k.ndim == 4
    if head_axis:
        kspec = vspec = pl.BlockSpec((1, n_mem, n_heads, HEAD_DIM), lambda bi, t: (bi, 0, 0, 0))
    else:
        kspec = pl.BlockSpec((1, n_mem, d_m), lambda bi, t: (bi, 0, 0))
        vspec = pl.BlockSpec((1, n_mem, d_m), lambda bi, t: (bi, 0, 1))
    return pl.pallas_call(
        functools.partial(_mem_attn_kernel, n_heads=n_heads, head_axis=head_axis),
        grid=(b, s // tm),
        in_specs=[pl.BlockSpec((1, tm, d_m), lambda bi, t: (bi, t, 0)), kspec, vspec],
        out_specs=pl.BlockSpec((1, tm, d_m), lambda bi, t: (bi, t, 0)),
        out_shape=jax.ShapeDtypeStruct((b, s, d_m), BF16),
        compiler_params=_params(("parallel", "parallel")),
        name="mem_attn",
    )(qm3, mem_k, mem_v)


def _rglru_core(xb_ref, gb_ref, past_ref, h0_ref, cw_ref, cb_ref, wa_ref, ba_ref, wi_ref, bi_ref, lam_ref,
                buf_ref, a_ref, u_ref, hc_ref, tt, width, after_gates=None):
    t = pl.program_id(1)
    pad = SUBLANES

    @pl.when(t == 0)
    def _():
        buf_ref[0:pad, :] = past_ref[0]
        hc_ref[...] = h0_ref[0]

    buf_ref[pad:pad + tt, :] = xb_ref[0]
    xc = cb_ref[...] + cw_ref[width - 1:width, :] * xb_ref[0]
    for k in range(1, width):
        xc = xc + cw_ref[width - 1 - k:width - k, :] * buf_ref[pad - k:pad - k + tt, :]
    buf_ref[0:pad, :] = buf_ref[tt:tt + pad, :]

    xcb = xc.astype(BF16)
    def sigmoid(z):
        return 0.5 * jnp.tanh(0.5 * z) + 0.5

    r = sigmoid(jnp.dot(xcb, wa_ref[...], preferred_element_type=F32) + ba_ref[...])
    i = sigmoid(jnp.dot(xcb, wi_ref[...], preferred_element_type=F32) + bi_ref[...])
    if after_gates is not None:
        after_gates()
    log_a = -LRU_C * r * jax.nn.softplus(-lam_ref[...])
    a = jnp.exp(log_a)
    th = jnp.tanh(log_a)
    n = -2.0 * th
    dd = 1.0 - th
    u = jnp.where(n > 0.0, n * lax.rsqrt(n * dd), 0.0) * (i * xc)

    d = a.shape[-1]
    a = a.reshape(tt // SUBLANES, SUBLANES, d)
    u = u.reshape(tt // SUBLANES, SUBLANES, d)
    rowm = lax.broadcasted_iota(jnp.int32, (1, SUBLANES, 1), 1)
    sh = 1
    while sh < SUBLANES:
        keep = rowm >= sh
        a_sh = pltpu.roll(a, sh, 1)
        u_sh = pltpu.roll(u, sh, 1)
        u = jnp.where(keep, a * u_sh + u, u)
        a = jnp.where(keep, a * a_sh, a)
        sh *= 2
    a_ref[...] = a.reshape(tt, d)
    u_ref[...] = u.reshape(tt, d)

    h_last = hc_ref[...]
    for r0 in range(0, tt, SUBLANES):
        hr = a_ref[r0:r0 + SUBLANES, :] * h_last + u_ref[r0:r0 + SUBLANES, :]
        u_ref[r0:r0 + SUBLANES, :] = hr
        h_last = hr[SUBLANES - 1:SUBLANES, :]
    hc_ref[...] = h_last
    return h_last, (u_ref[...] * jax.nn.gelu(gb_ref[0])).astype(BF16)


def _rglru_kernel(xb_ref, gb_ref, past_ref, h0_ref, cw_ref, cb_ref, wa_ref, ba_ref, wi_ref, bi_ref, lam_ref,
                  o_ref, hl_ref, buf_ref, a_ref, u_ref, hc_ref, *, tt, width):
    h_last, ob = _rglru_core(xb_ref, gb_ref, past_ref, h0_ref, cw_ref, cb_ref, wa_ref, ba_ref, wi_ref, bi_ref,
                             lam_ref, buf_ref, a_ref, u_ref, hc_ref, tt, width)
    hl_ref[0] = h_last
    o_ref[0] = ob


def _rglru_out_proj_kernel(x_ref, oa_ref, om_ref, xb_ref, gb_ref, past_ref, h0_ref, cw_ref, cb_ref, wa_ref, ba_ref,
                           wi_ref, bi_ref, lam_ref, woam_ref, wob_ref, g_ref,
                           o_ref, xn_ref, hl_ref, buf_ref, a_ref, u_ref, hc_ref, *, tt, width):
    def attention_share():
        lhs = jnp.concatenate([oa_ref[0], om_ref[0]], axis=1)
        o_ref[0] = jnp.dot(lhs, woam_ref[...], preferred_element_type=F32)

    h_last, ob = _rglru_core(xb_ref, gb_ref, past_ref, h0_ref, cw_ref, cb_ref, wa_ref, ba_ref, wi_ref, bi_ref,
                             lam_ref, buf_ref, a_ref, u_ref, hc_ref, tt, width, after_gates=attention_share)
    hl_ref[0] = h_last
    x1 = x_ref[0] + o_ref[0] + jnp.dot(ob, wob_ref[...], preferred_element_type=F32)
    o_ref[0] = x1
    xn_ref[0] = _rms(x1, g_ref[...]).astype(BF16)


def _rglru(xg3, past8, h0, cw, cb, wa_bd, ba, wi_bd, bi, lam, tt):
    b, s, _ = xg3.shape
    d = cw.shape[1]
    width = cw.shape[0]
    vec = pl.BlockSpec((1, d), lambda bi_, t: (0, 0))
    mat = pl.BlockSpec((d, d), lambda bi_, t: (0, 0))
    return pl.pallas_call(
        functools.partial(_rglru_kernel, tt=tt, width=width),
        grid=(b, s // tt),
        in_specs=[
            pl.BlockSpec((1, tt, d), lambda bi_, t: (bi_, t, 0)),
            pl.BlockSpec((1, tt, d), lambda bi_, t: (bi_, t, 1)),
            pl.BlockSpec((1, SUBLANES, d), lambda bi_, t: (bi_, 0, 0)),
            pl.BlockSpec((1, 1, d), lambda bi_, t: (bi_, 0, 0)),
            pl.BlockSpec((width, d), lambda bi_, t: (0, 0)),
            vec, mat, vec, mat, vec, vec,
        ],
        out_specs=[
            pl.BlockSpec((1, tt, d), lambda bi_, t: (bi_, t, 0)),
            pl.BlockSpec((1, 1, d), lambda bi_, t: (bi_, 0, 0)),
        ],
        out_shape=[jax.ShapeDtypeStruct((b, s, d), BF16), jax.ShapeDtypeStruct((b, 1, d), F32)],
        scratch_shapes=[
            pltpu.VMEM((tt + 2 * SUBLANES, d), F32),
            pltpu.VMEM((tt, d), F32),
            pltpu.VMEM((tt, d), F32),
            pltpu.VMEM((1, d), F32),
        ],
        compiler_params=_params(("parallel", "arbitrary")),
        name="rglru",
    )(xg3, xg3, past8, h0.reshape(b, 1, d), cw, cb.reshape(1, d), wa_bd, ba.reshape(1, d),
      wi_bd, bi.reshape(1, d), lam.reshape(1, d))


def _rglru_out_proj(x3, oa3, om3, xg3, past8, h0, cw, cb, wa_bd, ba, wi_bd, bi, lam, woam, wob, g_next, tt):
    b, s, dm = x3.shape
    d = cw.shape[1]
    width = cw.shape[0]
    vec = pl.BlockSpec((1, d), lambda bi_, t: (0, 0))
    mat = pl.BlockSpec((d, d), lambda bi_, t: (0, 0))
    row = lambda c, j=0: pl.BlockSpec((1, tt, c), lambda bi_, t: (bi_, t, j))
    full = lambda w: pl.BlockSpec(w.shape, lambda bi_, t: (0, 0), pipeline_mode=pl.Buffered(1))
    return pl.pallas_call(
        functools.partial(_rglru_out_proj_kernel, tt=tt, width=width),
        grid=(b, s // tt),
        in_specs=[
            row(dm), row(oa3.shape[2]), row(om3.shape[2]), row(d, 0), row(d, 1),
            pl.BlockSpec((1, SUBLANES, d), lambda bi_, t: (bi_, 0, 0)),
            pl.BlockSpec((1, 1, d), lambda bi_, t: (bi_, 0, 0)),
            pl.BlockSpec((width, d), lambda bi_, t: (0, 0)),
            vec, mat, vec, mat, vec, vec,
            full(woam), full(wob),
            pl.BlockSpec((1, dm), lambda bi_, t: (0, 0)),
        ],
        out_specs=[row(dm), row(dm), pl.BlockSpec((1, 1, d), lambda bi_, t: (bi_, 0, 0))],
        out_shape=[jax.ShapeDtypeStruct((b, s, dm), F32), jax.ShapeDtypeStruct((b, s, dm), BF16),
                   jax.ShapeDtypeStruct((b, 1, d), F32)],
        scratch_shapes=[
            pltpu.VMEM((tt + 2 * SUBLANES, d), F32),
            pltpu.VMEM((tt, d), F32),
            pltpu.VMEM((tt, d), F32),
            pltpu.VMEM((1, d), F32),
        ],
        compiler_params=_params(("parallel", "arbitrary")),
        name="rglru_out_proj",
    )(x3, oa3, om3, xg3, xg3, past8, h0.reshape(b, 1, d), cw, cb.reshape(1, d), wa_bd, ba.reshape(1, d),
      wi_bd, bi.reshape(1, d), lam.reshape(1, d), woam, wob, g_next.reshape(1, dm))


def _out_proj_kernel(x_ref, oa_ref, om_ref, ob_ref, wam_ref, wb_ref, g_ref, o_ref, xn_ref):
    lhs = jnp.concatenate([oa_ref[...], om_ref[...]], axis=1)
    acc = jnp.dot(lhs, wam_ref[...], preferred_element_type=F32)
    acc = acc + jnp.dot(ob_ref[...], wb_ref[...], preferred_element_type=F32)
    x1 = x_ref[...] + acc
    o_ref[...] = x1
    xn_ref[...] = _rms(x1, g_ref[...]).astype(BF16)


def _out_proj(x, oa, om, ob, wam, wb, g_next, tm):
    m, d = x.shape
    row = lambda c: pl.BlockSpec((tm, c), lambda i: (i, 0))
    full = lambda a: pl.BlockSpec(a.shape, lambda i: (0, 0))
    return pl.pallas_call(
        _out_proj_kernel,
        grid=(m // tm,),
        in_specs=[row(d), row(oa.shape[1]), row(om.shape[1]), row(ob.shape[1]), full(wam), full(wb),
                  pl.BlockSpec((1, d), lambda i: (0, 0))],
        out_specs=[row(d), row(d)],
        out_shape=[jax.ShapeDtypeStruct((m, d), F32), jax.ShapeDtypeStruct((m, d), BF16)],
        compiler_params=_params(("parallel",)),
        name="out_proj",
    )(x, oa, om, ob, wam, wb, g_next.reshape(1, d))


def _conv_ffn_kernel(x_hbm, xn_ref, hist_ref, wv_ref, wg_ref, cw_ref, cb_ref, wd_ref,
                     o_hbm, f_ref, acc_ref, bv_ref, bg_ref, car_ref, sem_in, sem_out,
                     *, nseq, seq_rows, tiles_per_seq, width):
    i = pl.program_id(0)
    c = pl.program_id(1)
    nt = pl.num_programs(0)
    nc = pl.num_programs(1)
    pad = SUBLANES
    tm = nseq * seq_rows
    tf = wv_ref.shape[1]
    sub = min(tf, MXU_COLS)
    streams = ((c, wv_ref, bv_ref), (nc + c, wg_ref, bg_ref))
    cols = lambda j, start=0, size=tf: pl.ds(pl.multiple_of(j * tf + start, sub), size)
    slot = i % 2

    def fetch(tile, sl):
        return pltpu.make_async_copy(x_hbm.at[pl.ds(tile * tm, tm)], acc_ref.at[sl], sem_in.at[sl])

    def drain(tile, sl):
        return pltpu.make_async_copy(acc_ref.at[sl], o_hbm.at[pl.ds(tile * tm, tm)], sem_out.at[sl])

    @pl.when(jnp.logical_and(i == 0, c == 0))
    def _():
        fetch(0, 0).start()

    @pl.when(c == 0)
    def _():
        fetch(i, slot).wait()

    @pl.when(c == 1)
    def _():
        @pl.when(i >= 1)
        def _():
            drain(i - 1, 1 - slot).wait()

        @pl.when(i + 1 < nt)
        def _():
            fetch(i + 1, 1 - slot).start()

    for j, _, buf_ref in streams:
        if tiles_per_seq == 1:
            buf_ref[:, pad - (width - 1):pad, :] = hist_ref[:, :, cols(j)]
        else:
            first = i % tiles_per_seq == 0

            @pl.when(first)
            def _(j=j, buf_ref=buf_ref):
                buf_ref[:, pad - (width - 1):pad, :] = hist_ref[:, :, cols(j)]

            @pl.when(jnp.logical_not(first))
            def _(j=j, buf_ref=buf_ref):
                buf_ref[:, 0:pad, :] = car_ref[j]

    def up_proj(cs):
        for _, w_ref, buf_ref in streams:
            up = jnp.dot(xn_ref[...], w_ref[:, cs], preferred_element_type=F32)
            buf_ref[:, pad:pad + seq_rows, cs] = up.reshape(nseq, seq_rows, sub)

    def conv(j, buf_ref, cs):
        lanes = cols(j, cs.start, sub)
        y = cb_ref[:, lanes]
        for k in range(width):
            y = y + cw_ref[width - 1 - k:width - k, lanes] * buf_ref[:, pad - k:pad - k + seq_rows, cs]
        return y

    subs = [slice(s * sub, (s + 1) * sub) for s in range(tf // sub)]
    up_proj(subs[0])
    for s, cs in enumerate(subs):
        if s + 1 < len(subs):
            up_proj(subs[s + 1])
        val = conv(streams[0][0], bv_ref, cs)
        gate = conv(streams[1][0], bg_ref, cs)
        h = (jax.nn.gelu(gate) * val).astype(BF16).reshape(tm, sub)
        acc_ref[slot] += jnp.dot(h, wd_ref[cs, :], preferred_element_type=F32)

    for j, _, buf_ref in streams:
        f_ref[:, :, cols(j)] = buf_ref[:, pad + seq_rows - (width - 1):pad + seq_rows, :]
        if tiles_per_seq > 1:
            car_ref[j] = buf_ref[:, seq_rows:seq_rows + pad, :]

    @pl.when(c == nc - 1)
    def _():
        drain(i, slot).start()

        @pl.when(i == nt - 1)
        def _():
            drain(i, slot).wait()


def _conv_ffn(x, xn, f_conv_past, w_up, cw, cb, w_down, nseq, seq_rows, tiles_per_seq, tf):
    m, d = x.shape
    d_ff = w_down.shape[0]
    width = cw.shape[0]
    tm = nseq * seq_rows
    nt = m // tm
    nc = d_ff // tf
    nb = f_conv_past.shape[0]
    assert nc >= 2 and m == nt * tm
    assert f_conv_past.shape == (nb, width - 1, 2 * d_ff) and nb * tiles_per_seq == nt * nseq
    kern = functools.partial(_conv_ffn_kernel, nseq=nseq, seq_rows=seq_rows,
                             tiles_per_seq=tiles_per_seq, width=width)
    car_shape = (2 * nc, nseq, SUBLANES, tf) if tiles_per_seq > 1 else (1, 1, SUBLANES, LANES)
    y, f = pl.pallas_call(
        kern,
        grid=(nt, nc),
        in_specs=[
            pl.BlockSpec(memory_space=pl.ANY),
            pl.BlockSpec((tm, d), lambda i, c: (i, 0)),
            pl.BlockSpec((nseq, width - 1, 2 * d_ff), lambda i, c: (i // tiles_per_seq, 0, 0)),
            pl.BlockSpec((d, tf), lambda i, c: (0, c)),
            pl.BlockSpec((d, tf), lambda i, c: (0, nc + c)),
            pl.BlockSpec((width, 2 * d_ff), lambda i, c: (0, 0)),
            pl.BlockSpec((1, 2 * d_ff), lambda i, c: (0, 0)),
            pl.BlockSpec((tf, d), lambda i, c: (c, 0)),
        ],
        out_specs=[
            pl.BlockSpec(memory_space=pl.ANY),
            pl.BlockSpec((nseq, width - 1, 2 * d_ff), lambda i, c: (i, 0, 0)),
        ],
        out_shape=[
            jax.ShapeDtypeStruct((m, d), F32),
            jax.ShapeDtypeStruct((nt * nseq, width - 1, 2 * d_ff), F32),
        ],
        scratch_shapes=[
            pltpu.VMEM((2, tm, d), F32),
            pltpu.VMEM((nseq, seq_rows + 2 * SUBLANES, tf), F32),
            pltpu.VMEM((nseq, seq_rows + 2 * SUBLANES, tf), F32),
            pltpu.VMEM(car_shape, F32),
            pltpu.SemaphoreType.DMA((2,)),
            pltpu.SemaphoreType.DMA((2,)),
        ],
        compiler_params=_params(("arbitrary", "arbitrary")),
        name="conv_ffn",
    )(x, xn, f_conv_past, w_up, w_up, cw, cb.reshape(1, 2 * d_ff), w_down)
    f = f.reshape(nt // tiles_per_seq, tiles_per_seq, nseq, width - 1, 2 * d_ff)[:, -1]
    return y, f.reshape(nb, width - 1, 2 * d_ff)


def _pad_rows_front(a, rows):
    return jnp.pad(a, ((0, 0), (rows - a.shape[1], 0), (0, 0)))


def _block_diag(w):
    n, bi, bo = w.shape
    eye = jnp.eye(n, dtype=w.dtype)
    return (w[:, :, None, :] * eye[:, None, :, None]).reshape(n * bi, n * bo)


def _trunk(x3, band_fn, mem_k, mem_v, b_conv_past, b_h0, f_conv_past, p, *, in_tiles, transposed, tq_mem, tt,
           ffn_tiles, ffn_cols):
    b, t, d = x3.shape
    n_heads_a, n_heads_m, d_rnn = p["n_heads_a"], p["n_heads_m"], p["d_rnn"]
    d_a, d_m = n_heads_a * HEAD_DIM, n_heads_m * HEAD_DIM
    x2 = x3.reshape(b * t, d)
    nseq_in, rows_in, tiles_in = in_tiles
    q, k, v, qm, xg, k_a, v_a = _in_proj(x2, p["g_attn"], p["w_in"], p["g_q_a"], p["g_k_a"], p["g_q_m"],
                                         n_heads_a, n_heads_m, d_rnn, nseq_in, rows_in, tiles_in, transposed)
    xg3 = xg.reshape(b, t, 2 * d_rnn)

    o_a = band_fn(q, k, v)
    o_m = _mem_attn(qm.reshape(b, t, d_m), mem_k, mem_v, n_heads_m, tq_mem)
    lru = (xg3, _pad_rows_front(b_conv_past, SUBLANES), b_h0, p["conv_b_w"], p["conv_b_b"],
           p["wa_bd"], p["b_rg_a"], p["wi_bd"], p["b_rg_i"], p["lru_lambda"])
    wo = (p["wo_am"], p["wo_b"], p["g_ffn"])
    if tt >= OUT_ROWS:
        x1, xn1, h_last = _rglru_out_proj(x3, o_a, o_m, *lru, *wo, tt)
        x1, xn1 = x1.reshape(b * t, d), xn1.reshape(b * t, d)
    else:
        o_b, h_last = _rglru(*lru, tt)
        x1, xn1 = _out_proj(x2, o_a.reshape(b * t, d_a), o_m.reshape(b * t, d_m), o_b.reshape(b * t, d_rnn),
                            *wo, OUT_ROWS)

    nseq, seq_rows, tiles_per_seq = ffn_tiles
    y, f_new = _conv_ffn(x1, xn1, f_conv_past, p["w_up"], p["conv_f_w"], p["conv_f_b"], p["w_down"],
                         nseq, seq_rows, tiles_per_seq, ffn_cols)

    wb = p["conv_b_w"].shape[0]
    xb = xg3[:, :, :d_rnn]
    b_conv_new = jnp.concatenate([b_conv_past, xb], axis=1)[:, -(wb - 1):] if t < wb - 1 else xb[:, t - (wb - 1):]
    return y.reshape(b, t, d), k_a, v_a, b_conv_new, h_last.reshape(b, d_rnn), f_new


def kernel(x_prompt, x_sample, cache_a_k, cache_a_v, cache_mem_k, cache_mem_v, cache_b_conv, state_b_h, cache_f_conv, mem_prompt, g_attn, w_in, g_q_a, g_k_a, rel_table, g_q_m, g_k_m, g_mem, w_mem_kv, conv_b_w, conv_b_b, w_rg_a, b_rg_a, w_rg_i, b_rg_i, lru_lambda, w_out, g_ffn, w_up, conv_f_w, conv_f_b, w_down):
    depth = w_in.shape[0]
    bp, sp, d = x_prompt.shape
    bs, ts, _ = x_sample.shape
    n_heads_a = cache_a_k.shape[3]
    n_heads_m = cache_mem_k.shape[3]
    n_mem = mem_prompt.shape[1]
    d_a, d_m = n_heads_a * HEAD_DIM, n_heads_m * HEAD_DIM
    d_rnn = conv_b_w.shape[2]
    d_ff = w_down.shape[1]
    keep = min(WINDOW, sp)
    ones = lambda n: jnp.ones((n,), F32)

    xp, xs = x_prompt, x_sample
    outs = [[] for _ in range(12)]
    for l in range(depth):
        p = dict(
            n_heads_a=n_heads_a, n_heads_m=n_heads_m, d_rnn=d_rnn,
            g_attn=g_attn[l], w_in=w_in[l].astype(BF16),
            g_q_a=g_q_a[l] * SCALE, g_k_a=g_k_a[l], g_q_m=g_q_m[l] * SCALE,
            conv_b_w=conv_b_w[l], conv_b_b=conv_b_b[l],
            wa_bd=_block_diag(w_rg_a[l]).astype(BF16), b_rg_a=b_rg_a[l],
            wi_bd=_block_diag(w_rg_i[l]).astype(BF16), b_rg_i=b_rg_i[l], lru_lambda=lru_lambda[l],
            wo_am=w_out[l, :d_a + d_m].astype(BF16), wo_b=w_out[l, d_a + d_m:].astype(BF16),
            g_ffn=g_ffn[l], w_up=w_up[l].astype(BF16), conv_f_w=conv_f_w[l], conv_f_b=conv_f_b[l],
            w_down=w_down[l].astype(BF16),
        )
        bias = _rel_bias(rel_table[l])

        mem_gains = jnp.concatenate([jnp.tile(g_k_m[l], n_heads_m), ones(d_m)])
        kv = _norm_matmul(mem_prompt.reshape(bp * n_mem, d), g_mem[l], w_mem_kv[l].astype(BF16), mem_gains,
                          ((0, 1),), OUT_ROWS, d_m)
        kv3 = kv.reshape(bp, n_mem, 2 * d_m)

        band_p = lambda q, kt, v: _band_attn(q.reshape(bp, sp, d_a), kt, v.reshape(bp, sp, d_a), bias,
                                             n_heads_a, WINDOW)
        xp, k_a, v_a, bc, bh, fc = _trunk(
            xp, band_p, kv3, kv3,
            jnp.zeros((bp, conv_b_w.shape[1] - 1, d_rnn), F32), jnp.zeros((bp, d_rnn), F32),
            jnp.zeros((bp, conv_f_w.shape[1] - 1, 2 * d_ff), F32), p,
            in_tiles=(1, keep, sp // keep), transposed=True, tq_mem=OUT_ROWS, tt=OUT_ROWS,
            ffn_tiles=(1, FFN_ROWS, sp // FFN_ROWS), ffn_cols=FFN_COLS)
        mem_k = kv3[:, :, :d_m].reshape(bp, n_mem, n_heads_m, HEAD_DIM)
        mem_v = kv3[:, :, d_m:].reshape(bp, n_mem, n_heads_m, HEAD_DIM)
        for lst, v in zip(outs[:7], (k_a, v_a, mem_k, mem_v, bc, bh, fc)):
            lst.append(v)

        band_s = lambda q, k, v: _band_attn_sample(
            q.reshape(bs, ts, d_a), k.reshape(bs, ts, d_a), v.reshape(bs, ts, d_a),
            cache_a_k[l], cache_a_v[l], bias, n_heads_a)
        xs, k_a, v_a, bc, bh, fc = _trunk(
            xs, band_s, cache_mem_k[l], cache_mem_v[l],
            cache_b_conv[l], state_b_h[l], cache_f_conv[l], p,
            in_tiles=(bs, ts, 1), transposed=False, tq_mem=ts, tt=ts, ffn_tiles=(bs, ts, 1),
            ffn_cols=FFN_COLS)
        for lst, v in zip(outs[7:], (k_a, v_a, bc, bh, fc)):
            lst.append(v)

    stacked = [jnp.stack(o) for o in outs]
    return (xp, xs, *stacked)
```

```python
import functools

import jax
import jax.numpy as jnp
from jax import lax
from jax.experimental import pallas as pl
from jax.experimental.pallas import tpu as pltpu

EPS = 1e-6
HEAD_DIM = 128
CHUNK = 64
LEFT_CHUNKS = 8
WINDOW = LEFT_CHUNKS * CHUNK
REL_CLIP = 256
LRU_C = 8.0
NEG = -1e30
SCALE = HEAD_DIM ** -0.5

LANES = 128
SUBLANES = 8
MXU_COLS = 256
VMEM_LIMIT = 56 * 2 ** 20

QSUB = 4 * CHUNK
KSPAN = WINDOW + QSUB
OUT_ROWS = 512
FFN_ROWS = 1024
FFN_COLS = 512

BF16 = jnp.bfloat16
F32 = jnp.float32


def _params(sem):
    return pltpu.CompilerParams(dimension_semantics=sem, vmem_limit_bytes=VMEM_LIMIT)


def _rms(x, g):
    ms = jnp.mean(x * x, axis=-1, keepdims=True)
    return x * lax.rsqrt(ms + EPS) * g


def _norm_matmul_kernel(x_ref, g_ref, w_ref, hg_ref, o_ref, xn_ref, *, norm_ranges, tn):
    j = pl.program_id(1)

    @pl.when(j == 0)
    def _():
        xn_ref[...] = _rms(x_ref[...], g_ref[...]).astype(BF16)

    z = jnp.dot(xn_ref[...], w_ref[...], preferred_element_type=F32)
    cond = None
    for lo, hi in norm_ranges:
        c = jnp.logical_and(j >= lo, j < hi)
        cond = c if cond is None else jnp.logical_or(cond, c)

    @pl.when(cond)
    def _():
        for h in range(tn // HEAD_DIM):
            cs = slice(h * HEAD_DIM, (h + 1) * HEAD_DIM)
            o_ref[:, cs] = _rms(z[:, cs], hg_ref[:, cs])

    @pl.when(jnp.logical_not(cond))
    def _():
        o_ref[...] = z


def _norm_matmul(x, g, w_bf16, head_gains, norm_ranges, tm, tn):
    m, k = x.shape
    n = w_bf16.shape[1]
    kern = functools.partial(_norm_matmul_kernel, norm_ranges=norm_ranges, tn=tn)
    return pl.pallas_call(
        kern,
        grid=(m // tm, n // tn),
        in_specs=[
            pl.BlockSpec((tm, k), lambda i, j: (i, 0)),
            pl.BlockSpec((1, k), lambda i, j: (0, 0)),
            pl.BlockSpec((k, tn), lambda i, j: (0, j)),
            pl.BlockSpec((1, tn), lambda i, j: (0, j)),
        ],
        out_specs=pl.BlockSpec((tm, tn), lambda i, j: (i, j)),
        out_shape=jax.ShapeDtypeStruct((m, n), F32),
        scratch_shapes=[pltpu.VMEM((tm, k), BF16)],
        compiler_params=_params(("parallel", "arbitrary")),
        name="norm_matmul",
    )(x, g.reshape(1, k), w_bf16, head_gains.reshape(1, n))


def _in_proj_kernel(x_ref, g_ref, w_ref, gq_ref, gk_ref, gqm_ref, q_ref, k_ref, v_ref, qm_ref, xg_ref, ak_ref, av_ref,
                    xn_ref, kv_ref, *, d_a, d_m, d_rnn, group, tiles_per_seq, transposed):
    i = pl.program_id(0)
    xn_ref[...] = _rms(x_ref[...], g_ref[...]).astype(BF16)

    def cols(c0):
        return jnp.dot(xn_ref[...], w_ref[:, c0:c0 + group], preferred_element_type=F32)

    def heads(z, c0):
        return [(slice(c0 + h * HEAD_DIM, c0 + (h + 1) * HEAD_DIM), z[:, h * HEAD_DIM:(h + 1) * HEAD_DIM])
                for h in range(group // HEAD_DIM)]

    def put(ref, cs, val):
        if transposed:
            ref[cs, :] = val.T.astype(BF16)
        else:
            ref[:, cs] = val.astype(BF16)

    for c0 in range(0, d_a, group):
        for cs, zh in heads(cols(c0), c0):
            q_ref[:, cs] = _rms(zh, gq_ref[...]).astype(BF16)
    for c0 in range(0, d_a, group):
        for cs, zh in heads(cols(d_a + c0), c0):
            kh = _rms(zh, gk_ref[...])
            put(k_ref, cs, kh)
            kv_ref[:, cs] = kh
    for c0 in range(0, d_a, group):
        for cs, zh in heads(cols(2 * d_a + c0), c0):
            v_ref[:, cs] = zh.astype(BF16)
            kv_ref[:, slice(d_a + cs.start, d_a + cs.stop)] = zh
    for c0 in range(0, d_m, group):
        for cs, zh in heads(cols(3 * d_a + c0), c0):
            qm_ref[:, cs] = _rms(zh, gqm_ref[...]).astype(BF16)
    for c0 in range(0, 2 * d_rnn, group):
        xg_ref[:, c0:c0 + group] = cols(3 * d_a + d_m + c0)

    @pl.when(i % tiles_per_seq == tiles_per_seq - 1)
    def _():
        nseq, rows, n_heads = ak_ref.shape[0], ak_ref.shape[1], ak_ref.shape[2]
        for h in range(n_heads):
            cs = slice(h * HEAD_DIM, (h + 1) * HEAD_DIM)
            ak_ref[:, :, h, :] = kv_ref[:, cs].reshape(nseq, rows, HEAD_DIM)
            av_ref[:, :, h, :] = kv_ref[:, slice(d_a + cs.start, d_a + cs.stop)].reshape(nseq, rows, HEAD_DIM)


def _in_proj(x, g, w_bf16, gq, gk, gqm, n_heads_a, n_heads_m, d_rnn, nseq, rows, tiles_per_seq, transposed):
    m, k = x.shape
    d_in = w_bf16.shape[1]
    d_a, d_m = n_heads_a * HEAD_DIM, n_heads_m * HEAD_DIM
    tm = nseq * rows
    nt = m // tm
    n_cache_seq = nt * nseq // tiles_per_seq
    kern = functools.partial(_in_proj_kernel, d_a=d_a, d_m=d_m, d_rnn=d_rnn, group=4 * HEAD_DIM,
                             tiles_per_seq=tiles_per_seq, transposed=transposed)
    row = lambda c: pl.BlockSpec((tm, c), lambda i: (i, 0))
    vec = pl.BlockSpec((1, HEAD_DIM), lambda i: (0, 0))
    cache = pl.BlockSpec((nseq, rows, n_heads_a, HEAD_DIM), lambda i: (i // tiles_per_seq, 0, 0, 0))
    k_spec = pl.BlockSpec((d_a, tm), lambda i: (0, i)) if transposed else row(d_a)
    k_shape = jax.ShapeDtypeStruct((d_a, m) if transposed else (m, d_a), BF16)
    return pl.pallas_call(
        kern,
        grid=(nt,),
        in_specs=[
            row(k),
            pl.BlockSpec((1, k), lambda i: (0, 0)),
            pl.BlockSpec((k, d_in), lambda i: (0, 0), pipeline_mode=pl.Buffered(1)),
            vec, vec, vec,
        ],
        out_specs=[row(d_a), k_spec, row(d_a), row(d_m), row(2 * d_rnn), cache, cache],
        out_shape=[
            jax.ShapeDtypeStruct((m, d_a), BF16),
            k_shape,
            jax.ShapeDtypeStruct((m, d_a), BF16),
            jax.ShapeDtypeStruct((m, d_m), BF16),
            jax.ShapeDtypeStruct((m, 2 * d_rnn), F32),
            jax.ShapeDtypeStruct((n_cache_seq, rows, n_heads_a, HEAD_DIM), F32),
            jax.ShapeDtypeStruct((n_cache_seq, rows, n_heads_a, HEAD_DIM), F32),
        ],
        scratch_shapes=[pltpu.VMEM((tm, k), BF16), pltpu.VMEM((tm, 2 * d_a), F32)],
        compiler_params=_params(("arbitrary",)),
        name="in_proj",
    )(x, g.reshape(1, k), w_bf16, gq.reshape(1, HEAD_DIM), gk.reshape(1, HEAD_DIM), gqm.reshape(1, HEAD_DIM))


def _rel_bias_kernel(tab_ref, o_ref, *, tpad, wext):
    t = tab_ref[...]
    hi = t.astype(BF16)
    r1 = t - hi.astype(F32)
    mid = r1.astype(BF16)
    lo = (r1 - mid.astype(F32)).astype(BF16)
    m = lax.broadcasted_iota(jnp.int32, (1, wext), 1)
    x = jnp.where(m >= wext - CHUNK, m - wext, m)
    idx = jnp.clip(WINDOW - x, -REL_CLIP, REL_CLIP) + REL_CLIP
    tix = lax.broadcasted_iota(jnp.int32, (tpad, wext), 0)
    onehot = jnp.where(tix == idx, 1.0, 0.0).astype(BF16)
    v = (jnp.dot(hi, onehot, preferred_element_type=F32)
         + jnp.dot(mid, onehot, preferred_element_type=F32)
         + jnp.dot(lo, onehot, preferred_element_type=F32))
    j = lax.broadcasted_iota(jnp.int32, (1, KSPAN), 1)
    for ci in range(QSUB // CHUNK):
        jj = j - CHUNK * ci
        valid = jnp.logical_and(jj >= 0, jj < WINDOW + CHUNK)
        for ii in range(CHUNK):
            s = ci * CHUNK + ii
            row = (v if s == 0 else pltpu.roll(v, s, 1))[:, :KSPAN]
            o_ref[:, s, :] = jnp.where(valid, row, NEG)


def _rel_bias(rel_table):
    n_heads, tlen = rel_table.shape
    tpad = -(-tlen // LANES) * LANES
    wext = KSPAN + LANES
    assert wext - CHUNK >= WINDOW + CHUNK and tlen == 2 * REL_CLIP + 1
    tab = jnp.pad(rel_table, ((0, 0), (0, tpad - tlen)))
    return pl.pallas_call(
        functools.partial(_rel_bias_kernel, tpad=tpad, wext=wext),
        out_shape=jax.ShapeDtypeStruct((n_heads, QSUB, KSPAN), F32),
        compiler_params=_params(None),
        name="rel_bias",
    )(tab)


def _band_attn_kernel(q_ref, ktc_ref, vc_ref, bias_ref, o_ref, ktcat_ref, vcat_ref, *, n_heads, tq):
    t = pl.program_id(1)

    @pl.when(t == 0)
    def _():
        ktcat_ref[:, 0:WINDOW] = jnp.zeros((ktcat_ref.shape[0], WINDOW), ktcat_ref.dtype)
        vcat_ref[0:WINDOW, :] = jnp.zeros((WINDOW, vcat_ref.shape[1]), vcat_ref.dtype)

    ktcat_ref[:, WINDOW:WINDOW + tq] = ktc_ref[...]
    vcat_ref[WINDOW:WINDOW + tq, :] = vc_ref[0]

    def run(first_tile):
        for s in range(tq // QSUB):
            r0 = s * QSUB
            if first_tile:
                before_start = lax.broadcasted_iota(jnp.int32, (1, KSPAN), 1) + r0 < WINDOW
            for h in range(n_heads):
                cs = slice(h * HEAD_DIM, (h + 1) * HEAD_DIM)
                sc = jnp.dot(q_ref[0, r0:r0 + QSUB, cs], ktcat_ref[cs, r0:r0 + KSPAN], preferred_element_type=F32)
                sc = sc + bias_ref[h]
                if first_tile:
                    sc = jnp.where(before_start, NEG, sc)
                m = jnp.max(sc, axis=-1, keepdims=True)
                p = jnp.exp(sc - m)
                l = jnp.sum(p, axis=-1, keepdims=True)
                o = jnp.dot(p.astype(BF16), vcat_ref[r0:r0 + KSPAN, cs], preferred_element_type=F32) / l
                o_ref[0, r0:r0 + QSUB, cs] = o.astype(o_ref.dtype)

    @pl.when(t == 0)
    def _():
        run(True)

    @pl.when(t > 0)
    def _():
        run(False)

    ktcat_ref[:, 0:WINDOW] = ktcat_ref[:, WINDOW:WINDOW + tq]
    vcat_ref[0:WINDOW, :] = vcat_ref[WINDOW:WINDOW + tq, :]


def _band_attn(q3, kt, v3, bias, n_heads, tq):
    b, s, d_a = q3.shape
    assert tq == WINDOW and s % tq == 0
    nt = s // tq
    cur = lambda bi, t: (bi, t, 0)
    return pl.pallas_call(
        functools.partial(_band_attn_kernel, n_heads=n_heads, tq=tq),
        grid=(b, nt),
        in_specs=[
            pl.BlockSpec((1, tq, d_a), cur),
            pl.BlockSpec((d_a, tq), lambda bi, t: (0, bi * nt + t)),
            pl.BlockSpec((1, tq, d_a), cur),
            pl.BlockSpec((n_heads, QSUB, KSPAN), lambda bi, t: (0, 0, 0)),
        ],
        out_specs=pl.BlockSpec((1, tq, d_a), cur),
        out_shape=jax.ShapeDtypeStruct((b, s, d_a), BF16),
        scratch_shapes=[pltpu.VMEM((d_a, WINDOW + tq), BF16), pltpu.VMEM((WINDOW + tq, d_a), BF16)],
        compiler_params=_params(("parallel", "arbitrary")),
        name="band_attn",
    )(q3, kt, v3, bias)


def _band_attn_sample_kernel(q_ref, kn_ref, vn_ref, ck_hbm, cv_hbm, bias_ref, o_ref, kbuf, vbuf, sem,
                             *, n_heads, t, p_len):
    b = pl.program_id(0)
    slot = b % 2

    def cache_copies(seq, sl):
        cps = []
        for h in range(n_heads):
            cps.append(pltpu.make_async_copy(ck_hbm.at[seq, :, h, :], kbuf.at[sl, h], sem.at[0, sl, h]))
            cps.append(pltpu.make_async_copy(cv_hbm.at[seq, :, h, :], vbuf.at[sl, h], sem.at[1, sl, h]))
        return cps

    @pl.when(b == 0)
    def _():
        for cp in cache_copies(0, 0):
            cp.start()

    @pl.when(b + 1 < pl.num_programs(0))
    def _():
        for cp in cache_copies(b + 1, 1 - slot):
            cp.start()

    for cp in cache_copies(b, slot):
        cp.wait()

    heads = [slice(h * HEAD_DIM, (h + 1) * HEAD_DIM) for h in range(n_heads)]
    dn = (((1,), (1,)), ((), ()))
    scores = []
    for h, cs in enumerate(heads):
        q = q_ref[0, :, cs]
        ck = kbuf[slot, h].astype(BF16)
        scores.append((lax.dot_general(q, ck, dn, preferred_element_type=F32),
                       lax.dot_general(q, kn_ref[0, :, cs], dn, preferred_element_type=F32)))
    for h, cs in enumerate(heads):
        cv = vbuf[slot, h].astype(BF16)
        vn = vn_ref[0, :, cs]
        s1 = scores[h][0] + bias_ref[h, 0:t, 0:p_len]
        s2 = scores[h][1] + bias_ref[h, 0:t, p_len:p_len + t]
        m = jnp.maximum(jnp.max(s1, axis=-1, keepdims=True), jnp.max(s2, axis=-1, keepdims=True))
        p1 = jnp.exp(s1 - m)
        p2 = jnp.exp(s2 - m)
        l = jnp.sum(p1, axis=-1, keepdims=True) + jnp.sum(p2, axis=-1, keepdims=True)
        o = (jnp.dot(p1.astype(BF16), cv, preferred_element_type=F32)
             + jnp.dot(p2.astype(BF16), vn, preferred_element_type=F32)) / l
        o_ref[0, :, cs] = o.astype(o_ref.dtype)


def _band_attn_sample(q3, k3, v3, cache_k, cache_v, bias, n_heads):
    b, t, d_a = q3.shape
    p_len = cache_k.shape[1]
    assert p_len == WINDOW and t <= CHUNK
    cache = pl.BlockSpec(memory_space=pl.ANY)
    return pl.pallas_call(
        functools.partial(_band_attn_sample_kernel, n_heads=n_heads, t=t, p_len=p_len),
        grid=(b,),
        in_specs=[
            pl.BlockSpec((1, t, d_a), lambda bi: (bi, 0, 0)),
            pl.BlockSpec((1, t, d_a), lambda bi: (bi, 0, 0)),
            pl.BlockSpec((1, t, d_a), lambda bi: (bi, 0, 0)),
            cache, cache,
            pl.BlockSpec((n_heads, QSUB, KSPAN), lambda bi: (0, 0, 0)),
        ],
        out_specs=pl.BlockSpec((1, t, d_a), lambda bi: (bi, 0, 0)),
        out_shape=jax.ShapeDtypeStruct((b, t, d_a), BF16),
        scratch_shapes=[
            pltpu.VMEM((2, n_heads, p_len, HEAD_DIM), cache_k.dtype),
            pltpu.VMEM((2, n_heads, p_len, HEAD_DIM), cache_v.dtype),
            pltpu.SemaphoreType.DMA((2, 2, n_heads)),
        ],
        compiler_params=_params(("arbitrary",)),
        name="band_attn_sample",
    )(q3, k3, v3, cache_k, cache_v, bias)


def _mem_attn_kernel(q_ref, mk_ref, mv_ref, o_ref, *, n_heads, head_axis):
    heads = [slice(h * HEAD_DIM, (h + 1) * HEAD_DIM) for h in range(n_heads)]
    load = (lambda ref, h, cs: ref[0, :, h, :]) if head_axis else (lambda ref, h, cs: ref[0, :, cs])
    scores = [lax.dot_general(q_ref[0, :, cs], load(mk_ref, h, cs).astype(BF16), (((1,), (1,)), ((), ())),
                              preferred_element_type=F32) for h, cs in enumerate(heads)]
    for h, cs in enumerate(heads):
        v = load(mv_ref, h, cs).astype(BF16)
        sc = scores[h]
        m = jnp.max(sc, axis=-1, keepdims=True)
        p = jnp.exp(sc - m)
        l = jnp.sum(p, axis=-1, keepdims=True)
        o = jnp.dot(p.astype(BF16), v, preferred_element_type=F32) / l
        o_ref[0, :, cs] = o.astype(o_ref.dtype)


def _mem_attn(qm3, mem_k, mem_v, n_heads, tm):
    b, s, d_m = qm3.shape
    n_mem = mem_k.shape[1]
    head_axis = mem_k.ndim == 4
    if head_axis:
        kspec = vspec = pl.BlockSpec((1, n_mem, n_heads, HEAD_DIM), lambda bi, t: (bi, 0, 0, 0))
    else:
        kspec = pl.BlockSpec((1, n_mem, d_m), lambda bi, t: (bi, 0, 0))
        vspec = pl.BlockSpec((1, n_mem, d_m), lambda bi, t: (bi, 0, 1))
    return pl.pallas_call(
        functools.partial(_mem_attn_kernel, n_heads=n_heads, head_axis=head_axis),
        grid=(b, s // tm),
        in_specs=[pl.BlockSpec((1, tm, d_m), lambda bi, t: (bi, t, 0)), kspec, vspec],
        out_specs=pl.BlockSpec((1, tm, d_m), lambda bi, t: (bi, t, 0)),
        out_shape=jax.ShapeDtypeStruct((b, s, d_m), BF16),
        compiler_params=_params(("parallel", "parallel")),
        name="mem_attn",
    )(qm3, mem_k, mem_v)


def _rglru_core(xb_ref, gb_ref, past_ref, h0_ref, cw_ref, cb_ref, wa_ref, ba_ref, wi_ref, bi_ref, lam_ref,
                buf_ref, a_ref, u_ref, hc_ref, tt, width, after_gates=None):
    t = pl.program_id(1)
    pad = SUBLANES

    @pl.when(t == 0)
    def _():
        buf_ref[0:pad, :] = past_ref[0]
        hc_ref[...] = h0_ref[0]

    buf_ref[pad:pad + tt, :] = xb_ref[0]
    xc = cb_ref[...] + cw_ref[width - 1:width, :] * xb_ref[0]
    for k in range(1, width):
        xc = xc + cw_ref[width - 1 - k:width - k, :] * buf_ref[pad - k:pad - k + tt, :]
    buf_ref[0:pad, :] = buf_ref[tt:tt + pad, :]

    xcb = xc.astype(BF16)
    def sigmoid(z):
        return 0.5 * jnp.tanh(0.5 * z) + 0.5

    r = sigmoid(jnp.dot(xcb, wa_ref[...], preferred_element_type=F32) + ba_ref[...])
    i = sigmoid(jnp.dot(xcb, wi_ref[...], preferred_element_type=F32) + bi_ref[...])
    if after_gates is not None:
        after_gates()
    log_a = -LRU_C * r * jax.nn.softplus(-lam_ref[...])
    a = jnp.exp(log_a)
    th = jnp.tanh(log_a)
    n = -2.0 * th
    dd = 1.0 - th
    u = jnp.where(n > 0.0, n * lax.rsqrt(n * dd), 0.0) * (i * xc)

    d = a.shape[-1]
    a = a.reshape(tt // SUBLANES, SUBLANES, d)
    u = u.reshape(tt // SUBLANES, SUBLANES, d)
    rowm = lax.broadcasted_iota(jnp.int32, (1, SUBLANES, 1), 1)
    sh = 1
    while sh < SUBLANES:
        keep = rowm >= sh
        a_sh = pltpu.roll(a, sh, 1)
        u_sh = pltpu.roll(u, sh, 1)
        u = jnp.where(keep, a * u_sh + u, u)
        a = jnp.where(keep, a * a_sh, a)
        sh *= 2
    a_ref[...] = a.reshape(tt, d)
    u_ref[...] = u.reshape(tt, d)

    h_last = hc_ref[...]
    for r0 in range(0, tt, SUBLANES):
        hr = a_ref[r0:r0 + SUBLANES, :] * h_last + u_ref[r0:r0 + SUBLANES, :]
        u_ref[r0:r0 + SUBLANES, :] = hr
        h_last = hr[SUBLANES - 1:SUBLANES, :]
    hc_ref[...] = h_last
    return h_last, (u_ref[...] * jax.nn.gelu(gb_ref[0])).astype(BF16)


def _rglru_kernel(xb_ref, gb_ref, past_ref, h0_ref, cw_ref, cb_ref, wa_ref, ba_ref, wi_ref, bi_ref, lam_ref,
                  o_ref, hl_ref, buf_ref, a_ref, u_ref, hc_ref, *, tt, width):
    h_last, ob = _rglru_core(xb_ref, gb_ref, past_ref, h0_ref, cw_ref, cb_ref, wa_ref, ba_ref, wi_ref, bi_ref,
                             lam_ref, buf_ref, a_ref, u_ref, hc_ref, tt, width)
    hl_ref[0] = h_last
    o_ref[0] = ob


def _rglru_out_proj_kernel(x_ref, oa_ref, om_ref, xb_ref, gb_ref, past_ref, h0_ref, cw_ref, cb_ref, wa_ref, ba_ref,
                           wi_ref, bi_ref, lam_ref, woam_ref, wob_ref, g_ref,
                           o_ref, xn_ref, hl_ref, buf_ref, a_ref, u_ref, hc_ref, *, tt, width):
    def attention_share():
        lhs = jnp.concatenate([oa_ref[0], om_ref[0]], axis=1)
        o_ref[0] = jnp.dot(lhs, woam_ref[...], preferred_element_type=F32)

    h_last, ob = _rglru_core(xb_ref, gb_ref, past_ref, h0_ref, cw_ref, cb_ref, wa_ref, ba_ref, wi_ref, bi_ref,
                             lam_ref, buf_ref, a_ref, u_ref, hc_ref, tt, width, after_gates=attention_share)
    hl_ref[0] = h_last
    x1 = x_ref[0] + o_ref[0] + jnp.dot(ob, wob_ref[...], preferred_element_type=F32)
    o_ref[0] = x1
    xn_ref[0] = _rms(x1, g_ref[...]).astype(BF16)


def _rglru(xg3, past8, h0, cw, cb, wa_bd, ba, wi_bd, bi, lam, tt):
    b, s, _ = xg3.shape
    d = cw.shape[1]
    width = cw.shape[0]
    vec = pl.BlockSpec((1, d), lambda bi_, t: (0, 0))
    mat = pl.BlockSpec((d, d), lambda bi_, t: (0, 0))
    return pl.pallas_call(
        functools.partial(_rglru_kernel, tt=tt, width=width),
        grid=(b, s // tt),
        in_specs=[
            pl.BlockSpec((1, tt, d), lambda bi_, t: (bi_, t, 0)),
            pl.BlockSpec((1, tt, d), lambda bi_, t: (bi_, t, 1)),
            pl.BlockSpec((1, SUBLANES, d), lambda bi_, t: (bi_, 0, 0)),
            pl.BlockSpec((1, 1, d), lambda bi_, t: (bi_, 0, 0)),
            pl.BlockSpec((width, d), lambda bi_, t: (0, 0)),
            vec, mat, vec, mat, vec, vec,
        ],
        out_specs=[
            pl.BlockSpec((1, tt, d), lambda bi_, t: (bi_, t, 0)),
            pl.BlockSpec((1, 1, d), lambda bi_, t: (bi_, 0, 0)),
        ],
        out_shape=[jax.ShapeDtypeStruct((b, s, d), BF16), jax.ShapeDtypeStruct((b, 1, d), F32)],
        scratch_shapes=[
            pltpu.VMEM((tt + 2 * SUBLANES, d), F32),
            pltpu.VMEM((tt, d), F32),
            pltpu.VMEM((tt, d), F32),
            pltpu.VMEM((1, d), F32),
        ],
        compiler_params=_params(("parallel", "arbitrary")),
        name="rglru",
    )(xg3, xg3, past8, h0.reshape(b, 1, d), cw, cb.reshape(1, d), wa_bd, ba.reshape(1, d),
      wi_bd, bi.reshape(1, d), lam.reshape(1, d))


def _rglru_out_proj(x3, oa3, om3, xg3, past8, h0, cw, cb, wa_bd, ba, wi_bd, bi, lam, woam, wob, g_next, tt):
    b, s, dm = x3.shape
    d = cw.shape[1]
    width = cw.shape[0]
    vec = pl.BlockSpec((1, d), lambda bi_, t: (0, 0))
    mat = pl.BlockSpec((d, d), lambda bi_, t: (0, 0))
    row = lambda c, j=0: pl.BlockSpec((1, tt, c), lambda bi_, t: (bi_, t, j))
    full = lambda w: pl.BlockSpec(w.shape, lambda bi_, t: (0, 0), pipeline_mode=pl.Buffered(1))
    return pl.pallas_call(
        functools.partial(_rglru_out_proj_kernel, tt=tt, width=width),
        grid=(b, s // tt),
        in_specs=[
            row(dm), row(oa3.shape[2]), row(om3.shape[2]), row(d, 0), row(d, 1),
            pl.BlockSpec((1, SUBLANES, d), lambda bi_, t: (bi_, 0, 0)),
            pl.BlockSpec((1, 1, d), lambda bi_, t: (bi_, 0, 0)),
            pl.BlockSpec((width, d), lambda bi_, t: (0, 0)),
            vec, mat, vec, mat, vec, vec,
            full(woam), full(wob),
            pl.BlockSpec((1, dm), lambda bi_, t: (0, 0)),
        ],
        out_specs=[row(dm), row(dm), pl.BlockSpec((1, 1, d), lambda bi_, t: (bi_, 0, 0))],
        out_shape=[jax.ShapeDtypeStruct((b, s, dm), F32), jax.ShapeDtypeStruct((b, s, dm), BF16),
                   jax.ShapeDtypeStruct((b, 1, d), F32)],
        scratch_shapes=[
            pltpu.VMEM((tt + 2 * SUBLANES, d), F32),
            pltpu.VMEM((tt, d), F32),
            pltpu.VMEM((tt, d), F32),
            pltpu.VMEM((1, d), F32),
        ],
        compiler_params=_params(("parallel", "arbitrary")),
        name="rglru_out_proj",
    )(x3, oa3, om3, xg3, xg3, past8, h0.reshape(b, 1, d), cw, cb.reshape(1, d), wa_bd, ba.reshape(1, d),
      wi_bd, bi.reshape(1, d), lam.reshape(1, d), woam, wob, g_next.reshape(1, dm))


def _out_proj_kernel(x_ref, oa_ref, om_ref, ob_ref, wam_ref, wb_ref, g_ref, o_ref, xn_ref):
    lhs = jnp.concatenate([oa_ref[...], om_ref[...]], axis=1)
    acc = jnp.dot(lhs, wam_ref[...], preferred_element_type=F32)
    acc = acc + jnp.dot(ob_ref[...], wb_ref[...], preferred_element_type=F32)
    x1 = x_ref[...] + acc
    o_ref[...] = x1
    xn_ref[...] = _rms(x1, g_ref[...]).astype(BF16)


def _out_proj(x, oa, om, ob, wam, wb, g_next, tm):
    m, d = x.shape
    row = lambda c: pl.BlockSpec((tm, c), lambda i: (i, 0))
    full = lambda a: pl.BlockSpec(a.shape, lambda i: (0, 0))
    return pl.pallas_call(
        _out_proj_kernel,
        grid=(m // tm,),
        in_specs=[row(d), row(oa.shape[1]), row(om.shape[1]), row(ob.shape[1]), full(wam), full(wb),
                  pl.BlockSpec((1, d), lambda i: (0, 0))],
        out_specs=[row(d), row(d)],
        out_shape=[jax.ShapeDtypeStruct((m, d), F32), jax.ShapeDtypeStruct((m, d), BF16)],
        compiler_params=_params(("parallel",)),
        name="out_proj",
    )(x, oa, om, ob, wam, wb, g_next.reshape(1, d))


def _conv_ffn_kernel(x_hbm, xn_ref, hist_ref, wv_ref, wg_ref, cw_ref, cb_ref, wd_ref,
                     o_hbm, f_ref, acc_ref, bv_ref, bg_ref, car_ref, sem_in, sem_out,
                     *, nseq, seq_rows, tiles_per_seq, width):
    i = pl.program_id(0)
    c = pl.program_id(1)
    nt = pl.num_programs(0)
    nc = pl.num_programs(1)
    pad = SUBLANES
    tm = nseq * seq_rows
    tf = wv_ref.shape[1]
    sub = min(tf, MXU_COLS)
    streams = ((c, wv_ref, bv_ref), (nc + c, wg_ref, bg_ref))
    cols = lambda j, start=0, size=tf: pl.ds(pl.multiple_of(j * tf + start, sub), size)
    slot = i % 2

    def fetch(tile, sl):
        return pltpu.make_async_copy(x_hbm.at[pl.ds(tile * tm, tm)], acc_ref.at[sl], sem_in.at[sl])

    def drain(tile, sl):
        return pltpu.make_async_copy(acc_ref.at[sl], o_hbm.at[pl.ds(tile * tm, tm)], sem_out.at[sl])

    @pl.when(jnp.logical_and(i == 0, c == 0))
    def _():
        fetch(0, 0).start()

    @pl.when(c == 0)
    def _():
        fetch(i, slot).wait()

    @pl.when(c == 1)
    def _():
        @pl.when(i >= 1)
        def _():
            drain(i - 1, 1 - slot).wait()

        @pl.when(i + 1 < nt)
        def _():
            fetch(i + 1, 1 - slot).start()

    for j, _, buf_ref in streams:
        if tiles_per_seq == 1:
            buf_ref[:, pad - (width - 1):pad, :] = hist_ref[:, :, cols(j)]
        else:
            first = i % tiles_per_seq == 0

            @pl.when(first)
            def _(j=j, buf_ref=buf_ref):
                buf_ref[:, pad - (width - 1):pad, :] = hist_ref[:, :, cols(j)]

            @pl.when(jnp.logical_not(first))
            def _(j=j, buf_ref=buf_ref):
                buf_ref[:, 0:pad, :] = car_ref[j]

    def up_proj(cs):
        for _, w_ref, buf_ref in streams:
            up = jnp.dot(xn_ref[...], w_ref[:, cs], preferred_element_type=F32)
            buf_ref[:, pad:pad + seq_rows, cs] = up.reshape(nseq, seq_rows, sub)

    def conv(j, buf_ref, cs):
        lanes = cols(j, cs.start, sub)
        y = cb_ref[:, lanes]
        for k in range(width):
            y = y + cw_ref[width - 1 - k:width - k, lanes] * buf_ref[:, pad - k:pad - k + seq_rows, cs]
        return y

    subs = [slice(s * sub, (s + 1) * sub) for s in range(tf // sub)]
    up_proj(subs[0])
    for s, cs in enumerate(subs):
        if s + 1 < len(subs):
            up_proj(subs[s + 1])
        val = conv(streams[0][0], bv_ref, cs)
        gate = conv(streams[1][0], bg_ref, cs)
        h = (jax.nn.gelu(gate) * val).astype(BF16).reshape(tm, sub)
        acc_ref[slot] += jnp.dot(h, wd_ref[cs, :], preferred_element_type=F32)

    for j, _, buf_ref in streams:
        f_ref[:, :, cols(j)] = buf_ref[:, pad + seq_rows - (width - 1):pad + seq_rows, :]
        if tiles_per_seq > 1:
            car_ref[j] = buf_ref[:, seq_rows:seq_rows + pad, :]

    @pl.when(c == nc - 1)
    def _():
        drain(i, slot).start()

        @pl.when(i == nt - 1)
        def _():
            drain(i, slot).wait()


def _conv_ffn(x, xn, f_conv_past, w_up, cw, cb, w_down, nseq, seq_rows, tiles_per_seq, tf):
    m, d = x.shape
    d_ff = w_down.shape[0]
    width = cw.shape[0]
    tm = nseq * seq_rows
    nt = m // tm
    nc = d_ff // tf
    nb = f_conv_past.shape[0]
    assert nc >= 2 and m == nt * tm
    assert f_conv_past.shape == (nb, width - 1, 2 * d_ff) and nb * tiles_per_seq == nt * nseq
    kern = functools.partial(_conv_ffn_kernel, nseq=nseq, seq_rows=seq_rows,
                             tiles_per_seq=tiles_per_seq, width=width)
    car_shape = (2 * nc, nseq, SUBLANES, tf) if tiles_per_seq > 1 else (1, 1, SUBLANES, LANES)
    y, f = pl.pallas_call(
        kern,
        grid=(nt, nc),
        in_specs=[
            pl.BlockSpec(memory_space=pl.ANY),
            pl.BlockSpec((tm, d), lambda i, c: (i, 0)),
            pl.BlockSpec((nseq, width - 1, 2 * d_ff), lambda i, c: (i // tiles_per_seq, 0, 0)),
            pl.BlockSpec((d, tf), lambda i, c: (0, c)),
            pl.BlockSpec((d, tf), lambda i, c: (0, nc + c)),
            pl.BlockSpec((width, 2 * d_ff), lambda i, c: (0, 0)),
            pl.BlockSpec((1, 2 * d_ff), lambda i, c: (0, 0)),
            pl.BlockSpec((tf, d), lambda i, c: (c, 0)),
        ],
        out_specs=[
            pl.BlockSpec(memory_space=pl.ANY),
            pl.BlockSpec((nseq, width - 1, 2 * d_ff), lambda i, c: (i, 0, 0)),
        ],
        out_shape=[
            jax.ShapeDtypeStruct((m, d), F32),
            jax.ShapeDtypeStruct((nt * nseq, width - 1, 2 * d_ff), F32),
        ],
        scratch_shapes=[
            pltpu.VMEM((2, tm, d), F32),
            pltpu.VMEM((nseq, seq_rows + 2 * SUBLANES, tf), F32),
            pltpu.VMEM((nseq, seq_rows + 2 * SUBLANES, tf), F32),
            pltpu.VMEM(car_shape, F32),
            pltpu.SemaphoreType.DMA((2,)),
            pltpu.SemaphoreType.DMA((2,)),
        ],
        compiler_params=_params(("arbitrary", "arbitrary")),
        name="conv_ffn",
    )(x, xn, f_conv_past, w_up, w_up, cw, cb.reshape(1, 2 * d_ff), w_down)
    f = f.reshape(nt // tiles_per_seq, tiles_per_seq, nseq, width - 1, 2 * d_ff)[:, -1]
    return y, f.reshape(nb, width - 1, 2 * d_ff)


def _pad_rows_front(a, rows):
    return jnp.pad(a, ((0, 0), (rows - a.shape[1], 0), (0, 0)))


def _block_diag(w):
    n, bi, bo = w.shape
    eye = jnp.eye(n, dtype=w.dtype)
    return (w[:, :, None, :] * eye[:, None, :, None]).reshape(n * bi, n * bo)


def _trunk(x3, band_fn, mem_k, mem_v, b_conv_past, b_h0, f_conv_past, p, *, in_tiles, transposed, tq_mem, tt,
           ffn_tiles, ffn_cols):
    b, t, d = x3.shape
    n_heads_a, n_heads_m, d_rnn = p["n_heads_a"], p["n_heads_m"], p["d_rnn"]
    d_a, d_m = n_heads_a * HEAD_DIM, n_heads_m * HEAD_DIM
    x2 = x3.reshape(b * t, d)
    nseq_in, rows_in, tiles_in = in_tiles
    q, k, v, qm, xg, k_a, v_a = _in_proj(x2, p["g_attn"], p["w_in"], p["g_q_a"], p["g_k_a"], p["g_q_m"],
                                         n_heads_a, n_heads_m, d_rnn, nseq_in, rows_in, tiles_in, transposed)
    xg3 = xg.reshape(b, t, 2 * d_rnn)

    o_a = band_fn(q, k, v)
    o_m = _mem_attn(qm.reshape(b, t, d_m), mem_k, mem_v, n_heads_m, tq_mem)
    lru = (xg3, _pad_rows_front(b_conv_past, SUBLANES), b_h0, p["conv_b_w"], p["conv_b_b"],
           p["wa_bd"], p["b_rg_a"], p["wi_bd"], p["b_rg_i"], p["lru_lambda"])
    wo = (p["wo_am"], p["wo_b"], p["g_ffn"])
    if tt >= OUT_ROWS:
        x1, xn1, h_last = _rglru_out_proj(x3, o_a, o_m, *lru, *wo, tt)
        x1, xn1 = x1.reshape(b * t, d), xn1.reshape(b * t, d)
    else:
        o_b, h_last = _rglru(*lru, tt)
        x1, xn1 = _out_proj(x2, o_a.reshape(b * t, d_a), o_m.reshape(b * t, d_m), o_b.reshape(b * t, d_rnn),
                            *wo, OUT_ROWS)

    nseq, seq_rows, tiles_per_seq = ffn_tiles
    y, f_new = _conv_ffn(x1, xn1, f_conv_past, p["w_up"], p["conv_f_w"], p["conv_f_b"], p["w_down"],
                         nseq, seq_rows, tiles_per_seq, ffn_cols)

    wb = p["conv_b_w"].shape[0]
    xb = xg3[:, :, :d_rnn]
    b_conv_new = jnp.concatenate([b_conv_past, xb], axis=1)[:, -(wb - 1):] if t < wb - 1 else xb[:, t - (wb - 1):]
    return y.reshape(b, t, d), k_a, v_a, b_conv_new, h_last.reshape(b, d_rnn), f_new


def kernel(x_prompt, x_sample, cache_a_k, cache_a_v, cache_mem_k, cache_mem_v, cache_b_conv, state_b_h, cache_f_conv, mem_prompt, g_attn, w_in, g_q_a, g_k_a, rel_table, g_q_m, g_k_m, g_mem, w_mem_kv, conv_b_w, conv_b_b, w_rg_a, b_rg_a, w_rg_i, b_rg_i, lru_lambda, w_out, g_ffn, w_up, conv_f_w, conv_f_b, w_down):
    depth = w_in.shape[0]
    bp, sp, d = x_prompt.shape
    bs, ts, _ = x_sample.shape
    n_heads_a = cache_a_k.shape[3]
    n_heads_m = cache_mem_k.shape[3]
    n_mem = mem_prompt.shape[1]
    d_a, d_m = n_heads_a * HEAD_DIM, n_heads_m * HEAD_DIM
    d_rnn = conv_b_w.shape[2]
    d_ff = w_down.shape[1]
    keep = min(WINDOW, sp)
    ones = lambda n: jnp.ones((n,), F32)

    xp, xs = x_prompt, x_sample
    outs = [[] for _ in range(12)]
    for l in range(depth):
        p = dict(
            n_heads_a=n_heads_a, n_heads_m=n_heads_m, d_rnn=d_rnn,
            g_attn=g_attn[l], w_in=w_in[l].astype(BF16),
            g_q_a=g_q_a[l] * SCALE, g_k_a=g_k_a[l], g_q_m=g_q_m[l] * SCALE,
            conv_b_w=conv_b_w[l], conv_b_b=conv_b_b[l],
            wa_bd=_block_diag(w_rg_a[l]).astype(BF16), b_rg_a=b_rg_a[l],
            wi_bd=_block_diag(w_rg_i[l]).astype(BF16), b_rg_i=b_rg_i[l], lru_lambda=lru_lambda[l],
            wo_am=w_out[l, :d_a + d_m].astype(BF16), wo_b=w_out[l, d_a + d_m:].astype(BF16),
            g_ffn=g_ffn[l], w_up=w_up[l].astype(BF16), conv_f_w=conv_f_w[l], conv_f_b=conv_f_b[l],
            w_down=w_down[l].astype(BF16),
        )
        bias = _rel_bias(rel_table[l])

        mem_gains = jnp.concatenate([jnp.tile(g_k_m[l], n_heads_m), ones(d_m)])
        kv = _norm_matmul(mem_prompt.reshape(bp * n_mem, d), g_mem[l], w_mem_kv[l].astype(BF16), mem_gains,
                          ((0, 1),), OUT_ROWS, d_m)
        kv3 = kv.reshape(bp, n_mem, 2 * d_m)

        band_p = lambda q, kt, v: _band_attn(q.reshape(bp, sp, d_a), kt, v.reshape(bp, sp, d_a), bias,
                                             n_heads_a, WINDOW)
        xp, k_a, v_a, bc, bh, fc = _trunk(
            xp, band_p, kv3, kv3,
            jnp.zeros((bp, conv_b_w.shape[1] - 1, d_rnn), F32), jnp.zeros((bp, d_rnn), F32),
            jnp.zeros((bp, conv_f_w.shape[1] - 1, 2 * d_ff), F32), p,
            in_tiles=(1, keep, sp // keep), transposed=True, tq_mem=OUT_ROWS, tt=OUT_ROWS,
            ffn_tiles=(1, FFN_ROWS, sp // FFN_ROWS), ffn_cols=FFN_COLS)
        mem_k = kv3[:, :, :d_m].reshape(bp, n_mem, n_heads_m, HEAD_DIM)
        mem_v = kv3[:, :, d_m:].reshape(bp, n_mem, n_heads_m, HEAD_DIM)
        for lst, v in zip(outs[:7], (k_a, v_a, mem_k, mem_v, bc, bh, fc)):
            lst.append(v)

        band_s = lambda q, k, v: _band_attn_sample(
            q.reshape(bs, ts, d_a), k.reshape(bs, ts, d_a), v.reshape(bs, ts, d_a),
            cache_a_k[l], cache_a_v[l], bias, n_heads_a)
        xs, k_a, v_a, bc, bh, fc = _trunk(
            xs, band_s, cache_mem_k[l], cache_mem_v[l],
            cache_b_conv[l], state_b_h[l], cache_f_conv[l], p,
            in_tiles=(bs, ts, 1), transposed=False, tq_mem=ts, tt=ts, ffn_tiles=(bs, ts, 1),
            ffn_cols=FFN_COLS)
        for lst, v in zip(outs[7:], (k_a, v_a, bc, bh, fc)):
            lst.append(v)

    stacked = [jnp.stack(o) for o in outs]
    return (xp, xs, *stacked)
```

```python
import functools

import jax
import jax.numpy as jnp
from jax import lax
from jax.experimental import pallas as pl
from jax.experimental.pallas import tpu as pltpu

EPS = 1e-6
HEAD_DIM = 128
CHUNK = 64
LEFT_CHUNKS = 8
WINDOW = LEFT_CHUNKS * CHUNK
REL_CLIP = 256
LRU_C = 8.0
NEG = -1e30
SCALE = HEAD_DIM ** -0.5

LANES = 128
SUBLANES = 8
MXU_COLS = 256
VMEM_LIMIT = 56 * 2 ** 20

QSUB = 4 * CHUNK
KSPAN = WINDOW + QSUB
OUT_ROWS = 512
FFN_ROWS = 1024
FFN_COLS = 512

BF16 = jnp.bfloat16
F32 = jnp.float32


def _params(sem):
    return pltpu.CompilerParams(dimension_semantics=sem, vmem_limit_bytes=VMEM_LIMIT)


def _rms(x, g):
    ms = jnp.mean(x * x, axis=-1, keepdims=True)
    return x * lax.rsqrt(ms + EPS) * g


def _norm_matmul_kernel(x_ref, g_ref, w_ref, hg_ref, o_ref, xn_ref, *, norm_ranges, tn):
    j = pl.program_id(1)

    @pl.when(j == 0)
    def _():
        xn_ref[...] = _rms(x_ref[...], g_ref[...]).astype(BF16)

    z = jnp.dot(xn_ref[...], w_ref[...], preferred_element_type=F32)
    cond = None
    for lo, hi in norm_ranges:
        c = jnp.logical_and(j >= lo, j < hi)
        cond = c if cond is None else jnp.logical_or(cond, c)

    @pl.when(cond)
    def _():
        for h in range(tn // HEAD_DIM):
            cs = slice(h * HEAD_DIM, (h + 1) * HEAD_DIM)
            o_ref[:, cs] = _rms(z[:, cs], hg_ref[:, cs])

    @pl.when(jnp.logical_not(cond))
    def _():
        o_ref[...] = z


def _norm_matmul(x, g, w_bf16, head_gains, norm_ranges, tm, tn):
    m, k = x.shape
    n = w_bf16.shape[1]
    kern = functools.partial(_norm_matmul_kernel, norm_ranges=norm_ranges, tn=tn)
    return pl.pallas_call(
        kern,
        grid=(m // tm, n // tn),
        in_specs=[
            pl.BlockSpec((tm, k), lambda i, j: (i, 0)),
            pl.BlockSpec((1, k), lambda i, j: (0, 0)),
            pl.BlockSpec((k, tn), lambda i, j: (0, j)),
            pl.BlockSpec((1, tn), lambda i, j: (0, j)),
        ],
        out_specs=pl.BlockSpec((tm, tn), lambda i, j: (i, j)),
        out_shape=jax.ShapeDtypeStruct((m, n), F32),
        scratch_shapes=[pltpu.VMEM((tm, k), BF16)],
        compiler_params=_params(("parallel", "arbitrary")),
        name="norm_matmul",
    )(x, g.reshape(1, k), w_bf16, head_gains.reshape(1, n))


def _in_proj_kernel(x_ref, g_ref, w_ref, gq_ref, gk_ref, gqm_ref, q_ref, k_ref, v_ref, qm_ref, xg_ref, ak_ref, av_ref,
                    xn_ref, kv_ref, *, d_a, d_m, d_rnn, group, tiles_per_seq, transposed):
    i = pl.program_id(0)
    xn_ref[...] = _rms(x_ref[...], g_ref[...]).astype(BF16)

    def cols(c0):
        return jnp.dot(xn_ref[...], w_ref[:, c0:c0 + group], preferred_element_type=F32)

    def heads(z, c0):
        return [(slice(c0 + h * HEAD_DIM, c0 + (h + 1) * HEAD_DIM), z[:, h * HEAD_DIM:(h + 1) * HEAD_DIM])
                for h in range(group // HEAD_DIM)]

    def put(ref, cs, val):
        if transposed:
            ref[cs, :] = val.T.astype(BF16)
        else:
            ref[:, cs] = val.astype(BF16)

    for c0 in range(0, d_a, group):
        for cs, zh in heads(cols(c0), c0):
            q_ref[:, cs] = _rms(zh, gq_ref[...]).astype(BF16)
    for c0 in range(0, d_a, group):
        for cs, zh in heads(cols(d_a + c0), c0):
            kh = _rms(zh, gk_ref[...])
            put(k_ref, cs, kh)
            kv_ref[:, cs] = kh
    for c0 in range(0, d_a, group):
        for cs, zh in heads(cols(2 * d_a + c0), c0):
            v_ref[:, cs] = zh.astype(BF16)
            kv_ref[:, slice(d_a + cs.start, d_a + cs.stop)] = zh
    for c0 in range(0, d_m, group):
        for cs, zh in heads(cols(3 * d_a + c0), c0):
            qm_ref[:, cs] = _rms(zh, gqm_ref[...]).astype(BF16)
    for c0 in range(0, 2 * d_rnn, group):
        xg_ref[:, c0:c0 + group] = cols(3 * d_a + d_m + c0)

    @pl.when(i % tiles_per_seq == tiles_per_seq - 1)
    def _():
        nseq, rows, n_heads = ak_ref.shape[0], ak_ref.shape[1], ak_ref.shape[2]
        for h in range(n_heads):
            cs = slice(h * HEAD_DIM, (h + 1) * HEAD_DIM)
            ak_ref[:, :, h, :] = kv_ref[:, cs].reshape(nseq, rows, HEAD_DIM)
            av_ref[:, :, h, :] = kv_ref[:, slice(d_a + cs.start, d_a + cs.stop)].reshape(nseq, rows, HEAD_DIM)


def _in_proj(x, g, w_bf16, gq, gk, gqm, n_heads_a, n_heads_m, d_rnn, nseq, rows, tiles_per_seq, transposed):
    m, k = x.shape
    d_in = w_bf16.shape[1]
    d_a, d_m = n_heads_a * HEAD_DIM, n_heads_m * HEAD_DIM
    tm = nseq * rows
    nt = m // tm
    n_cache_seq = nt * nseq // tiles_per_seq
    kern = functools.partial(_in_proj_kernel, d_a=d_a, d_m=d_m, d_rnn=d_rnn, group=4 * HEAD_DIM,
                             tiles_per_seq=tiles_per_seq, transposed=transposed)
    row = lambda c: pl.BlockSpec((tm, c), lambda i: (i, 0))
    vec = pl.BlockSpec((1, HEAD_DIM), lambda i: (0, 0))
    cache = pl.BlockSpec((nseq, rows, n_heads_a, HEAD_DIM), lambda i: (i // tiles_per_seq, 0, 0, 0))
    k_spec = pl.BlockSpec((d_a, tm), lambda i: (0, i)) if transposed else row(d_a)
    k_shape = jax.ShapeDtypeStruct((d_a, m) if transposed else (m, d_a), BF16)
    return pl.pallas_call(
        kern,
        grid=(nt,),
        in_specs=[
            row(k),
            pl.BlockSpec((1, k), lambda i: (0, 0)),
            pl.BlockSpec((k, d_in), lambda i: (0, 0), pipeline_mode=pl.Buffered(1)),
            vec, vec, vec,
        ],
        out_specs=[row(d_a), k_spec, row(d_a), row(d_m), row(2 * d_rnn), cache, cache],
        out_shape=[
            jax.ShapeDtypeStruct((m, d_a), BF16),
            k_shape,
            jax.ShapeDtypeStruct((m, d_a), BF16),
            jax.ShapeDtypeStruct((m, d_m), BF16),
            jax.ShapeDtypeStruct((m, 2 * d_rnn), F32),
            jax.ShapeDtypeStruct((n_cache_seq, rows, n_heads_a, HEAD_DIM), F32),
            jax.ShapeDtypeStruct((n_cache_seq, rows, n_heads_a, HEAD_DIM), F32),
        ],
        scratch_shapes=[pltpu.VMEM((tm, k), BF16), pltpu.VMEM((tm, 2 * d_a), F32)],
        compiler_params=_params(("arbitrary",)),
        name="in_proj",
    )(x, g.reshape(1, k), w_bf16, gq.reshape(1, HEAD_DIM), gk.reshape(1, HEAD_DIM), gqm.reshape(1, HEAD_DIM))


def _rel_bias_kernel(tab_ref, o_ref, *, tpad, wext):
    t = tab_ref[...]
    hi = t.astype(BF16)
    r1 = t - hi.astype(F32)
    mid = r1.astype(BF16)
    lo = (r1 - mid.astype(F32)).astype(BF16)
    m = lax.broadcasted_iota(jnp.int32, (1, wext), 1)
    x = jnp.where(m >= wext - CHUNK, m - wext, m)
    idx = jnp.clip(WINDOW - x, -REL_CLIP, REL_CLIP) + REL_CLIP
    tix = lax.broadcasted_iota(jnp.int32, (tpad, wext), 0)
    onehot = jnp.where(tix == idx, 1.0, 0.0).astype(BF16)
    v = (jnp.dot(hi, onehot, preferred_element_type=F32)
         + jnp.dot(mid, onehot, preferred_element_type=F32)
         + jnp.dot(lo, onehot, preferred_element_type=F32))
    j = lax.broadcasted_iota(jnp.int32, (1, KSPAN), 1)
    for ci in range(QSUB // CHUNK):
        jj = j - CHUNK * ci
        valid = jnp.logical_and(jj >= 0, jj < WINDOW + CHUNK)
        for ii in range(CHUNK):
            s = ci * CHUNK + ii
            row = (v if s == 0 else pltpu.roll(v, s, 1))[:, :KSPAN]
            o_ref[:, s, :] = jnp.where(valid, row, NEG)


def _rel_bias(rel_table):
    n_heads, tlen = rel_table.shape
    tpad = -(-tlen // LANES) * LANES
    wext = KSPAN + LANES
    assert wext - CHUNK >= WINDOW + CHUNK and tlen == 2 * REL_CLIP + 1
    tab = jnp.pad(rel_table, ((0, 0), (0, tpad - tlen)))
    return pl.pallas_call(
        functools.partial(_rel_bias_kernel, tpad=tpad, wext=wext),
        out_shape=jax.ShapeDtypeStruct((n_heads, QSUB, KSPAN), F32),
        compiler_params=_params(None),
        name="rel_bias",
    )(tab)


def _band_attn_kernel(q_ref, ktc_ref, vc_ref, bias_ref, o_ref, ktcat_ref, vcat_ref, *, n_heads, tq):
    t = pl.program_id(1)

    @pl.when(t == 0)
    def _():
        ktcat_ref[:, 0:WINDOW] = jnp.zeros((ktcat_ref.shape[0], WINDOW), ktcat_ref.dtype)
        vcat_ref[0:WINDOW, :] = jnp.zeros((WINDOW, vcat_ref.shape[1]), vcat_ref.dtype)

    ktcat_ref[:, WINDOW:WINDOW + tq] = ktc_ref[...]
    vcat_ref[WINDOW:WINDOW + tq, :] = vc_ref[0]

    for s in range(tq // QSUB):
        r0 = s * QSUB
        before_start = jnp.logical_and(lax.broadcasted_iota(jnp.int32, (1, KSPAN), 1) + r0 < WINDOW, t == 0)
        for h in range(n_heads):
            cs = slice(h * HEAD_DIM, (h + 1) * HEAD_DIM)
            sc = jnp.dot(q_ref[0, r0:r0 + QSUB, cs], ktcat_ref[cs, r0:r0 + KSPAN], preferred_element_type=F32)
            sc = jnp.where(before_start, NEG, sc + bias_ref[h])
            m = jnp.max(sc, axis=-1, keepdims=True)
            p = jnp.exp(sc - m)
            l = jnp.sum(p, axis=-1, keepdims=True)
            o = jnp.dot(p.astype(BF16), vcat_ref[r0:r0 + KSPAN, cs], preferred_element_type=F32) / l
            o_ref[0, r0:r0 + QSUB, cs] = o.astype(o_ref.dtype)

    ktcat_ref[:, 0:WINDOW] = ktcat_ref[:, WINDOW:WINDOW + tq]
    vcat_ref[0:WINDOW, :] = vcat_ref[WINDOW:WINDOW + tq, :]


def _band_attn(q3, kt, v3, bias, n_heads, tq):
    b, s, d_a = q3.shape
    assert tq == WINDOW and s % tq == 0
    nt = s // tq
    cur = lambda bi, t: (bi, t, 0)
    return pl.pallas_call(
        functools.partial(_band_attn_kernel, n_heads=n_heads, tq=tq),
        grid=(b, nt),
        in_specs=[
            pl.BlockSpec((1, tq, d_a), cur),
            pl.BlockSpec((d_a, tq), lambda bi, t: (0, bi * nt + t)),
            pl.BlockSpec((1, tq, d_a), cur),
            pl.BlockSpec((n_heads, QSUB, KSPAN), lambda bi, t: (0, 0, 0)),
        ],
        out_specs=pl.BlockSpec((1, tq, d_a), cur),
        out_shape=jax.ShapeDtypeStruct((b, s, d_a), BF16),
        scratch_shapes=[pltpu.VMEM((d_a, WINDOW + tq), BF16), pltpu.VMEM((WINDOW + tq, d_a), BF16)],
        compiler_params=_params(("parallel", "arbitrary")),
        name="band_attn",
    )(q3, kt, v3, bias)


def _band_attn_sample_kernel(q_ref, kn_ref, vn_ref, ck_hbm, cv_hbm, bias_ref, o_ref, kbuf, vbuf, sem,
                             *, n_heads, t, p_len):
    b = pl.program_id(0)
    slot = b % 2

    def cache_copies(seq, sl):
        cps = []
        for h in range(n_heads):
            cps.append(pltpu.make_async_copy(ck_hbm.at[seq, :, h, :], kbuf.at[sl, h], sem.at[0, sl, h]))
            cps.append(pltpu.make_async_copy(cv_hbm.at[seq, :, h, :], vbuf.at[sl, h], sem.at[1, sl, h]))
        return cps

    @pl.when(b == 0)
    def _():
        for cp in cache_copies(0, 0):
            cp.start()

    @pl.when(b + 1 < pl.num_programs(0))
    def _():
        for cp in cache_copies(b + 1, 1 - slot):
            cp.start()

    for cp in cache_copies(b, slot):
        cp.wait()

    heads = [slice(h * HEAD_DIM, (h + 1) * HEAD_DIM) for h in range(n_heads)]
    dn = (((1,), (1,)), ((), ()))
    scores = []
    for h, cs in enumerate(heads):
        q = q_ref[0, :, cs]
        ck = kbuf[slot, h].astype(BF16)
        scores.append((lax.dot_general(q, ck, dn, preferred_element_type=F32),
                       lax.dot_general(q, kn_ref[0, :, cs], dn, preferred_element_type=F32)))
    for h, cs in enumerate(heads):
        cv = vbuf[slot, h].astype(BF16)
        vn = vn_ref[0, :, cs]
        s1 = scores[h][0] + bias_ref[h, 0:t, 0:p_len]
        s2 = scores[h][1] + bias_ref[h, 0:t, p_len:p_len + t]
        m = jnp.maximum(jnp.max(s1, axis=-1, keepdims=True), jnp.max(s2, axis=-1, keepdims=True))
        p1 = jnp.exp(s1 - m)
        p2 = jnp.exp(s2 - m)
        l = jnp.sum(p1, axis=-1, keepdims=True) + jnp.sum(p2, axis=-1, keepdims=True)
        o = (jnp.dot(p1.astype(BF16), cv, preferred_element_type=F32)
             + jnp.dot(p2.astype(BF16), vn, preferred_element_type=F32)) / l
        o_ref[0, :, cs] = o.astype(o_ref.dtype)


def _band_attn_sample(q3, k3, v3, cache_k, cache_v, bias, n_heads):
    b, t, d_a = q3.shape
    p_len = cache_k.shape[1]
    assert p_len == WINDOW and t <= CHUNK
    cache = pl.BlockSpec(memory_space=pl.ANY)
    return pl.pallas_call(
        functools.partial(_band_attn_sample_kernel, n_heads=n_heads, t=t, p_len=p_len),
        grid=(b,),
        in_specs=[
            pl.BlockSpec((1, t, d_a), lambda bi: (bi, 0, 0)),
            pl.BlockSpec((1, t, d_a), lambda bi: (bi, 0, 0)),
            pl.BlockSpec((1, t, d_a), lambda bi: (bi, 0, 0)),
            cache, cache,
            pl.BlockSpec((n_heads, QSUB, KSPAN), lambda bi: (0, 0, 0)),
        ],
        out_specs=pl.BlockSpec((1, t, d_a), lambda bi: (bi, 0, 0)),
        out_shape=jax.ShapeDtypeStruct((b, t, d_a), BF16),
        scratch_shapes=[
            pltpu.VMEM((2, n_heads, p_len, HEAD_DIM), cache_k.dtype),
            pltpu.VMEM((2, n_heads, p_len, HEAD_DIM), cache_v.dtype),
            pltpu.SemaphoreType.DMA((2, 2, n_heads)),
        ],
        compiler_params=_params(("arbitrary",)),
        name="band_attn_sample",
    )(q3, k3, v3, cache_k, cache_v, bias)


def _mem_attn_kernel(q_ref, mk_ref, mv_ref, o_ref, *, n_heads, head_axis):
    heads = [slice(h * HEAD_DIM, (h + 1) * HEAD_DIM) for h in range(n_heads)]
    load = (lambda ref, h, cs: ref[0, :, h, :]) if head_axis else (lambda ref, h, cs: ref[0, :, cs])
    scores = [lax.dot_general(q_ref[0, :, cs], load(mk_ref, h, cs).astype(BF16), (((1,), (1,)), ((), ())),
                              preferred_element_type=F32) for h, cs in enumerate(heads)]
    for h, cs in enumerate(heads):
        v = load(mv_ref, h, cs).astype(BF16)
        sc = scores[h]
        m = jnp.max(sc, axis=-1, keepdims=True)
        p = jnp.exp(sc - m)
        l = jnp.sum(p, axis=-1, keepdims=True)
        o = jnp.dot(p.astype(BF16), v, preferred_element_type=F32) / l
        o_ref[0, :, cs] = o.astype(o_ref.dtype)


def _mem_attn(qm3, mem_k, mem_v, n_heads, tm):
    b, s, d_m = qm3.shape
    n_mem = mem_k.shape[1]
    head_axis = mem_k.ndim == 4
    if head_axis:
        kspec = vspec = pl.BlockSpec((1, n_mem, n_heads, HEAD_DIM), lambda bi, t: (bi, 0, 0, 0))
    else:
        kspec = pl.BlockSpec((1, n_mem, d_m), lambda bi, t: (bi, 0, 0))
        vspec = pl.BlockSpec((1, n_mem, d_m), lambda bi, t: (bi, 0, 1))
    return pl.pallas_call(
        functools.partial(_mem_attn_kernel, n_heads=n_heads, head_axis=head_axis),
        grid=(b, s // tm),
        in_specs=[pl.BlockSpec((1, tm, d_m), lambda bi, t: (bi, t, 0)), kspec, vspec],
        out_specs=pl.BlockSpec((1, tm, d_m), lambda bi, t: (bi, t, 0)),
        out_shape=jax.ShapeDtypeStruct((b, s, d_m), BF16),
        compiler_params=_params(("parallel", "parallel")),
        name="mem_attn",
    )(qm3, mem_k, mem_v)


def _rglru_core(xb_ref, gb_ref, past_ref, h0_ref, cw_ref, cb_ref, wa_ref, ba_ref, wi_ref, bi_ref, lam_ref,
                buf_ref, a_ref, u_ref, hc_ref, tt, width, after_gates=None):
    t = pl.program_id(1)
    pad = SUBLANES

    @pl.when(t == 0)
    def _():
        buf_ref[0:pad, :] = past_ref[0]
        hc_ref[...] = h0_ref[0]

    buf_ref[pad:pad + tt, :] = xb_ref[0]
    xc = cb_ref[...] + cw_ref[width - 1:width, :] * xb_ref[0]
    for k in range(1, width):
        xc = xc + cw_ref[width - 1 - k:width - k, :] * buf_ref[pad - k:pad - k + tt, :]
    buf_ref[0:pad, :] = buf_ref[tt:tt + pad, :]

    xcb = xc.astype(BF16)
    def sigmoid(z):
        return 0.5 * jnp.tanh(0.5 * z) + 0.5

    r = sigmoid(jnp.dot(xcb, wa_ref[...], preferred_element_type=F32) + ba_ref[...])
    i = sigmoid(jnp.dot(xcb, wi_ref[...], preferred_element_type=F32) + bi_ref[...])
    if after_gates is not None:
        after_gates()
    log_a = -LRU_C * r * jax.nn.softplus(-lam_ref[...])
    a = jnp.exp(log_a)
    th = jnp.tanh(log_a)
    n = -2.0 * th
    dd = 1.0 - th
    u = jnp.where(n > 0.0, n * lax.rsqrt(n * dd), 0.0) * (i * xc)

    d = a.shape[-1]
    a = a.reshape(tt // SUBLANES, SUBLANES, d)
    u = u.reshape(tt // SUBLANES, SUBLANES, d)
    rowm = lax.broadcasted_iota(jnp.int32, (1, SUBLANES, 1), 1)
    sh = 1
    while sh < SUBLANES:
        keep = rowm >= sh
        a_sh = pltpu.roll(a, sh, 1)
        u_sh = pltpu.roll(u, sh, 1)
        u = jnp.where(keep, a * u_sh + u, u)
        a = jnp.where(keep, a * a_sh, a)
        sh *= 2
    a_ref[...] = a.reshape(tt, d)
    u_ref[...] = u.reshape(tt, d)

    h_last = hc_ref[...]
    for r0 in range(0, tt, SUBLANES):
        hr = a_ref[r0:r0 + SUBLANES, :] * h_last + u_ref[r0:r0 + SUBLANES, :]
        u_ref[r0:r0 + SUBLANES, :] = hr
        h_last = hr[SUBLANES - 1:SUBLANES, :]
    hc_ref[...] = h_last
    return h_last, (u_ref[...] * jax.nn.gelu(gb_ref[0])).astype(BF16)


def _rglru_kernel(xb_ref, gb_ref, past_ref, h0_ref, cw_ref, cb_ref, wa_ref, ba_ref, wi_ref, bi_ref, lam_ref,
                  o_ref, hl_ref, buf_ref, a_ref, u_ref, hc_ref, *, tt, width):
    h_last, ob = _rglru_core(xb_ref, gb_ref, past_ref, h0_ref, cw_ref, cb_ref, wa_ref, ba_ref, wi_ref, bi_ref,
                             lam_ref, buf_ref, a_ref, u_ref, hc_ref, tt, width)
    hl_ref[0] = h_last
    o_ref[0] = ob


def _rglru_out_proj_kernel(x_ref, oa_ref, om_ref, xb_ref, gb_ref, past_ref, h0_ref, cw_ref, cb_ref, wa_ref, ba_ref,
                           wi_ref, bi_ref, lam_ref, woam_ref, wob_ref, g_ref,
                           o_ref, xn_ref, hl_ref, buf_ref, a_ref, u_ref, hc_ref, *, tt, width):
    def attention_share():
        lhs = jnp.concatenate([oa_ref[0], om_ref[0]], axis=1)
        o_ref[0] = jnp.dot(lhs, woam_ref[...], preferred_element_type=F32)

    h_last, ob = _rglru_core(xb_ref, gb_ref, past_ref, h0_ref, cw_ref, cb_ref, wa_ref, ba_ref, wi_ref, bi_ref,
                             lam_ref, buf_ref, a_ref, u_ref, hc_ref, tt, width, after_gates=attention_share)
    hl_ref[0] = h_last
    x1 = x_ref[0] + o_ref[0] + jnp.dot(ob, wob_ref[...], preferred_element_type=F32)
    o_ref[0] = x1
    xn_ref[0] = _rms(x1, g_ref[...]).astype(BF16)


def _rglru(xg3, past8, h0, cw, cb, wa_bd, ba, wi_bd, bi, lam, tt):
    b, s, _ = xg3.shape
    d = cw.shape[1]
    width = cw.shape[0]
    vec = pl.BlockSpec((1, d), lambda bi_, t: (0, 0))
    mat = pl.BlockSpec((d, d), lambda bi_, t: (0, 0))
    return pl.pallas_call(
        functools.partial(_rglru_kernel, tt=tt, width=width),
        grid=(b, s // tt),
        in_specs=[
            pl.BlockSpec((1, tt, d), lambda bi_, t: (bi_, t, 0)),
            pl.BlockSpec((1, tt, d), lambda bi_, t: (bi_, t, 1)),
            pl.BlockSpec((1, SUBLANES, d), lambda bi_, t: (bi_, 0, 0)),
            pl.BlockSpec((1, 1, d), lambda bi_, t: (bi_, 0, 0)),
            pl.BlockSpec((width, d), lambda bi_, t: (0, 0)),
            vec, mat, vec, mat, vec, vec,
        ],
        out_specs=[
            pl.BlockSpec((1, tt, d), lambda bi_, t: (bi_, t, 0)),
            pl.BlockSpec((1, 1, d), lambda bi_, t: (bi_, 0, 0)),
        ],
        out_shape=[jax.ShapeDtypeStruct((b, s, d), BF16), jax.ShapeDtypeStruct((b, 1, d), F32)],
        scratch_shapes=[
            pltpu.VMEM((tt + 2 * SUBLANES, d), F32),
            pltpu.VMEM((tt, d), F32),
            pltpu.VMEM((tt, d), F32),
            pltpu.VMEM((1, d), F32),
        ],
        compiler_params=_params(("parallel", "arbitrary")),
        name="rglru",
    )(xg3, xg3, past8, h0.reshape(b, 1, d), cw, cb.reshape(1, d), wa_bd, ba.reshape(1, d),
      wi_bd, bi.reshape(1, d), lam.reshape(1, d))


def _rglru_out_proj(x3, oa3, om3, xg3, past8, h0, cw, cb, wa_bd, ba, wi_bd, bi, lam, woam, wob, g_next, tt):
    b, s, dm = x3.shape
    d = cw.shape[1]
    width = cw.shape[0]
    vec = pl.BlockSpec((1, d), lambda bi_, t: (0, 0))
    mat = pl.BlockSpec((d, d), lambda bi_, t: (0, 0))
    row = lambda c, j=0: pl.BlockSpec((1, tt, c), lambda bi_, t: (bi_, t, j))
    full = lambda w: pl.BlockSpec(w.shape, lambda bi_, t: (0, 0), pipeline_mode=pl.Buffered(1))
    return pl.pallas_call(
        functools.partial(_rglru_out_proj_kernel, tt=tt, width=width),
        grid=(b, s // tt),
        in_specs=[
            row(dm), row(oa3.shape[2]), row(om3.shape[2]), row(d, 0), row(d, 1),
            pl.BlockSpec((1, SUBLANES, d), lambda bi_, t: (bi_, 0, 0)),
            pl.BlockSpec((1, 1, d), lambda bi_, t: (bi_, 0, 0)),
            pl.BlockSpec((width, d), lambda bi_, t: (0, 0)),
            vec, mat, vec, mat, vec, vec,
            full(woam), full(wob),
            pl.BlockSpec((1, dm), lambda bi_, t: (0, 0)),
        ],
        out_specs=[row(dm), row(dm), pl.BlockSpec((1, 1, d), lambda bi_, t: (bi_, 0, 0))],
        out_shape=[jax.ShapeDtypeStruct((b, s, dm), F32), jax.ShapeDtypeStruct((b, s, dm), BF16),
                   jax.ShapeDtypeStruct((b, 1, d), F32)],
        scratch_shapes=[
            pltpu.VMEM((tt + 2 * SUBLANES, d), F32),
            pltpu.VMEM((tt, d), F32),
            pltpu.VMEM((tt, d), F32),
            pltpu.VMEM((1, d), F32),
        ],
        compiler_params=_params(("parallel", "arbitrary")),
        name="rglru_out_proj",
    )(x3, oa3, om3, xg3, xg3, past8, h0.reshape(b, 1, d), cw, cb.reshape(1, d), wa_bd, ba.reshape(1, d),
      wi_bd, bi.reshape(1, d), lam.reshape(1, d), woam, wob, g_next.reshape(1, dm))


def _out_proj_kernel(x_ref, oa_ref, om_ref, ob_ref, wam_ref, wb_ref, g_ref, o_ref, xn_ref):
    lhs = jnp.concatenate([oa_ref[...], om_ref[...]], axis=1)
    acc = jnp.dot(lhs, wam_ref[...], preferred_element_type=F32)
    acc = acc + jnp.dot(ob_ref[...], wb_ref[...], preferred_element_type=F32)
    x1 = x_ref[...] + acc
    o_ref[...] = x1
    xn_ref[...] = _rms(x1, g_ref[...]).astype(BF16)


def _out_proj(x, oa, om, ob, wam, wb, g_next, tm):
    m, d = x.shape
    row = lambda c: pl.BlockSpec((tm, c), lambda i: (i, 0))
    full = lambda a: pl.BlockSpec(a.shape, lambda i: (0, 0))
    return pl.pallas_call(
        _out_proj_kernel,
        grid=(m // tm,),
        in_specs=[row(d), row(oa.shape[1]), row(om.shape[1]), row(ob.shape[1]), full(wam), full(wb),
                  pl.BlockSpec((1, d), lambda i: (0, 0))],
        out_specs=[row(d), row(d)],
        out_shape=[jax.ShapeDtypeStruct((m, d), F32), jax.ShapeDtypeStruct((m, d), BF16)],
        compiler_params=_params(("parallel",)),
        name="out_proj",
    )(x, oa, om, ob, wam, wb, g_next.reshape(1, d))


def _conv_ffn_kernel(x_hbm, xn_ref, hist_ref, wv_ref, wg_ref, cw_ref, cb_ref, wd_ref,
                     o_hbm, f_ref, acc_ref, bv_ref, bg_ref, car_ref, sem_in, sem_out,
                     *, nseq, seq_rows, tiles_per_seq, width):
    i = pl.program_id(0)
    c = pl.program_id(1)
    nt = pl.num_programs(0)
    nc = pl.num_programs(1)
    pad = SUBLANES
    tm = nseq * seq_rows
    tf = wv_ref.shape[1]
    sub = min(tf, MXU_COLS)
    streams = ((c, wv_ref, bv_ref), (nc + c, wg_ref, bg_ref))
    cols = lambda j, start=0, size=tf: pl.ds(pl.multiple_of(j * tf + start, sub), size)
    slot = i % 2

    def fetch(tile, sl):
        return pltpu.make_async_copy(x_hbm.at[pl.ds(tile * tm, tm)], acc_ref.at[sl], sem_in.at[sl])

    def drain(tile, sl):
        return pltpu.make_async_copy(acc_ref.at[sl], o_hbm.at[pl.ds(tile * tm, tm)], sem_out.at[sl])

    @pl.when(jnp.logical_and(i == 0, c == 0))
    def _():
        fetch(0, 0).start()

    @pl.when(c == 0)
    def _():
        fetch(i, slot).wait()

    @pl.when(c == 1)
    def _():
        @pl.when(i >= 1)
        def _():
            drain(i - 1, 1 - slot).wait()

        @pl.when(i + 1 < nt)
        def _():
            fetch(i + 1, 1 - slot).start()

    for j, _, buf_ref in streams:
        if tiles_per_seq == 1:
            buf_ref[:, pad - (width - 1):pad, :] = hist_ref[:, :, cols(j)]
        else:
            first = i % tiles_per_seq == 0

            @pl.when(first)
            def _(j=j, buf_ref=buf_ref):
                buf_ref[:, pad - (width - 1):pad, :] = hist_ref[:, :, cols(j)]

            @pl.when(jnp.logical_not(first))
            def _(j=j, buf_ref=buf_ref):
                buf_ref[:, 0:pad, :] = car_ref[j]

    def up_proj(cs):
        for _, w_ref, buf_ref in streams:
            up = jnp.dot(xn_ref[...], w_ref[:, cs], preferred_element_type=F32)
            buf_ref[:, pad:pad + seq_rows, cs] = up.reshape(nseq, seq_rows, sub)

    def conv(j, buf_ref, cs):
        lanes = cols(j, cs.start, sub)
        y = cb_ref[:, lanes]
        for k in range(width):
            y = y + cw_ref[width - 1 - k:width - k, lanes] * buf_ref[:, pad - k:pad - k + seq_rows, cs]
        return y

    subs = [slice(s * sub, (s + 1) * sub) for s in range(tf // sub)]
    up_proj(subs[0])
    for s, cs in enumerate(subs):
        if s + 1 < len(subs):
            up_proj(subs[s + 1])
        val = conv(streams[0][0], bv_ref, cs)
        gate = conv(streams[1][0], bg_ref, cs)
        h = (jax.nn.gelu(gate) * val).astype(BF16).reshape(tm, sub)
        acc_ref[slot] += jnp.dot(h, wd_ref[cs, :], preferred_element_type=F32)

    for j, _, buf_ref in streams:
        f_ref[:, :, cols(j)] = buf_ref[:, pad + seq_rows - (width - 1):pad + seq_rows, :]
        if tiles_per_seq > 1:
            car_ref[j] = buf_ref[:, seq_rows:seq_rows + pad, :]

    @pl.when(c == nc - 1)
    def _():
        drain(i, slot).start()

        @pl.when(i == nt - 1)
        def _():
            drain(i, slot).wait()


def _conv_ffn(x, xn, f_conv_past, w_up, cw, cb, w_down, nseq, seq_rows, tiles_per_seq, tf):
    m, d = x.shape
    d_ff = w_down.shape[0]
    width = cw.shape[0]
    tm = nseq * seq_rows
    nt = m // tm
    nc = d_ff // tf
    nb = f_conv_past.shape[0]
    assert nc >= 2 and m == nt * tm
    assert f_conv_past.shape == (nb, width - 1, 2 * d_ff) and nb * tiles_per_seq == nt * nseq
    kern = functools.partial(_conv_ffn_kernel, nseq=nseq, seq_rows=seq_rows,
                             tiles_per_seq=tiles_per_seq, width=width)
    car_shape = (2 * nc, nseq, SUBLANES, tf) if tiles_per_seq > 1 else (1, 1, SUBLANES, LANES)
    y, f = pl.pallas_call(
        kern,
        grid=(nt, nc),
        in_specs=[
            pl.BlockSpec(memory_space=pl.ANY),
            pl.BlockSpec((tm, d), lambda i, c: (i, 0)),
            pl.BlockSpec((nseq, width - 1, 2 * d_ff), lambda i, c: (i // tiles_per_seq, 0, 0)),
            pl.BlockSpec((d, tf), lambda i, c: (0, c)),
            pl.BlockSpec((d, tf), lambda i, c: (0, nc + c)),
            pl.BlockSpec((width, 2 * d_ff), lambda i, c: (0, 0)),
            pl.BlockSpec((1, 2 * d_ff), lambda i, c: (0, 0)),
            pl.BlockSpec((tf, d), lambda i, c: (c, 0)),
        ],
        out_specs=[
            pl.BlockSpec(memory_space=pl.ANY),
            pl.BlockSpec((nseq, width - 1, 2 * d_ff), lambda i, c: (i, 0, 0)),
        ],
        out_shape=[
            jax.ShapeDtypeStruct((m, d), F32),
            jax.ShapeDtypeStruct((nt * nseq, width - 1, 2 * d_ff), F32),
        ],
        scratch_shapes=[
            pltpu.VMEM((2, tm, d), F32),
            pltpu.VMEM((nseq, seq_rows + 2 * SUBLANES, tf), F32),
            pltpu.VMEM((nseq, seq_rows + 2 * SUBLANES, tf), F32),
            pltpu.VMEM(car_shape, F32),
            pltpu.SemaphoreType.DMA((2,)),
            pltpu.SemaphoreType.DMA((2,)),
        ],
        compiler_params=_params(("arbitrary", "arbitrary")),
        name="conv_ffn",
    )(x, xn, f_conv_past, w_up, w_up, cw, cb.reshape(1, 2 * d_ff), w_down)
    f = f.reshape(nt // tiles_per_seq, tiles_per_seq, nseq, width - 1, 2 * d_ff)[:, -1]
    return y, f.reshape(nb, width - 1, 2 * d_ff)


def _pad_rows_front(a, rows):
    return jnp.pad(a, ((0, 0), (rows - a.shape[1], 0), (0, 0)))


def _block_diag(w):
    n, bi, bo = w.shape
    eye = jnp.eye(n, dtype=w.dtype)
    return (w[:, :, None, :] * eye[:, None, :, None]).reshape(n * bi, n * bo)


def _trunk(x3, band_fn, mem_k, mem_v, b_conv_past, b_h0, f_conv_past, p, *, in_tiles, transposed, tq_mem, tt,
           ffn_tiles, ffn_cols):
    b, t, d = x3.shape
    n_heads_a, n_heads_m, d_rnn = p["n_heads_a"], p["n_heads_m"], p["d_rnn"]
    d_a, d_m = n_heads_a * HEAD_DIM, n_heads_m * HEAD_DIM
    x2 = x3.reshape(b * t, d)
    nseq_in, rows_in, tiles_in = in_tiles
    q, k, v, qm, xg, k_a, v_a = _in_proj(x2, p["g_attn"], p["w_in"], p["g_q_a"], p["g_k_a"], p["g_q_m"],
                                         n_heads_a, n_heads_m, d_rnn, nseq_in, rows_in, tiles_in, transposed)
    xg3 = xg.reshape(b, t, 2 * d_rnn)

    o_a = band_fn(q, k, v)
    o_m = _mem_attn(qm.reshape(b, t, d_m), mem_k, mem_v, n_heads_m, tq_mem)
    lru = (xg3, _pad_rows_front(b_conv_past, SUBLANES), b_h0, p["conv_b_w"], p["conv_b_b"],
           p["wa_bd"], p["b_rg_a"], p["wi_bd"], p["b_rg_i"], p["lru_lambda"])
    wo = (p["wo_am"], p["wo_b"], p["g_ffn"])
    if tt >= OUT_ROWS:
        x1, xn1, h_last = _rglru_out_proj(x3, o_a, o_m, *lru, *wo, tt)
        x1, xn1 = x1.reshape(b * t, d), xn1.reshape(b * t, d)
    else:
        o_b, h_last = _rglru(*lru, tt)
        x1, xn1 = _out_proj(x2, o_a.reshape(b * t, d_a), o_m.reshape(b * t, d_m), o_b.reshape(b * t, d_rnn),
                            *wo, OUT_ROWS)

    nseq, seq_rows, tiles_per_seq = ffn_tiles
    y, f_new = _conv_ffn(x1, xn1, f_conv_past, p["w_up"], p["conv_f_w"], p["conv_f_b"], p["w_down"],
                         nseq, seq_rows, tiles_per_seq, ffn_cols)

    wb = p["conv_b_w"].shape[0]
    xb = xg3[:, :, :d_rnn]
    b_conv_new = jnp.concatenate([b_conv_past, xb], axis=1)[:, -(wb - 1):] if t < wb - 1 else xb[:, t - (wb - 1):]
    return y.reshape(b, t, d), k_a, v_a, b_conv_new, h_last.reshape(b, d_rnn), f_new


def kernel(x_prompt, x_sample, cache_a_k, cache_a_v, cache_mem_k, cache_mem_v, cache_b_conv, state_b_h, cache_f_conv, mem_prompt, g_attn, w_in, g_q_a, g_k_a, rel_table, g_q_m, g_k_m, g_mem, w_mem_kv, conv_b_w, conv_b_b, w_rg_a, b_rg_a, w_rg_i, b_rg_i, lru_lambda, w_out, g_ffn, w_up, conv_f_w, conv_f_b, w_down):
    depth = w_in.shape[0]
    bp, sp, d = x_prompt.shape
    bs, ts, _ = x_sample.shape
    n_heads_a = cache_a_k.shape[3]
    n_heads_m = cache_mem_k.shape[3]
    n_mem = mem_prompt.shape[1]
    d_a, d_m = n_heads_a * HEAD_DIM, n_heads_m * HEAD_DIM
    d_rnn = conv_b_w.shape[2]
    d_ff = w_down.shape[1]
    keep = min(WINDOW, sp)
    ones = lambda n: jnp.ones((n,), F32)

    xp, xs = x_prompt, x_sample
    outs = [[] for _ in range(12)]
    for l in range(depth):
        p = dict(
            n_heads_a=n_heads_a, n_heads_m=n_heads_m, d_rnn=d_rnn,
            g_attn=g_attn[l], w_in=w_in[l].astype(BF16),
            g_q_a=g_q_a[l] * SCALE, g_k_a=g_k_a[l], g_q_m=g_q_m[l] * SCALE,
            conv_b_w=conv_b_w[l], conv_b_b=conv_b_b[l],
            wa_bd=_block_diag(w_rg_a[l]).astype(BF16), b_rg_a=b_rg_a[l],
            wi_bd=_block_diag(w_rg_i[l]).astype(BF16), b_rg_i=b_rg_i[l], lru_lambda=lru_lambda[l],
            wo_am=w_out[l, :d_a + d_m].astype(BF16), wo_b=w_out[l, d_a + d_m:].astype(BF16),
            g_ffn=g_ffn[l], w_up=w_up[l].astype(BF16), conv_f_w=conv_f_w[l], conv_f_b=conv_f_b[l],
            w_down=w_down[l].astype(BF16),
        )
        bias = _rel_bias(rel_table[l])

        mem_gains = jnp.concatenate([jnp.tile(g_k_m[l], n_heads_m), ones(d_m)])
        kv = _norm_matmul(mem_prompt.reshape(bp * n_mem, d), g_mem[l], w_mem_kv[l].astype(BF16), mem_gains,
                          ((0, 1),), OUT_ROWS, d_m)
        kv3 = kv.reshape(bp, n_mem, 2 * d_m)

        band_p = lambda q, kt, v: _band_attn(q.reshape(bp, sp, d_a), kt, v.reshape(bp, sp, d_a), bias,
                                             n_heads_a, WINDOW)
        xp, k_a, v_a, bc, bh, fc = _trunk(
            xp, band_p, kv3, kv3,
            jnp.zeros((bp, conv_b_w.shape[1] - 1, d_rnn), F32), jnp.zeros((bp, d_rnn), F32),
            jnp.zeros((bp, conv_f_w.shape[1] - 1, 2 * d_ff), F32), p,
            in_tiles=(1, keep, sp // keep), transposed=True, tq_mem=FFN_ROWS, tt=OUT_ROWS,
            ffn_tiles=(1, FFN_ROWS, sp // FFN_ROWS), ffn_cols=FFN_COLS)
        mem_k = kv3[:, :, :d_m].reshape(bp, n_mem, n_heads_m, HEAD_DIM)
        mem_v = kv3[:, :, d_m:].reshape(bp, n_mem, n_heads_m, HEAD_DIM)
        for lst, v in zip(outs[:7], (k_a, v_a, mem_k, mem_v, bc, bh, fc)):
            lst.append(v)

        band_s = lambda q, k, v: _band_attn_sample(
            q.reshape(bs, ts, d_a), k.reshape(bs, ts, d_a), v.reshape(bs, ts, d_a),
            cache_a_k[l], cache_a_v[l], bias, n_heads_a)
        xs, k_a, v_a, bc, bh, fc = _trunk(
            xs, band_s, cache_mem_k[l], cache_mem_v[l],
            cache_b_conv[l], state_b_h[l], cache_f_conv[l], p,
            in_tiles=(bs, ts, 1), transposed=False, tq_mem=ts, tt=ts, ffn_tiles=(bs, ts, 1),
            ffn_cols=FFN_COLS)
        for lst, v in zip(outs[7:], (k_a, v_a, bc, bh, fc)):
            lst.append(v)

    stacked = [jnp.stack(o) for o in outs]
    return (xp, xs, *stacked)
```

```python
import functools

import jax
import jax.numpy as jnp
from jax import lax
from jax.experimental import pallas as pl
from jax.experimental.pallas import tpu as pltpu

EPS = 1e-6
HEAD_DIM = 128
CHUNK = 64
LEFT_CHUNKS = 8
WINDOW = LEFT_CHUNKS * CHUNK
REL_CLIP = 256
LRU_C = 8.0
NEG = -1e30
SCALE = HEAD_DIM ** -0.5

LANES = 128
SUBLANES = 8
MXU_COLS = 256
VMEM_LIMIT = 56 * 2 ** 20

QSUB = 4 * CHUNK
KSPAN = WINDOW + QSUB
OUT_ROWS = 512
FFN_ROWS = 1024
FFN_COLS = 512

BF16 = jnp.bfloat16
F32 = jnp.float32


def _params(sem):
    return pltpu.CompilerParams(dimension_semantics=sem, vmem_limit_bytes=VMEM_LIMIT)


def _rms(x, g):
    ms = jnp.mean(x * x, axis=-1, keepdims=True)
    return x * lax.rsqrt(ms + EPS) * g


def _gelu(x):
    k = (2.0 / jnp.pi) ** 0.5
    hx = 0.5 * x
    return hx + hx * jnp.tanh(x * (x * x * (0.044715 * k) + k))


def _norm_matmul_kernel(x_ref, g_ref, w_ref, hg_ref, o_ref, xn_ref, *, norm_ranges, tn):
    j = pl.program_id(1)

    @pl.when(j == 0)
    def _():
        xn_ref[...] = _rms(x_ref[...], g_ref[...]).astype(BF16)

    z = jnp.dot(xn_ref[...], w_ref[...], preferred_element_type=F32)
    cond = None
    for lo, hi in norm_ranges:
        c = jnp.logical_and(j >= lo, j < hi)
        cond = c if cond is None else jnp.logical_or(cond, c)

    @pl.when(cond)
    def _():
        for h in range(tn // HEAD_DIM):
            cs = slice(h * HEAD_DIM, (h + 1) * HEAD_DIM)
            o_ref[:, cs] = _rms(z[:, cs], hg_ref[:, cs])

    @pl.when(jnp.logical_not(cond))
    def _():
        o_ref[...] = z


def _norm_matmul(x, g, w_bf16, head_gains, norm_ranges, tm, tn):
    m, k = x.shape
    n = w_bf16.shape[1]
    kern = functools.partial(_norm_matmul_kernel, norm_ranges=norm_ranges, tn=tn)
    return pl.pallas_call(
        kern,
        grid=(m // tm, n // tn),
        in_specs=[
            pl.BlockSpec((tm, k), lambda i, j: (i, 0)),
            pl.BlockSpec((1, k), lambda i, j: (0, 0)),
            pl.BlockSpec((k, tn), lambda i, j: (0, j)),
            pl.BlockSpec((1, tn), lambda i, j: (0, j)),
        ],
        out_specs=pl.BlockSpec((tm, tn), lambda i, j: (i, j)),
        out_shape=jax.ShapeDtypeStruct((m, n), F32),
        scratch_shapes=[pltpu.VMEM((tm, k), BF16)],
        compiler_params=_params(("parallel", "arbitrary")),
        name="norm_matmul",
    )(x, g.reshape(1, k), w_bf16, head_gains.reshape(1, n))


def _in_proj_kernel(x_ref, g_ref, w_ref, gq_ref, gk_ref, gqm_ref, q_ref, k_ref, v_ref, qm_ref, xg_ref, ak_ref, av_ref,
                    xn_ref, kv_ref, *, d_a, d_m, d_rnn, group, tiles_per_seq, transposed):
    i = pl.program_id(0)
    xn_ref[...] = _rms(x_ref[...], g_ref[...]).astype(BF16)

    def cols(c0):
        return jnp.dot(xn_ref[...], w_ref[:, c0:c0 + group], preferred_element_type=F32)

    def heads(z, c0):
        return [(slice(c0 + h * HEAD_DIM, c0 + (h + 1) * HEAD_DIM), z[:, h * HEAD_DIM:(h + 1) * HEAD_DIM])
                for h in range(group // HEAD_DIM)]

    def put(ref, cs, val):
        if transposed:
            ref[cs, :] = val.T.astype(BF16)
        else:
            ref[:, cs] = val.astype(BF16)

    for c0 in range(0, d_a, group):
        for cs, zh in heads(cols(c0), c0):
            q_ref[:, cs] = _rms(zh, gq_ref[...]).astype(BF16)
    for c0 in range(0, d_a, group):
        for cs, zh in heads(cols(d_a + c0), c0):
            kh = _rms(zh, gk_ref[...])
            put(k_ref, cs, kh)
            kv_ref[:, cs] = kh
    for c0 in range(0, d_a, group):
        for cs, zh in heads(cols(2 * d_a + c0), c0):
            v_ref[:, cs] = zh.astype(BF16)
            kv_ref[:, slice(d_a + cs.start, d_a + cs.stop)] = zh
    for c0 in range(0, d_m, group):
        for cs, zh in heads(cols(3 * d_a + c0), c0):
            qm_ref[:, cs] = _rms(zh, gqm_ref[...]).astype(BF16)
    for c0 in range(0, 2 * d_rnn, group):
        xg_ref[:, c0:c0 + group] = cols(3 * d_a + d_m + c0)

    @pl.when(i % tiles_per_seq == tiles_per_seq - 1)
    def _():
        nseq, rows, n_heads = ak_ref.shape[0], ak_ref.shape[1], ak_ref.shape[2]
        for h in range(n_heads):
            cs = slice(h * HEAD_DIM, (h + 1) * HEAD_DIM)
            ak_ref[:, :, h, :] = kv_ref[:, cs].reshape(nseq, rows, HEAD_DIM)
            av_ref[:, :, h, :] = kv_ref[:, slice(d_a + cs.start, d_a + cs.stop)].reshape(nseq, rows, HEAD_DIM)


def _in_proj(x, g, w_bf16, gq, gk, gqm, n_heads_a, n_heads_m, d_rnn, nseq, rows, tiles_per_seq, transposed):
    m, k = x.shape
    d_in = w_bf16.shape[1]
    d_a, d_m = n_heads_a * HEAD_DIM, n_heads_m * HEAD_DIM
    tm = nseq * rows
    nt = m // tm
    n_cache_seq = nt * nseq // tiles_per_seq
    kern = functools.partial(_in_proj_kernel, d_a=d_a, d_m=d_m, d_rnn=d_rnn, group=4 * HEAD_DIM,
                             tiles_per_seq=tiles_per_seq, transposed=transposed)
    row = lambda c: pl.BlockSpec((tm, c), lambda i: (i, 0))
    vec = pl.BlockSpec((1, HEAD_DIM), lambda i: (0, 0))
    cache = pl.BlockSpec((nseq, rows, n_heads_a, HEAD_DIM), lambda i: (i // tiles_per_seq, 0, 0, 0))
    k_spec = pl.BlockSpec((d_a, tm), lambda i: (0, i)) if transposed else row(d_a)
    k_shape = jax.ShapeDtypeStruct((d_a, m) if transposed else (m, d_a), BF16)
    return pl.pallas_call(
        kern,
        grid=(nt,),
        in_specs=[
            row(k),
            pl.BlockSpec((1, k), lambda i: (0, 0)),
            pl.BlockSpec((k, d_in), lambda i: (0, 0), pipeline_mode=pl.Buffered(1)),
            vec, vec, vec,
        ],
        out_specs=[row(d_a), k_spec, row(d_a), row(d_m), row(2 * d_rnn), cache, cache],
        out_shape=[
            jax.ShapeDtypeStruct((m, d_a), BF16),
            k_shape,
            jax.ShapeDtypeStruct((m, d_a), BF16),
            jax.ShapeDtypeStruct((m, d_m), BF16),
            jax.ShapeDtypeStruct((m, 2 * d_rnn), F32),
            jax.ShapeDtypeStruct((n_cache_seq, rows, n_heads_a, HEAD_DIM), F32),
            jax.ShapeDtypeStruct((n_cache_seq, rows, n_heads_a, HEAD_DIM), F32),
        ],
        scratch_shapes=[pltpu.VMEM((tm, k), BF16), pltpu.VMEM((tm, 2 * d_a), F32)],
        compiler_params=_params(("arbitrary",)),
        name="in_proj",
    )(x, g.reshape(1, k), w_bf16, gq.reshape(1, HEAD_DIM), gk.reshape(1, HEAD_DIM), gqm.reshape(1, HEAD_DIM))


def _rel_bias_kernel(tab_ref, o_ref, *, tpad, wext):
    t = tab_ref[...]
    hi = t.astype(BF16)
    r1 = t - hi.astype(F32)
    mid = r1.astype(BF16)
    lo = (r1 - mid.astype(F32)).astype(BF16)
    m = lax.broadcasted_iota(jnp.int32, (1, wext), 1)
    x = jnp.where(m >= wext - CHUNK, m - wext, m)
    idx = jnp.clip(WINDOW - x, -REL_CLIP, REL_CLIP) + REL_CLIP
    tix = lax.broadcasted_iota(jnp.int32, (tpad, wext), 0)
    onehot = jnp.where(tix == idx, 1.0, 0.0).astype(BF16)
    v = (jnp.dot(hi, onehot, preferred_element_type=F32)
         + jnp.dot(mid, onehot, preferred_element_type=F32)
         + jnp.dot(lo, onehot, preferred_element_type=F32))
    j = lax.broadcasted_iota(jnp.int32, (1, KSPAN), 1)
    for ci in range(QSUB // CHUNK):
        jj = j - CHUNK * ci
        valid = jnp.logical_and(jj >= 0, jj < WINDOW + CHUNK)
        for ii in range(CHUNK):
            s = ci * CHUNK + ii
            row = (v if s == 0 else pltpu.roll(v, s, 1))[:, :KSPAN]
            o_ref[:, s, :] = jnp.where(valid, row, NEG)


def _rel_bias(rel_table):
    n_heads, tlen = rel_table.shape
    tpad = -(-tlen // LANES) * LANES
    wext = KSPAN + LANES
    assert wext - CHUNK >= WINDOW + CHUNK and tlen == 2 * REL_CLIP + 1
    tab = jnp.pad(rel_table, ((0, 0), (0, tpad - tlen)))
    return pl.pallas_call(
        functools.partial(_rel_bias_kernel, tpad=tpad, wext=wext),
        out_shape=jax.ShapeDtypeStruct((n_heads, QSUB, KSPAN), F32),
        compiler_params=_params(None),
        name="rel_bias",
    )(tab)


def _band_attn_kernel(q_ref, ktc_ref, vc_ref, bias_ref, o_ref, ktcat_ref, vcat_ref, *, n_heads, tq):
    t = pl.program_id(1)

    @pl.when(t == 0)
    def _():
        ktcat_ref[:, 0:WINDOW] = jnp.zeros((ktcat_ref.shape[0], WINDOW), ktcat_ref.dtype)
        vcat_ref[0:WINDOW, :] = jnp.zeros((WINDOW, vcat_ref.shape[1]), vcat_ref.dtype)

    ktcat_ref[:, WINDOW:WINDOW + tq] = ktc_ref[...]
    vcat_ref[WINDOW:WINDOW + tq, :] = vc_ref[0]

    for s in range(tq // QSUB):
        r0 = s * QSUB
        before_start = jnp.logical_and(lax.broadcasted_iota(jnp.int32, (1, KSPAN), 1) + r0 < WINDOW, t == 0)
        for h in range(n_heads):
            cs = slice(h * HEAD_DIM, (h + 1) * HEAD_DIM)
            sc = jnp.dot(q_ref[0, r0:r0 + QSUB, cs], ktcat_ref[cs, r0:r0 + KSPAN], preferred_element_type=F32)
            sc = jnp.where(before_start, NEG, sc + bias_ref[h])
            m = jnp.max(sc, axis=-1, keepdims=True)
            p = jnp.exp(sc - m)
            l = jnp.sum(p, axis=-1, keepdims=True)
            o = jnp.dot(p.astype(BF16), vcat_ref[r0:r0 + KSPAN, cs], preferred_element_type=F32) / l
            o_ref[0, r0:r0 + QSUB, cs] = o.astype(o_ref.dtype)

    ktcat_ref[:, 0:WINDOW] = ktcat_ref[:, WINDOW:WINDOW + tq]
    vcat_ref[0:WINDOW, :] = vcat_ref[WINDOW:WINDOW + tq, :]


def _band_attn(q3, kt, v3, bias, n_heads, tq):
    b, s, d_a = q3.shape
    assert tq == WINDOW and s % tq == 0
    nt = s // tq
    cur = lambda bi, t: (bi, t, 0)
    return pl.pallas_call(
        functools.partial(_band_attn_kernel, n_heads=n_heads, tq=tq),
        grid=(b, nt),
        in_specs=[
            pl.BlockSpec((1, tq, d_a), cur),
            pl.BlockSpec((d_a, tq), lambda bi, t: (0, bi * nt + t)),
            pl.BlockSpec((1, tq, d_a), cur),
            pl.BlockSpec((n_heads, QSUB, KSPAN), lambda bi, t: (0, 0, 0)),
        ],
        out_specs=pl.BlockSpec((1, tq, d_a), cur),
        out_shape=jax.ShapeDtypeStruct((b, s, d_a), BF16),
        scratch_shapes=[pltpu.VMEM((d_a, WINDOW + tq), BF16), pltpu.VMEM((WINDOW + tq, d_a), BF16)],
        compiler_params=_params(("parallel", "arbitrary")),
        name="band_attn",
    )(q3, kt, v3, bias)


def _band_attn_sample_kernel(q_ref, kn_ref, vn_ref, ck_hbm, cv_hbm, bias_ref, o_ref, kbuf, vbuf, sem,
                             *, n_heads, t, p_len):
    b = pl.program_id(0)
    slot = b % 2

    def cache_copies(seq, sl):
        cps = []
        for h in range(n_heads):
            cps.append(pltpu.make_async_copy(ck_hbm.at[seq, :, h, :], kbuf.at[sl, h], sem.at[0, sl, h]))
            cps.append(pltpu.make_async_copy(cv_hbm.at[seq, :, h, :], vbuf.at[sl, h], sem.at[1, sl, h]))
        return cps

    @pl.when(b == 0)
    def _():
        for cp in cache_copies(0, 0):
            cp.start()

    @pl.when(b + 1 < pl.num_programs(0))
    def _():
        for cp in cache_copies(b + 1, 1 - slot):
            cp.start()

    for cp in cache_copies(b, slot):
        cp.wait()

    heads = [slice(h * HEAD_DIM, (h + 1) * HEAD_DIM) for h in range(n_heads)]
    dn = (((1,), (1,)), ((), ()))
    scores = []
    for h, cs in enumerate(heads):
        q = q_ref[0, :, cs]
        ck = kbuf[slot, h].astype(BF16)
        scores.append((lax.dot_general(q, ck, dn, preferred_element_type=F32),
                       lax.dot_general(q, kn_ref[0, :, cs], dn, preferred_element_type=F32)))
    for h, cs in enumerate(heads):
        cv = vbuf[slot, h].astype(BF16)
        vn = vn_ref[0, :, cs]
        s1 = scores[h][0] + bias_ref[h, 0:t, 0:p_len]
        s2 = scores[h][1] + bias_ref[h, 0:t, p_len:p_len + t]
        m = jnp.maximum(jnp.max(s1, axis=-1, keepdims=True), jnp.max(s2, axis=-1, keepdims=True))
        p1 = jnp.exp(s1 - m)
        p2 = jnp.exp(s2 - m)
        l = jnp.sum(p1, axis=-1, keepdims=True) + jnp.sum(p2, axis=-1, keepdims=True)
        o = (jnp.dot(p1.astype(BF16), cv, preferred_element_type=F32)
             + jnp.dot(p2.astype(BF16), vn, preferred_element_type=F32)) / l
        o_ref[0, :, cs] = o.astype(o_ref.dtype)


def _band_attn_sample(q3, k3, v3, cache_k, cache_v, bias, n_heads):
    b, t, d_a = q3.shape
    p_len = cache_k.shape[1]
    assert p_len == WINDOW and t <= CHUNK
    cache = pl.BlockSpec(memory_space=pl.ANY)
    return pl.pallas_call(
        functools.partial(_band_attn_sample_kernel, n_heads=n_heads, t=t, p_len=p_len),
        grid=(b,),
        in_specs=[
            pl.BlockSpec((1, t, d_a), lambda bi: (bi, 0, 0)),
            pl.BlockSpec((1, t, d_a), lambda bi: (bi, 0, 0)),
            pl.BlockSpec((1, t, d_a), lambda bi: (bi, 0, 0)),
            cache, cache,
            pl.BlockSpec((n_heads, QSUB, KSPAN), lambda bi: (0, 0, 0)),
        ],
        out_specs=pl.BlockSpec((1, t, d_a), lambda bi: (bi, 0, 0)),
        out_shape=jax.ShapeDtypeStruct((b, t, d_a), BF16),
        scratch_shapes=[
            pltpu.VMEM((2, n_heads, p_len, HEAD_DIM), cache_k.dtype),
            pltpu.VMEM((2, n_heads, p_len, HEAD_DIM), cache_v.dtype),
            pltpu.SemaphoreType.DMA((2, 2, n_heads)),
        ],
        compiler_params=_params(("arbitrary",)),
        name="band_attn_sample",
    )(q3, k3, v3, cache_k, cache_v, bias)


def _mem_attn_kernel(q_ref, mk_ref, mv_ref, o_ref, *, n_heads, head_axis):
    heads = [slice(h * HEAD_DIM, (h + 1) * HEAD_DIM) for h in range(n_heads)]
    load = (lambda ref, h, cs: ref[0, :, h, :]) if head_axis else (lambda ref, h, cs: ref[0, :, cs])
    scores = [lax.dot_general(q_ref[0, :, cs], load(mk_ref, h, cs).astype(BF16), (((1,), (1,)), ((), ())),
                              preferred_element_type=F32) for h, cs in enumerate(heads)]
    for h, cs in enumerate(heads):
        v = load(mv_ref, h, cs).astype(BF16)
        sc = scores[h]
        m = jnp.max(sc, axis=-1, keepdims=True)
        p = jnp.exp(sc - m)
        l = jnp.sum(p, axis=-1, keepdims=True)
        o = jnp.dot(p.astype(BF16), v, preferred_element_type=F32) / l
        o_ref[0, :, cs] = o.astype(o_ref.dtype)


def _mem_attn(qm3, mem_k, mem_v, n_heads, tm):
    b, s, d_m = qm3.shape
    n_mem = mem_k.shape[1]
    head_axis = mem_k.ndim == 4
    if head_axis:
        kspec = vspec = pl.BlockSpec((1, n_mem, n_heads, HEAD_DIM), lambda bi, t: (bi, 0, 0, 0))
    else:
        kspec = pl.BlockSpec((1, n_mem, d_m), lambda bi, t: (bi, 0, 0))
        vspec = pl.BlockSpec((1, n_mem, d_m), lambda bi, t: (bi, 0, 1))
    return pl.pallas_call(
        functools.partial(_mem_attn_kernel, n_heads=n_heads, head_axis=head_axis),
        grid=(b, s // tm),
        in_specs=[pl.BlockSpec((1, tm, d_m), lambda bi, t: (bi, t, 0)), kspec, vspec],
        out_specs=pl.BlockSpec((1, tm, d_m), lambda bi, t: (bi, t, 0)),
        out_shape=jax.ShapeDtypeStruct((b, s, d_m), BF16),
        compiler_params=_params(("parallel", "parallel")),
        name="mem_attn",
    )(qm3, mem_k, mem_v)


def _rglru_core(xb_ref, gb_ref, past_ref, h0_ref, cw_ref, cb_ref, wa_ref, ba_ref, wi_ref, bi_ref, lam_ref,
                buf_ref, a_ref, u_ref, hc_ref, tt, width, after_gates=None):
    t = pl.program_id(1)
    pad = SUBLANES

    @pl.when(t == 0)
    def _():
        buf_ref[0:pad, :] = past_ref[0]
        hc_ref[...] = h0_ref[0]

    buf_ref[pad:pad + tt, :] = xb_ref[0]
    xc = cb_ref[...] + cw_ref[width - 1:width, :] * xb_ref[0]
    for k in range(1, width):
        xc = xc + cw_ref[width - 1 - k:width - k, :] * buf_ref[pad - k:pad - k + tt, :]
    buf_ref[0:pad, :] = buf_ref[tt:tt + pad, :]

    xcb = xc.astype(BF16)
    def sigmoid(z):
        return 0.5 * jnp.tanh(0.5 * z) + 0.5

    r = sigmoid(jnp.dot(xcb, wa_ref[...], preferred_element_type=F32) + ba_ref[...])
    i = sigmoid(jnp.dot(xcb, wi_ref[...], preferred_element_type=F32) + bi_ref[...])
    if after_gates is not None:
        after_gates()
    log_a = -LRU_C * r * jax.nn.softplus(-lam_ref[...])
    a = jnp.exp(log_a)
    th = jnp.tanh(log_a)
    n = -2.0 * th
    dd = 1.0 - th
    u = jnp.where(n > 0.0, n * lax.rsqrt(n * dd), 0.0) * (i * xc)

    d = a.shape[-1]
    a = a.reshape(tt // SUBLANES, SUBLANES, d)
    u = u.reshape(tt // SUBLANES, SUBLANES, d)
    rowm = lax.broadcasted_iota(jnp.int32, (1, SUBLANES, 1), 1)
    sh = 1
    while sh < SUBLANES:
        keep = rowm >= sh
        a_sh = pltpu.roll(a, sh, 1)
        u_sh = pltpu.roll(u, sh, 1)
        u = jnp.where(keep, a * u_sh + u, u)
        a = jnp.where(keep, a * a_sh, a)
        sh *= 2
    a_ref[...] = a.reshape(tt, d)
    u_ref[...] = u.reshape(tt, d)

    h_last = hc_ref[...]
    for r0 in range(0, tt, SUBLANES):
        hr = a_ref[r0:r0 + SUBLANES, :] * h_last + u_ref[r0:r0 + SUBLANES, :]
        u_ref[r0:r0 + SUBLANES, :] = hr
        h_last = hr[SUBLANES - 1:SUBLANES, :]
    hc_ref[...] = h_last
    return h_last, (u_ref[...] * _gelu(gb_ref[0])).astype(BF16)


def _rglru_kernel(xb_ref, gb_ref, past_ref, h0_ref, cw_ref, cb_ref, wa_ref, ba_ref, wi_ref, bi_ref, lam_ref,
                  o_ref, hl_ref, buf_ref, a_ref, u_ref, hc_ref, *, tt, width):
    h_last, ob = _rglru_core(xb_ref, gb_ref, past_ref, h0_ref, cw_ref, cb_ref, wa_ref, ba_ref, wi_ref, bi_ref,
                             lam_ref, buf_ref, a_ref, u_ref, hc_ref, tt, width)
    hl_ref[0] = h_last
    o_ref[0] = ob


def _rglru_out_proj_kernel(x_ref, oa_ref, om_ref, xb_ref, gb_ref, past_ref, h0_ref, cw_ref, cb_ref, wa_ref, ba_ref,
                           wi_ref, bi_ref, lam_ref, woam_ref, wob_ref, g_ref,
                           o_ref, xn_ref, hl_ref, buf_ref, a_ref, u_ref, hc_ref, *, tt, width):
    def attention_share():
        lhs = jnp.concatenate([oa_ref[0], om_ref[0]], axis=1)
        o_ref[0] = jnp.dot(lhs, woam_ref[...], preferred_element_type=F32)

    h_last, ob = _rglru_core(xb_ref, gb_ref, past_ref, h0_ref, cw_ref, cb_ref, wa_ref, ba_ref, wi_ref, bi_ref,
                             lam_ref, buf_ref, a_ref, u_ref, hc_ref, tt, width, after_gates=attention_share)
    hl_ref[0] = h_last
    x1 = x_ref[0] + o_ref[0] + jnp.dot(ob, wob_ref[...], preferred_element_type=F32)
    o_ref[0] = x1
    xn_ref[0] = _rms(x1, g_ref[...]).astype(BF16)


def _rglru(xg3, past8, h0, cw, cb, wa_bd, ba, wi_bd, bi, lam, tt):
    b, s, _ = xg3.shape
    d = cw.shape[1]
    width = cw.shape[0]
    vec = pl.BlockSpec((1, d), lambda bi_, t: (0, 0))
    mat = pl.BlockSpec((d, d), lambda bi_, t: (0, 0))
    return pl.pallas_call(
        functools.partial(_rglru_kernel, tt=tt, width=width),
        grid=(b, s // tt),
        in_specs=[
            pl.BlockSpec((1, tt, d), lambda bi_, t: (bi_, t, 0)),
            pl.BlockSpec((1, tt, d), lambda bi_, t: (bi_, t, 1)),
            pl.BlockSpec((1, SUBLANES, d), lambda bi_, t: (bi_, 0, 0)),
            pl.BlockSpec((1, 1, d), lambda bi_, t: (bi_, 0, 0)),
            pl.BlockSpec((width, d), lambda bi_, t: (0, 0)),
            vec, mat, vec, mat, vec, vec,
        ],
        out_specs=[
            pl.BlockSpec((1, tt, d), lambda bi_, t: (bi_, t, 0)),
            pl.BlockSpec((1, 1, d), lambda bi_, t: (bi_, 0, 0)),
        ],
        out_shape=[jax.ShapeDtypeStruct((b, s, d), BF16), jax.ShapeDtypeStruct((b, 1, d), F32)],
        scratch_shapes=[
            pltpu.VMEM((tt + 2 * SUBLANES, d), F32),
            pltpu.VMEM((tt, d), F32),
            pltpu.VMEM((tt, d), F32),
            pltpu.VMEM((1, d), F32),
        ],
        compiler_params=_params(("parallel", "arbitrary")),
        name="rglru",
    )(xg3, xg3, past8, h0.reshape(b, 1, d), cw, cb.reshape(1, d), wa_bd, ba.reshape(1, d),
      wi_bd, bi.reshape(1, d), lam.reshape(1, d))


def _rglru_out_proj(x3, oa3, om3, xg3, past8, h0, cw, cb, wa_bd, ba, wi_bd, bi, lam, woam, wob, g_next, tt):
    b, s, dm = x3.shape
    d = cw.shape[1]
    width = cw.shape[0]
    vec = pl.BlockSpec((1, d), lambda bi_, t: (0, 0))
    mat = pl.BlockSpec((d, d), lambda bi_, t: (0, 0))
    row = lambda c, j=0: pl.BlockSpec((1, tt, c), lambda bi_, t: (bi_, t, j))
    full = lambda w: pl.BlockSpec(w.shape, lambda bi_, t: (0, 0), pipeline_mode=pl.Buffered(1))
    return pl.pallas_call(
        functools.partial(_rglru_out_proj_kernel, tt=tt, width=width),
        grid=(b, s // tt),
        in_specs=[
            row(dm), row(oa3.shape[2]), row(om3.shape[2]), row(d, 0), row(d, 1),
            pl.BlockSpec((1, SUBLANES, d), lambda bi_, t: (bi_, 0, 0)),
            pl.BlockSpec((1, 1, d), lambda bi_, t: (bi_, 0, 0)),
            pl.BlockSpec((width, d), lambda bi_, t: (0, 0)),
            vec, mat, vec, mat, vec, vec,
            full(woam), full(wob),
            pl.BlockSpec((1, dm), lambda bi_, t: (0, 0)),
        ],
        out_specs=[row(dm), row(dm), pl.BlockSpec((1, 1, d), lambda bi_, t: (bi_, 0, 0))],
        out_shape=[jax.ShapeDtypeStruct((b, s, dm), F32), jax.ShapeDtypeStruct((b, s, dm), BF16),
                   jax.ShapeDtypeStruct((b, 1, d), F32)],
        scratch_shapes=[
            pltpu.VMEM((tt + 2 * SUBLANES, d), F32),
            pltpu.VMEM((tt, d), F32),
            pltpu.VMEM((tt, d), F32),
            pltpu.VMEM((1, d), F32),
        ],
        compiler_params=_params(("parallel", "arbitrary")),
        name="rglru_out_proj",
    )(x3, oa3, om3, xg3, xg3, past8, h0.reshape(b, 1, d), cw, cb.reshape(1, d), wa_bd, ba.reshape(1, d),
      wi_bd, bi.reshape(1, d), lam.reshape(1, d), woam, wob, g_next.reshape(1, dm))


def _out_proj_kernel(x_ref, oa_ref, om_ref, ob_ref, wam_ref, wb_ref, g_ref, o_ref, xn_ref):
    lhs = jnp.concatenate([oa_ref[...], om_ref[...]], axis=1)
    acc = jnp.dot(lhs, wam_ref[...], preferred_element_type=F32)
    acc = acc + jnp.dot(ob_ref[...], wb_ref[...], preferred_element_type=F32)
    x1 = x_ref[...] + acc
    o_ref[...] = x1
    xn_ref[...] = _rms(x1, g_ref[...]).astype(BF16)


def _out_proj(x, oa, om, ob, wam, wb, g_next, tm):
    m, d = x.shape
    row = lambda c: pl.BlockSpec((tm, c), lambda i: (i, 0))
    full = lambda a: pl.BlockSpec(a.shape, lambda i: (0, 0))
    return pl.pallas_call(
        _out_proj_kernel,
        grid=(m // tm,),
        in_specs=[row(d), row(oa.shape[1]), row(om.shape[1]), row(ob.shape[1]), full(wam), full(wb),
                  pl.BlockSpec((1, d), lambda i: (0, 0))],
        out_specs=[row(d), row(d)],
        out_shape=[jax.ShapeDtypeStruct((m, d), F32), jax.ShapeDtypeStruct((m, d), BF16)],
        compiler_params=_params(("parallel",)),
        name="out_proj",
    )(x, oa, om, ob, wam, wb, g_next.reshape(1, d))


def _conv_ffn_kernel(x_hbm, xn_ref, hist_ref, wv_ref, wg_ref, cw_ref, cb_ref, wd_ref,
                     o_hbm, f_ref, acc_ref, bv_ref, bg_ref, car_ref, sem_in, sem_out,
                     *, nseq, seq_rows, tiles_per_seq, width):
    i = pl.program_id(0)
    c = pl.program_id(1)
    nt = pl.num_programs(0)
    nc = pl.num_programs(1)
    pad = SUBLANES
    tm = nseq * seq_rows
    tf = wv_ref.shape[1]
    sub = min(tf, MXU_COLS)
    streams = ((c, wv_ref, bv_ref), (nc + c, wg_ref, bg_ref))
    cols = lambda j, start=0, size=tf: pl.ds(pl.multiple_of(j * tf + start, sub), size)
    slot = i % 2

    def fetch(tile, sl):
        return pltpu.make_async_copy(x_hbm.at[pl.ds(tile * tm, tm)], acc_ref.at[sl], sem_in.at[sl])

    def drain(tile, sl):
        return pltpu.make_async_copy(acc_ref.at[sl], o_hbm.at[pl.ds(tile * tm, tm)], sem_out.at[sl])

    @pl.when(jnp.logical_and(i == 0, c == 0))
    def _():
        fetch(0, 0).start()

    @pl.when(c == 0)
    def _():
        fetch(i, slot).wait()

    @pl.when(c == 1)
    def _():
        @pl.when(i >= 1)
        def _():
            drain(i - 1, 1 - slot).wait()

        @pl.when(i + 1 < nt)
        def _():
            fetch(i + 1, 1 - slot).start()

    for j, _, buf_ref in streams:
        if tiles_per_seq == 1:
            buf_ref[:, pad - (width - 1):pad, :] = hist_ref[:, :, cols(j)]
        else:
            first = i % tiles_per_seq == 0

            @pl.when(first)
            def _(j=j, buf_ref=buf_ref):
                buf_ref[:, pad - (width - 1):pad, :] = hist_ref[:, :, cols(j)]

            @pl.when(jnp.logical_not(first))
            def _(j=j, buf_ref=buf_ref):
                buf_ref[:, 0:pad, :] = car_ref[j]

    def up_proj(cs):
        for _, w_ref, buf_ref in streams:
            up = jnp.dot(xn_ref[...], w_ref[:, cs], preferred_element_type=F32)
            buf_ref[:, pad:pad + seq_rows, cs] = up.reshape(nseq, seq_rows, sub)

    def conv(j, buf_ref, cs):
        lanes = cols(j, cs.start, sub)
        y = cb_ref[:, lanes]
        for k in range(width):
            y = y + cw_ref[width - 1 - k:width - k, lanes] * buf_ref[:, pad - k:pad - k + seq_rows, cs]
        return y

    subs = [slice(s * sub, (s + 1) * sub) for s in range(tf // sub)]
    up_proj(subs[0])
    for s, cs in enumerate(subs):
        if s + 1 < len(subs):
            up_proj(subs[s + 1])
        val = conv(streams[0][0], bv_ref, cs)
        gate = conv(streams[1][0], bg_ref, cs)
        h = (_gelu(gate) * val).astype(BF16).reshape(tm, sub)
        acc_ref[slot] += jnp.dot(h, wd_ref[cs, :], preferred_element_type=F32)

    for j, _, buf_ref in streams:
        f_ref[:, :, cols(j)] = buf_ref[:, pad + seq_rows - (width - 1):pad + seq_rows, :]
        if tiles_per_seq > 1:
            car_ref[j] = buf_ref[:, seq_rows:seq_rows + pad, :]

    @pl.when(c == nc - 1)
    def _():
        drain(i, slot).start()

        @pl.when(i == nt - 1)
        def _():
            drain(i, slot).wait()


def _conv_ffn(x, xn, f_conv_past, w_up, cw, cb, w_down, nseq, seq_rows, tiles_per_seq, tf):
    m, d = x.shape
    d_ff = w_down.shape[0]
    width = cw.shape[0]
    tm = nseq * seq_rows
    nt = m // tm
    nc = d_ff // tf
    nb = f_conv_past.shape[0]
    assert nc >= 2 and m == nt * tm
    assert f_conv_past.shape == (nb, width - 1, 2 * d_ff) and nb * tiles_per_seq == nt * nseq
    kern = functools.partial(_conv_ffn_kernel, nseq=nseq, seq_rows=seq_rows,
                             tiles_per_seq=tiles_per_seq, width=width)
    car_shape = (2 * nc, nseq, SUBLANES, tf) if tiles_per_seq > 1 else (1, 1, SUBLANES, LANES)
    y, f = pl.pallas_call(
        kern,
        grid=(nt, nc),
        in_specs=[
            pl.BlockSpec(memory_space=pl.ANY),
            pl.BlockSpec((tm, d), lambda i, c: (i, 0)),
            pl.BlockSpec((nseq, width - 1, 2 * d_ff), lambda i, c: (i // tiles_per_seq, 0, 0)),
            pl.BlockSpec((d, tf), lambda i, c: (0, c)),
            pl.BlockSpec((d, tf), lambda i, c: (0, nc + c)),
            pl.BlockSpec((width, 2 * d_ff), lambda i, c: (0, 0)),
            pl.BlockSpec((1, 2 * d_ff), lambda i, c: (0, 0)),
            pl.BlockSpec((tf, d), lambda i, c: (c, 0)),
        ],
        out_specs=[
            pl.BlockSpec(memory_space=pl.ANY),
            pl.BlockSpec((nseq, width - 1, 2 * d_ff), lambda i, c: (i, 0, 0)),
        ],
        out_shape=[
            jax.ShapeDtypeStruct((m, d), F32),
            jax.ShapeDtypeStruct((nt * nseq, width - 1, 2 * d_ff), F32),
        ],
        scratch_shapes=[
            pltpu.VMEM((2, tm, d), F32),
            pltpu.VMEM((nseq, seq_rows + 2 * SUBLANES, tf), F32),
            pltpu.VMEM((nseq, seq_rows + 2 * SUBLANES, tf), F32),
            pltpu.VMEM(car_shape, F32),
            pltpu.SemaphoreType.DMA((2,)),
            pltpu.SemaphoreType.DMA((2,)),
        ],
        compiler_params=_params(("arbitrary", "arbitrary")),
        name="conv_ffn",
    )(x, xn, f_conv_past, w_up, w_up, cw, cb.reshape(1, 2 * d_ff), w_down)
    f = f.reshape(nt // tiles_per_seq, tiles_per_seq, nseq, width - 1, 2 * d_ff)[:, -1]
    return y, f.reshape(nb, width - 1, 2 * d_ff)


def _pad_rows_front(a, rows):
    return jnp.pad(a, ((0, 0), (rows - a.shape[1], 0), (0, 0)))


def _block_diag(w):
    n, bi, bo = w.shape
    eye = jnp.eye(n, dtype=w.dtype)
    return (w[:, :, None, :] * eye[:, None, :, None]).reshape(n * bi, n * bo)


def _trunk(x3, band_fn, mem_k, mem_v, b_conv_past, b_h0, f_conv_past, p, *, in_tiles, transposed, tq_mem, tt,
           ffn_tiles, ffn_cols):
    b, t, d = x3.shape
    n_heads_a, n_heads_m, d_rnn = p["n_heads_a"], p["n_heads_m"], p["d_rnn"]
    d_a, d_m = n_heads_a * HEAD_DIM, n_heads_m * HEAD_DIM
    x2 = x3.reshape(b * t, d)
    nseq_in, rows_in, tiles_in = in_tiles
    q, k, v, qm, xg, k_a, v_a = _in_proj(x2, p["g_attn"], p["w_in"], p["g_q_a"], p["g_k_a"], p["g_q_m"],
                                         n_heads_a, n_heads_m, d_rnn, nseq_in, rows_in, tiles_in, transposed)
    xg3 = xg.reshape(b, t, 2 * d_rnn)

    o_a = band_fn(q, k, v)
    o_m = _mem_attn(qm.reshape(b, t, d_m), mem_k, mem_v, n_heads_m, tq_mem)
    lru = (xg3, _pad_rows_front(b_conv_past, SUBLANES), b_h0, p["conv_b_w"], p["conv_b_b"],
           p["wa_bd"], p["b_rg_a"], p["wi_bd"], p["b_rg_i"], p["lru_lambda"])
    wo = (p["wo_am"], p["wo_b"], p["g_ffn"])
    if tt >= OUT_ROWS:
        x1, xn1, h_last = _rglru_out_proj(x3, o_a, o_m, *lru, *wo, tt)
        x1, xn1 = x1.reshape(b * t, d), xn1.reshape(b * t, d)
    else:
        o_b, h_last = _rglru(*lru, tt)
        x1, xn1 = _out_proj(x2, o_a.reshape(b * t, d_a), o_m.reshape(b * t, d_m), o_b.reshape(b * t, d_rnn),
                            *wo, OUT_ROWS)

    nseq, seq_rows, tiles_per_seq = ffn_tiles
    y, f_new = _conv_ffn(x1, xn1, f_conv_past, p["w_up"], p["conv_f_w"], p["conv_f_b"], p["w_down"],
                         nseq, seq_rows, tiles_per_seq, ffn_cols)

    wb = p["conv_b_w"].shape[0]
    xb = xg3[:, :, :d_rnn]
    b_conv_new = jnp.concatenate([b_conv_past, xb], axis=1)[:, -(wb - 1):] if t < wb - 1 else xb[:, t - (wb - 1):]
    return y.reshape(b, t, d), k_a, v_a, b_conv_new, h_last.reshape(b, d_rnn), f_new


def kernel(x_prompt, x_sample, cache_a_k, cache_a_v, cache_mem_k, cache_mem_v, cache_b_conv, state_b_h, cache_f_conv, mem_prompt, g_attn, w_in, g_q_a, g_k_a, rel_table, g_q_m, g_k_m, g_mem, w_mem_kv, conv_b_w, conv_b_b, w_rg_a, b_rg_a, w_rg_i, b_rg_i, lru_lambda, w_out, g_ffn, w_up, conv_f_w, conv_f_b, w_down):
    depth = w_in.shape[0]
    bp, sp, d = x_prompt.shape
    bs, ts, _ = x_sample.shape
    n_heads_a = cache_a_k.shape[3]
    n_heads_m = cache_mem_k.shape[3]
    n_mem = mem_prompt.shape[1]
    d_a, d_m = n_heads_a * HEAD_DIM, n_heads_m * HEAD_DIM
    d_rnn = conv_b_w.shape[2]
    d_ff = w_down.shape[1]
    keep = min(WINDOW, sp)
    ones = lambda n: jnp.ones((n,), F32)

    xp, xs = x_prompt, x_sample
    outs = [[] for _ in range(12)]
    for l in range(depth):
        p = dict(
            n_heads_a=n_heads_a, n_heads_m=n_heads_m, d_rnn=d_rnn,
            g_attn=g_attn[l], w_in=w_in[l].astype(BF16),
            g_q_a=g_q_a[l] * SCALE, g_k_a=g_k_a[l], g_q_m=g_q_m[l] * SCALE,
            conv_b_w=conv_b_w[l], conv_b_b=conv_b_b[l],
            wa_bd=_block_diag(w_rg_a[l]).astype(BF16), b_rg_a=b_rg_a[l],
            wi_bd=_block_diag(w_rg_i[l]).astype(BF16), b_rg_i=b_rg_i[l], lru_lambda=lru_lambda[l],
            wo_am=w_out[l, :d_a + d_m].astype(BF16), wo_b=w_out[l, d_a + d_m:].astype(BF16),
            g_ffn=g_ffn[l], w_up=w_up[l].astype(BF16), conv_f_w=conv_f_w[l], conv_f_b=conv_f_b[l],
            w_down=w_down[l].astype(BF16),
        )
        bias = _rel_bias(rel_table[l])

        mem_gains = jnp.concatenate([jnp.tile(g_k_m[l], n_heads_m), ones(d_m)])
        kv = _norm_matmul(mem_prompt.reshape(bp * n_mem, d), g_mem[l], w_mem_kv[l].astype(BF16), mem_gains,
                          ((0, 1),), OUT_ROWS, d_m)
        kv3 = kv.reshape(bp, n_mem, 2 * d_m)

        band_p = lambda q, kt, v: _band_attn(q.reshape(bp, sp, d_a), kt, v.reshape(bp, sp, d_a), bias,
                                             n_heads_a, WINDOW)
        xp, k_a, v_a, bc, bh, fc = _trunk(
            xp, band_p, kv3, kv3,
            jnp.zeros((bp, conv_b_w.shape[1] - 1, d_rnn), F32), jnp.zeros((bp, d_rnn), F32),
            jnp.zeros((bp, conv_f_w.shape[1] - 1, 2 * d_ff), F32), p,
            in_tiles=(1, keep, sp // keep), transposed=True, tq_mem=FFN_ROWS, tt=OUT_ROWS,
            ffn_tiles=(1, FFN_ROWS, sp // FFN_ROWS), ffn_cols=FFN_COLS)
        mem_k = kv3[:, :, :d_m].reshape(bp, n_mem, n_heads_m, HEAD_DIM)
        mem_v = kv3[:, :, d_m:].reshape(bp, n_mem, n_heads_m, HEAD_DIM)
        for lst, v in zip(outs[:7], (k_a, v_a, mem_k, mem_v, bc, bh, fc)):
            lst.append(v)

        band_s = lambda q, k, v: _band_attn_sample(
            q.reshape(bs, ts, d_a), k.reshape(bs, ts, d_a), v.reshape(bs, ts, d_a),
            cache_a_k[l], cache_a_v[l], bias, n_heads_a)
        xs, k_a, v_a, bc, bh, fc = _trunk(
            xs, band_s, cache_mem_k[l], cache_mem_v[l],
            cache_b_conv[l], state_b_h[l], cache_f_conv[l], p,
            in_tiles=(bs, ts, 1), transposed=False, tq_mem=ts, tt=ts, ffn_tiles=(bs, ts, 1),
            ffn_cols=FFN_COLS)
        for lst, v in zip(outs[7:], (k_a, v_a, bc, bh, fc)):
            lst.append(v)

    stacked = [jnp.stack(o) for o in outs]
    return (xp, xs, *stacked)
```
